```python
import math, functools
import jax, jax.numpy as jnp
from jax import lax
import numpy as np

D_MODEL = 1024
BATCH = 8
SEQ = 2048
DEPTH = 1
DEC_BATCH = 32
DEC_SEQ = 4
PAST_LEN = 16384
PAGE_SIZE = 128

GLA_HEADS = 4
GLA_DV = D_MODEL // (2 * GLA_HEADS)
GLA_DK = GLA_DV // 2
GLA_RANK = 16
GLA_TAU = 16.0
GLA_CHUNK = 64
GLA_WIDTH = GLA_HEADS * GLA_DV
NSA_HEADS = 8
NSA_KV_HEADS = 2
NSA_GROUP = NSA_HEADS // NSA_KV_HEADS
NSA_HEAD_DIM = D_MODEL // (2 * NSA_HEADS)
NSA_WIDTH = NSA_HEADS * NSA_HEAD_DIM
CMP_BLK = 32
CMP_STRIDE = 16
CMP_HIDDEN = 2 * NSA_HEAD_DIM
SLC_BLK = 64
SLC_TOP_N = 16
WINDOW = 512
NSA_QBLK = 64
FORCE_BONUS = 1e4
NEG = -1e30
MIX_WIDTH = GLA_WIDTH + NSA_WIDTH
D_FF = -(-8 * D_MODEL // (3 * 256)) * 256
ALPHA = (2 * DEPTH) ** 0.25
BETA = (8 * DEPTH) ** -0.25
KV_W = 2 * NSA_KV_HEADS * NSA_HEAD_DIM
IN_SIZES = (GLA_HEADS * GLA_DK, GLA_HEADS * GLA_DK, GLA_WIDTH, GLA_RANK, GLA_WIDTH,
            NSA_WIDTH, KV_W, KV_W, KV_W, 3 * NSA_HEADS)
IN_WIDTH = sum(IN_SIZES)
IN_SPLITS = tuple(int(v) for v in np.cumsum(IN_SIZES)[:-1])

kernel_name = 'hybrid_gla_nsa_deepnorm_adaln_step'


def layer_norm(x, g, b, eps=1e-5):
    xf = x.astype(jnp.float32)
    mu = jnp.mean(xf, axis=-1, keepdims=True)
    xc = xf - mu
    var = jnp.mean(xc * xc, axis=-1, keepdims=True)
    return (xc * lax.rsqrt(var + eps) * g.astype(jnp.float32) + b.astype(jnp.float32)).astype(x.dtype)


def masked_softmax(s, mask):
    s = jnp.where(mask, s.astype(jnp.float32), NEG)
    e = jnp.where(mask, jnp.exp(s - jnp.max(s, axis=-1, keepdims=True)), 0.0)
    return e / jnp.maximum(jnp.sum(e, axis=-1, keepdims=True), 1e-30)


def gla_chunked(q, k, v, log_a, s0):
    f32 = jnp.float32
    B, L, H, _ = q.shape
    c = min(GLA_CHUNK, L)
    pad = (-L) % c

    def prep(t):
        t = jnp.pad(t.astype(f32), ((0, 0), (0, pad), (0, 0), (0, 0)))
        return t.reshape(B, -1, c, H, t.shape[-1]).transpose(1, 0, 3, 2, 4)

    qs, ks, vs, gs = prep(q), prep(k), prep(v), prep(log_a)
    causal = jnp.tril(jnp.ones((c, c), bool))[:, :, None]

    def step(S, inp):
        qc, kc, vc, gc = inp
        b = jnp.cumsum(gc, axis=2)
        o_inter = jnp.einsum('bhcd,bhde->bhce', qc * jnp.exp(b), S)
        diff = jnp.where(causal, b[:, :, :, None, :] - b[:, :, None, :, :], -jnp.inf)
        att = jnp.einsum('bhid,bhjd,bhijd->bhij', qc, kc, jnp.exp(diff))
        o = o_inter + jnp.einsum('bhij,bhje->bhie', att, vc)
        b_last = b[:, :, -1:, :]
        S = jnp.exp(b_last[:, :, 0, :])[..., None] * S + jnp.einsum('bhcd,bhce->bhde', kc * jnp.exp(b_last - b), vc)
        return S, o

    S, o = lax.scan(step, s0.astype(f32), (qs, ks, vs, gs))
    o = o.transpose(1, 0, 3, 2, 4).reshape(B, -1, H, o.shape[-1])[:, :L]
    return o, S


def compress_blocks(kv, pe, w1, b1, w2):
    B, T = kv.shape[:2]
    n_seg = T // CMP_STRIDE
    r = CMP_BLK // CMP_STRIDE
    nc = n_seg - r + 1
    seg = kv[:, :n_seg * CMP_STRIDE].reshape(B, n_seg, CMP_STRIDE, 2, NSA_KV_HEADS, NSA_HEAD_DIM)
    w1r = w1.reshape(2, r, CMP_STRIDE, NSA_HEAD_DIM, CMP_HIDDEN)
    proj = jnp.einsum('bnscgd,cmsdh->mbncgh', seg, w1r)
    pe_bias = jnp.einsum('pcd,cpdh->ch', pe, w1.reshape(2, CMP_BLK, NSA_HEAD_DIM, CMP_HIDDEN))
    h = sum(proj[m][:, m:m + nc] for m in range(r)) + (pe_bias + b1)[None, None, :, None, :]
    out = jnp.einsum('bncgh,chd->bncgd', jax.nn.gelu(h), w2)
    t_end = jnp.arange(nc, dtype=jnp.int32) * CMP_STRIDE + CMP_BLK - 1
    return out[:, :, 0], out[:, :, 1], t_end


def to_blocks(k):
    B, T = k.shape[:2]
    k = jnp.pad(k, ((0, 0), (0, (-T) % SLC_BLK), (0, 0), (0, 0)))
    return k.reshape(B, -1, SLC_BLK, NSA_KV_HEADS, NSA_HEAD_DIM).transpose(0, 3, 1, 2, 4)


def overlap_matrix(nc, nb):
    i = jnp.arange(nc)[:, None] * CMP_STRIDE
    j = jnp.arange(nb)[None, :] * SLC_BLK
    return ((i < j + SLC_BLK) & (i + CMP_BLK > j)).astype(jnp.float32)


def nsa_core(qg, tq, kc, vc, tc_end, kb, vb, kw, vw, tw, gates):
    f32 = jnp.float32
    B, Q, G, R, D = qg.shape
    nb = kb.shape[2]
    s_c = jnp.einsum('bqgrd,bngd->bqgrn', qg, kc.astype(f32))
    p_c = masked_softmax(s_c, (tc_end[None, :] <= tq[:, None])[None, :, None, None, :])
    o_c = jnp.einsum('bqgrn,bngd->bqgrd', p_c, vc.astype(f32))
    imp = jnp.einsum('bqgn,nj->bqgj', p_c.sum(axis=3), overlap_matrix(kc.shape[1], nb))
    j = jnp.arange(nb)[None, :]
    cur = (tq // SLC_BLK)[:, None]
    valid = j * SLC_BLK <= tq[:, None]
    forced = (j == 0) | (j == cur) | (j == cur - 1)
    score = jnp.where(valid[None, :, None, :], imp + FORCE_BONUS * forced[None, :, None, :].astype(f32), -jnp.inf)
    _, idx = lax.top_k(score, min(SLC_TOP_N, nb))
    take = jax.vmap(jax.vmap(lambda blocks, ids: blocks[ids]))
    idx_t = idx.transpose(0, 2, 1, 3)
    ks = take(kb, idx_t)
    vs = take(vb, idx_t)
    s_s = jnp.einsum('bqgrd,bgqnkd->bqgrnk', qg, ks.astype(f32))
    pos = idx[..., None] * SLC_BLK + jnp.arange(SLC_BLK)
    m_s = (pos <= tq[None, :, None, None, None]).reshape(B, Q, G, 1, -1)
    p_s = masked_softmax(s_s.reshape(B, Q, G, R, -1), m_s).reshape(s_s.shape)
    o_s = jnp.einsum('bqgrnk,bgqnkd->bqgrd', p_s, vs.astype(f32))
    s_w = jnp.einsum('bqgrd,bwgd->bqgrw', qg, kw.astype(f32))
    m_w = (tw[None, :] <= tq[:, None]) & (tw[None, :] > tq[:, None] - WINDOW) & (tw[None, :] >= 0)
    p_w = masked_softmax(s_w, m_w[None, :, None, None, :])
    o_w = jnp.einsum('bqgrw,bwgd->bqgrd', p_w, vw.astype(f32))
    g = gates.reshape(B, Q, G, R, 3)
    o = g[..., 0:1] * o_c + g[..., 1:2] * o_s + g[..., 2:3] * o_w
    return o.reshape(B, Q, G * R * D)


def nsa_prompt(q, kv_c, kv_s, kv_w, gates, pe, w1, b1, w2):
    B, L = q.shape[:2]
    qg = q.reshape(B, L, NSA_KV_HEADS, NSA_GROUP, NSA_HEAD_DIM).astype(jnp.float32) * NSA_HEAD_DIM ** -0.5
    kc, vc, tc_end = compress_blocks(kv_c, pe, w1, b1, w2)
    kb, vb = to_blocks(kv_s[:, :, 0]), to_blocks(kv_s[:, :, 1])
    kw_pad = jnp.pad(kv_w, ((0, 0), (WINDOW, 0), (0, 0), (0, 0), (0, 0)))
    qb = min(NSA_QBLK, L)

    def block(i):
        s0 = i * qb
        tq = s0 + jnp.arange(qb, dtype=jnp.int32)
        tw = s0 - WINDOW + jnp.arange(WINDOW + qb, dtype=jnp.int32)
        qi = lax.dynamic_slice_in_dim(qg, s0, qb, axis=1)
        gi = lax.dynamic_slice_in_dim(gates, s0, qb, axis=1)
        wi = lax.dynamic_slice_in_dim(kw_pad, s0, WINDOW + qb, axis=1)
        return nsa_core(qi, tq, kc, vc, tc_end, kb, vb, wi[:, :, 0], wi[:, :, 1], tw, gi)

    o = lax.map(block, jnp.arange(L // qb, dtype=jnp.int32))
    o = o.transpose(1, 0, 2, 3).reshape(B, L, NSA_WIDTH)
    return o, kv_w[:, L - min(WINDOW, L):]


def nsa_sample(cmp_past, slc_past, win_past, q, kv_c, kv_s, kv_w, gates, pe, w1, b1, w2):
    B, S = q.shape[:2]
    past_len = cmp_past.shape[1]
    qg = q.reshape(B, S, NSA_KV_HEADS, NSA_GROUP, NSA_HEAD_DIM).astype(jnp.float32) * NSA_HEAD_DIM ** -0.5
    kc, vc, tc_end = compress_blocks(jnp.concatenate([cmp_past, kv_c], axis=1), pe, w1, b1, w2)
    slc_all = jnp.concatenate([slc_past, kv_s], axis=1)
    kb, vb = to_blocks(slc_all[:, :, 0]), to_blocks(slc_all[:, :, 1])
    win_all = jnp.concatenate([win_past, kv_w], axis=1)
    wb = win_past.shape[1]
    tw = past_len - wb + jnp.arange(wb + S, dtype=jnp.int32)
    tq = past_len + jnp.arange(S, dtype=jnp.int32)
    o = nsa_core(qg, tq, kc, vc, tc_end, kb, vb, win_all[:, :, 0], win_all[:, :, 1], tw, gates)
    return o, win_all[:, S:]


def decoder_layer(x, c, s0, nsa_fn, w_ada, b_ada, w_in, gla_w_a2, gla_b_a, gla_norm_g, cmp_pe, cmp_w1, cmp_b1,
                  cmp_w2, w_o, ln1_g, ln1_b, w_ffn_in, w_ffn_out, ln2_g, ln2_b):
    f32 = jnp.float32
    B, L, _ = x.shape
    mod = jnp.einsum('bd,dm->bm', jax.nn.silu(c), w_ada) + b_ada
    sh1, sc1, ga1, sh2, sc2, ga2 = [m[:, None, :] for m in jnp.split(mod, 6, axis=-1)]
    u = x * (1 + sc1) + sh1
    z = jnp.einsum('bld,dp->blp', u, w_in)
    q_g, k_g, v_g, a_g, r_g, q_n, kv_c, kv_s, kv_w, g_n = jnp.split(z, IN_SPLITS, axis=-1)
    log_a = jax.nn.log_sigmoid((jnp.einsum('blr,rk->blk', a_g, gla_w_a2) + gla_b_a).astype(f32)) / GLA_TAU
    hs = (B, L, GLA_HEADS, -1)
    o_g, s_new = gla_chunked(q_g.reshape(hs) * GLA_DK ** -0.5, k_g.reshape(hs), v_g.reshape(hs), log_a.reshape(hs), s0)
    o_g = o_g * lax.rsqrt(jnp.mean(o_g * o_g, axis=-1, keepdims=True) + 1e-6) * gla_norm_g.astype(f32)
    o_g = o_g.reshape(B, L, GLA_WIDTH) * jax.nn.silu(r_g.astype(f32))
    kvs = (B, L, 2, NSA_KV_HEADS, NSA_HEAD_DIM)
    kv_c, kv_s, kv_w = kv_c.reshape(kvs), kv_s.reshape(kvs), kv_w.reshape(kvs)
    gates = jax.nn.sigmoid(g_n.astype(f32))
    o_n, win_new = nsa_fn(q_n.reshape(B, L, NSA_HEADS, NSA_HEAD_DIM), kv_c, kv_s, kv_w, gates,
                          cmp_pe, cmp_w1, cmp_b1, cmp_w2)
    mix = jnp.einsum('blm,md->bld', jnp.concatenate([o_g, o_n], axis=-1).astype(x.dtype), w_o)
    x = layer_norm(ALPHA * x + ga1 * mix, ln1_g, ln1_b)
    u2 = x * (1 + sc2) + sh2
    gate, up = jnp.split(jnp.einsum('bld,df->blf', u2, w_ffn_in), 2, axis=-1)
    ffn = jnp.einsum('blf,fd->bld', jax.nn.silu(gate) * up, w_ffn_out)
    x = layer_norm(ALPHA * x + ga2 * ffn, ln2_g, ln2_b)
    return x, s_new, kv_c, kv_s, win_new


def setup_inputs(seed: int = 0) -> dict:
    key = jax.random.key(seed)
    ks = jax.random.split(key, 32)
    f32 = jnp.float32

    def nrm(k, shape, s):
        return jax.random.normal(k, shape, f32) * s

    n_pages = PAST_LEN // PAGE_SIZE
    n_used = DEC_BATCH * n_pages
    n_pool = n_used + n_used // 4
    win_buf = min(WINDOW, PAST_LEN)
    kvt = (2, NSA_KV_HEADS, NSA_HEAD_DIM)
    perm = jax.random.permutation(ks[0], n_pool)
    page_table = perm[:n_used].reshape(DEC_BATCH, n_pages).astype(jnp.int32)
    return {
        'x_prompt': nrm(ks[1], (BATCH, SEQ, D_MODEL), 1.0),
        'x_sample': nrm(ks[2], (DEC_BATCH, DEC_SEQ, D_MODEL), 1.0),
        'state_gla': nrm(ks[3], (DEPTH, DEC_BATCH, GLA_HEADS, GLA_DK, GLA_DV), 1.0),
        'cache_cmp_kv': nrm(ks[4], (DEPTH, n_pool, PAGE_SIZE) + kvt, 1.0),
        'cache_slc_kv': nrm(ks[5], (DEPTH, n_pool, PAGE_SIZE) + kvt, 1.0),
        'cache_win_kv': nrm(ks[6], (DEPTH, DEC_BATCH, win_buf) + kvt, 1.0),
        'page_table': page_table,
        'c_prompt': nrm(ks[7], (BATCH, D_MODEL), 1.0),
        'c_sample': nrm(ks[8], (DEC_BATCH, D_MODEL), 1.0),
        'ln_in_g': 1.0 + nrm(ks[9], (D_MODEL,), 0.02),
        'ln_in_b': nrm(ks[10], (D_MODEL,), 0.02),
        'w_ada': nrm(ks[11], (DEPTH, D_MODEL, 6 * D_MODEL), D_MODEL ** -0.5),
        'b_ada': nrm(ks[12], (DEPTH, 6 * D_MODEL), 0.01),
        'w_in': nrm(ks[13], (DEPTH, D_MODEL, IN_WIDTH), D_MODEL ** -0.5),
        'gla_w_a2': nrm(ks[14], (DEPTH, GLA_RANK, GLA_HEADS * GLA_DK), GLA_RANK ** -0.5),
        'gla_b_a': nrm(ks[15], (DEPTH, GLA_HEADS * GLA_DK), 0.1),
        'gla_norm_g': 1.0 + nrm(ks[16], (DEPTH, GLA_DV), 0.02),
        'cmp_pe': nrm(ks[17], (DEPTH, CMP_BLK, 2, NSA_HEAD_DIM), 0.1),
        'cmp_w1': nrm(ks[18], (DEPTH, 2, CMP_BLK * NSA_HEAD_DIM, CMP_HIDDEN), (CMP_BLK * NSA_HEAD_DIM) ** -0.5),
        'cmp_b1': nrm(ks[19], (DEPTH, 2, CMP_HIDDEN), 0.01),
        'cmp_w2': nrm(ks[20], (DEPTH, 2, CMP_HIDDEN, NSA_HEAD_DIM), CMP_HIDDEN ** -0.5),
        'w_o': nrm(ks[21], (DEPTH, MIX_WIDTH, D_MODEL), BETA * MIX_WIDTH ** -0.5),
        'ln1_g': 1.0 + nrm(ks[22], (DEPTH, D_MODEL), 0.02),
        'ln1_b': nrm(ks[23], (DEPTH, D_MODEL), 0.02),
        'w_ffn_in': nrm(ks[24], (DEPTH, D_MODEL, 2 * D_FF), D_MODEL ** -0.5),
        'w_ffn_out': nrm(ks[25], (DEPTH, D_FF, D_MODEL), BETA * D_FF ** -0.5),
        'ln2_g': 1.0 + nrm(ks[26], (DEPTH, D_MODEL), 0.02),
        'ln2_b': nrm(ks[27], (DEPTH, D_MODEL), 0.02),
    }


def reference(x_prompt, x_sample, state_gla, cache_cmp_kv, cache_slc_kv, cache_win_kv, page_table, c_prompt,
              c_sample, ln_in_g, ln_in_b, w_ada, b_ada, w_in, gla_w_a2, gla_b_a, gla_norm_g, cmp_pe, cmp_w1,
              cmp_b1, cmp_w2, w_o, ln1_g, ln1_b, w_ffn_in, w_ffn_out, ln2_g, ln2_b):
    xp = layer_norm(x_prompt, ln_in_g, ln_in_b)
    xs = layer_norm(x_sample, ln_in_g, ln_in_b)
    db, n_pages = page_table.shape
    kvt = (2, NSA_KV_HEADS, NSA_HEAD_DIM)
    g_p, g_s, c_p, c_s, s_p, s_s, w_p, w_s = [], [], [], [], [], [], [], []
    for l in range(DEPTH):
        lw = (w_ada[l], b_ada[l], w_in[l], gla_w_a2[l], gla_b_a[l], gla_norm_g[l], cmp_pe[l], cmp_w1[l],
              cmp_b1[l], cmp_w2[l], w_o[l], ln1_g[l], ln1_b[l], w_ffn_in[l], w_ffn_out[l], ln2_g[l], ln2_b[l])
        s0 = jnp.zeros((xp.shape[0], GLA_HEADS, GLA_DK, GLA_DV), jnp.float32)
        xp, gp, cp, sp, wp = decoder_layer(xp, c_prompt, s0, nsa_prompt, *lw)
        cmp_past = cache_cmp_kv[l][page_table].reshape((db, n_pages * PAGE_SIZE) + kvt)
        slc_past = cache_slc_kv[l][page_table].reshape((db, n_pages * PAGE_SIZE) + kvt)
        fn = functools.partial(nsa_sample, cmp_past, slc_past, cache_win_kv[l])
        xs, gs, cs, ss, ws = decoder_layer(xs, c_sample, state_gla[l], fn, *lw)
        g_p.append(gp); g_s.append(gs); c_p.append(cp); c_s.append(cs)
        s_p.append(sp); s_s.append(ss); w_p.append(wp); w_s.append(ws)
    return (xp, xs, jnp.stack(g_p), jnp.stack(g_s), jnp.stack(c_p), jnp.stack(c_s), jnp.stack(s_p),
            jnp.stack(s_s), jnp.stack(w_p), jnp.stack(w_s))
```

```python
import functools
import math

import numpy as np
import jax
import jax.numpy as jnp
from jax import lax
from jax.experimental import pallas as pl
from jax.experimental.pallas import tpu as pltpu

F32 = jnp.float32
BF16 = jnp.bfloat16

D_MODEL = 1024
DEPTH = 1
PAGE_SIZE = 128
GLA_HEADS = 4
GLA_DV = D_MODEL // (2 * GLA_HEADS)
GLA_DK = GLA_DV // 2
GLA_RANK = 16
GLA_TAU = 16.0
GLA_CHUNK = 64
GLA_SUB = 16
GLA_WIDTH = GLA_HEADS * GLA_DV
GLA_KW = GLA_HEADS * GLA_DK
NSA_HEADS = 8
NSA_KV_HEADS = 2
NSA_GROUP = NSA_HEADS // NSA_KV_HEADS
NSA_HEAD_DIM = D_MODEL // (2 * NSA_HEADS)
NSA_WIDTH = NSA_HEADS * NSA_HEAD_DIM
CMP_BLK = 32
CMP_STRIDE = 16
CMP_HIDDEN = 2 * NSA_HEAD_DIM
SLC_BLK = 64
SLC_TOP_N = 16
WINDOW = 512
FORCE_BONUS = 1e4
NEG = -1e30
D_FF = -(-8 * D_MODEL // (3 * 256)) * 256
ALPHA = (2 * DEPTH) ** 0.25
KV_W = 2 * NSA_KV_HEADS * NSA_HEAD_DIM
N_GATES = 3 * NSA_HEADS
IN_SIZES = (GLA_KW, GLA_KW, GLA_WIDTH, GLA_RANK, GLA_WIDTH, NSA_WIDTH, KV_W, KV_W, KV_W, N_GATES)
IN_WIDTH = sum(IN_SIZES)
LN_EPS = 1e-5

LANES = 128
SUBLANES = 8
VMEM_LIMIT_BYTES = 56 * 1024 * 1024

AUX_W = LANES
IN_GROUPS = (2 * GLA_KW, GLA_WIDTH, GLA_WIDTH, NSA_WIDTH, KV_W, KV_W, KV_W, AUX_W)
IN_PERM_W = sum(IN_GROUPS)
GATE_OFF = GLA_RANK


def _cparams(sem):
    return pltpu.CompilerParams(dimension_semantics=sem, vmem_limit_bytes=VMEM_LIMIT_BYTES)


def _split3(a):
    hi = a.astype(BF16)
    r1 = a - hi.astype(F32)
    mid = r1.astype(BF16)
    lo = (r1 - mid.astype(F32)).astype(BF16)
    return hi, mid, lo


def _dot(a, b):
    return jnp.dot(a, b, preferred_element_type=F32)


def _dot_nt(a, b):
    return lax.dot_general(a, b, (((1,), (1,)), ((), ())), preferred_element_type=F32)


def _dot_tn(a, b):
    return lax.dot_general(a, b, (((0,), (0,)), ((), ())), preferred_element_type=F32)


def _layer_norm(x, g, b):
    mu = jnp.mean(x, axis=-1, keepdims=True)
    xc = x - mu
    var = jnp.mean(xc * xc, axis=-1, keepdims=True)
    return xc * lax.rsqrt(var + LN_EPS) * g + b


def _ada_kernel(c_ref, w_ref, b_ref, o_ref):
    c = c_ref[...]
    a = (c * jax.nn.sigmoid(c)).astype(BF16)
    o_ref[...] = _dot(a, w_ref[...].astype(BF16)) + b_ref[...]


def _ada(c, w_ada, b_ada):
    n, d = c.shape
    m = w_ada.shape[1]
    tn = D_MODEL
    return pl.pallas_call(
        _ada_kernel,
        grid=(m // tn,),
        in_specs=[pl.BlockSpec((n, d), lambda j: (0, 0)),
                  pl.BlockSpec((d, tn), lambda j: (0, j)),
                  pl.BlockSpec((1, tn), lambda j: (0, j))],
        out_specs=pl.BlockSpec((n, tn), lambda j: (0, j)),
        out_shape=jax.ShapeDtypeStruct((n, m), F32),
        compiler_params=_cparams(("parallel",)),
        name="ada_mod",
    )(c, w_ada, b_ada.reshape(1, m))


def _inproj_kernel(x_ref, sc_ref, sh_ref, g_ref, b_ref, w_ref, *o_refs):
    xn = _layer_norm(x_ref[...], g_ref[...], b_ref[...])
    u = (xn * (1.0 + sc_ref[...]) + sh_ref[...]).astype(BF16)
    lo = 0
    for o_ref, wdt in zip(o_refs, IN_GROUPS):
        o_ref[...] = _dot(u, w_ref[:, lo:lo + wdt]).astype(o_ref.dtype)
        lo += wdt


_IN_OUT_DTYPES = (BF16, BF16, BF16, BF16, F32, F32, F32, F32)


def _inproj(x, sc, sh, ln_g, ln_b, w_perm, tm, rows_per_mod):
    t, d = x.shape
    r = sc.shape[1]
    mod_spec = pl.BlockSpec((None, r, d), lambda i: (i // rows_per_mod, 0, 0))
    return pl.pallas_call(
        _inproj_kernel,
        grid=(t // tm,),
        in_specs=[pl.BlockSpec((tm, d), lambda i: (i, 0)), mod_spec, mod_spec,
                  pl.BlockSpec((1, d), lambda i: (0, 0)), pl.BlockSpec((1, d), lambda i: (0, 0)),
                  pl.BlockSpec((d, IN_PERM_W), lambda i: (0, 0))],
        out_specs=[pl.BlockSpec((tm, w), lambda i: (i, 0)) for w in IN_GROUPS],
        out_shape=[jax.ShapeDtypeStruct((t, w), dt) for w, dt in zip(IN_GROUPS, _IN_OUT_DTYPES)],
        compiler_params=_cparams(("parallel",)),
        name="ln_mod_inproj",
    )(x, sc, sh, ln_g.reshape(1, d), ln_b.reshape(1, d), w_perm)


def _permute_w_in(w_in):
    q_g, k_g, v_g, a_g, r_g, q_n, kv_c, kv_s, kv_w, g_n = jnp.split(w_in, np.cumsum(IN_SIZES)[:-1], axis=1)
    pad = jnp.zeros((w_in.shape[0], AUX_W - GLA_RANK - N_GATES), w_in.dtype)
    return jnp.concatenate([q_g, k_g, v_g, r_g, q_n, kv_c, kv_s, kv_w, a_g, g_n, pad], axis=1).astype(BF16)


FF_CHUNK = 256


def _out_ffn_kernel(x_ref, og_ref, on_ref, ga1_ref, sc2_ref, sh2_ref, ga2_ref, lng_ref, lnb_ref,
                    wo_ref, l1g_ref, l1b_ref, wfi_ref, wfo_ref, l2g_ref, l2b_ref, y_ref):
    x = _layer_norm(x_ref[...], lng_ref[...], lnb_ref[...])
    mix = _dot(og_ref[...], wo_ref[0:GLA_WIDTH, :]) + _dot(on_ref[...], wo_ref[GLA_WIDTH:, :])
    x1 = _layer_norm(ALPHA * x + ga1_ref[...] * mix, l1g_ref[...], l1b_ref[...])
    u2 = (x1 * (1.0 + sc2_ref[...]) + sh2_ref[...]).astype(BF16)
    ffn = jnp.zeros(x1.shape, F32)
    for c in range(D_FF // FF_CHUNK):
        lo = c * FF_CHUNK
        gate = _dot(u2, wfi_ref[:, lo:lo + FF_CHUNK])
        up = _dot(u2, wfi_ref[:, D_FF + lo:D_FF + lo + FF_CHUNK])
        f = (gate * jax.nn.sigmoid(gate) * up).astype(BF16)
        ffn = ffn + _dot(f, wfo_ref[lo:lo + FF_CHUNK, :])
    y_ref[...] = _layer_norm(ALPHA * x1 + ga2_ref[...] * ffn, l2g_ref[...], l2b_ref[...])


def _out_ffn(x, o_g, o_n, ga1, sc2, sh2, ga2, ln_in_g, ln_in_b, w_o, ln1_g, ln1_b, w_fi, w_fo, ln2_g, ln2_b,
             tm, rows_per_mod):
    t, d = x.shape
    r = ga1.shape[1]
    mod_spec = pl.BlockSpec((None, r, d), lambda i: (i // rows_per_mod, 0, 0))
    vec = lambda: pl.BlockSpec((1, d), lambda i: (0, 0))
    const = lambda shp: pl.BlockSpec(shp, lambda i: (0, 0), pipeline_mode=pl.Buffered(1))
    row = lambda a: a.reshape(1, d)
    return pl.pallas_call(
        _out_ffn_kernel,
        grid=(t // tm,),
        in_specs=[pl.BlockSpec((tm, d), lambda i: (i, 0)),
                  pl.BlockSpec((tm, GLA_WIDTH), lambda i: (i, 0)),
                  pl.BlockSpec((tm, NSA_WIDTH), lambda i: (i, 0)),
                  mod_spec, mod_spec, mod_spec, mod_spec, vec(), vec(),
                  const((d, d)), vec(), vec(), const((d, 2 * D_FF)), const((D_FF, d)), vec(), vec()],
        out_specs=pl.BlockSpec((tm, d), lambda i: (i, 0)),
        out_shape=jax.ShapeDtypeStruct((t, d), F32),
        compiler_params=_cparams(("parallel",)),
        name="outproj_ffn",
    )(x, o_g, o_n, ga1, sc2, sh2, ga2, row(ln_in_g), row(ln_in_b), w_o, row(ln1_g), row(ln1_b), w_fi, w_fo,
      row(ln2_g), row(ln2_b))


GLA_EXP_CLAMP = 80.0


def _gla_kernel(qk_ref, v_ref, r_ref, aux_ref, wa_ref, ba_ref, ng_ref, s0_ref, o_ref, sout_ref, s_scr,
                *, chunk, sub, tl, l_valid, l_pad):
    t = pl.program_id(1)
    c = chunk
    n_sub = c // sub
    hw = GLA_KW

    @pl.when(t == 0)
    def _():
        s_scr[...] = s0_ref[...]

    ri = lax.broadcasted_iota(jnp.int32, (c, c), 0)
    ci = lax.broadcasted_iota(jnp.int32, (c, c), 1)
    causal = ci <= ri
    tril = causal.astype(BF16)
    rowid = lax.broadcasted_iota(jnp.int32, (c, hw), 0)
    head_of_lane = lax.broadcasted_iota(jnp.int32, (c, hw), 1) // GLA_DK
    wa = wa_ref[...]
    wa_hi = wa.astype(BF16)
    wa_mid = (wa - wa_hi.astype(F32)).astype(BF16)
    ba = ba_ref[...]
    ng = ng_ref[...]

    def body(i, carry):
        r0 = pl.multiple_of(i * c, c)
        aux = aux_ref[pl.ds(r0, c), :]
        a_hi = aux.astype(BF16)
        a_mid = (aux - a_hi.astype(F32)).astype(BF16)
        z = _dot(a_hi, wa_hi) + _dot(a_mid, wa_hi) + _dot(a_hi, wa_mid) + ba
        g = jax.nn.log_sigmoid(z) / GLA_TAU
        if l_pad != l_valid:
            g = jnp.where(t * tl + r0 + rowid < l_valid, g, 0.0)
        g_hi, g_mid, g_lo = _split3(g)
        b = _dot(tril, g_hi) + _dot(tril, g_mid) + _dot(tril, g_lo)
        qk = qk_ref[pl.ds(r0, c), :]
        q = qk[:, :hw].astype(F32) * (GLA_DK ** -0.5)
        k = qk[:, hw:].astype(F32)
        v = v_ref[pl.ds(r0, c), :]
        b_last = b[c - 1:c, :]

        def heads_on_rows(x):
            return jnp.concatenate([jnp.where(head_of_lane == h, x, 0.0) for h in range(GLA_HEADS)], axis=0)

        s_old = s_scr[...]
        o_inter = _dot(heads_on_rows(q * jnp.exp(b)).astype(BF16), s_old.astype(BF16))

        q_parts, k_parts = [], []
        for s_i in range(n_sub):
            b_ref = b[s_i * sub - 1:s_i * sub, :] if s_i > 0 else jnp.zeros((1, hw), F32)
            in_rows = (rowid >= s_i * sub) & (rowid < (s_i + 1) * sub)
            qt = jnp.where(in_rows, q * jnp.exp(jnp.minimum(b - b_ref, 0.0)), 0.0)
            kt = jnp.where(rowid < (s_i + 1) * sub, k * jnp.exp(jnp.minimum(b_ref - b, GLA_EXP_CLAMP)), 0.0)
            q_parts.append(heads_on_rows(qt).astype(BF16))
            k_parts.append(kt.astype(BF16))
        q_cat = jnp.concatenate(q_parts, axis=1) if n_sub > 1 else q_parts[0]
        k_cat = jnp.concatenate(k_parts, axis=1) if n_sub > 1 else k_parts[0]
        att = _dot_nt(q_cat, k_cat)

        r_t = r_ref[pl.ds(r0, c), :].astype(F32)
        for h in range(GLA_HEADS):
            att_h = jnp.where(causal, att[h * c:(h + 1) * c, :], 0.0).astype(BF16)
            o_h = o_inter[h * c:(h + 1) * c, :] + _dot(att_h, v[:, h * GLA_DV:(h + 1) * GLA_DV])
            o_h = o_h * lax.rsqrt(jnp.mean(o_h * o_h, axis=-1, keepdims=True) + 1e-6) * ng
            r_h = r_t[:, h * GLA_DV:(h + 1) * GLA_DV]
            o_ref[pl.ds(r0, c), h * GLA_DV:(h + 1) * GLA_DV] = (o_h * (r_h * jax.nn.sigmoid(r_h))).astype(o_ref.dtype)

        kd = (k * jnp.exp(b_last - b)).astype(BF16)
        upd = _dot_tn(kd, v)
        decay = jnp.transpose(jnp.broadcast_to(jnp.exp(b_last), (LANES, hw)))
        upd_d = jnp.concatenate([upd[h * GLA_DK:(h + 1) * GLA_DK, h * GLA_DV:(h + 1) * GLA_DV]
                                 for h in range(GLA_HEADS)], axis=0)
        s_scr[...] = decay * s_old + upd_d
        return carry

    lax.fori_loop(0, tl // c, body, 0)

    @pl.when(t == pl.num_programs(1) - 1)
    def _():
        sout_ref[...] = s_scr[...]


def _gla(qk, v, r, aux, w_a2p, b_a, norm_g, s0, *, chunk, sub, tl, l_valid):
    bsz, l_pad, _ = qk.shape
    nt = l_pad // tl
    kern = functools.partial(_gla_kernel, chunk=chunk, sub=sub, tl=tl, l_valid=l_valid, l_pad=l_pad)
    tile = lambda w: pl.BlockSpec((None, tl, w), lambda b, t: (b, t, 0))
    full = lambda shp: pl.BlockSpec(shp, lambda b, t: (0, 0))
    st = pl.BlockSpec((None, GLA_KW, GLA_DV), lambda b, t: (b, 0, 0))
    return pl.pallas_call(
        kern,
        grid=(bsz, nt),
        in_specs=[tile(2 * GLA_KW), tile(GLA_WIDTH), tile(GLA_WIDTH), tile(AUX_W),
                  full((AUX_W, GLA_KW)), full((1, GLA_KW)), full((1, GLA_DV)), st],
        out_specs=[tile(GLA_WIDTH), st],
        out_shape=[jax.ShapeDtypeStruct((bsz, l_pad, GLA_WIDTH), BF16),
                   jax.ShapeDtypeStruct((bsz, GLA_KW, GLA_DV), F32)],
        scratch_shapes=[pltpu.VMEM((GLA_KW, GLA_DV), F32)],
        compiler_params=_cparams(("parallel", "arbitrary")),
        name="gla_scan",
    )(qk, v, r, aux, w_a2p, b_a, norm_g, s0)


SEG_W = CMP_STRIDE * KV_W
CMP_R = CMP_BLK // CMP_STRIDE
P_W = CMP_R * 2 * NSA_KV_HEADS * CMP_HIDDEN
H_W = P_W // CMP_R


def _cmp_weights(cmp_pe, cmp_w1, cmp_b1, cmp_w2):
    eye = jnp.eye(2, dtype=F32)
    w1r = cmp_w1.reshape(2, CMP_R, CMP_STRIDE, NSA_HEAD_DIM, CMP_HIDDEN)
    w1_full = jnp.einsum('cmsdh,cC,gG->scgdmCGh', w1r, eye, eye).reshape(SEG_W, P_W).astype(BF16)
    w2_bd = jnp.einsum('chd,cC,gG->cghCGd', cmp_w2, eye, eye).reshape(H_W, KV_W).astype(BF16)
    pe_flat = jnp.transpose(cmp_pe, (1, 0, 2)).reshape(2, CMP_BLK * NSA_HEAD_DIM)
    return w1_full, w2_bd, pe_flat


def _cmp_bias(pe_ref, w1_ref, b1_ref):
    parts = []
    for c in range(2):
        pe = jnp.broadcast_to(pe_ref[c:c + 1, :], (SUBLANES, pe_ref.shape[1]))
        w1c = w1_ref[c]
        pe_hi = pe.astype(BF16)
        pe_mid = (pe - pe_hi.astype(F32)).astype(BF16)
        w_hi = w1c.astype(BF16)
        w_mid = (w1c - w_hi.astype(F32)).astype(BF16)
        pb = _dot(pe_hi, w_hi) + _dot(pe_mid, w_hi) + _dot(pe_hi, w_mid)
        bc = pb[0:1, :] + b1_ref[c:c + 1, :]
        parts += [bc] * NSA_KV_HEADS
    return jnp.concatenate(parts, axis=1)


def _cmp_second_layer(p, bias, w2_ref):
    n = p.shape[0]
    h = p[:, :H_W] + pltpu.roll(p[:, H_W:], n - 1, 0) + bias
    return _dot(jax.nn.gelu(h).astype(BF16), w2_ref[...])


def _cmp_prompt_kernel(x_ref, w1f_ref, pe_ref, w1_ref, b1_ref, w2_ref, o_ref):
    p = _dot(x_ref[...].astype(BF16), w1f_ref[...])
    o_ref[...] = _cmp_second_layer(p, _cmp_bias(pe_ref, w1_ref, b1_ref), w2_ref)


def _cmp_prompt(x_seg, w1_full, pe_flat, cmp_w1, cmp_b1, w2_bd):
    bsz, n_seg, _ = x_seg.shape
    const2 = lambda a: pl.BlockSpec(a.shape, lambda b: (0,) * a.ndim)
    return pl.pallas_call(
        _cmp_prompt_kernel,
        grid=(bsz,),
        in_specs=[pl.BlockSpec((None, n_seg, SEG_W), lambda b: (b, 0, 0)),
                  const2(w1_full), const2(pe_flat), const2(cmp_w1), const2(cmp_b1), const2(w2_bd)],
        out_specs=pl.BlockSpec((None, n_seg, KV_W), lambda b: (b, 0, 0)),
        out_shape=jax.ShapeDtypeStruct((bsz, n_seg, KV_W), F32),
        compiler_params=_cparams(("parallel",)),
        name="nsa_compress_prompt",
    )(x_seg, w1_full, pe_flat, cmp_w1, cmp_b1, w2_bd)


QK_SCALE = NSA_HEAD_DIM ** -0.5


def _masked_softmax(s, mask):
    s = jnp.where(mask, s, NEG)
    e = jnp.where(mask, jnp.exp(s - jnp.max(s, axis=-1, keepdims=True)), 0.0)
    return e / jnp.maximum(jnp.sum(e, axis=-1, keepdims=True), 1e-30)


def _group_queries(q, g):
    hd = NSA_HEAD_DIM
    return jnp.concatenate([q[:, (NSA_GROUP * g + r) * hd:(NSA_GROUP * g + r + 1) * hd]
                            for r in range(NSA_GROUP)], axis=0)


def _topk_rows(score_t, n_pick):
    nb, nq = score_t.shape
    rowid = lax.broadcasted_iota(jnp.int32, (nb, nq), 0)
    taken = jnp.zeros((nb, nq), jnp.int32)
    picks = []
    for _ in range(n_pick):
        free = taken == 0
        cand = jnp.where(free, score_t, -jnp.inf)
        m = jnp.max(cand, axis=0, keepdims=True)
        hit = free & (cand == m)
        idx = jnp.min(jnp.where(hit, rowid, nb), axis=0, keepdims=True)
        taken = jnp.where(rowid == idx, 1, taken)
        picks.append(idx)
    return taken.astype(F32), picks


def _importance_t(psum, ov_t):
    hi, mid, lo = _split3(psum)
    return _dot_nt(ov_t, hi) + _dot_nt(ov_t, mid) + _dot_nt(ov_t, lo)


def _overlap_t(nb, nc_pad, nc):
    j = lax.broadcasted_iota(jnp.int32, (nb, nc_pad), 0) * SLC_BLK
    i = lax.broadcasted_iota(jnp.int32, (nb, nc_pad), 1) * CMP_STRIDE
    return ((i < j + SLC_BLK) & (i + CMP_BLK > j) & (i < nc * CMP_STRIDE)).astype(BF16)


def _select_scores_t(imp_t, tq_row):
    nb, nq = imp_t.shape
    j = lax.broadcasted_iota(jnp.int32, (nb, nq), 0)
    cur = tq_row // SLC_BLK
    forced = (j == 0) | (j == cur) | (j == cur - 1)
    return jnp.where(j * SLC_BLK <= tq_row, imp_t + FORCE_BONUS * forced.astype(F32), -jnp.inf)


NSA_TQ = 128
NSA_TK = 512


def _nsa_prompt_kernel(q_ref, aux_ref, cmp_ref, kvs_ref, kvw_ref, o_ref, m_scr, l_scr, acc_scr, *, seq, nc):
    tq_n, tk_n, hd, grp = NSA_TQ, NSA_TK, NSA_HEAD_DIM, NSA_GROUP
    m_rows = grp * tq_n
    q0 = pl.program_id(1) * tq_n
    q_all = q_ref[...]
    gates = jax.nn.sigmoid(aux_ref[...])
    n_cmp = cmp_ref.shape[0]
    nb = seq // SLC_BLK

    tq_rows = q0 + lax.broadcasted_iota(jnp.int32, (m_rows, 1), 0) % tq_n
    tq_lane = q0 + lax.broadcasted_iota(jnp.int32, (1, tq_n), 1)
    t_end = lax.broadcasted_iota(jnp.int32, (1, n_cmp), 1) * CMP_STRIDE + (CMP_BLK - 1)
    ov_t = _overlap_t(nb, n_cmp, nc)
    n_chunks = (q0 + tq_n + tk_n - 1) // tk_n
    w_len = WINDOW + tq_n
    w0 = pl.multiple_of(jnp.maximum(q0 - WINDOW, 0), tq_n)

    for g in range(NSA_KV_HEADS):
        qg = _group_queries(q_all, g)
        kl, vl = g * hd, (NSA_KV_HEADS + g) * hd
        cmp = cmp_ref[...]
        s_c = _dot_nt(qg, cmp[:, kl:kl + hd].astype(BF16)) * QK_SCALE
        p_c = _masked_softmax(s_c, t_end <= tq_rows)
        o_c = _dot(p_c.astype(BF16), cmp[:, vl:vl + hd].astype(BF16))
        psum = p_c[0:tq_n]
        for r in range(1, grp):
            psum = psum + p_c[r * tq_n:(r + 1) * tq_n]
        sel_t, _ = _topk_rows(_select_scores_t(_importance_t(psum, ov_t), tq_lane), min(SLC_TOP_N, nb))
        sel_t = sel_t.astype(BF16)
        m_scr[...] = jnp.full((m_rows, 1), NEG, F32)
        l_scr[...] = jnp.zeros((m_rows, 1), F32)
        acc_scr[...] = jnp.zeros((m_rows, hd), F32)

        def slc_chunk(ci, carry):
            k0 = pl.multiple_of(ci * tk_n, tk_n)
            ks = kvs_ref[pl.ds(k0, tk_n), kl:kl + hd].astype(BF16)
            vs = kvs_ref[pl.ds(k0, tk_n), vl:vl + hd].astype(BF16)
            s = _dot_nt(qg, ks) * QK_SCALE
            kpos = k0 + lax.broadcasted_iota(jnp.int32, (1, tk_n), 1)
            blk = lax.broadcasted_iota(jnp.int32, (nb, tk_n), 0)
            expand = (blk == kpos // SLC_BLK).astype(BF16)
            sel_k = _dot_tn(sel_t, expand)
            sel_k = jnp.concatenate([sel_k] * grp, axis=0)
            mask = (sel_k > 0.5) & (kpos <= tq_rows)
            s = jnp.where(mask, s, NEG)
            m_old = m_scr[...]
            m_new = jnp.maximum(m_old, jnp.max(s, axis=-1, keepdims=True))
            e = jnp.where(mask, jnp.exp(s - m_new), 0.0)
            corr = jnp.exp(m_old - m_new)
            l_scr[...] = corr * l_scr[...] + jnp.sum(e, axis=-1, keepdims=True)
            acc_scr[...] = corr * acc_scr[...] + _dot(e.astype(BF16), vs)
            m_scr[...] = m_new
            return carry

        lax.fori_loop(0, n_chunks, slc_chunk, 0)
        o_s = acc_scr[...] / jnp.maximum(l_scr[...], 1e-30)
        kw = kvw_ref[pl.ds(w0, w_len), kl:kl + hd].astype(BF16)
        vw = kvw_ref[pl.ds(w0, w_len), vl:vl + hd].astype(BF16)
        s_w = _dot_nt(qg, kw) * QK_SCALE
        tw = w0 + lax.broadcasted_iota(jnp.int32, (1, w_len), 1)
        p_w = _masked_softmax(s_w, (tw <= tq_rows) & (tw > tq_rows - WINDOW))
        o_w = _dot(p_w.astype(BF16), vw)
        for r in range(grp):
            h = grp * g + r
            rows = slice(r * tq_n, (r + 1) * tq_n)
            gc = GATE_OFF + 3 * h
            o = (gates[:, gc:gc + 1] * o_c[rows] + gates[:, gc + 1:gc + 2] * o_s[rows]
                 + gates[:, gc + 2:gc + 3] * o_w[rows])
            o_ref[:, h * hd:(h + 1) * hd] = o.astype(o_ref.dtype)


def _nsa_prompt(q, aux, cmpkv, kv_s, kv_w):
    bsz, seq, _ = q.shape
    n_seg = cmpkv.shape[1]
    kern = functools.partial(_nsa_prompt_kernel, seq=seq, nc=n_seg - CMP_R + 1)
    tile = lambda w: pl.BlockSpec((None, NSA_TQ, w), lambda b, t: (b, t, 0))
    whole = lambda n, w: pl.BlockSpec((None, n, w), lambda b, t: (b, 0, 0))
    m_rows = NSA_GROUP * NSA_TQ
    return pl.pallas_call(
        kern,
        grid=(bsz, seq // NSA_TQ),
        in_specs=[tile(NSA_WIDTH), tile(AUX_W), whole(n_seg, KV_W), whole(seq, KV_W), whole(seq, KV_W)],
        out_specs=tile(NSA_WIDTH),
        out_shape=jax.ShapeDtypeStruct((bsz, seq, NSA_WIDTH), BF16),
        scratch_shapes=[pltpu.VMEM((m_rows, 1), F32), pltpu.VMEM((m_rows, 1), F32),
                        pltpu.VMEM((m_rows, NSA_HEAD_DIM), F32)],
        compiler_params=_cparams(("parallel", "arbitrary")),
        name="nsa_attn_prompt",
    )(q, aux, cmpkv, kv_s, kv_w)


SEGS_PER_PAGE = PAGE_SIZE // CMP_STRIDE
CMP_PAGES_PER_STEP = 64
CMP_ROW_CHUNK = 256


def _cmp_sample_kernel(pt_ref, cache_ref, w1f_ref, pe_ref, w1_ref, b1_ref, w2_ref, o_ref, xbuf, p_scr, sem,
                       *, steps_per_batch):
    b = pl.program_id(0)
    h = pl.program_id(1)
    step = b * steps_per_batch + h
    n_steps = pl.num_programs(0) * steps_per_batch
    pps = CMP_PAGES_PER_STEP
    rows = pps * SEGS_PER_PAGE

    def page_copy(s, p, slot):
        page = pt_ref[s // steps_per_batch, (s % steps_per_batch) * pps + p]
        return pltpu.make_async_copy(cache_ref.at[page], xbuf.at[slot, pl.ds(p * SEGS_PER_PAGE, SEGS_PER_PAGE), :],
                                     sem.at[slot])

    def start_fetch(s, slot):
        def issue(p, c):
            page_copy(s, p, slot).start()
            return c
        lax.fori_loop(0, pps, issue, 0)

    @pl.when(step == 0)
    def _():
        start_fetch(step, 0)

    @pl.when(step + 1 < n_steps)
    def _():
        start_fetch(step + 1, (step + 1) % 2)

    slot = step % 2

    def wait_one(p, c):
        page_copy(step, p, slot).wait()
        return c
    lax.fori_loop(0, pps, wait_one, 0)

    for rc in range(rows // CMP_ROW_CHUNK):
        x = xbuf[slot, rc * CMP_ROW_CHUNK:(rc + 1) * CMP_ROW_CHUNK, :].astype(BF16)
        r0 = pl.multiple_of(h * rows + rc * CMP_ROW_CHUNK, CMP_ROW_CHUNK)
        p_scr[pl.ds(r0, CMP_ROW_CHUNK), :] = _dot(x, w1f_ref[...])

    @pl.when(h == steps_per_batch - 1)
    def _():
        o_ref[...] = _cmp_second_layer(p_scr[...], _cmp_bias(pe_ref, w1_ref, b1_ref), w2_ref)


def _cmp_sample(page_table, cache_seg, w1_full, pe_flat, cmp_w1, cmp_b1, w2_bd):
    bsz, n_pages = page_table.shape
    steps = n_pages // CMP_PAGES_PER_STEP
    n_seg = n_pages * SEGS_PER_PAGE
    rows = CMP_PAGES_PER_STEP * SEGS_PER_PAGE
    const = lambda a: pl.BlockSpec(a.shape, lambda b, h, pt: (0,) * a.ndim, pipeline_mode=pl.Buffered(1))
    grid_spec = pltpu.PrefetchScalarGridSpec(
        num_scalar_prefetch=1,
        grid=(bsz, steps),
        in_specs=[pl.BlockSpec(memory_space=pl.ANY), const(w1_full), const(pe_flat), const(cmp_w1),
                  const(cmp_b1), const(w2_bd)],
        out_specs=pl.BlockSpec((None, n_seg, KV_W), lambda b, h, pt: (b, 0, 0)),
        scratch_shapes=[pltpu.VMEM((2, rows, SEG_W), F32), pltpu.VMEM((n_seg, P_W), F32),
                        pltpu.SemaphoreType.DMA((2,))],
    )
    return pl.pallas_call(
        functools.partial(_cmp_sample_kernel, steps_per_batch=steps),
        grid_spec=grid_spec,
        out_shape=jax.ShapeDtypeStruct((bsz, n_seg, KV_W), F32),
        compiler_params=_cparams(("arbitrary", "arbitrary")),
        name="nsa_compress_sample",
    )(page_table, cache_seg, w1_full, pe_flat, cmp_w1, cmp_b1, w2_bd)


SQ_ROWS = SUBLANES


def _nsa_sample_select_kernel(q_ref, cmp_ref, oc_ref, idx_ref, *, past_len, nb, nb_pad, nc):
    tq_n, hd, grp = NSA_TQ, NSA_HEAD_DIM, NSA_GROUP
    m_rows = grp * tq_n
    q_all = q_ref[...]
    n_cmp = cmp_ref.shape[0]
    tq_rows = past_len + lax.broadcasted_iota(jnp.int32, (m_rows, 1), 0) % tq_n
    tq_lane = past_len + lax.broadcasted_iota(jnp.int32, (1, tq_n), 1)
    t_end = lax.broadcasted_iota(jnp.int32, (1, n_cmp), 1) * CMP_STRIDE + (CMP_BLK - 1)
    ov_t = _overlap_t(nb_pad, n_cmp, nc)
    cmp = cmp_ref[...]
    for g in range(NSA_KV_HEADS):
        qg = _group_queries(q_all, g)
        kl, vl = g * hd, (NSA_KV_HEADS + g) * hd
        s_c = _dot_nt(qg, cmp[:, kl:kl + hd].astype(BF16)) * QK_SCALE
        p_c = _masked_softmax(s_c, t_end <= tq_rows)
        o_c = _dot(p_c.astype(BF16), cmp[:, vl:vl + hd].astype(BF16))
        psum = p_c[0:tq_n]
        for r in range(1, grp):
            psum = psum + p_c[r * tq_n:(r + 1) * tq_n]
        score_t = _select_scores_t(_importance_t(psum, ov_t), tq_lane)
        rowid = lax.broadcasted_iota(jnp.int32, score_t.shape, 0)
        score_t = jnp.where(rowid < nb, score_t, -jnp.inf)
        _, picks = _topk_rows(score_t, min(SLC_TOP_N, nb))
        idx_ref[g] = jnp.concatenate(picks, axis=0)
        for r in range(grp):
            h = grp * g + r
            oc_ref[:, h * hd:(h + 1) * hd] = o_c[r * tq_n:r * tq_n + SQ_ROWS]


def _nsa_sample_select(q_pad, cmpkv, past_len, seq_new):
    bsz = q_pad.shape[0]
    n_seg = cmpkv.shape[1]
    nb = -(-(past_len + seq_new) // SLC_BLK)
    nb_pad = -(-nb // SUBLANES) * SUBLANES
    n_pick = min(SLC_TOP_N, nb)
    kern = functools.partial(_nsa_sample_select_kernel, past_len=past_len, nb=nb, nb_pad=nb_pad,
                             nc=n_seg - CMP_R + 1)
    return pl.pallas_call(
        kern,
        grid=(bsz,),
        in_specs=[pl.BlockSpec((None, NSA_TQ, NSA_WIDTH), lambda b: (b, 0, 0)),
                  pl.BlockSpec((None, n_seg, KV_W), lambda b: (b, 0, 0))],
        out_specs=[pl.BlockSpec((None, SQ_ROWS, NSA_WIDTH), lambda b: (b, 0, 0)),
                   pl.BlockSpec((None, NSA_KV_HEADS, n_pick, NSA_TQ), lambda b: (b, 0, 0, 0))],
        out_shape=[jax.ShapeDtypeStruct((bsz, SQ_ROWS, NSA_WIDTH), F32),
                   jax.ShapeDtypeStruct((bsz, NSA_KV_HEADS, n_pick, NSA_TQ), jnp.int32)],
        compiler_params=_cparams(("parallel",)),
        name="nsa_select_sample",
    )(q_pad, cmpkv)


def _nsa_sample_attend_kernel(pt_ref, idx_ref, q_ref, aux_ref, oc_ref, cache_ref, tail_ref, winp_ref, winn_ref,
                              o_ref, kbuf, sem, *, past_len, seq_new, n_pick):
    b = pl.program_id(0)
    nbatch = pl.num_programs(0)
    hd, grp = NSA_HEAD_DIM, NSA_GROUP
    n_items = seq_new * NSA_KV_HEADS * n_pick
    n_past_blocks = past_len // SLC_BLK
    blocks_per_page = PAGE_SIZE // SLC_BLK
    key_rows = n_pick * SLC_BLK

    def block_id(bb, item):
        return jnp.minimum(idx_ref[bb * n_items + item], n_past_blocks)

    def dst(slot, item):
        return kbuf.at[slot, item // n_pick, pl.ds((item % n_pick) * SLC_BLK, SLC_BLK), :]

    def start_fetch(bb, slot):
        def issue(item, c):
            blk = block_id(bb, item)

            @pl.when(blk < n_past_blocks)
            def _():
                page = pt_ref[bb, blk // blocks_per_page]
                src = cache_ref.at[page, pl.ds((blk % blocks_per_page) * SLC_BLK, SLC_BLK), :]
                pltpu.make_async_copy(src, dst(slot, item), sem.at[slot]).start()

            @pl.when(blk >= n_past_blocks)
            def _():
                pltpu.make_async_copy(tail_ref.at[bb], dst(slot, item), sem.at[slot]).start()
            return c
        lax.fori_loop(0, n_items, issue, 0)

    @pl.when(b == 0)
    def _():
        start_fetch(b, 0)

    @pl.when(b + 1 < nbatch)
    def _():
        start_fetch(b + 1, (b + 1) % 2)

    slot = b % 2

    def wait_one(item, c):
        pltpu.make_async_copy(tail_ref.at[b], dst(slot, item), sem.at[slot]).wait()
        return c
    lax.fori_loop(0, n_items, wait_one, 0)

    q_all = q_ref[...]
    gates = jax.nn.sigmoid(aux_ref[...])
    m_rows = grp * SQ_ROWS
    row_q = lax.broadcasted_iota(jnp.int32, (m_rows, 1), 0) % SQ_ROWS
    tq_rows = past_len + row_q
    key_slot = lax.broadcasted_iota(jnp.int32, (1, key_rows), 1) // SLC_BLK
    key_off = lax.broadcasted_iota(jnp.int32, (1, key_rows), 1) % SLC_BLK
    wb = winp_ref.shape[0]
    wn = winn_ref.shape[0]
    tw_p = past_len - wb + lax.broadcasted_iota(jnp.int32, (1, wb), 1)
    tw_n = past_len + lax.broadcasted_iota(jnp.int32, (1, wn), 1)

    for g in range(NSA_KV_HEADS):
        qg = _group_queries(q_all, g)
        kl, vl = g * hd, (NSA_KV_HEADS + g) * hd
        o_s = jnp.zeros((m_rows, hd), F32)
        for qi in range(seq_new):
            ks = kbuf[slot, qi * NSA_KV_HEADS + g, :, kl:kl + hd].astype(BF16)
            vs = kbuf[slot, qi * NSA_KV_HEADS + g, :, vl:vl + hd].astype(BF16)
            s = _dot_nt(qg, ks) * QK_SCALE
            base = jnp.zeros((1, key_rows), jnp.int32)
            for kk in range(n_pick):
                blk = block_id(b, (qi * NSA_KV_HEADS + g) * n_pick + kk)
                base = jnp.where(key_slot == kk, blk * SLC_BLK, base)
            p = _masked_softmax(s, base + key_off <= past_len + qi)
            o_s = jnp.where(row_q == qi, _dot(p.astype(BF16), vs), o_s)
        kp = winp_ref[:, kl:kl + hd].astype(BF16)
        vp = winp_ref[:, vl:vl + hd].astype(BF16)
        kn = winn_ref[:, kl:kl + hd].astype(BF16)
        vn = winn_ref[:, vl:vl + hd].astype(BF16)
        s_p = jnp.where((tw_p <= tq_rows) & (tw_p > tq_rows - WINDOW) & (tw_p >= 0),
                        _dot_nt(qg, kp) * QK_SCALE, NEG)
        s_n = jnp.where((tw_n <= tq_rows) & (tw_n > tq_rows - WINDOW) & (tw_n < past_len + seq_new),
                        _dot_nt(qg, kn) * QK_SCALE, NEG)
        m = jnp.maximum(jnp.max(s_p, axis=-1, keepdims=True), jnp.max(s_n, axis=-1, keepdims=True))
        e_p = jnp.where(s_p > 0.5 * NEG, jnp.exp(s_p - m), 0.0)
        e_n = jnp.where(s_n > 0.5 * NEG, jnp.exp(s_n - m), 0.0)
        den = jnp.maximum(jnp.sum(e_p, axis=-1, keepdims=True) + jnp.sum(e_n, axis=-1, keepdims=True), 1e-30)
        o_w = (_dot(e_p.astype(BF16), vp) + _dot(e_n.astype(BF16), vn)) / den
        for r in range(grp):
            h = grp * g + r
            rows = slice(r * SQ_ROWS, (r + 1) * SQ_ROWS)
            gc = GATE_OFF + 3 * h
            o = (gates[:, gc:gc + 1] * oc_ref[:, h * hd:(h + 1) * hd] + gates[:, gc + 1:gc + 2] * o_s[rows]
                 + gates[:, gc + 2:gc + 3] * o_w[rows])
            o_ref[:, h * hd:(h + 1) * hd] = o.astype(o_ref.dtype)


def _nsa_sample_attend(page_table, idx_flat, q_pad, aux_pad, o_c, cache_rows, tail, win_past, win_new,
                       past_len, seq_new, n_pick):
    bsz = q_pad.shape[0]
    wb, wn = win_past.shape[1], win_new.shape[1]
    kern = functools.partial(_nsa_sample_attend_kernel, past_len=past_len, seq_new=seq_new, n_pick=n_pick)
    per_b = lambda n, w: pl.BlockSpec((None, n, w), lambda b, pt, ix: (b, 0, 0))
    grid_spec = pltpu.PrefetchScalarGridSpec(
        num_scalar_prefetch=2,
        grid=(bsz,),
        in_specs=[per_b(SQ_ROWS, NSA_WIDTH), per_b(SQ_ROWS, AUX_W), per_b(SQ_ROWS, NSA_WIDTH),
                  pl.BlockSpec(memory_space=pl.ANY), pl.BlockSpec(memory_space=pl.ANY),
                  per_b(wb, KV_W), per_b(wn, KV_W)],
        out_specs=per_b(SQ_ROWS, NSA_WIDTH),
        scratch_shapes=[pltpu.VMEM((2, seq_new * NSA_KV_HEADS, n_pick * SLC_BLK, KV_W), F32),
                        pltpu.SemaphoreType.DMA((2,))],
    )
    return pl.pallas_call(
        kern,
        grid_spec=grid_spec,
        out_shape=jax.ShapeDtypeStruct((bsz, SQ_ROWS, NSA_WIDTH), BF16),
        compiler_params=_cparams(("arbitrary",)),
        name="nsa_attend_sample",
    )(page_table, idx_flat, q_pad, aux_pad, o_c, cache_rows, tail, win_past, win_new)


PROMPT_TM = 512
GLA_TL = 512
SAMPLE_GLA_ROWS = 16


def _pad_rows(x, n):
    return jnp.pad(x, ((0, 0), (0, n - x.shape[1]), (0, 0)))


def kernel(x_prompt, x_sample, state_gla, cache_cmp_kv, cache_slc_kv, cache_win_kv, page_table, c_prompt,
           c_sample, ln_in_g, ln_in_b, w_ada, b_ada, w_in, gla_w_a2, gla_b_a, gla_norm_g, cmp_pe, cmp_w1,
           cmp_b1, cmp_w2, w_o, ln1_g, ln1_b, w_ffn_in, w_ffn_out, ln2_g, ln2_b):
    assert w_in.shape[0] == DEPTH == 1
    l = 0
    bp, lp, d = x_prompt.shape
    bs, ls, _ = x_sample.shape
    n_pool = cache_cmp_kv.shape[1]
    n_pages = page_table.shape[1]
    past_len = n_pages * PAGE_SIZE
    wb = cache_win_kv.shape[2]
    assert ((past_len + ls) // CMP_STRIDE) * CMP_STRIDE <= past_len and past_len % SLC_BLK == 0
    assert ls <= SQ_ROWS and ls <= SLC_BLK and wb == WINDOW

    w_perm = _permute_w_in(w_in[l])
    w_o_b, w_fi_b, w_fo_b = w_o[l].astype(BF16), w_ffn_in[l].astype(BF16), w_ffn_out[l].astype(BF16)
    w_a2p = jnp.zeros((AUX_W, GLA_KW), F32).at[:GLA_RANK].set(gla_w_a2[l])
    b_a = gla_b_a[l].reshape(1, GLA_KW)
    norm_g = gla_norm_g[l].reshape(1, GLA_DV)
    w1_full, w2_bd, pe_flat = _cmp_weights(cmp_pe[l], cmp_w1[l], cmp_b1[l], cmp_w2[l])

    mod = _ada(jnp.concatenate([c_prompt, c_sample], axis=0), w_ada[l], b_ada[l])
    mods_p = [m[:bp, None, :] for m in jnp.split(mod, 6, axis=-1)]
    mods_s = [jnp.repeat(m[bp:], ls, axis=0)[None] for m in jnp.split(mod, 6, axis=-1)]

    def out_ffn(x2d, o_g, o_n, mods, tm, rpm):
        sh1, sc1, ga1, sh2, sc2, ga2 = mods
        return _out_ffn(x2d, o_g, o_n, ga1, sc2, sh2, ga2, ln_in_g, ln_in_b, w_o_b, ln1_g[l], ln1_b[l], w_fi_b,
                        w_fo_b, ln2_g[l], ln2_b[l], tm, rpm)

    xp2 = x_prompt.reshape(bp * lp, d)
    rpm = lp // PROMPT_TM
    qk, v, r, qn, kvc, kvs, kvw, aux = _inproj(xp2, mods_p[1], mods_p[0], ln_in_g, ln_in_b, w_perm, PROMPT_TM, rpm)
    b3 = lambda a: a.reshape(bp, lp, a.shape[-1])
    o_g, s_p = _gla(b3(qk), b3(v), b3(r), b3(aux), w_a2p, b_a, norm_g, jnp.zeros((bp, GLA_KW, GLA_DV), F32),
                    chunk=GLA_CHUNK, sub=GLA_SUB, tl=GLA_TL, l_valid=lp)
    cmpkv_p = _cmp_prompt(kvc.reshape(bp, lp // CMP_STRIDE, SEG_W), w1_full, pe_flat, cmp_w1[l], cmp_b1[l], w2_bd)
    o_n = _nsa_prompt(b3(qn), b3(aux), cmpkv_p, b3(kvs), b3(kvw))
    y_p = out_ffn(xp2, o_g.reshape(bp * lp, GLA_WIDTH), o_n.reshape(bp * lp, NSA_WIDTH), mods_p, PROMPT_TM, rpm)
    kvt = (2, NSA_KV_HEADS, NSA_HEAD_DIM)
    w_keep = min(WINDOW, lp)
    outs_p = (y_p.reshape(bp, lp, d), s_p.reshape(1, bp, GLA_HEADS, GLA_DK, GLA_DV),
              kvc.reshape((1, bp, lp) + kvt), kvs.reshape((1, bp, lp) + kvt),
              kvw.reshape((bp, lp) + kvt)[None, :, lp - w_keep:])

    ts = bs * ls
    xs2 = x_sample.reshape(ts, d)
    qk, v, r, qn, kvc, kvs, kvw, aux = _inproj(xs2, mods_s[1], mods_s[0], ln_in_g, ln_in_b, w_perm, ts, 1)
    s3 = lambda a: a.reshape(bs, ls, a.shape[-1])
    g16 = lambda a: _pad_rows(s3(a), SAMPLE_GLA_ROWS)
    o_g, s_s = _gla(g16(qk), g16(v), g16(r), g16(aux), w_a2p, b_a, norm_g, state_gla[l].reshape(bs, GLA_KW, GLA_DV),
                    chunk=SAMPLE_GLA_ROWS, sub=SAMPLE_GLA_ROWS, tl=SAMPLE_GLA_ROWS, l_valid=ls)
    o_g = o_g[:, :ls].reshape(ts, GLA_WIDTH)
    cmpkv_s = _cmp_sample(page_table, cache_cmp_kv[l].reshape(n_pool, SEGS_PER_PAGE, SEG_W), w1_full, pe_flat,
                          cmp_w1[l], cmp_b1[l], w2_bd)
    q_pad = _pad_rows(s3(qn), NSA_TQ)
    o_c, idx = _nsa_sample_select(q_pad, cmpkv_s, past_len, ls)
    n_pick = idx.shape[2]
    idx_flat = jnp.transpose(idx[..., :ls], (0, 3, 1, 2)).reshape(-1)
    tail = _pad_rows(s3(kvs), SLC_BLK)
    o_n = _nsa_sample_attend(page_table, idx_flat, q_pad, _pad_rows(s3(aux), SQ_ROWS), o_c,
                             cache_slc_kv[l].reshape(n_pool, PAGE_SIZE, KV_W), tail,
                             cache_win_kv[l].reshape(bs, wb, KV_W), _pad_rows(s3(kvw), LANES),
                             past_len, ls, n_pick)
    o_n = o_n[:, :ls].reshape(ts, NSA_WIDTH)
    y_s = out_ffn(xs2, o_g, o_n, mods_s, ts, 1)
    win_s = jnp.concatenate([cache_win_kv[l].reshape(bs, wb, KV_W), s3(kvw)], axis=1)[:, ls:]
    outs_s = (y_s.reshape(bs, ls, d), s_s.reshape(1, bs, GLA_HEADS, GLA_DK, GLA_DV),
              kvc.reshape((1, bs, ls) + kvt), kvs.reshape((1, bs, ls) + kvt), win_s.reshape((1, bs, wb) + kvt))

    return (outs_p[0], outs_s[0], outs_p[1], outs_s[1], outs_p[2], outs_s[2], outs_p[3], outs_s[3],
            outs_p[4], outs_s[4])
```

```python
import functools
import math

import numpy as np
import jax
import jax.numpy as jnp
from jax import lax
from jax.experimental import pallas as pl
from jax.experimental.pallas import tpu as pltpu

F32 = jnp.float32
BF16 = jnp.bfloat16

D_MODEL = 1024
DEPTH = 1
PAGE_SIZE = 128
GLA_HEADS = 4
GLA_DV = D_MODEL // (2 * GLA_HEADS)
GLA_DK = GLA_DV // 2
GLA_RANK = 16
GLA_TAU = 16.0
GLA_CHUNK = 64
GLA_SUB = 16
GLA_WIDTH = GLA_HEADS * GLA_DV
GLA_KW = GLA_HEADS * GLA_DK
NSA_HEADS = 8
NSA_KV_HEADS = 2
NSA_GROUP = NSA_HEADS // NSA_KV_HEADS
NSA_HEAD_DIM = D_MODEL // (2 * NSA_HEADS)
NSA_WIDTH = NSA_HEADS * NSA_HEAD_DIM
CMP_BLK = 32
CMP_STRIDE = 16
CMP_HIDDEN = 2 * NSA_HEAD_DIM
SLC_BLK = 64
SLC_TOP_N = 16
WINDOW = 512
FORCE_BONUS = 1e4
NEG = -1e30
D_FF = -(-8 * D_MODEL // (3 * 256)) * 256
ALPHA = (2 * DEPTH) ** 0.25
KV_W = 2 * NSA_KV_HEADS * NSA_HEAD_DIM
N_GATES = 3 * NSA_HEADS
IN_SIZES = (GLA_KW, GLA_KW, GLA_WIDTH, GLA_RANK, GLA_WIDTH, NSA_WIDTH, KV_W, KV_W, KV_W, N_GATES)
IN_WIDTH = sum(IN_SIZES)
LN_EPS = 1e-5

LANES = 128
SUBLANES = 8
VMEM_LIMIT_BYTES = 56 * 1024 * 1024

AUX_W = LANES
IN_GROUPS = (2 * GLA_KW, GLA_WIDTH, GLA_WIDTH, NSA_WIDTH, KV_W, KV_W, KV_W, AUX_W)
IN_PERM_W = sum(IN_GROUPS)
GATE_OFF = GLA_RANK


def _cparams(sem):
    return pltpu.CompilerParams(dimension_semantics=sem, vmem_limit_bytes=VMEM_LIMIT_BYTES)


def _split3(a):
    hi = a.astype(BF16)
    r1 = a - hi.astype(F32)
    mid = r1.astype(BF16)
    lo = (r1 - mid.astype(F32)).astype(BF16)
    return hi, mid, lo


def _dot(a, b):
    return jnp.dot(a, b, preferred_element_type=F32)


def _dot_nt(a, b):
    return lax.dot_general(a, b, (((1,), (1,)), ((), ())), preferred_element_type=F32)


def _dot_tn(a, b):
    return lax.dot_general(a, b, (((0,), (0,)), ((), ())), preferred_element_type=F32)


def _layer_norm(x, g, b):
    mu = jnp.mean(x, axis=-1, keepdims=True)
    xc = x - mu
    var = jnp.mean(xc * xc, axis=-1, keepdims=True)
    return xc * lax.rsqrt(var + LN_EPS) * g + b


def _ada_kernel(c_ref, w_ref, b_ref, o_ref):
    c = c_ref[...]
    a = (c * jax.nn.sigmoid(c)).astype(BF16)
    o_ref[...] = _dot(a, w_ref[...].astype(BF16)) + b_ref[...]


def _ada(c, w_ada, b_ada):
    n, d = c.shape
    m = w_ada.shape[1]
    tn = D_MODEL
    return pl.pallas_call(
        _ada_kernel,
        grid=(m // tn,),
        in_specs=[pl.BlockSpec((n, d), lambda j: (0, 0)),
                  pl.BlockSpec((d, tn), lambda j: (0, j)),
                  pl.BlockSpec((1, tn), lambda j: (0, j))],
        out_specs=pl.BlockSpec((n, tn), lambda j: (0, j)),
        out_shape=jax.ShapeDtypeStruct((n, m), F32),
        compiler_params=_cparams(("parallel",)),
        name="ada_mod",
    )(c, w_ada, b_ada.reshape(1, m))


N_KV_GROUPS = 3
KVC_GROUP = 4


def _inproj_kernel(x_ref, sc_ref, sh_ref, g_ref, b_ref, w_ref, wt_ref, *o_refs, emit_t):
    xn = _layer_norm(x_ref[...], g_ref[...], b_ref[...])
    u = (xn * (1.0 + sc_ref[...]) + sh_ref[...]).astype(BF16)
    lo = 0
    for j, (o_ref, wdt) in enumerate(zip(o_refs[:len(IN_GROUPS)], IN_GROUPS)):
        z = _dot(u, w_ref[:, lo:lo + wdt]).astype(o_ref.dtype)
        if j == KVC_GROUP:
            o_ref[0] = z[:, :wdt // 2]
            o_ref[1] = z[:, wdt // 2:]
        else:
            o_ref[...] = z
        lo += wdt
    if emit_t:
        for j, o_ref in enumerate(o_refs[len(IN_GROUPS):]):
            o_ref[...] = _dot_nt(wt_ref[j * KV_W:(j + 1) * KV_W, :], u)


_IN_OUT_DTYPES = (BF16, BF16, BF16, BF16, F32, F32, F32, F32)


def _inproj(x, sc, sh, ln_g, ln_b, w_perm, w_kv_t, tm, rows_per_mod, seq_per_batch=None):
    t, d = x.shape
    r = sc.shape[1]
    emit_t = seq_per_batch is not None
    mod_spec = pl.BlockSpec((None, r, d), lambda i: (i // rows_per_mod, 0, 0))
    out_specs = [pl.BlockSpec((tm, w), lambda i: (i, 0)) for w in IN_GROUPS]
    out_shape = [jax.ShapeDtypeStruct((t, w), dt) for w, dt in zip(IN_GROUPS, _IN_OUT_DTYPES)]
    out_specs[KVC_GROUP] = pl.BlockSpec((2, tm, KV_W // 2), lambda i: (0, i, 0))
    out_shape[KVC_GROUP] = jax.ShapeDtypeStruct((2, t, KV_W // 2), F32)
    if emit_t:
        tpb = seq_per_batch // tm
        out_specs += [pl.BlockSpec((None, KV_W, tm), lambda i: (i // tpb, 0, i % tpb))] * N_KV_GROUPS
        out_shape += [jax.ShapeDtypeStruct((t // seq_per_batch, KV_W, seq_per_batch), F32)] * N_KV_GROUPS
    return pl.pallas_call(
        functools.partial(_inproj_kernel, emit_t=emit_t),
        grid=(t // tm,),
        in_specs=[pl.BlockSpec((tm, d), lambda i: (i, 0)), mod_spec, mod_spec,
                  pl.BlockSpec((1, d), lambda i: (0, 0)), pl.BlockSpec((1, d), lambda i: (0, 0)),
                  pl.BlockSpec((d, IN_PERM_W), lambda i: (0, 0)),
                  pl.BlockSpec((N_KV_GROUPS * KV_W, d), lambda i: (0, 0))],
        out_specs=out_specs,
        out_shape=out_shape,
        compiler_params=_cparams(("parallel",)),
        name="ln_mod_inproj",
    )(x, sc, sh, ln_g.reshape(1, d), ln_b.reshape(1, d), w_perm, w_kv_t)


def _permute_w_in(w_in):
    q_g, k_g, v_g, a_g, r_g, q_n, kv_c, kv_s, kv_w, g_n = jnp.split(w_in, np.cumsum(IN_SIZES)[:-1], axis=1)
    pad = jnp.zeros((w_in.shape[0], AUX_W - GLA_RANK - N_GATES), w_in.dtype)
    w_perm = jnp.concatenate([q_g, k_g, v_g, r_g, q_n, kv_c, kv_s, kv_w, a_g, g_n, pad], axis=1).astype(BF16)
    w_kv_t = jnp.concatenate([kv_c, kv_s, kv_w], axis=1).T.astype(BF16)
    return w_perm, w_kv_t


FF_CHUNK = 256


def _out_ffn_kernel(x_ref, og_ref, on_ref, ga1_ref, sc2_ref, sh2_ref, ga2_ref, lng_ref, lnb_ref,
                    wo_ref, l1g_ref, l1b_ref, wfi_ref, wfo_ref, l2g_ref, l2b_ref, y_ref):
    x = _layer_norm(x_ref[...], lng_ref[...], lnb_ref[...])
    mix = _dot(og_ref[...], wo_ref[0:GLA_WIDTH, :]) + _dot(on_ref[...], wo_ref[GLA_WIDTH:, :])
    x1 = _layer_norm(ALPHA * x + ga1_ref[...] * mix, l1g_ref[...], l1b_ref[...])
    u2 = (x1 * (1.0 + sc2_ref[...]) + sh2_ref[...]).astype(BF16)
    ffn = jnp.zeros(x1.shape, F32)
    for c in range(D_FF // FF_CHUNK):
        lo = c * FF_CHUNK
        gate = _dot(u2, wfi_ref[:, lo:lo + FF_CHUNK])
        up = _dot(u2, wfi_ref[:, D_FF + lo:D_FF + lo + FF_CHUNK])
        f = (gate * jax.nn.sigmoid(gate) * up).astype(BF16)
        ffn = ffn + _dot(f, wfo_ref[lo:lo + FF_CHUNK, :])
    y_ref[...] = _layer_norm(ALPHA * x1 + ga2_ref[...] * ffn, l2g_ref[...], l2b_ref[...])


def _out_ffn(x, o_g, o_n, ga1, sc2, sh2, ga2, ln_in_g, ln_in_b, w_o, ln1_g, ln1_b, w_fi, w_fo, ln2_g, ln2_b,
             tm, rows_per_mod):
    t, d = x.shape
    r = ga1.shape[1]
    mod_spec = pl.BlockSpec((None, r, d), lambda i: (i // rows_per_mod, 0, 0))
    vec = lambda: pl.BlockSpec((1, d), lambda i: (0, 0))
    const = lambda shp: pl.BlockSpec(shp, lambda i: (0, 0), pipeline_mode=pl.Buffered(1))
    row = lambda a: a.reshape(1, d)
    return pl.pallas_call(
        _out_ffn_kernel,
        grid=(t // tm,),
        in_specs=[pl.BlockSpec((tm, d), lambda i: (i, 0)),
                  pl.BlockSpec((tm, GLA_WIDTH), lambda i: (i, 0)),
                  pl.BlockSpec((tm, NSA_WIDTH), lambda i: (i, 0)),
                  mod_spec, mod_spec, mod_spec, mod_spec, vec(), vec(),
                  const((d, d)), vec(), vec(), const((d, 2 * D_FF)), const((D_FF, d)), vec(), vec()],
        out_specs=pl.BlockSpec((tm, d), lambda i: (i, 0)),
        out_shape=jax.ShapeDtypeStruct((t, d), F32),
        compiler_params=_cparams(("parallel",)),
        name="outproj_ffn",
    )(x, o_g, o_n, ga1, sc2, sh2, ga2, row(ln_in_g), row(ln_in_b), w_o, row(ln1_g), row(ln1_b), w_fi, w_fo,
      row(ln2_g), row(ln2_b))


GLA_EXP_CLAMP = 80.0


def _gla_kernel(qk_ref, v_ref, r_ref, aux_ref, wa_ref, ba_ref, ng_ref, s0_ref, o_ref, sout_ref, s_scr,
                *, chunk, sub, tl, l_valid, l_pad):
    t = pl.program_id(1)
    c = chunk
    n_sub = c // sub
    hw = GLA_KW

    @pl.when(t == 0)
    def _():
        s_scr[...] = s0_ref[...]

    ri = lax.broadcasted_iota(jnp.int32, (c, c), 0)
    ci = lax.broadcasted_iota(jnp.int32, (c, c), 1)
    causal = ci <= ri
    tril = causal.astype(BF16)
    rowid = lax.broadcasted_iota(jnp.int32, (c, hw), 0)
    head_of_lane = lax.broadcasted_iota(jnp.int32, (c, hw), 1) // GLA_DK
    wa = wa_ref[...]
    wa_hi = wa.astype(BF16)
    wa_mid = (wa - wa_hi.astype(F32)).astype(BF16)
    ba = ba_ref[...]
    ng = ng_ref[...]

    def body(i, carry):
        r0 = pl.multiple_of(i * c, c)
        aux = aux_ref[pl.ds(r0, c), :]
        a_hi = aux.astype(BF16)
        a_mid = (aux - a_hi.astype(F32)).astype(BF16)
        z = _dot(a_hi, wa_hi) + _dot(a_mid, wa_hi) + _dot(a_hi, wa_mid) + ba
        g = jax.nn.log_sigmoid(z) / GLA_TAU
        if l_pad != l_valid:
            g = jnp.where(t * tl + r0 + rowid < l_valid, g, 0.0)
        g_hi, g_mid, g_lo = _split3(g)
        b = _dot(tril, g_hi) + _dot(tril, g_mid) + _dot(tril, g_lo)
        qk = qk_ref[pl.ds(r0, c), :]
        q = qk[:, :hw].astype(F32) * (GLA_DK ** -0.5)
        k = qk[:, hw:].astype(F32)
        v = v_ref[pl.ds(r0, c), :]
        b_last = b[c - 1:c, :]

        def heads_on_rows(x):
            return jnp.concatenate([jnp.where(head_of_lane == h, x, 0.0) for h in range(GLA_HEADS)], axis=0)

        s_old = s_scr[...]
        o_inter = _dot(heads_on_rows(q * jnp.exp(b)).astype(BF16), s_old.astype(BF16))

        q_parts, k_parts = [], []
        for s_i in range(n_sub):
            b_ref = b[s_i * sub - 1:s_i * sub, :] if s_i > 0 else jnp.zeros((1, hw), F32)
            in_rows = (rowid >= s_i * sub) & (rowid < (s_i + 1) * sub)
            qt = jnp.where(in_rows, q * jnp.exp(jnp.minimum(b - b_ref, 0.0)), 0.0)
            kt = jnp.where(rowid < (s_i + 1) * sub, k * jnp.exp(jnp.minimum(b_ref - b, GLA_EXP_CLAMP)), 0.0)
            q_parts.append(heads_on_rows(qt).astype(BF16))
            k_parts.append(kt.astype(BF16))
        q_cat = jnp.concatenate(q_parts, axis=1) if n_sub > 1 else q_parts[0]
        k_cat = jnp.concatenate(k_parts, axis=1) if n_sub > 1 else k_parts[0]
        att = _dot_nt(q_cat, k_cat)

        r_t = r_ref[pl.ds(r0, c), :].astype(F32)
        for h in range(GLA_HEADS):
            att_h = jnp.where(causal, att[h * c:(h + 1) * c, :], 0.0).astype(BF16)
            o_h = o_inter[h * c:(h + 1) * c, :] + _dot(att_h, v[:, h * GLA_DV:(h + 1) * GLA_DV])
            o_h = o_h * lax.rsqrt(jnp.mean(o_h * o_h, axis=-1, keepdims=True) + 1e-6) * ng
            r_h = r_t[:, h * GLA_DV:(h + 1) * GLA_DV]
            o_ref[pl.ds(r0, c), h * GLA_DV:(h + 1) * GLA_DV] = (o_h * (r_h * jax.nn.sigmoid(r_h))).astype(o_ref.dtype)

        kd = (k * jnp.exp(b_last - b)).astype(BF16)
        upd = _dot_tn(kd, v)
        decay = jnp.transpose(jnp.broadcast_to(jnp.exp(b_last), (LANES, hw)))
        upd_d = jnp.concatenate([upd[h * GLA_DK:(h + 1) * GLA_DK, h * GLA_DV:(h + 1) * GLA_DV]
                                 for h in range(GLA_HEADS)], axis=0)
        s_scr[...] = decay * s_old + upd_d
        return carry

    lax.fori_loop(0, tl // c, body, 0)

    @pl.when(t == pl.num_programs(1) - 1)
    def _():
        sout_ref[...] = s_scr[...]


def _gla(qk, v, r, aux, w_a2p, b_a, norm_g, s0, *, chunk, sub, tl, l_valid):
    bsz, l_pad, _ = qk.shape
    nt = l_pad // tl
    kern = functools.partial(_gla_kernel, chunk=chunk, sub=sub, tl=tl, l_valid=l_valid, l_pad=l_pad)
    tile = lambda w: pl.BlockSpec((None, tl, w), lambda b, t: (b, t, 0))
    full = lambda shp: pl.BlockSpec(shp, lambda b, t: (0, 0))
    st = pl.BlockSpec((None, GLA_KW, GLA_DV), lambda b, t: (b, 0, 0))
    return pl.pallas_call(
        kern,
        grid=(bsz, nt),
        in_specs=[tile(2 * GLA_KW), tile(GLA_WIDTH), tile(GLA_WIDTH), tile(AUX_W),
                  full((AUX_W, GLA_KW)), full((1, GLA_KW)), full((1, GLA_DV)), st],
        out_specs=[tile(GLA_WIDTH), st],
        out_shape=[jax.ShapeDtypeStruct((bsz, l_pad, GLA_WIDTH), BF16),
                   jax.ShapeDtypeStruct((bsz, GLA_KW, GLA_DV), F32)],
        scratch_shapes=[pltpu.VMEM((GLA_KW, GLA_DV), F32)],
        compiler_params=_cparams(("parallel", "arbitrary")),
        name="gla_scan",
    )(qk, v, r, aux, w_a2p, b_a, norm_g, s0)


SEG_W = CMP_STRIDE * KV_W
CMP_R = CMP_BLK // CMP_STRIDE
P_W = CMP_R * 2 * NSA_KV_HEADS * CMP_HIDDEN
H_W = P_W // CMP_R


def _cmp_weights(cmp_pe, cmp_w1, cmp_b1, cmp_w2):
    eye = jnp.eye(2, dtype=F32)
    w1r = cmp_w1.reshape(2, CMP_R, CMP_STRIDE, NSA_HEAD_DIM, CMP_HIDDEN)
    w1_full = jnp.einsum('cmsdh,cC,gG->scgdmCGh', w1r, eye, eye).reshape(SEG_W, P_W).astype(BF16)
    w2_bd = jnp.einsum('chd,cC,gG->cghCGd', cmp_w2, eye, eye).reshape(H_W, KV_W).astype(BF16)
    pe_flat = jnp.transpose(cmp_pe, (1, 0, 2)).reshape(2, CMP_BLK * NSA_HEAD_DIM)
    return w1_full, w2_bd, pe_flat


def _cmp_bias(pe_ref, w1_ref, b1_ref):
    parts = []
    for c in range(2):
        pe = jnp.broadcast_to(pe_ref[c:c + 1, :], (SUBLANES, pe_ref.shape[1]))
        w1c = w1_ref[c]
        pe_hi = pe.astype(BF16)
        pe_mid = (pe - pe_hi.astype(F32)).astype(BF16)
        w_hi = w1c.astype(BF16)
        w_mid = (w1c - w_hi.astype(F32)).astype(BF16)
        pb = _dot(pe_hi, w_hi) + _dot(pe_mid, w_hi) + _dot(pe_hi, w_mid)
        bc = pb[0:1, :] + b1_ref[c:c + 1, :]
        parts += [bc] * NSA_KV_HEADS
    return jnp.concatenate(parts, axis=1)


def _cmp_second_layer(p, bias, w2_ref):
    n = p.shape[0]
    h = p[:, :H_W] + pltpu.roll(p[:, H_W:], n - 1, 0) + bias
    return _dot(jax.nn.gelu(h).astype(BF16), w2_ref[...])


def _cmp_prompt_kernel(x_ref, w1f_ref, pe_ref, w1_ref, b1_ref, w2_ref, o_ref):
    p = _dot(x_ref[...].astype(BF16), w1f_ref[...])
    o_ref[...] = _cmp_second_layer(p, _cmp_bias(pe_ref, w1_ref, b1_ref), w2_ref)


def _cmp_prompt(x_seg, w1_full, pe_flat, cmp_w1, cmp_b1, w2_bd):
    bsz, n_seg, _ = x_seg.shape
    const2 = lambda a: pl.BlockSpec(a.shape, lambda b: (0,) * a.ndim)
    return pl.pallas_call(
        _cmp_prompt_kernel,
        grid=(bsz,),
        in_specs=[pl.BlockSpec((None, n_seg, SEG_W), lambda b: (b, 0, 0)),
                  const2(w1_full), const2(pe_flat), const2(cmp_w1), const2(cmp_b1), const2(w2_bd)],
        out_specs=pl.BlockSpec((None, n_seg, KV_W), lambda b: (b, 0, 0)),
        out_shape=jax.ShapeDtypeStruct((bsz, n_seg, KV_W), F32),
        compiler_params=_cparams(("parallel",)),
        name="nsa_compress_prompt",
    )(x_seg, w1_full, pe_flat, cmp_w1, cmp_b1, w2_bd)


QK_SCALE = NSA_HEAD_DIM ** -0.5


def _masked_softmax(s, mask):
    s = jnp.where(mask, s, NEG)
    e = jnp.where(mask, jnp.exp(s - jnp.max(s, axis=-1, keepdims=True)), 0.0)
    return e / jnp.maximum(jnp.sum(e, axis=-1, keepdims=True), 1e-30)


def _group_queries(q, g):
    hd = NSA_HEAD_DIM
    return jnp.concatenate([q[:, (NSA_GROUP * g + r) * hd:(NSA_GROUP * g + r + 1) * hd]
                            for r in range(NSA_GROUP)], axis=0)


def _topk_rows(score_t, n_pick):
    nb, nq = score_t.shape
    rowid = lax.broadcasted_iota(jnp.int32, (nb, nq), 0)
    taken = jnp.zeros((nb, nq), jnp.int32)
    picks = []
    for _ in range(n_pick):
        free = taken == 0
        cand = jnp.where(free, score_t, -jnp.inf)
        m = jnp.max(cand, axis=0, keepdims=True)
        hit = free & (cand == m)
        idx = jnp.min(jnp.where(hit, rowid, nb), axis=0, keepdims=True)
        taken = jnp.where(rowid == idx, 1, taken)
        picks.append(idx)
    return taken.astype(F32), picks


def _importance_t(psum, ov_t):
    hi, mid, lo = _split3(psum)
    return _dot_nt(ov_t, hi) + _dot_nt(ov_t, mid) + _dot_nt(ov_t, lo)


def _overlap_t(nb, nc_pad, nc):
    j = lax.broadcasted_iota(jnp.int32, (nb, nc_pad), 0) * SLC_BLK
    i = lax.broadcasted_iota(jnp.int32, (nb, nc_pad), 1) * CMP_STRIDE
    return ((i < j + SLC_BLK) & (i + CMP_BLK > j) & (i < nc * CMP_STRIDE)).astype(BF16)


def _select_scores_t(imp_t, tq_row):
    nb, nq = imp_t.shape
    j = lax.broadcasted_iota(jnp.int32, (nb, nq), 0)
    cur = tq_row // SLC_BLK
    forced = (j == 0) | (j == cur) | (j == cur - 1)
    return jnp.where(j * SLC_BLK <= tq_row, imp_t + FORCE_BONUS * forced.astype(F32), -jnp.inf)


NSA_TQ = 128
NSA_TK = 512


def _nsa_prompt_kernel(q_ref, aux_ref, cmp_ref, kvs_ref, kvw_ref, o_ref, m_scr, l_scr, acc_scr, *, seq, nc):
    tq_n, tk_n, hd, grp = NSA_TQ, NSA_TK, NSA_HEAD_DIM, NSA_GROUP
    m_rows = grp * tq_n
    q0 = pl.program_id(1) * tq_n
    q_all = q_ref[...]
    gates = jax.nn.sigmoid(aux_ref[...])
    n_cmp = cmp_ref.shape[0]
    nb = seq // SLC_BLK

    tq_rows = q0 + lax.broadcasted_iota(jnp.int32, (m_rows, 1), 0) % tq_n
    tq_lane = q0 + lax.broadcasted_iota(jnp.int32, (1, tq_n), 1)
    t_end = lax.broadcasted_iota(jnp.int32, (1, n_cmp), 1) * CMP_STRIDE + (CMP_BLK - 1)
    ov_t = _overlap_t(nb, n_cmp, nc)
    n_chunks = (q0 + tq_n + tk_n - 1) // tk_n
    w_len = WINDOW + tq_n
    w0 = pl.multiple_of(jnp.maximum(q0 - WINDOW, 0), tq_n)

    for g in range(NSA_KV_HEADS):
        qg = _group_queries(q_all, g)
        kl, vl = g * hd, (NSA_KV_HEADS + g) * hd
        cmp = cmp_ref[...]
        s_c = _dot_nt(qg, cmp[:, kl:kl + hd].astype(BF16)) * QK_SCALE
        p_c = _masked_softmax(s_c, t_end <= tq_rows)
        o_c = _dot(p_c.astype(BF16), cmp[:, vl:vl + hd].astype(BF16))
        psum = p_c[0:tq_n]
        for r in range(1, grp):
            psum = psum + p_c[r * tq_n:(r + 1) * tq_n]
        sel_t, _ = _topk_rows(_select_scores_t(_importance_t(psum, ov_t), tq_lane), min(SLC_TOP_N, nb))
        sel_t = sel_t.astype(BF16)
        m_scr[...] = jnp.full((m_rows, 1), NEG, F32)
        l_scr[...] = jnp.zeros((m_rows, 1), F32)
        acc_scr[...] = jnp.zeros((m_rows, hd), F32)

        def slc_chunk(ci, carry):
            k0 = pl.multiple_of(ci * tk_n, tk_n)
            ks = kvs_ref[pl.ds(k0, tk_n), kl:kl + hd].astype(BF16)
            vs = kvs_ref[pl.ds(k0, tk_n), vl:vl + hd].astype(BF16)
            s = _dot_nt(qg, ks) * QK_SCALE
            kpos = k0 + lax.broadcasted_iota(jnp.int32, (1, tk_n), 1)
            blk = lax.broadcasted_iota(jnp.int32, (nb, tk_n), 0)
            expand = (blk == kpos // SLC_BLK).astype(BF16)
            sel_k = _dot_tn(sel_t, expand)
            sel_k = jnp.concatenate([sel_k] * grp, axis=0)
            mask = (sel_k > 0.5) & (kpos <= tq_rows)
            s = jnp.where(mask, s, NEG)
            m_old = m_scr[...]
            m_new = jnp.maximum(m_old, jnp.max(s, axis=-1, keepdims=True))
            e = jnp.where(mask, jnp.exp(s - m_new), 0.0)
            corr = jnp.exp(m_old - m_new)
            l_scr[...] = corr * l_scr[...] + jnp.sum(e, axis=-1, keepdims=True)
            acc_scr[...] = corr * acc_scr[...] + _dot(e.astype(BF16), vs)
            m_scr[...] = m_new
            return carry

        lax.fori_loop(0, n_chunks, slc_chunk, 0)
        o_s = acc_scr[...] / jnp.maximum(l_scr[...], 1e-30)
        kw = kvw_ref[pl.ds(w0, w_len), kl:kl + hd].astype(BF16)
        vw = kvw_ref[pl.ds(w0, w_len), vl:vl + hd].astype(BF16)
        s_w = _dot_nt(qg, kw) * QK_SCALE
        tw = w0 + lax.broadcasted_iota(jnp.int32, (1, w_len), 1)
        p_w = _masked_softmax(s_w, (tw <= tq_rows) & (tw > tq_rows - WINDOW))
        o_w = _dot(p_w.astype(BF16), vw)
        for r in range(grp):
            h = grp * g + r
            rows = slice(r * tq_n, (r + 1) * tq_n)
            gc = GATE_OFF + 3 * h
            o = (gates[:, gc:gc + 1] * o_c[rows] + gates[:, gc + 1:gc + 2] * o_s[rows]
                 + gates[:, gc + 2:gc + 3] * o_w[rows])
            o_ref[:, h * hd:(h + 1) * hd] = o.astype(o_ref.dtype)


def _nsa_prompt(q, aux, cmpkv, kv_s, kv_w):
    bsz, seq, _ = q.shape
    n_seg = cmpkv.shape[1]
    kern = functools.partial(_nsa_prompt_kernel, seq=seq, nc=n_seg - CMP_R + 1)
    tile = lambda w: pl.BlockSpec((None, NSA_TQ, w), lambda b, t: (b, t, 0))
    whole = lambda n, w: pl.BlockSpec((None, n, w), lambda b, t: (b, 0, 0))
    m_rows = NSA_GROUP * NSA_TQ
    return pl.pallas_call(
        kern,
        grid=(bsz, seq // NSA_TQ),
        in_specs=[tile(NSA_WIDTH), tile(AUX_W), whole(n_seg, KV_W), whole(seq, KV_W), whole(seq, KV_W)],
        out_specs=tile(NSA_WIDTH),
        out_shape=jax.ShapeDtypeStruct((bsz, seq, NSA_WIDTH), BF16),
        scratch_shapes=[pltpu.VMEM((m_rows, 1), F32), pltpu.VMEM((m_rows, 1), F32),
                        pltpu.VMEM((m_rows, NSA_HEAD_DIM), F32)],
        compiler_params=_cparams(("parallel", "arbitrary")),
        name="nsa_attn_prompt",
    )(q, aux, cmpkv, kv_s, kv_w)


SEGS_PER_PAGE = PAGE_SIZE // CMP_STRIDE
CMP_PAGES_PER_STEP = 64
CMP_ROW_CHUNK = 256


def _cmp_sample_kernel(pt_ref, cache_ref, w1f_ref, pe_ref, w1_ref, b1_ref, w2_ref, o_ref, xbuf, p_scr, sem,
                       *, steps_per_batch):
    b = pl.program_id(0)
    h = pl.program_id(1)
    step = b * steps_per_batch + h
    n_steps = pl.num_programs(0) * steps_per_batch
    pps = CMP_PAGES_PER_STEP
    rows = pps * SEGS_PER_PAGE

    def page_copy(s, p, slot):
        page = pt_ref[s // steps_per_batch, (s % steps_per_batch) * pps + p]
        return pltpu.make_async_copy(cache_ref.at[page], xbuf.at[slot, pl.ds(p * SEGS_PER_PAGE, SEGS_PER_PAGE), :],
                                     sem.at[slot])

    def start_fetch(s, slot):
        def issue(p, c):
            page_copy(s, p, slot).start()
            return c
        lax.fori_loop(0, pps, issue, 0)

    @pl.when(step == 0)
    def _():
        start_fetch(step, 0)

    @pl.when(step + 1 < n_steps)
    def _():
        start_fetch(step + 1, (step + 1) % 2)

    slot = step % 2

    def wait_one(p, c):
        page_copy(step, p, slot).wait()
        return c
    lax.fori_loop(0, pps, wait_one, 0)

    for rc in range(rows // CMP_ROW_CHUNK):
        x = xbuf[slot, rc * CMP_ROW_CHUNK:(rc + 1) * CMP_ROW_CHUNK, :].astype(BF16)
        r0 = pl.multiple_of(h * rows + rc * CMP_ROW_CHUNK, CMP_ROW_CHUNK)
        p_scr[pl.ds(r0, CMP_ROW_CHUNK), :] = _dot(x, w1f_ref[...])

    @pl.when(h == steps_per_batch - 1)
    def _():
        o_ref[...] = _cmp_second_layer(p_scr[...], _cmp_bias(pe_ref, w1_ref, b1_ref), w2_ref)


def _cmp_sample(page_table, cache_seg, w1_full, pe_flat, cmp_w1, cmp_b1, w2_bd):
    bsz, n_pages = page_table.shape
    steps = n_pages // CMP_PAGES_PER_STEP
    n_seg = n_pages * SEGS_PER_PAGE
    rows = CMP_PAGES_PER_STEP * SEGS_PER_PAGE
    const = lambda a: pl.BlockSpec(a.shape, lambda b, h, pt: (0,) * a.ndim, pipeline_mode=pl.Buffered(1))
    grid_spec = pltpu.PrefetchScalarGridSpec(
        num_scalar_prefetch=1,
        grid=(bsz, steps),
        in_specs=[pl.BlockSpec(memory_space=pl.ANY), const(w1_full), const(pe_flat), const(cmp_w1),
                  const(cmp_b1), const(w2_bd)],
        out_specs=pl.BlockSpec((None, n_seg, KV_W), lambda b, h, pt: (b, 0, 0)),
        scratch_shapes=[pltpu.VMEM((2, rows, SEG_W), F32), pltpu.VMEM((n_seg, P_W), F32),
                        pltpu.SemaphoreType.DMA((2,))],
    )
    return pl.pallas_call(
        functools.partial(_cmp_sample_kernel, steps_per_batch=steps),
        grid_spec=grid_spec,
        out_shape=jax.ShapeDtypeStruct((bsz, n_seg, KV_W), F32),
        compiler_params=_cparams(("arbitrary", "arbitrary")),
        name="nsa_compress_sample",
    )(page_table, cache_seg, w1_full, pe_flat, cmp_w1, cmp_b1, w2_bd)


SQ_ROWS = SUBLANES


def _nsa_sample_select_kernel(q_ref, cmp_ref, oc_ref, idx_ref, *, past_len, nb, nb_pad, nc):
    tq_n, hd, grp = NSA_TQ, NSA_HEAD_DIM, NSA_GROUP
    m_rows = grp * tq_n
    q_all = q_ref[...]
    n_cmp = cmp_ref.shape[0]
    tq_rows = past_len + lax.broadcasted_iota(jnp.int32, (m_rows, 1), 0) % tq_n
    tq_lane = past_len + lax.broadcasted_iota(jnp.int32, (1, tq_n), 1)
    t_end = lax.broadcasted_iota(jnp.int32, (1, n_cmp), 1) * CMP_STRIDE + (CMP_BLK - 1)
    ov_t = _overlap_t(nb_pad, n_cmp, nc)
    cmp = cmp_ref[...]
    for g in range(NSA_KV_HEADS):
        qg = _group_queries(q_all, g)
        kl, vl = g * hd, (NSA_KV_HEADS + g) * hd
        s_c = _dot_nt(qg, cmp[:, kl:kl + hd].astype(BF16)) * QK_SCALE
        p_c = _masked_softmax(s_c, t_end <= tq_rows)
        o_c = _dot(p_c.astype(BF16), cmp[:, vl:vl + hd].astype(BF16))
        psum = p_c[0:tq_n]
        for r in range(1, grp):
            psum = psum + p_c[r * tq_n:(r + 1) * tq_n]
        score_t = _select_scores_t(_importance_t(psum, ov_t), tq_lane)
        rowid = lax.broadcasted_iota(jnp.int32, score_t.shape, 0)
        score_t = jnp.where(rowid < nb, score_t, -jnp.inf)
        _, picks = _topk_rows(score_t, min(SLC_TOP_N, nb))
        idx_ref[g] = jnp.concatenate(picks, axis=0)
        for r in range(grp):
            h = grp * g + r
            oc_ref[:, h * hd:(h + 1) * hd] = o_c[r * tq_n:r * tq_n + SQ_ROWS]


def _nsa_sample_select(q_pad, cmpkv, past_len, seq_new):
    bsz = q_pad.shape[0]
    n_seg = cmpkv.shape[1]
    nb = -(-(past_len + seq_new) // SLC_BLK)
    nb_pad = -(-nb // SUBLANES) * SUBLANES
    n_pick = min(SLC_TOP_N, nb)
    kern = functools.partial(_nsa_sample_select_kernel, past_len=past_len, nb=nb, nb_pad=nb_pad,
                             nc=n_seg - CMP_R + 1)
    return pl.pallas_call(
        kern,
        grid=(bsz,),
        in_specs=[pl.BlockSpec((None, NSA_TQ, NSA_WIDTH), lambda b: (b, 0, 0)),
                  pl.BlockSpec((None, n_seg, KV_W), lambda b: (b, 0, 0))],
        out_specs=[pl.BlockSpec((None, SQ_ROWS, NSA_WIDTH), lambda b: (b, 0, 0)),
                   pl.BlockSpec((None, NSA_KV_HEADS, n_pick, NSA_TQ), lambda b: (b, 0, 0, 0))],
        out_shape=[jax.ShapeDtypeStruct((bsz, SQ_ROWS, NSA_WIDTH), F32),
                   jax.ShapeDtypeStruct((bsz, NSA_KV_HEADS, n_pick, NSA_TQ), jnp.int32)],
        compiler_params=_cparams(("parallel",)),
        name="nsa_select_sample",
    )(q_pad, cmpkv)


def _nsa_sample_attend_kernel(pt_ref, idx_ref, q_ref, aux_ref, oc_ref, cache_ref, tail_ref, winp_ref, winn_ref,
                              o_ref, kbuf, sem, *, past_len, seq_new, n_pick):
    b = pl.program_id(0)
    nbatch = pl.num_programs(0)
    hd, grp = NSA_HEAD_DIM, NSA_GROUP
    n_items = seq_new * NSA_KV_HEADS * n_pick
    n_past_blocks = past_len // SLC_BLK
    blocks_per_page = PAGE_SIZE // SLC_BLK
    key_rows = n_pick * SLC_BLK

    def block_id(bb, item):
        return jnp.minimum(idx_ref[bb * n_items + item], n_past_blocks)

    def dst(slot, item):
        return kbuf.at[slot, item // n_pick, pl.ds((item % n_pick) * SLC_BLK, SLC_BLK), :]

    def start_fetch(bb, slot):
        def issue(item, c):
            blk = block_id(bb, item)

            @pl.when(blk < n_past_blocks)
            def _():
                page = pt_ref[bb, blk // blocks_per_page]
                src = cache_ref.at[page, pl.ds((blk % blocks_per_page) * SLC_BLK, SLC_BLK), :]
                pltpu.make_async_copy(src, dst(slot, item), sem.at[slot]).start()

            @pl.when(blk >= n_past_blocks)
            def _():
                pltpu.make_async_copy(tail_ref.at[bb], dst(slot, item), sem.at[slot]).start()
            return c
        lax.fori_loop(0, n_items, issue, 0)

    @pl.when(b == 0)
    def _():
        start_fetch(b, 0)

    @pl.when(b + 1 < nbatch)
    def _():
        start_fetch(b + 1, (b + 1) % 2)

    slot = b % 2

    def wait_one(item, c):
        pltpu.make_async_copy(tail_ref.at[b], dst(slot, item), sem.at[slot]).wait()
        return c
    lax.fori_loop(0, n_items, wait_one, 0)

    q_all = q_ref[...]
    gates = jax.nn.sigmoid(aux_ref[...])
    m_rows = grp * SQ_ROWS
    row_q = lax.broadcasted_iota(jnp.int32, (m_rows, 1), 0) % SQ_ROWS
    tq_rows = past_len + row_q
    key_slot = lax.broadcasted_iota(jnp.int32, (1, key_rows), 1) // SLC_BLK
    key_off = lax.broadcasted_iota(jnp.int32, (1, key_rows), 1) % SLC_BLK
    wb = winp_ref.shape[0]
    wn = winn_ref.shape[0]
    tw_p = past_len - wb + lax.broadcasted_iota(jnp.int32, (1, wb), 1)
    tw_n = past_len + lax.broadcasted_iota(jnp.int32, (1, wn), 1)

    for g in range(NSA_KV_HEADS):
        qg = _group_queries(q_all, g)
        kl, vl = g * hd, (NSA_KV_HEADS + g) * hd
        o_s = jnp.zeros((m_rows, hd), F32)
        for qi in range(seq_new):
            ks = kbuf[slot, qi * NSA_KV_HEADS + g, :, kl:kl + hd].astype(BF16)
            vs = kbuf[slot, qi * NSA_KV_HEADS + g, :, vl:vl + hd].astype(BF16)
            s = _dot_nt(qg, ks) * QK_SCALE
            base = jnp.zeros((1, key_rows), jnp.int32)
            for kk in range(n_pick):
                blk = block_id(b, (qi * NSA_KV_HEADS + g) * n_pick + kk)
                base = jnp.where(key_slot == kk, blk * SLC_BLK, base)
            p = _masked_softmax(s, base + key_off <= past_len + qi)
            o_s = jnp.where(row_q == qi, _dot(p.astype(BF16), vs), o_s)
        kp = winp_ref[:, kl:kl + hd].astype(BF16)
        vp = winp_ref[:, vl:vl + hd].astype(BF16)
        kn = winn_ref[:, kl:kl + hd].astype(BF16)
        vn = winn_ref[:, vl:vl + hd].astype(BF16)
        s_p = jnp.where((tw_p <= tq_rows) & (tw_p > tq_rows - WINDOW) & (tw_p >= 0),
                        _dot_nt(qg, kp) * QK_SCALE, NEG)
        s_n = jnp.where((tw_n <= tq_rows) & (tw_n > tq_rows - WINDOW) & (tw_n < past_len + seq_new),
                        _dot_nt(qg, kn) * QK_SCALE, NEG)
        m = jnp.maximum(jnp.max(s_p, axis=-1, keepdims=True), jnp.max(s_n, axis=-1, keepdims=True))
        e_p = jnp.where(s_p > 0.5 * NEG, jnp.exp(s_p - m), 0.0)
        e_n = jnp.where(s_n > 0.5 * NEG, jnp.exp(s_n - m), 0.0)
        den = jnp.maximum(jnp.sum(e_p, axis=-1, keepdims=True) + jnp.sum(e_n, axis=-1, keepdims=True), 1e-30)
        o_w = (_dot(e_p.astype(BF16), vp) + _dot(e_n.astype(BF16), vn)) / den
        for r in range(grp):
            h = grp * g + r
            rows = slice(r * SQ_ROWS, (r + 1) * SQ_ROWS)
            gc = GATE_OFF + 3 * h
            o = (gates[:, gc:gc + 1] * oc_ref[:, h * hd:(h + 1) * hd] + gates[:, gc + 1:gc + 2] * o_s[rows]
                 + gates[:, gc + 2:gc + 3] * o_w[rows])
            o_ref[:, h * hd:(h + 1) * hd] = o.astype(o_ref.dtype)


def _nsa_sample_attend(page_table, idx_flat, q_pad, aux_pad, o_c, cache_rows, tail, win_past, win_new,
                       past_len, seq_new, n_pick):
    bsz = q_pad.shape[0]
    wb, wn = win_past.shape[1], win_new.shape[1]
    kern = functools.partial(_nsa_sample_attend_kernel, past_len=past_len, seq_new=seq_new, n_pick=n_pick)
    per_b = lambda n, w: pl.BlockSpec((None, n, w), lambda b, pt, ix: (b, 0, 0))
    grid_spec = pltpu.PrefetchScalarGridSpec(
        num_scalar_prefetch=2,
        grid=(bsz,),
        in_specs=[per_b(SQ_ROWS, NSA_WIDTH), per_b(SQ_ROWS, AUX_W), per_b(SQ_ROWS, NSA_WIDTH),
                  pl.BlockSpec(memory_space=pl.ANY), pl.BlockSpec(memory_space=pl.ANY),
                  per_b(wb, KV_W), per_b(wn, KV_W)],
        out_specs=per_b(SQ_ROWS, NSA_WIDTH),
        scratch_shapes=[pltpu.VMEM((2, seq_new * NSA_KV_HEADS, n_pick * SLC_BLK, KV_W), F32),
                        pltpu.SemaphoreType.DMA((2,))],
    )
    return pl.pallas_call(
        kern,
        grid_spec=grid_spec,
        out_shape=jax.ShapeDtypeStruct((bsz, SQ_ROWS, NSA_WIDTH), BF16),
        compiler_params=_cparams(("arbitrary",)),
        name="nsa_attend_sample",
    )(page_table, idx_flat, q_pad, aux_pad, o_c, cache_rows, tail, win_past, win_new)


C_W = NSA_KV_HEADS * NSA_HEAD_DIM
HC_W = NSA_KV_HEADS * CMP_HIDDEN
PC_W = CMP_R * HC_W


def _cmp_weights_fm(cmp_pe, cmp_w1, cmp_w2):
    eye = jnp.eye(NSA_KV_HEADS, dtype=F32)
    w1r = cmp_w1.reshape(2, CMP_R, CMP_STRIDE, NSA_HEAD_DIM, CMP_HIDDEN)
    w1c = jnp.einsum('cmsdh,gG->csgdmGh', w1r, eye).reshape(2, CMP_STRIDE * C_W, PC_W).astype(BF16)
    w2c_t = jnp.einsum('chd,gG->cGdgh', cmp_w2, eye).reshape(2, C_W, HC_W).astype(BF16)
    pe_flat = jnp.transpose(cmp_pe, (1, 0, 2)).reshape(2, CMP_BLK * NSA_HEAD_DIM)
    return w1c, w2c_t, pe_flat


def _cmp_bias_c(pe_ref, w1_ref, b1_ref, c):
    pe = jnp.broadcast_to(pe_ref[c:c + 1, :], (SUBLANES, pe_ref.shape[1]))
    w1c = w1_ref[c]
    pe_hi = pe.astype(BF16)
    pe_mid = (pe - pe_hi.astype(F32)).astype(BF16)
    w_hi = w1c.astype(BF16)
    w_mid = (w1c - w_hi.astype(F32)).astype(BF16)
    pb = _dot(pe_hi, w_hi) + _dot(pe_mid, w_hi) + _dot(pe_hi, w_mid)
    bc = pb[0:1, :] + b1_ref[c:c + 1, :]
    return jnp.concatenate([bc] * NSA_KV_HEADS, axis=1)


def _cmp_first_layer(x_ref, c, row0, n_seg, w1c_ref):
    acc = None
    for s in range(CMP_STRIDE):
        xs = x_ref[c, pl.ds(row0 + s, n_seg, stride=CMP_STRIDE), :].astype(BF16)
        d = _dot(xs, w1c_ref[c, s * C_W:(s + 1) * C_W, :])
        acc = d if acc is None else acc + d
    return acc


def _cmp_second_layer_fm(p, bias, w2t):
    n = p.shape[0]
    h = p[:, :HC_W] + pltpu.roll(p[:, HC_W:], n - 1, 0) + bias
    return _dot_nt(w2t, jax.nn.gelu(h).astype(BF16))


def _cmp_prompt_fm_kernel(x_ref, w1c_ref, pe_ref, w1_ref, b1_ref, w2t_ref, o_ref):
    n_seg = o_ref.shape[1]
    for c in range(2):
        p = _cmp_first_layer(x_ref, c, 0, n_seg, w1c_ref)
        o_ref[c * C_W:(c + 1) * C_W, :] = _cmp_second_layer_fm(p, _cmp_bias_c(pe_ref, w1_ref, b1_ref, c), w2t_ref[c])


def _cmp_prompt_fm(x_tok, seq, w1c, pe_flat, cmp_w1, cmp_b1, w2c_t):
    bsz = x_tok.shape[1] // seq
    n_seg = seq // CMP_STRIDE
    const = lambda a: pl.BlockSpec(a.shape, lambda b: (0,) * a.ndim)
    return pl.pallas_call(
        _cmp_prompt_fm_kernel,
        grid=(bsz,),
        in_specs=[pl.BlockSpec((2, seq, C_W), lambda b: (0, b, 0)),
                  const(w1c), const(pe_flat), const(cmp_w1), const(cmp_b1), const(w2c_t)],
        out_specs=pl.BlockSpec((None, KV_W, n_seg), lambda b: (b, 0, 0)),
        out_shape=jax.ShapeDtypeStruct((bsz, KV_W, n_seg), F32),
        compiler_params=_cparams(("parallel",)),
        name="nsa_compress_prompt",
    )(x_tok, w1c, pe_flat, cmp_w1, cmp_b1, w2c_t)


def _scaled_group_queries(q, g):
    return (_group_queries(q, g).astype(F32) * QK_SCALE).astype(BF16)


def _cmp_branch_fm(qg, cmp_ref, g, valid):
    hd = NSA_HEAD_DIM
    kl, vl = g * hd, (NSA_KV_HEADS + g) * hd
    s_c = _dot(qg, cmp_ref[kl:kl + hd, :].astype(BF16))
    p_c = _masked_softmax(s_c, valid)
    return _dot_nt(p_c.astype(BF16), cmp_ref[vl:vl + hd, :].astype(BF16)), p_c


def _sum_heads(p, rows):
    out = p[0:rows]
    for r in range(1, NSA_GROUP):
        out = out + p[r * rows:(r + 1) * rows]
    return out


def _nsa_prompt_fm_kernel(q_ref, aux_ref, cmp_ref, kvs_ref, kvw_ref, o_ref, m_scr, l_scr, acc_scr, *, seq, nc):
    tq_n, tk_n, hd, grp = NSA_TQ, NSA_TK, NSA_HEAD_DIM, NSA_GROUP
    m_rows = grp * tq_n
    q0 = pl.program_id(1) * tq_n
    q_all = q_ref[...]
    gates = jax.nn.sigmoid(aux_ref[...])
    n_cmp = cmp_ref.shape[1]
    nb = seq // SLC_BLK

    tq_col = q0 + lax.broadcasted_iota(jnp.int32, (tq_n, 1), 0)
    tq_rows = jnp.concatenate([tq_col] * grp, axis=0)
    tq_lane = q0 + lax.broadcasted_iota(jnp.int32, (1, NSA_KV_HEADS * tq_n), 1) % tq_n
    t_end = lax.broadcasted_iota(jnp.int32, (1, n_cmp), 1) * CMP_STRIDE + (CMP_BLK - 1)
    ov_t = _overlap_t(nb, n_cmp, nc)
    n_chunks = (q0 + tq_n + tk_n - 1) // tk_n
    w_len = WINDOW + tq_n
    w0 = pl.multiple_of(jnp.maximum(q0 - WINDOW, 0), tq_n)

    qgs = [_scaled_group_queries(q_all, g) for g in range(NSA_KV_HEADS)]
    o_cs, imps = [], []
    for g in range(NSA_KV_HEADS):
        o_c, p_c = _cmp_branch_fm(qgs[g], cmp_ref, g, t_end <= tq_rows)
        o_cs.append(o_c)
        imps.append(_importance_t(_sum_heads(p_c, tq_n), ov_t))
    sel_all, _ = _topk_rows(_select_scores_t(jnp.concatenate(imps, axis=1), tq_lane), min(SLC_TOP_N, nb))

    tw = w0 + lax.broadcasted_iota(jnp.int32, (1, w_len), 1)
    bias_w = jnp.where((tw <= tq_col) & (tw > tq_col - WINDOW), 0.0, NEG)
    bias_w = jnp.concatenate([bias_w] * grp, axis=0)

    for g in range(NSA_KV_HEADS):
        qg = qgs[g]
        kl, vl = g * hd, (NSA_KV_HEADS + g) * hd
        sel_t = sel_all[:, g * tq_n:(g + 1) * tq_n].astype(BF16)
        m_scr[...] = jnp.full((m_rows, 1), NEG, F32)
        l_scr[...] = jnp.zeros((m_rows, 1), F32)
        acc_scr[...] = jnp.zeros((m_rows, hd), F32)

        def slc_chunk(ci, carry):
            k0 = pl.multiple_of(ci * tk_n, tk_n)
            ks = kvs_ref[kl:kl + hd, pl.ds(k0, tk_n)].astype(BF16)
            vs = kvs_ref[vl:vl + hd, pl.ds(k0, tk_n)].astype(BF16)
            kpos = k0 + lax.broadcasted_iota(jnp.int32, (1, tk_n), 1)
            blk = lax.broadcasted_iota(jnp.int32, (nb, tk_n), 0)
            sel_k = _dot_tn(sel_t, (blk == kpos // SLC_BLK).astype(BF16))
            bias = jnp.where((sel_k > 0.5) & (kpos <= tq_col), 0.0, NEG)
            s = _dot(qg, ks) + jnp.concatenate([bias] * grp, axis=0)
            m_old = m_scr[...]
            m_new = jnp.maximum(m_old, jnp.max(s, axis=-1, keepdims=True))
            e = jnp.exp(s - m_new)
            corr = jnp.exp(m_old - m_new)
            l_scr[...] = corr * l_scr[...] + jnp.sum(e, axis=-1, keepdims=True)
            acc_scr[...] = corr * acc_scr[...] + _dot_nt(e.astype(BF16), vs)
            m_scr[...] = m_new
            return carry

        lax.fori_loop(0, n_chunks, slc_chunk, 0)
        o_s = acc_scr[...] / jnp.maximum(l_scr[...], 1e-30)
        kw = kvw_ref[kl:kl + hd, pl.ds(w0, w_len)].astype(BF16)
        vw = kvw_ref[vl:vl + hd, pl.ds(w0, w_len)].astype(BF16)
        s_w = _dot(qg, kw) + bias_w
        e_w = jnp.exp(s_w - jnp.max(s_w, axis=-1, keepdims=True))
        o_w = _dot_nt(e_w.astype(BF16), vw) / jnp.maximum(jnp.sum(e_w, axis=-1, keepdims=True), 1e-30)
        for r in range(grp):
            h = grp * g + r
            rows = slice(r * tq_n, (r + 1) * tq_n)
            gc = GATE_OFF + 3 * h
            o = (gates[:, gc:gc + 1] * o_cs[g][rows] + gates[:, gc + 1:gc + 2] * o_s[rows]
                 + gates[:, gc + 2:gc + 3] * o_w[rows])
            o_ref[:, h * hd:(h + 1) * hd] = o.astype(o_ref.dtype)


def _nsa_prompt_fm(q, aux, cmp_t, kvs_t, kvw_t):
    bsz, seq, _ = q.shape
    n_seg = cmp_t.shape[2]
    kern = functools.partial(_nsa_prompt_fm_kernel, seq=seq, nc=n_seg - CMP_R + 1)
    tile = lambda w: pl.BlockSpec((None, NSA_TQ, w), lambda b, t: (b, t, 0))
    whole = lambda n: pl.BlockSpec((None, KV_W, n), lambda b, t: (b, 0, 0))
    m_rows = NSA_GROUP * NSA_TQ
    return pl.pallas_call(
        kern,
        grid=(bsz, seq // NSA_TQ),
        in_specs=[tile(NSA_WIDTH), tile(AUX_W), whole(n_seg), whole(seq), whole(seq)],
        out_specs=tile(NSA_WIDTH),
        out_shape=jax.ShapeDtypeStruct((bsz, seq, NSA_WIDTH), BF16),
        scratch_shapes=[pltpu.VMEM((m_rows, 1), F32), pltpu.VMEM((m_rows, 1), F32),
                        pltpu.VMEM((m_rows, NSA_HEAD_DIM), F32)],
        compiler_params=_cparams(("parallel", "arbitrary")),
        name="nsa_attn_prompt",
    )(q, aux, cmp_t, kvs_t, kvw_t)


def _cmp_sample_fm_kernel(pt_ref, cache_ref, w1f_ref, pe_ref, w1_ref, b1_ref, w2t_ref, o_ref,
                          xbuf, stage, p_scr, sem, *, steps_per_batch):
    b = pl.program_id(0)
    h = pl.program_id(1)
    step = b * steps_per_batch + h
    n_steps = pl.num_programs(0) * steps_per_batch
    pps = CMP_PAGES_PER_STEP
    segs = pps * SEGS_PER_PAGE

    def page_copy(s, p, slot):
        page = pt_ref[s // steps_per_batch, (s % steps_per_batch) * pps + p]
        return pltpu.make_async_copy(cache_ref.at[page], xbuf.at[slot, p], sem.at[slot])

    def start_fetch(s, slot):
        def issue(p, c):
            page_copy(s, p, slot).start()
            return c
        lax.fori_loop(0, pps, issue, 0)

    @pl.when(step == 0)
    def _():
        start_fetch(step, 0)

    @pl.when(step + 1 < n_steps)
    def _():
        start_fetch(step + 1, (step + 1) % 2)

    slot = step % 2

    def wait_one(p, c):
        page_copy(step, p, slot).wait()
        return c
    lax.fori_loop(0, pps, wait_one, 0)

    def to_token_major(p, carry):
        r0 = pl.multiple_of(p * PAGE_SIZE, PAGE_SIZE)
        for c in range(2):
            stage[c, pl.ds(r0, PAGE_SIZE), :] = jnp.transpose(xbuf[slot, p, c * C_W:(c + 1) * C_W, :])
        return carry
    lax.fori_loop(0, pps, to_token_major, 0)

    for c in range(2):
        for rc in range(segs // CMP_ROW_CHUNK):
            p_rows = _cmp_first_layer(stage, c, rc * CMP_ROW_CHUNK * CMP_STRIDE, CMP_ROW_CHUNK, w1f_ref)
            r0 = pl.multiple_of(h * segs + rc * CMP_ROW_CHUNK, CMP_ROW_CHUNK)
            p_scr[c, pl.ds(r0, CMP_ROW_CHUNK), :] = p_rows

    @pl.when(h == steps_per_batch - 1)
    def _():
        for c in range(2):
            o_ref[c * C_W:(c + 1) * C_W, :] = _cmp_second_layer_fm(
                p_scr[c], _cmp_bias_c(pe_ref, w1_ref, b1_ref, c), w2t_ref[c])


def _cmp_sample_fm(page_table, cache_fm, w1_full, pe_flat, cmp_w1, cmp_b1, w2_bd_t):
    bsz, n_pages = page_table.shape
    steps = n_pages // CMP_PAGES_PER_STEP
    n_seg = n_pages * SEGS_PER_PAGE
    const = lambda a: pl.BlockSpec(a.shape, lambda b, h, pt: (0,) * a.ndim, pipeline_mode=pl.Buffered(1))
    grid_spec = pltpu.PrefetchScalarGridSpec(
        num_scalar_prefetch=1,
        grid=(bsz, steps),
        in_specs=[pl.BlockSpec(memory_space=pl.ANY), const(w1_full), const(pe_flat), const(cmp_w1),
                  const(cmp_b1), const(w2_bd_t)],
        out_specs=pl.BlockSpec((None, KV_W, n_seg), lambda b, h, pt: (b, 0, 0)),
        scratch_shapes=[pltpu.VMEM((2, CMP_PAGES_PER_STEP, KV_W, PAGE_SIZE), F32),
                        pltpu.VMEM((2, CMP_PAGES_PER_STEP * PAGE_SIZE, C_W), F32),
                        pltpu.VMEM((2, n_seg, PC_W), F32),
                        pltpu.SemaphoreType.DMA((2,))],
    )
    return pl.pallas_call(
        functools.partial(_cmp_sample_fm_kernel, steps_per_batch=steps),
        grid_spec=grid_spec,
        out_shape=jax.ShapeDtypeStruct((bsz, KV_W, n_seg), F32),
        compiler_params=_cparams(("arbitrary", "arbitrary")),
        name="nsa_compress_sample",
    )(page_table, cache_fm, w1_full, pe_flat, cmp_w1, cmp_b1, w2_bd_t)


def _nsa_sample_select_fm_kernel(q_ref, cmp_ref, oc_ref, idx_ref, *, past_len, nb, nb_pad, nc):
    hd, grp = NSA_HEAD_DIM, NSA_GROUP
    m_rows = grp * SQ_ROWS
    q_all = q_ref[...]
    n_cmp = cmp_ref.shape[1]
    tq_rows = past_len + lax.broadcasted_iota(jnp.int32, (m_rows, 1), 0) % SQ_ROWS
    tq_lane = past_len + lax.broadcasted_iota(jnp.int32, (1, LANES), 1)
    t_end = lax.broadcasted_iota(jnp.int32, (1, n_cmp), 1) * CMP_STRIDE + (CMP_BLK - 1)
    ov_t = _overlap_t(nb_pad, n_cmp, nc)
    for g in range(NSA_KV_HEADS):
        o_c, p_c = _cmp_branch_fm(_scaled_group_queries(q_all, g), cmp_ref, g, t_end <= tq_rows)
        psum = jnp.concatenate([_sum_heads(p_c, SQ_ROWS), jnp.zeros((LANES - SQ_ROWS, n_cmp), F32)], axis=0)
        score_t = _select_scores_t(_importance_t(psum, ov_t), tq_lane)
        rowid = lax.broadcasted_iota(jnp.int32, score_t.shape, 0)
        _, picks = _topk_rows(jnp.where(rowid < nb, score_t, -jnp.inf), min(SLC_TOP_N, nb))
        idx_ref[g] = jnp.concatenate(picks, axis=0)
        for r in range(grp):
            h = grp * g + r
            oc_ref[:, h * hd:(h + 1) * hd] = o_c[r * SQ_ROWS:(r + 1) * SQ_ROWS]


def _nsa_sample_select_fm(q_pad, cmp_t, past_len, seq_new):
    bsz = q_pad.shape[0]
    n_seg = cmp_t.shape[2]
    nb = -(-(past_len + seq_new) // SLC_BLK)
    nb_pad = -(-nb // SUBLANES) * SUBLANES
    n_pick = min(SLC_TOP_N, nb)
    kern = functools.partial(_nsa_sample_select_fm_kernel, past_len=past_len, nb=nb, nb_pad=nb_pad,
                             nc=n_seg - CMP_R + 1)
    return pl.pallas_call(
        kern,
        grid=(bsz,),
        in_specs=[pl.BlockSpec((None, SQ_ROWS, NSA_WIDTH), lambda b: (b, 0, 0)),
                  pl.BlockSpec((None, KV_W, n_seg), lambda b: (b, 0, 0))],
        out_specs=[pl.BlockSpec((None, SQ_ROWS, NSA_WIDTH), lambda b: (b, 0, 0)),
                   pl.BlockSpec((None, NSA_KV_HEADS, n_pick, LANES), lambda b: (b, 0, 0, 0))],
        out_shape=[jax.ShapeDtypeStruct((bsz, SQ_ROWS, NSA_WIDTH), F32),
                   jax.ShapeDtypeStruct((bsz, NSA_KV_HEADS, n_pick, LANES), jnp.int32)],
        compiler_params=_cparams(("parallel",)),
        name="nsa_select_sample",
    )(q_pad, cmp_t)


def _nsa_sample_attend_fm_kernel(pt_ref, idx_ref, q_ref, aux_ref, oc_ref, cache_ref, tail_ref, winp_ref,
                                 winn_ref, o_ref, kbuf, vbuf, sem, *, past_len, seq_new, n_pick):
    b = pl.program_id(0)
    nbatch = pl.num_programs(0)
    hd, grp = NSA_HEAD_DIM, NSA_GROUP
    n_items = seq_new * NSA_KV_HEADS * n_pick
    n_past_blocks = past_len // SLC_BLK
    blocks_per_page = PAGE_SIZE // SLC_BLK
    key_lanes = n_pick * PAGE_SIZE

    def block_id(bb, item):
        return jnp.minimum(idx_ref[bb * n_items + item], n_past_blocks)

    def copies(bb, item, slot, from_tail):
        g = (item // n_pick) % NSA_KV_HEADS
        lanes = pl.ds(pl.multiple_of((item % n_pick) * PAGE_SIZE, PAGE_SIZE), PAGE_SIZE)
        k_rows = pl.ds(pl.multiple_of(g * hd, hd), hd)
        v_rows = pl.ds(pl.multiple_of((NSA_KV_HEADS + g) * hd, hd), hd)
        if from_tail:
            src = tail_ref.at[bb]
        else:
            src = cache_ref.at[pt_ref[bb, block_id(bb, item) // blocks_per_page]]
        return (pltpu.make_async_copy(src.at[k_rows, :], kbuf.at[slot, item // n_pick, :, lanes], sem.at[slot]),
                pltpu.make_async_copy(src.at[v_rows, :], vbuf.at[slot, item // n_pick, :, lanes], sem.at[slot]))

    def start_fetch(bb, slot):
        def issue(item, c):
            is_past = block_id(bb, item) < n_past_blocks

            @pl.when(is_past)
            def _():
                for cp in copies(bb, item, slot, False):
                    cp.start()

            @pl.when(jnp.logical_not(is_past))
            def _():
                for cp in copies(bb, item, slot, True):
                    cp.start()
            return c
        lax.fori_loop(0, n_items, issue, 0)

    @pl.when(b == 0)
    def _():
        start_fetch(b, 0)

    @pl.when(b + 1 < nbatch)
    def _():
        start_fetch(b + 1, (b + 1) % 2)

    slot = b % 2

    def wait_one(item, c):
        for cp in copies(b, item, slot, True):
            cp.wait()
        return c
    lax.fori_loop(0, n_items, wait_one, 0)

    q_all = q_ref[...]
    gates = jax.nn.sigmoid(aux_ref[...])
    m_rows = grp * SQ_ROWS
    row_q = lax.broadcasted_iota(jnp.int32, (m_rows, 1), 0) % SQ_ROWS
    tq_rows = past_len + row_q
    lane = lax.broadcasted_iota(jnp.int32, (1, key_lanes), 1)
    key_slot, key_tok = lane // PAGE_SIZE, lane % PAGE_SIZE
    wb, wn = winp_ref.shape[1], winn_ref.shape[1]
    tw_p = past_len - wb + lax.broadcasted_iota(jnp.int32, (1, wb), 1)
    tw_n = past_len + lax.broadcasted_iota(jnp.int32, (1, wn), 1)
    bias_p = jnp.where((tw_p <= tq_rows) & (tw_p > tq_rows - WINDOW) & (tw_p >= 0), 0.0, NEG)
    bias_n = jnp.where((tw_n <= tq_rows) & (tw_n > tq_rows - WINDOW), 0.0, NEG)

    for g in range(NSA_KV_HEADS):
        qg = _scaled_group_queries(q_all, g)
        kl, vl = g * hd, (NSA_KV_HEADS + g) * hd
        o_s = jnp.zeros((m_rows, hd), F32)
        for qi in range(seq_new):
            qg_i = qi * NSA_KV_HEADS + g
            blk = jnp.zeros((1, key_lanes), jnp.int32)
            for kk in range(n_pick):
                blk = jnp.where(key_slot == kk, block_id(b, qg_i * n_pick + kk), blk)
            pos = blk * SLC_BLK + key_tok % SLC_BLK
            mask = (key_tok // SLC_BLK == blk % blocks_per_page) & (pos <= past_len + qi)
            s = _dot(qg, kbuf[slot, qg_i].astype(BF16))
            p = _masked_softmax(s, mask)
            o_s = jnp.where(row_q == qi, _dot_nt(p.astype(BF16), vbuf[slot, qg_i].astype(BF16)), o_s)
        s_p = _dot(qg, winp_ref[kl:kl + hd, :].astype(BF16)) + bias_p
        s_n = _dot(qg, winn_ref[kl:kl + hd, :].astype(BF16)) + bias_n
        m = jnp.maximum(jnp.max(s_p, axis=-1, keepdims=True), jnp.max(s_n, axis=-1, keepdims=True))
        e_p, e_n = jnp.exp(s_p - m), jnp.exp(s_n - m)
        den = jnp.maximum(jnp.sum(e_p, axis=-1, keepdims=True) + jnp.sum(e_n, axis=-1, keepdims=True), 1e-30)
        o_w = (_dot_nt(e_p.astype(BF16), winp_ref[vl:vl + hd, :].astype(BF16))
               + _dot_nt(e_n.astype(BF16), winn_ref[vl:vl + hd, :].astype(BF16))) / den
        for r in range(grp):
            h = grp * g + r
            rows = slice(r * SQ_ROWS, (r + 1) * SQ_ROWS)
            gc = GATE_OFF + 3 * h
            o = (gates[:, gc:gc + 1] * oc_ref[:, h * hd:(h + 1) * hd] + gates[:, gc + 1:gc + 2] * o_s[rows]
                 + gates[:, gc + 2:gc + 3] * o_w[rows])
            o_ref[:, h * hd:(h + 1) * hd] = o.astype(o_ref.dtype)


def _nsa_sample_attend_fm(page_table, idx_flat, q_pad, aux_pad, o_c, cache_fm, tail_fm, win_past, win_new,
                          past_len, seq_new, n_pick):
    bsz = q_pad.shape[0]
    wb, wn = win_past.shape[2], win_new.shape[2]
    kern = functools.partial(_nsa_sample_attend_fm_kernel, past_len=past_len, seq_new=seq_new, n_pick=n_pick)
    per_b = lambda n, w: pl.BlockSpec((None, n, w), lambda b, pt, ix: (b, 0, 0))
    n_qg = seq_new * NSA_KV_HEADS
    buf = pltpu.VMEM((2, n_qg, NSA_HEAD_DIM, n_pick * PAGE_SIZE), F32)
    grid_spec = pltpu.PrefetchScalarGridSpec(
        num_scalar_prefetch=2,
        grid=(bsz,),
        in_specs=[per_b(SQ_ROWS, NSA_WIDTH), per_b(SQ_ROWS, AUX_W), per_b(SQ_ROWS, NSA_WIDTH),
                  pl.BlockSpec(memory_space=pl.ANY), pl.BlockSpec(memory_space=pl.ANY),
                  per_b(KV_W, wb), per_b(KV_W, wn)],
        out_specs=per_b(SQ_ROWS, NSA_WIDTH),
        scratch_shapes=[buf, buf, pltpu.SemaphoreType.DMA((2,))],
    )
    return pl.pallas_call(
        kern,
        grid_spec=grid_spec,
        out_shape=jax.ShapeDtypeStruct((bsz, SQ_ROWS, NSA_WIDTH), BF16),
        compiler_params=_cparams(("arbitrary",)),
        name="nsa_attend_sample",
    )(page_table, idx_flat, q_pad, aux_pad, o_c, cache_fm, tail_fm, win_past, win_new)


PROMPT_TM = 512
GLA_TL = 512
SAMPLE_GLA_ROWS = 16


def _pad_rows(x, n):
    return jnp.pad(x, ((0, 0), (0, n - x.shape[1]), (0, 0)))


def kernel(x_prompt, x_sample, state_gla, cache_cmp_kv, cache_slc_kv, cache_win_kv, page_table, c_prompt,
           c_sample, ln_in_g, ln_in_b, w_ada, b_ada, w_in, gla_w_a2, gla_b_a, gla_norm_g, cmp_pe, cmp_w1,
           cmp_b1, cmp_w2, w_o, ln1_g, ln1_b, w_ffn_in, w_ffn_out, ln2_g, ln2_b):
    assert w_in.shape[0] == DEPTH == 1
    l = 0
    bp, lp, d = x_prompt.shape
    bs, ls, _ = x_sample.shape
    n_pool = cache_cmp_kv.shape[1]
    n_pages = page_table.shape[1]
    past_len = n_pages * PAGE_SIZE
    wb = cache_win_kv.shape[2]
    assert ((past_len + ls) // CMP_STRIDE) * CMP_STRIDE <= past_len and past_len % SLC_BLK == 0
    assert ls <= SQ_ROWS and ls <= SLC_BLK and wb == WINDOW

    w_perm, w_kv_t = _permute_w_in(w_in[l])
    w_o_b, w_fi_b, w_fo_b = w_o[l].astype(BF16), w_ffn_in[l].astype(BF16), w_ffn_out[l].astype(BF16)
    w_a2p = jnp.zeros((AUX_W, GLA_KW), F32).at[:GLA_RANK].set(gla_w_a2[l])
    b_a = gla_b_a[l].reshape(1, GLA_KW)
    norm_g = gla_norm_g[l].reshape(1, GLA_DV)
    w1c, w2c_t, pe_flat = _cmp_weights_fm(cmp_pe[l], cmp_w1[l], cmp_w2[l])
    kvt = (2, NSA_KV_HEADS, NSA_HEAD_DIM)

    def fm_view(a):
        return jnp.transpose(a, (0, 2, 3, 4, 1)).reshape(a.shape[0], KV_W, a.shape[1])

    def tok_view(a_fm):
        n, _, t = a_fm.shape
        return jnp.transpose(a_fm.reshape((n,) + kvt + (t,)), (0, 4, 1, 2, 3))[None]

    mod = _ada(jnp.concatenate([c_prompt, c_sample], axis=0), w_ada[l], b_ada[l])
    mods_p = [m[:bp, None, :] for m in jnp.split(mod, 6, axis=-1)]
    mods_s = [jnp.repeat(m[bp:], ls, axis=0)[None] for m in jnp.split(mod, 6, axis=-1)]

    def out_ffn(x2d, o_g, o_n, mods, tm, rpm):
        sh1, sc1, ga1, sh2, sc2, ga2 = mods
        return _out_ffn(x2d, o_g, o_n, ga1, sc2, sh2, ga2, ln_in_g, ln_in_b, w_o_b, ln1_g[l], ln1_b[l], w_fi_b,
                        w_fo_b, ln2_g[l], ln2_b[l], tm, rpm)

    xp2 = x_prompt.reshape(bp * lp, d)
    rpm = lp // PROMPT_TM
    qk, v, r, qn, kvc, _, _, aux, kvc_t, kvs_t, kvw_t = _inproj(
        xp2, mods_p[1], mods_p[0], ln_in_g, ln_in_b, w_perm, w_kv_t, PROMPT_TM, rpm, seq_per_batch=lp)
    b3 = lambda a: a.reshape(bp, lp, a.shape[-1])
    o_g, s_p = _gla(b3(qk), b3(v), b3(r), b3(aux), w_a2p, b_a, norm_g, jnp.zeros((bp, GLA_KW, GLA_DV), F32),
                    chunk=GLA_CHUNK, sub=GLA_SUB, tl=GLA_TL, l_valid=lp)
    cmp_t_p = _cmp_prompt_fm(kvc, lp, w1c, pe_flat, cmp_w1[l], cmp_b1[l], w2c_t)
    o_n = _nsa_prompt_fm(b3(qn), b3(aux), cmp_t_p, kvs_t, kvw_t)
    y_p = out_ffn(xp2, o_g.reshape(bp * lp, GLA_WIDTH), o_n.reshape(bp * lp, NSA_WIDTH), mods_p, PROMPT_TM, rpm)
    w_keep = min(WINDOW, lp)
    outs_p = (y_p.reshape(bp, lp, d), s_p.reshape(1, bp, GLA_HEADS, GLA_DK, GLA_DV),
              tok_view(kvc_t), tok_view(kvs_t), tok_view(kvw_t[:, :, lp - w_keep:]))

    ts = bs * ls
    xs2 = x_sample.reshape(ts, d)
    qk, v, r, qn, kvc, kvs, kvw, aux = _inproj(xs2, mods_s[1], mods_s[0], ln_in_g, ln_in_b, w_perm, w_kv_t, ts, 1)
    s3 = lambda a: a.reshape(bs, ls, a.shape[-1])
    g16 = lambda a: _pad_rows(s3(a), SAMPLE_GLA_ROWS)
    o_g, s_s = _gla(g16(qk), g16(v), g16(r), g16(aux), w_a2p, b_a, norm_g, state_gla[l].reshape(bs, GLA_KW, GLA_DV),
                    chunk=SAMPLE_GLA_ROWS, sub=SAMPLE_GLA_ROWS, tl=SAMPLE_GLA_ROWS, l_valid=ls)
    o_g = o_g[:, :ls].reshape(ts, GLA_WIDTH)
    kvc = jnp.concatenate([kvc[0], kvc[1]], axis=1)
    cmp_t_s = _cmp_sample_fm(page_table, fm_view(cache_cmp_kv[l]), w1c, pe_flat, cmp_w1[l], cmp_b1[l], w2c_t)
    q_pad = _pad_rows(s3(qn), SQ_ROWS)
    o_c, idx = _nsa_sample_select_fm(q_pad, cmp_t_s, past_len, ls)
    n_pick = idx.shape[2]
    idx_flat = jnp.transpose(idx[..., :ls], (0, 3, 1, 2)).reshape(-1)
    new_fm = lambda a: jnp.pad(jnp.transpose(s3(a), (0, 2, 1)), ((0, 0), (0, 0), (0, LANES - ls)))
    win_past = fm_view(cache_win_kv[l])
    o_n = _nsa_sample_attend_fm(page_table, idx_flat, q_pad, _pad_rows(s3(aux), SQ_ROWS), o_c,
                                fm_view(cache_slc_kv[l]), new_fm(kvs), win_past, new_fm(kvw),
                                past_len, ls, n_pick)
    o_n = o_n[:, :ls].reshape(ts, NSA_WIDTH)
    y_s = out_ffn(xs2, o_g, o_n, mods_s, ts, 1)
    win_s = jnp.concatenate([win_past[:, :, ls:], jnp.transpose(s3(kvw), (0, 2, 1))], axis=2)
    outs_s = (y_s.reshape(bs, ls, d), s_s.reshape(1, bs, GLA_HEADS, GLA_DK, GLA_DV),
              kvc.reshape((1, bs, ls) + kvt), kvs.reshape((1, bs, ls) + kvt), tok_view(win_s))

    return (outs_p[0], outs_s[0], outs_p[1], outs_s[1], outs_p[2], outs_s[2], outs_p[3], outs_s[3],
            outs_p[4], outs_s[4])
```

```python
import functools
import math

import numpy as np
import jax
import jax.numpy as jnp
from jax import lax
from jax.experimental import pallas as pl
from jax.experimental.pallas import tpu as pltpu

F32 = jnp.float32
BF16 = jnp.bfloat16

D_MODEL = 1024
DEPTH = 1
PAGE_SIZE = 128
GLA_HEADS = 4
GLA_DV = D_MODEL // (2 * GLA_HEADS)
GLA_DK = GLA_DV // 2
GLA_RANK = 16
GLA_TAU = 16.0
GLA_CHUNK = 64
GLA_SUB = 16
GLA_WIDTH = GLA_HEADS * GLA_DV
GLA_KW = GLA_HEADS * GLA_DK
NSA_HEADS = 8
NSA_KV_HEADS = 2
NSA_GROUP = NSA_HEADS // NSA_KV_HEADS
NSA_HEAD_DIM = D_MODEL // (2 * NSA_HEADS)
NSA_WIDTH = NSA_HEADS * NSA_HEAD_DIM
CMP_BLK = 32
CMP_STRIDE = 16
CMP_HIDDEN = 2 * NSA_HEAD_DIM
SLC_BLK = 64
SLC_TOP_N = 16
WINDOW = 512
FORCE_BONUS = 1e4
NEG = -1e30
D_FF = -(-8 * D_MODEL // (3 * 256)) * 256
ALPHA = (2 * DEPTH) ** 0.25
KV_W = 2 * NSA_KV_HEADS * NSA_HEAD_DIM
N_GATES = 3 * NSA_HEADS
IN_SIZES = (GLA_KW, GLA_KW, GLA_WIDTH, GLA_RANK, GLA_WIDTH, NSA_WIDTH, KV_W, KV_W, KV_W, N_GATES)
IN_WIDTH = sum(IN_SIZES)
LN_EPS = 1e-5

LANES = 128
SUBLANES = 8
VMEM_LIMIT_BYTES = 56 * 1024 * 1024

AUX_W = LANES
IN_GROUPS = (2 * GLA_KW, GLA_WIDTH, GLA_WIDTH, NSA_WIDTH, KV_W, KV_W, KV_W, AUX_W)
IN_PERM_W = sum(IN_GROUPS)
GATE_OFF = GLA_RANK


def _cparams(sem):
    return pltpu.CompilerParams(dimension_semantics=sem, vmem_limit_bytes=VMEM_LIMIT_BYTES)


def _split3(a):
    hi = a.astype(BF16)
    r1 = a - hi.astype(F32)
    mid = r1.astype(BF16)
    lo = (r1 - mid.astype(F32)).astype(BF16)
    return hi, mid, lo


def _dot(a, b):
    return jnp.dot(a, b, preferred_element_type=F32)


def _dot_nt(a, b):
    return lax.dot_general(a, b, (((1,), (1,)), ((), ())), preferred_element_type=F32)


def _dot_tn(a, b):
    return lax.dot_general(a, b, (((0,), (0,)), ((), ())), preferred_element_type=F32)


def _layer_norm(x, g, b):
    mu = jnp.mean(x, axis=-1, keepdims=True)
    xc = x - mu
    var = jnp.mean(xc * xc, axis=-1, keepdims=True)
    return xc * lax.rsqrt(var + LN_EPS) * g + b


def _ada_kernel(c_ref, w_ref, b_ref, o_ref):
    c = c_ref[...]
    a = (c * jax.nn.sigmoid(c)).astype(BF16)
    o_ref[...] = _dot(a, w_ref[...].astype(BF16)) + b_ref[...]


def _ada(c, w_ada, b_ada):
    n, d = c.shape
    m = w_ada.shape[1]
    tn = D_MODEL
    return pl.pallas_call(
        _ada_kernel,
        grid=(m // tn,),
        in_specs=[pl.BlockSpec((n, d), lambda j: (0, 0)),
                  pl.BlockSpec((d, tn), lambda j: (0, j)),
                  pl.BlockSpec((1, tn), lambda j: (0, j))],
        out_specs=pl.BlockSpec((n, tn), lambda j: (0, j)),
        out_shape=jax.ShapeDtypeStruct((n, m), F32),
        compiler_params=_cparams(("parallel",)),
        name="ada_mod",
    )(c, w_ada, b_ada.reshape(1, m))


N_KV_GROUPS = 3
KVC_GROUP = 4


def _inproj_kernel(x_ref, sc_ref, sh_ref, g_ref, b_ref, w_ref, wt_ref, *o_refs, emit_t):
    xn = _layer_norm(x_ref[...], g_ref[...], b_ref[...])
    u = (xn * (1.0 + sc_ref[...]) + sh_ref[...]).astype(BF16)
    lo = 0
    for j, (o_ref, wdt) in enumerate(zip(o_refs[:len(IN_GROUPS)], IN_GROUPS)):
        z = _dot(u, w_ref[:, lo:lo + wdt]).astype(o_ref.dtype)
        if j == KVC_GROUP:
            o_ref[0] = z[:, :wdt // 2]
            o_ref[1] = z[:, wdt // 2:]
        else:
            o_ref[...] = z
        lo += wdt
    if emit_t:
        for j, o_ref in enumerate(o_refs[len(IN_GROUPS):]):
            o_ref[...] = _dot_nt(wt_ref[j * KV_W:(j + 1) * KV_W, :], u)


_IN_OUT_DTYPES = (BF16, BF16, BF16, BF16, F32, F32, F32, F32)


def _inproj(x, sc, sh, ln_g, ln_b, w_perm, w_kv_t, tm, rows_per_mod, seq_per_batch=None):
    t, d = x.shape
    r = sc.shape[1]
    emit_t = seq_per_batch is not None
    mod_spec = pl.BlockSpec((None, r, d), lambda i: (i // rows_per_mod, 0, 0))
    out_specs = [pl.BlockSpec((tm, w), lambda i: (i, 0)) for w in IN_GROUPS]
    out_shape = [jax.ShapeDtypeStruct((t, w), dt) for w, dt in zip(IN_GROUPS, _IN_OUT_DTYPES)]
    out_specs[KVC_GROUP] = pl.BlockSpec((2, tm, KV_W // 2), lambda i: (0, i, 0))
    out_shape[KVC_GROUP] = jax.ShapeDtypeStruct((2, t, KV_W // 2), F32)
    if emit_t:
        tpb = seq_per_batch // tm
        out_specs += [pl.BlockSpec((None, KV_W, tm), lambda i: (i // tpb, 0, i % tpb))] * N_KV_GROUPS
        out_shape += [jax.ShapeDtypeStruct((t // seq_per_batch, KV_W, seq_per_batch), F32)] * N_KV_GROUPS
    return pl.pallas_call(
        functools.partial(_inproj_kernel, emit_t=emit_t),
        grid=(t // tm,),
        in_specs=[pl.BlockSpec((tm, d), lambda i: (i, 0)), mod_spec, mod_spec,
                  pl.BlockSpec((1, d), lambda i: (0, 0)), pl.BlockSpec((1, d), lambda i: (0, 0)),
                  pl.BlockSpec((d, IN_PERM_W), lambda i: (0, 0)),
                  pl.BlockSpec((N_KV_GROUPS * KV_W, d), lambda i: (0, 0))],
        out_specs=out_specs,
        out_shape=out_shape,
        compiler_params=_cparams(("parallel",)),
        name="ln_mod_inproj",
    )(x, sc, sh, ln_g.reshape(1, d), ln_b.reshape(1, d), w_perm, w_kv_t)


def _permute_w_in(w_in):
    q_g, k_g, v_g, a_g, r_g, q_n, kv_c, kv_s, kv_w, g_n = jnp.split(w_in, np.cumsum(IN_SIZES)[:-1], axis=1)
    pad = jnp.zeros((w_in.shape[0], AUX_W - GLA_RANK - N_GATES), w_in.dtype)
    w_perm = jnp.concatenate([q_g, k_g, v_g, r_g, q_n, kv_c, kv_s, kv_w, a_g, g_n, pad], axis=1).astype(BF16)
    w_kv_t = jnp.concatenate([kv_c, kv_s, kv_w], axis=1).T.astype(BF16)
    return w_perm, w_kv_t


FF_CHUNK = 256


def _out_ffn_kernel(x_ref, og_ref, on_ref, ga1_ref, sc2_ref, sh2_ref, ga2_ref, lng_ref, lnb_ref,
                    wo_ref, l1g_ref, l1b_ref, wfi_ref, wfo_ref, l2g_ref, l2b_ref, y_ref):
    x = _layer_norm(x_ref[...], lng_ref[...], lnb_ref[...])
    mix = _dot(og_ref[...], wo_ref[0:GLA_WIDTH, :]) + _dot(on_ref[...], wo_ref[GLA_WIDTH:, :])
    x1 = _layer_norm(ALPHA * x + ga1_ref[...] * mix, l1g_ref[...], l1b_ref[...])
    u2 = (x1 * (1.0 + sc2_ref[...]) + sh2_ref[...]).astype(BF16)
    ffn = jnp.zeros(x1.shape, F32)
    for c in range(D_FF // FF_CHUNK):
        lo = c * FF_CHUNK
        gate = _dot(u2, wfi_ref[:, lo:lo + FF_CHUNK])
        up = _dot(u2, wfi_ref[:, D_FF + lo:D_FF + lo + FF_CHUNK])
        f = (gate * jax.nn.sigmoid(gate) * up).astype(BF16)
        ffn = ffn + _dot(f, wfo_ref[lo:lo + FF_CHUNK, :])
    y_ref[...] = _layer_norm(ALPHA * x1 + ga2_ref[...] * ffn, l2g_ref[...], l2b_ref[...])


def _out_ffn(x, o_g, o_n, ga1, sc2, sh2, ga2, ln_in_g, ln_in_b, w_o, ln1_g, ln1_b, w_fi, w_fo, ln2_g, ln2_b,
             tm, rows_per_mod):
    t, d = x.shape
    r = ga1.shape[1]
    mod_spec = pl.BlockSpec((None, r, d), lambda i: (i // rows_per_mod, 0, 0))
    vec = lambda: pl.BlockSpec((1, d), lambda i: (0, 0))
    const = lambda shp: pl.BlockSpec(shp, lambda i: (0, 0), pipeline_mode=pl.Buffered(1))
    row = lambda a: a.reshape(1, d)
    return pl.pallas_call(
        _out_ffn_kernel,
        grid=(t // tm,),
        in_specs=[pl.BlockSpec((tm, d), lambda i: (i, 0)),
                  pl.BlockSpec((tm, GLA_WIDTH), lambda i: (i, 0)),
                  pl.BlockSpec((tm, NSA_WIDTH), lambda i: (i, 0)),
                  mod_spec, mod_spec, mod_spec, mod_spec, vec(), vec(),
                  const((d, d)), vec(), vec(), const((d, 2 * D_FF)), const((D_FF, d)), vec(), vec()],
        out_specs=pl.BlockSpec((tm, d), lambda i: (i, 0)),
        out_shape=jax.ShapeDtypeStruct((t, d), F32),
        compiler_params=_cparams(("parallel",)),
        name="outproj_ffn",
    )(x, o_g, o_n, ga1, sc2, sh2, ga2, row(ln_in_g), row(ln_in_b), w_o, row(ln1_g), row(ln1_b), w_fi, w_fo,
      row(ln2_g), row(ln2_b))


GLA_EXP_CLAMP = 80.0


def _gla_kernel(qk_ref, v_ref, r_ref, aux_ref, wa_ref, ba_ref, ng_ref, s0_ref, o_ref, sout_ref, s_scr,
                *, chunk, sub, tl, l_valid, l_pad, nbb):
    t = pl.program_id(1)
    c = chunk
    n_sub = c // sub
    hw = GLA_KW

    @pl.when(t == 0)
    def _():
        s_scr[...] = s0_ref[...]

    ri = lax.broadcasted_iota(jnp.int32, (c, c), 0)
    ci = lax.broadcasted_iota(jnp.int32, (c, c), 1)
    causal = ci <= ri
    tril = causal.astype(BF16)
    rowid = lax.broadcasted_iota(jnp.int32, (c, hw), 0)
    head_of_lane = lax.broadcasted_iota(jnp.int32, (c, hw), 1) // GLA_DK
    wa = wa_ref[...]
    wa_hi = wa.astype(BF16)
    wa_mid = (wa - wa_hi.astype(F32)).astype(BF16)
    ba = ba_ref[...]
    ng = ng_ref[...]

    def body(i, carry):
        for e in range(nbb):
            one_chunk(i, e)
        return carry

    def one_chunk(i, e):
        r0 = pl.multiple_of(i * c, c)
        aux = aux_ref[e, pl.ds(r0, c), :]
        a_hi = aux.astype(BF16)
        a_mid = (aux - a_hi.astype(F32)).astype(BF16)
        z = _dot(a_hi, wa_hi) + _dot(a_mid, wa_hi) + _dot(a_hi, wa_mid) + ba
        g = jax.nn.log_sigmoid(z) / GLA_TAU
        if l_pad != l_valid:
            g = jnp.where(t * tl + r0 + rowid < l_valid, g, 0.0)
        g_hi, g_mid, g_lo = _split3(g)
        b = _dot(tril, g_hi) + _dot(tril, g_mid) + _dot(tril, g_lo)
        qk = qk_ref[e, pl.ds(r0, c), :]
        q = qk[:, :hw].astype(F32) * (GLA_DK ** -0.5)
        k = qk[:, hw:].astype(F32)
        v = v_ref[e, pl.ds(r0, c), :]
        b_last = b[c - 1:c, :]

        def heads_on_rows(x):
            return jnp.concatenate([jnp.where(head_of_lane == h, x, 0.0) for h in range(GLA_HEADS)], axis=0)

        s_old = s_scr[e]
        o_inter =_dot(heads_on_rows(q * jnp.exp(b)).astype(BF16), s_old.astype(BF16))

        q_parts, k_parts = [], []
        for s_i in range(n_sub):
            b_ref = b[s_i * sub - 1:s_i * sub, :] if s_i > 0 else jnp.zeros((1, hw), F32)
            in_rows = (rowid >= s_i * sub) & (rowid < (s_i + 1) * sub)
            qt = jnp.where(in_rows, q * jnp.exp(jnp.minimum(b - b_ref, 0.0)), 0.0)
            kt = jnp.where(rowid < (s_i + 1) * sub, k * jnp.exp(jnp.minimum(b_ref - b, GLA_EXP_CLAMP)), 0.0)
            q_parts.append(heads_on_rows(qt).astype(BF16))
            k_parts.append(kt.astype(BF16))
        q_cat = jnp.concatenate(q_parts, axis=1) if n_sub > 1 else q_parts[0]
        k_cat = jnp.concatenate(k_parts, axis=1) if n_sub > 1 else k_parts[0]
        att = _dot_nt(q_cat, k_cat)

        r_t = r_ref[e, pl.ds(r0, c), :].astype(F32)
        for h in range(GLA_HEADS):
            att_h = jnp.where(causal, att[h * c:(h + 1) * c, :], 0.0).astype(BF16)
            o_h = o_inter[h * c:(h + 1) * c, :] + _dot(att_h, v[:, h * GLA_DV:(h + 1) * GLA_DV])
            o_h = o_h * lax.rsqrt(jnp.mean(o_h * o_h, axis=-1, keepdims=True) + 1e-6) * ng
            r_h = r_t[:, h * GLA_DV:(h + 1) * GLA_DV]
            o_ref[e, pl.ds(r0, c), h * GLA_DV:(h + 1) * GLA_DV] = (
                o_h * (r_h * jax.nn.sigmoid(r_h))).astype(o_ref.dtype)

        kd = (k * jnp.exp(b_last - b)).astype(BF16)
        upd = _dot_tn(kd, v)
        decay = jnp.transpose(jnp.broadcast_to(jnp.exp(b_last), (LANES, hw)))
        upd_d = jnp.concatenate([upd[h * GLA_DK:(h + 1) * GLA_DK, h * GLA_DV:(h + 1) * GLA_DV]
                                 for h in range(GLA_HEADS)], axis=0)
        s_scr[e] = decay * s_old + upd_d

    lax.fori_loop(0, tl // c, body, 0)

    @pl.when(t == pl.num_programs(1) - 1)
    def _():
        sout_ref[...] = s_scr[...]


def _gla(qk, v, r, aux, w_a2p, b_a, norm_g, s0, *, chunk, sub, tl, l_valid, nbb):
    bsz, l_pad, _ = qk.shape
    nt = l_pad // tl
    kern = functools.partial(_gla_kernel, chunk=chunk, sub=sub, tl=tl, l_valid=l_valid, l_pad=l_pad, nbb=nbb)
    tile = lambda w: pl.BlockSpec((nbb, tl, w), lambda b, t: (b, t, 0))
    full = lambda shp: pl.BlockSpec(shp, lambda b, t: (0, 0))
    st = pl.BlockSpec((nbb, GLA_KW, GLA_DV), lambda b, t: (b, 0, 0))
    return pl.pallas_call(
        kern,
        grid=(bsz // nbb, nt),
        in_specs=[tile(2 * GLA_KW), tile(GLA_WIDTH), tile(GLA_WIDTH), tile(AUX_W),
                  full((AUX_W, GLA_KW)), full((1, GLA_KW)), full((1, GLA_DV)), st],
        out_specs=[tile(GLA_WIDTH), st],
        out_shape=[jax.ShapeDtypeStruct((bsz, l_pad, GLA_WIDTH), BF16),
                   jax.ShapeDtypeStruct((bsz, GLA_KW, GLA_DV), F32)],
        scratch_shapes=[pltpu.VMEM((nbb, GLA_KW, GLA_DV), F32)],
        compiler_params=_cparams(("parallel", "arbitrary")),
        name="gla_scan",
    )(qk, v, r, aux, w_a2p, b_a, norm_g, s0)


SEG_W = CMP_STRIDE * KV_W
CMP_R = CMP_BLK // CMP_STRIDE
P_W = CMP_R * 2 * NSA_KV_HEADS * CMP_HIDDEN
H_W = P_W // CMP_R


def _cmp_weights(cmp_pe, cmp_w1, cmp_b1, cmp_w2):
    eye = jnp.eye(2, dtype=F32)
    w1r = cmp_w1.reshape(2, CMP_R, CMP_STRIDE, NSA_HEAD_DIM, CMP_HIDDEN)
    w1_full = jnp.einsum('cmsdh,cC,gG->scgdmCGh', w1r, eye, eye).reshape(SEG_W, P_W).astype(BF16)
    w2_bd = jnp.einsum('chd,cC,gG->cghCGd', cmp_w2, eye, eye).reshape(H_W, KV_W).astype(BF16)
    pe_flat = jnp.transpose(cmp_pe, (1, 0, 2)).reshape(2, CMP_BLK * NSA_HEAD_DIM)
    return w1_full, w2_bd, pe_flat


def _cmp_bias(pe_ref, w1_ref, b1_ref):
    parts = []
    for c in range(2):
        pe = jnp.broadcast_to(pe_ref[c:c + 1, :], (SUBLANES, pe_ref.shape[1]))
        w1c = w1_ref[c]
        pe_hi = pe.astype(BF16)
        pe_mid = (pe - pe_hi.astype(F32)).astype(BF16)
        w_hi = w1c.astype(BF16)
        w_mid = (w1c - w_hi.astype(F32)).astype(BF16)
        pb = _dot(pe_hi, w_hi) + _dot(pe_mid, w_hi) + _dot(pe_hi, w_mid)
        bc = pb[0:1, :] + b1_ref[c:c + 1, :]
        parts += [bc] * NSA_KV_HEADS
    return jnp.concatenate(parts, axis=1)


def _cmp_second_layer(p, bias, w2_ref):
    n = p.shape[0]
    h = p[:, :H_W] + pltpu.roll(p[:, H_W:], n - 1, 0) + bias
    return _dot(jax.nn.gelu(h).astype(BF16), w2_ref[...])


def _cmp_prompt_kernel(x_ref, w1f_ref, pe_ref, w1_ref, b1_ref, w2_ref, o_ref):
    p = _dot(x_ref[...].astype(BF16), w1f_ref[...])
    o_ref[...] = _cmp_second_layer(p, _cmp_bias(pe_ref, w1_ref, b1_ref), w2_ref)


def _cmp_prompt(x_seg, w1_full, pe_flat, cmp_w1, cmp_b1, w2_bd):
    bsz, n_seg, _ = x_seg.shape
    const2 = lambda a: pl.BlockSpec(a.shape, lambda b: (0,) * a.ndim)
    return pl.pallas_call(
        _cmp_prompt_kernel,
        grid=(bsz,),
        in_specs=[pl.BlockSpec((None, n_seg, SEG_W), lambda b: (b, 0, 0)),
                  const2(w1_full), const2(pe_flat), const2(cmp_w1), const2(cmp_b1), const2(w2_bd)],
        out_specs=pl.BlockSpec((None, n_seg, KV_W), lambda b: (b, 0, 0)),
        out_shape=jax.ShapeDtypeStruct((bsz, n_seg, KV_W), F32),
        compiler_params=_cparams(("parallel",)),
        name="nsa_compress_prompt",
    )(x_seg, w1_full, pe_flat, cmp_w1, cmp_b1, w2_bd)


QK_SCALE = NSA_HEAD_DIM ** -0.5
LOG2E = math.log2(math.e)


def _masked_softmax(s, mask):
    s = jnp.where(mask, s, NEG)
    e = jnp.where(mask, jnp.exp(s - jnp.max(s, axis=-1, keepdims=True)), 0.0)
    return e / jnp.maximum(jnp.sum(e, axis=-1, keepdims=True), 1e-30)


def _group_queries(q, g):
    hd = NSA_HEAD_DIM
    return jnp.concatenate([q[:, (NSA_GROUP * g + r) * hd:(NSA_GROUP * g + r + 1) * hd]
                            for r in range(NSA_GROUP)], axis=0)


def _topk_rows(score_t, n_pick):
    nb, nq = score_t.shape
    rowid = lax.broadcasted_iota(jnp.int32, (nb, nq), 0)
    taken = jnp.zeros((nb, nq), jnp.int32)
    picks = []
    for _ in range(n_pick):
        free = taken == 0
        cand = jnp.where(free, score_t, -jnp.inf)
        m = jnp.max(cand, axis=0, keepdims=True)
        hit = free & (cand == m)
        idx = jnp.min(jnp.where(hit, rowid, nb), axis=0, keepdims=True)
        taken = jnp.where(rowid == idx, 1, taken)
        picks.append(idx)
    return taken.astype(F32), picks


def _importance_t(psum, ov_t):
    hi, mid, lo = _split3(psum)
    return _dot_nt(ov_t, hi) + _dot_nt(ov_t, mid) + _dot_nt(ov_t, lo)


def _overlap_t(nb, nc_pad, nc):
    j = lax.broadcasted_iota(jnp.int32, (nb, nc_pad), 0) * SLC_BLK
    i = lax.broadcasted_iota(jnp.int32, (nb, nc_pad), 1) * CMP_STRIDE
    return ((i < j + SLC_BLK) & (i + CMP_BLK > j) & (i < nc * CMP_STRIDE)).astype(BF16)


def _select_scores_t(imp_t, tq_row):
    nb, nq = imp_t.shape
    j = lax.broadcasted_iota(jnp.int32, (nb, nq), 0)
    cur = tq_row // SLC_BLK
    forced = (j == 0) | (j == cur) | (j == cur - 1)
    return jnp.where(j * SLC_BLK <= tq_row, imp_t + FORCE_BONUS * forced.astype(F32), -jnp.inf)


NSA_TQ = 128
NSA_TK = 512


def _nsa_prompt_kernel(q_ref, aux_ref, cmp_ref, kvs_ref, kvw_ref, o_ref, m_scr, l_scr, acc_scr, *, seq, nc):
    tq_n, tk_n, hd, grp = NSA_TQ, NSA_TK, NSA_HEAD_DIM, NSA_GROUP
    m_rows = grp * tq_n
    q0 = pl.program_id(1) * tq_n
    q_all = q_ref[...]
    gates = jax.nn.sigmoid(aux_ref[...])
    n_cmp = cmp_ref.shape[0]
    nb = seq // SLC_BLK

    tq_rows = q0 + lax.broadcasted_iota(jnp.int32, (m_rows, 1), 0) % tq_n
    tq_lane = q0 + lax.broadcasted_iota(jnp.int32, (1, tq_n), 1)
    t_end = lax.broadcasted_iota(jnp.int32, (1, n_cmp), 1) * CMP_STRIDE + (CMP_BLK - 1)
    ov_t = _overlap_t(nb, n_cmp, nc)
    n_chunks = (q0 + tq_n + tk_n - 1) // tk_n
    w_len = WINDOW + tq_n
    w0 = pl.multiple_of(jnp.maximum(q0 - WINDOW, 0), tq_n)

    for g in range(NSA_KV_HEADS):
        qg = _group_queries(q_all, g)
        kl, vl = g * hd, (NSA_KV_HEADS + g) * hd
        cmp = cmp_ref[...]
        s_c = _dot_nt(qg, cmp[:, kl:kl + hd].astype(BF16)) * QK_SCALE
        p_c = _masked_softmax(s_c, t_end <= tq_rows)
        o_c = _dot(p_c.astype(BF16), cmp[:, vl:vl + hd].astype(BF16))
        psum = p_c[0:tq_n]
        for r in range(1, grp):
            psum = psum + p_c[r * tq_n:(r + 1) * tq_n]
        sel_t, _ = _topk_rows(_select_scores_t(_importance_t(psum, ov_t), tq_lane), min(SLC_TOP_N, nb))
        sel_t = sel_t.astype(BF16)
        m_scr[...] = jnp.full((m_rows, 1), NEG, F32)
        l_scr[...] = jnp.zeros((m_rows, 1), F32)
        acc_scr[...] = jnp.zeros((m_rows, hd), F32)

        def slc_chunk(ci, carry):
            k0 = pl.multiple_of(ci * tk_n, tk_n)
            ks = kvs_ref[pl.ds(k0, tk_n), kl:kl + hd].astype(BF16)
            vs = kvs_ref[pl.ds(k0, tk_n), vl:vl + hd].astype(BF16)
            s = _dot_nt(qg, ks) * QK_SCALE
            kpos = k0 + lax.broadcasted_iota(jnp.int32, (1, tk_n), 1)
            blk = lax.broadcasted_iota(jnp.int32, (nb, tk_n), 0)
            expand = (blk == kpos // SLC_BLK).astype(BF16)
            sel_k = _dot_tn(sel_t, expand)
            sel_k = jnp.concatenate([sel_k] * grp, axis=0)
            mask = (sel_k > 0.5) & (kpos <= tq_rows)
            s = jnp.where(mask, s, NEG)
            m_old = m_scr[...]
            m_new = jnp.maximum(m_old, jnp.max(s, axis=-1, keepdims=True))
            e = jnp.where(mask, jnp.exp(s - m_new), 0.0)
            corr = jnp.exp(m_old - m_new)
            l_scr[...] = corr * l_scr[...] + jnp.sum(e, axis=-1, keepdims=True)
            acc_scr[...] = corr * acc_scr[...] + _dot(e.astype(BF16), vs)
            m_scr[...] = m_new
            return carry

        lax.fori_loop(0, n_chunks, slc_chunk, 0)
        o_s = acc_scr[...] / jnp.maximum(l_scr[...], 1e-30)
        kw = kvw_ref[pl.ds(w0, w_len), kl:kl + hd].astype(BF16)
        vw = kvw_ref[pl.ds(w0, w_len), vl:vl + hd].astype(BF16)
        s_w = _dot_nt(qg, kw) * QK_SCALE
        tw = w0 + lax.broadcasted_iota(jnp.int32, (1, w_len), 1)
        p_w = _masked_softmax(s_w, (tw <= tq_rows) & (tw > tq_rows - WINDOW))
        o_w = _dot(p_w.astype(BF16), vw)
        for r in range(grp):
            h = grp * g + r
            rows = slice(r * tq_n, (r + 1) * tq_n)
            gc = GATE_OFF + 3 * h
            o = (gates[:, gc:gc + 1] * o_c[rows] + gates[:, gc + 1:gc + 2] * o_s[rows]
                 + gates[:, gc + 2:gc + 3] * o_w[rows])
            o_ref[:, h * hd:(h + 1) * hd] = o.astype(o_ref.dtype)


def _nsa_prompt(q, aux, cmpkv, kv_s, kv_w):
    bsz, seq, _ = q.shape
    n_seg = cmpkv.shape[1]
    kern = functools.partial(_nsa_prompt_kernel, seq=seq, nc=n_seg - CMP_R + 1)
    tile = lambda w: pl.BlockSpec((None, NSA_TQ, w), lambda b, t: (b, t, 0))
    whole = lambda n, w: pl.BlockSpec((None, n, w), lambda b, t: (b, 0, 0))
    m_rows = NSA_GROUP * NSA_TQ
    return pl.pallas_call(
        kern,
        grid=(bsz, seq // NSA_TQ),
        in_specs=[tile(NSA_WIDTH), tile(AUX_W), whole(n_seg, KV_W), whole(seq, KV_W), whole(seq, KV_W)],
        out_specs=tile(NSA_WIDTH),
        out_shape=jax.ShapeDtypeStruct((bsz, seq, NSA_WIDTH), BF16),
        scratch_shapes=[pltpu.VMEM((m_rows, 1), F32), pltpu.VMEM((m_rows, 1), F32),
                        pltpu.VMEM((m_rows, NSA_HEAD_DIM), F32)],
        compiler_params=_cparams(("parallel", "arbitrary")),
        name="nsa_attn_prompt",
    )(q, aux, cmpkv, kv_s, kv_w)


SEGS_PER_PAGE = PAGE_SIZE // CMP_STRIDE
CMP_PAGES_PER_STEP = 64
CMP_ROW_CHUNK = 256


def _cmp_sample_kernel(pt_ref, cache_ref, w1f_ref, pe_ref, w1_ref, b1_ref, w2_ref, o_ref, xbuf, p_scr, sem,
                       *, steps_per_batch):
    b = pl.program_id(0)
    h = pl.program_id(1)
    step = b * steps_per_batch + h
    n_steps = pl.num_programs(0) * steps_per_batch
    pps = CMP_PAGES_PER_STEP
    rows = pps * SEGS_PER_PAGE

    def page_copy(s, p, slot):
        page = pt_ref[s // steps_per_batch, (s % steps_per_batch) * pps + p]
        return pltpu.make_async_copy(cache_ref.at[page], xbuf.at[slot, pl.ds(p * SEGS_PER_PAGE, SEGS_PER_PAGE), :],
                                     sem.at[slot])

    def start_fetch(s, slot):
        def issue(p, c):
            page_copy(s, p, slot).start()
            return c
        lax.fori_loop(0, pps, issue, 0)

    @pl.when(step == 0)
    def _():
        start_fetch(step, 0)

    @pl.when(step + 1 < n_steps)
    def _():
        start_fetch(step + 1, (step + 1) % 2)

    slot = step % 2

    def wait_one(p, c):
        page_copy(step, p, slot).wait()
        return c
    lax.fori_loop(0, pps, wait_one, 0)

    for rc in range(rows // CMP_ROW_CHUNK):
        x = xbuf[slot, rc * CMP_ROW_CHUNK:(rc + 1) * CMP_ROW_CHUNK, :].astype(BF16)
        r0 = pl.multiple_of(h * rows + rc * CMP_ROW_CHUNK, CMP_ROW_CHUNK)
        p_scr[pl.ds(r0, CMP_ROW_CHUNK), :] = _dot(x, w1f_ref[...])

    @pl.when(h == steps_per_batch - 1)
    def _():
        o_ref[...] = _cmp_second_layer(p_scr[...], _cmp_bias(pe_ref, w1_ref, b1_ref), w2_ref)


def _cmp_sample(page_table, cache_seg, w1_full, pe_flat, cmp_w1, cmp_b1, w2_bd):
    bsz, n_pages = page_table.shape
    steps = n_pages // CMP_PAGES_PER_STEP
    n_seg = n_pages * SEGS_PER_PAGE
    rows = CMP_PAGES_PER_STEP * SEGS_PER_PAGE
    const = lambda a: pl.BlockSpec(a.shape, lambda b, h, pt: (0,) * a.ndim, pipeline_mode=pl.Buffered(1))
    grid_spec = pltpu.PrefetchScalarGridSpec(
        num_scalar_prefetch=1,
        grid=(bsz, steps),
        in_specs=[pl.BlockSpec(memory_space=pl.ANY), const(w1_full), const(pe_flat), const(cmp_w1),
                  const(cmp_b1), const(w2_bd)],
        out_specs=pl.BlockSpec((None, n_seg, KV_W), lambda b, h, pt: (b, 0, 0)),
        scratch_shapes=[pltpu.VMEM((2, rows, SEG_W), F32), pltpu.VMEM((n_seg, P_W), F32),
                        pltpu.SemaphoreType.DMA((2,))],
    )
    return pl.pallas_call(
        functools.partial(_cmp_sample_kernel, steps_per_batch=steps),
        grid_spec=grid_spec,
        out_shape=jax.ShapeDtypeStruct((bsz, n_seg, KV_W), F32),
        compiler_params=_cparams(("arbitrary", "arbitrary")),
        name="nsa_compress_sample",
    )(page_table, cache_seg, w1_full, pe_flat, cmp_w1, cmp_b1, w2_bd)


SQ_ROWS = SUBLANES


def _nsa_sample_select_kernel(q_ref, cmp_ref, oc_ref, idx_ref, *, past_len, nb, nb_pad, nc):
    tq_n, hd, grp = NSA_TQ, NSA_HEAD_DIM, NSA_GROUP
    m_rows = grp * tq_n
    q_all = q_ref[...]
    n_cmp = cmp_ref.shape[0]
    tq_rows = past_len + lax.broadcasted_iota(jnp.int32, (m_rows, 1), 0) % tq_n
    tq_lane = past_len + lax.broadcasted_iota(jnp.int32, (1, tq_n), 1)
    t_end = lax.broadcasted_iota(jnp.int32, (1, n_cmp), 1) * CMP_STRIDE + (CMP_BLK - 1)
    ov_t = _overlap_t(nb_pad, n_cmp, nc)
    cmp = cmp_ref[...]
    for g in range(NSA_KV_HEADS):
        qg = _group_queries(q_all, g)
        kl, vl = g * hd, (NSA_KV_HEADS + g) * hd
        s_c = _dot_nt(qg, cmp[:, kl:kl + hd].astype(BF16)) * QK_SCALE
        p_c = _masked_softmax(s_c, t_end <= tq_rows)
        o_c = _dot(p_c.astype(BF16), cmp[:, vl:vl + hd].astype(BF16))
        psum = p_c[0:tq_n]
        for r in range(1, grp):
            psum = psum + p_c[r * tq_n:(r + 1) * tq_n]
        score_t = _select_scores_t(_importance_t(psum, ov_t), tq_lane)
        rowid = lax.broadcasted_iota(jnp.int32, score_t.shape, 0)
        score_t = jnp.where(rowid < nb, score_t, -jnp.inf)
        _, picks = _topk_rows(score_t, min(SLC_TOP_N, nb))
        idx_ref[g] = jnp.concatenate(picks, axis=0)
        for r in range(grp):
            h = grp * g + r
            oc_ref[:, h * hd:(h + 1) * hd] = o_c[r * tq_n:r * tq_n + SQ_ROWS]


def _nsa_sample_select(q_pad, cmpkv, past_len, seq_new):
    bsz = q_pad.shape[0]
    n_seg = cmpkv.shape[1]
    nb = -(-(past_len + seq_new) // SLC_BLK)
    nb_pad = -(-nb // SUBLANES) * SUBLANES
    n_pick = min(SLC_TOP_N, nb)
    kern = functools.partial(_nsa_sample_select_kernel, past_len=past_len, nb=nb, nb_pad=nb_pad,
                             nc=n_seg - CMP_R + 1)
    return pl.pallas_call(
        kern,
        grid=(bsz,),
        in_specs=[pl.BlockSpec((None, NSA_TQ, NSA_WIDTH), lambda b: (b, 0, 0)),
                  pl.BlockSpec((None, n_seg, KV_W), lambda b: (b, 0, 0))],
        out_specs=[pl.BlockSpec((None, SQ_ROWS, NSA_WIDTH), lambda b: (b, 0, 0)),
                   pl.BlockSpec((None, NSA_KV_HEADS, n_pick, NSA_TQ), lambda b: (b, 0, 0, 0))],
        out_shape=[jax.ShapeDtypeStruct((bsz, SQ_ROWS, NSA_WIDTH), F32),
                   jax.ShapeDtypeStruct((bsz, NSA_KV_HEADS, n_pick, NSA_TQ), jnp.int32)],
        compiler_params=_cparams(("parallel",)),
        name="nsa_select_sample",
    )(q_pad, cmpkv)


def _nsa_sample_attend_kernel(pt_ref, idx_ref, q_ref, aux_ref, oc_ref, cache_ref, tail_ref, winp_ref, winn_ref,
                              o_ref, kbuf, sem, *, past_len, seq_new, n_pick):
    b = pl.program_id(0)
    nbatch = pl.num_programs(0)
    hd, grp = NSA_HEAD_DIM, NSA_GROUP
    n_items = seq_new * NSA_KV_HEADS * n_pick
    n_past_blocks = past_len // SLC_BLK
    blocks_per_page = PAGE_SIZE // SLC_BLK
    key_rows = n_pick * SLC_BLK

    def block_id(bb, item):
        return jnp.minimum(idx_ref[bb * n_items + item], n_past_blocks)

    def dst(slot, item):
        return kbuf.at[slot, item // n_pick, pl.ds((item % n_pick) * SLC_BLK, SLC_BLK), :]

    def start_fetch(bb, slot):
        def issue(item, c):
            blk = block_id(bb, item)

            @pl.when(blk < n_past_blocks)
            def _():
                page = pt_ref[bb, blk // blocks_per_page]
                src = cache_ref.at[page, pl.ds((blk % blocks_per_page) * SLC_BLK, SLC_BLK), :]
                pltpu.make_async_copy(src, dst(slot, item), sem.at[slot]).start()

            @pl.when(blk >= n_past_blocks)
            def _():
                pltpu.make_async_copy(tail_ref.at[bb], dst(slot, item), sem.at[slot]).start()
            return c
        lax.fori_loop(0, n_items, issue, 0)

    @pl.when(b == 0)
    def _():
        start_fetch(b, 0)

    @pl.when(b + 1 < nbatch)
    def _():
        start_fetch(b + 1, (b + 1) % 2)

    slot = b % 2

    def wait_one(item, c):
        pltpu.make_async_copy(tail_ref.at[b], dst(slot, item), sem.at[slot]).wait()
        return c
    lax.fori_loop(0, n_items, wait_one, 0)

    q_all = q_ref[...]
    gates = jax.nn.sigmoid(aux_ref[...])
    m_rows = grp * SQ_ROWS
    row_q = lax.broadcasted_iota(jnp.int32, (m_rows, 1), 0) % SQ_ROWS
    tq_rows = past_len + row_q
    key_slot = lax.broadcasted_iota(jnp.int32, (1, key_rows), 1) // SLC_BLK
    key_off = lax.broadcasted_iota(jnp.int32, (1, key_rows), 1) % SLC_BLK
    wb = winp_ref.shape[0]
    wn = winn_ref.shape[0]
    tw_p = past_len - wb + lax.broadcasted_iota(jnp.int32, (1, wb), 1)
    tw_n = past_len + lax.broadcasted_iota(jnp.int32, (1, wn), 1)

    for g in range(NSA_KV_HEADS):
        qg = _group_queries(q_all, g)
        kl, vl = g * hd, (NSA_KV_HEADS + g) * hd
        o_s = jnp.zeros((m_rows, hd), F32)
        for qi in range(seq_new):
            ks = kbuf[slot, qi * NSA_KV_HEADS + g, :, kl:kl + hd].astype(BF16)
            vs = kbuf[slot, qi * NSA_KV_HEADS + g, :, vl:vl + hd].astype(BF16)
            s = _dot_nt(qg, ks) * QK_SCALE
            base = jnp.zeros((1, key_rows), jnp.int32)
            for kk in range(n_pick):
                blk = block_id(b, (qi * NSA_KV_HEADS + g) * n_pick + kk)
                base = jnp.where(key_slot == kk, blk * SLC_BLK, base)
            p = _masked_softmax(s, base + key_off <= past_len + qi)
            o_s = jnp.where(row_q == qi, _dot(p.astype(BF16), vs), o_s)
        kp = winp_ref[:, kl:kl + hd].astype(BF16)
        vp = winp_ref[:, vl:vl + hd].astype(BF16)
        kn = winn_ref[:, kl:kl + hd].astype(BF16)
        vn = winn_ref[:, vl:vl + hd].astype(BF16)
        s_p = jnp.where((tw_p <= tq_rows) & (tw_p > tq_rows - WINDOW) & (tw_p >= 0),
                        _dot_nt(qg, kp) * QK_SCALE, NEG)
        s_n = jnp.where((tw_n <= tq_rows) & (tw_n > tq_rows - WINDOW) & (tw_n < past_len + seq_new),
                        _dot_nt(qg, kn) * QK_SCALE, NEG)
        m = jnp.maximum(jnp.max(s_p, axis=-1, keepdims=True), jnp.max(s_n, axis=-1, keepdims=True))
        e_p = jnp.where(s_p > 0.5 * NEG, jnp.exp(s_p - m), 0.0)
        e_n = jnp.where(s_n > 0.5 * NEG, jnp.exp(s_n - m), 0.0)
        den = jnp.maximum(jnp.sum(e_p, axis=-1, keepdims=True) + jnp.sum(e_n, axis=-1, keepdims=True), 1e-30)
        o_w = (_dot(e_p.astype(BF16), vp) + _dot(e_n.astype(BF16), vn)) / den
        for r in range(grp):
            h = grp * g + r
            rows = slice(r * SQ_ROWS, (r + 1) * SQ_ROWS)
            gc = GATE_OFF + 3 * h
            o = (gates[:, gc:gc + 1] * oc_ref[:, h * hd:(h + 1) * hd] + gates[:, gc + 1:gc + 2] * o_s[rows]
                 + gates[:, gc + 2:gc + 3] * o_w[rows])
            o_ref[:, h * hd:(h + 1) * hd] = o.astype(o_ref.dtype)


def _nsa_sample_attend(page_table, idx_flat, q_pad, aux_pad, o_c, cache_rows, tail, win_past, win_new,
                       past_len, seq_new, n_pick):
    bsz = q_pad.shape[0]
    wb, wn = win_past.shape[1], win_new.shape[1]
    kern = functools.partial(_nsa_sample_attend_kernel, past_len=past_len, seq_new=seq_new, n_pick=n_pick)
    per_b = lambda n, w: pl.BlockSpec((None, n, w), lambda b, pt, ix: (b, 0, 0))
    grid_spec = pltpu.PrefetchScalarGridSpec(
        num_scalar_prefetch=2,
        grid=(bsz,),
        in_specs=[per_b(SQ_ROWS, NSA_WIDTH), per_b(SQ_ROWS, AUX_W), per_b(SQ_ROWS, NSA_WIDTH),
                  pl.BlockSpec(memory_space=pl.ANY), pl.BlockSpec(memory_space=pl.ANY),
                  per_b(wb, KV_W), per_b(wn, KV_W)],
        out_specs=per_b(SQ_ROWS, NSA_WIDTH),
        scratch_shapes=[pltpu.VMEM((2, seq_new * NSA_KV_HEADS, n_pick * SLC_BLK, KV_W), F32),
                        pltpu.SemaphoreType.DMA((2,))],
    )
    return pl.pallas_call(
        kern,
        grid_spec=grid_spec,
        out_shape=jax.ShapeDtypeStruct((bsz, SQ_ROWS, NSA_WIDTH), BF16),
        compiler_params=_cparams(("arbitrary",)),
        name="nsa_attend_sample",
    )(page_table, idx_flat, q_pad, aux_pad, o_c, cache_rows, tail, win_past, win_new)


C_W = NSA_KV_HEADS * NSA_HEAD_DIM
HC_W = NSA_KV_HEADS * CMP_HIDDEN
PC_W = CMP_R * HC_W


def _cmp_weights_fm(cmp_pe, cmp_w1, cmp_w2):
    eye = jnp.eye(NSA_KV_HEADS, dtype=F32)
    w1r = cmp_w1.reshape(2, CMP_R, CMP_STRIDE, NSA_HEAD_DIM, CMP_HIDDEN)
    w1c = jnp.einsum('cmsdh,gG->csgdmGh', w1r, eye).reshape(2, CMP_STRIDE * C_W, PC_W).astype(BF16)
    w2c_t = jnp.einsum('chd,gG->cGdgh', cmp_w2, eye).reshape(2, C_W, HC_W).astype(BF16)
    pe_flat = jnp.transpose(cmp_pe, (1, 0, 2)).reshape(2, CMP_BLK * NSA_HEAD_DIM)
    return w1c, w2c_t, pe_flat


def _cmp_bias_c(pe_ref, w1_ref, b1_ref, c):
    pe = jnp.broadcast_to(pe_ref[c:c + 1, :], (SUBLANES, pe_ref.shape[1]))
    w1c = w1_ref[c]
    pe_hi = pe.astype(BF16)
    pe_mid = (pe - pe_hi.astype(F32)).astype(BF16)
    w_hi = w1c.astype(BF16)
    w_mid = (w1c - w_hi.astype(F32)).astype(BF16)
    pb = _dot(pe_hi, w_hi) + _dot(pe_mid, w_hi) + _dot(pe_hi, w_mid)
    bc = pb[0:1, :] + b1_ref[c:c + 1, :]
    return jnp.concatenate([bc] * NSA_KV_HEADS, axis=1)


def _cmp_first_layer(x_ref, c, row0, n_seg, w1c_ref):
    acc = None
    for s in range(CMP_STRIDE):
        xs = x_ref[c, pl.ds(row0 + s, n_seg, stride=CMP_STRIDE), :].astype(BF16)
        d = _dot(xs, w1c_ref[c, s * C_W:(s + 1) * C_W, :])
        acc = d if acc is None else acc + d
    return acc


def _cmp_second_layer_fm(p, bias, w2t):
    n = p.shape[0]
    h = p[:, :HC_W] + pltpu.roll(p[:, HC_W:], n - 1, 0) + bias
    return _dot_nt(w2t, jax.nn.gelu(h).astype(BF16))


def _cmp_prompt_fm_kernel(x_ref, w1c_ref, pe_ref, w1_ref, b1_ref, w2t_ref, o_ref):
    n_seg = o_ref.shape[1]
    for c in range(2):
        p = _cmp_first_layer(x_ref, c, 0, n_seg, w1c_ref)
        o_ref[c * C_W:(c + 1) * C_W, :] = _cmp_second_layer_fm(p, _cmp_bias_c(pe_ref, w1_ref, b1_ref, c), w2t_ref[c])


def _cmp_prompt_fm(x_tok, seq, w1c, pe_flat, cmp_w1, cmp_b1, w2c_t):
    bsz = x_tok.shape[1] // seq
    n_seg = seq // CMP_STRIDE
    const = lambda a: pl.BlockSpec(a.shape, lambda b: (0,) * a.ndim)
    return pl.pallas_call(
        _cmp_prompt_fm_kernel,
        grid=(bsz,),
        in_specs=[pl.BlockSpec((2, seq, C_W), lambda b: (0, b, 0)),
                  const(w1c), const(pe_flat), const(cmp_w1), const(cmp_b1), const(w2c_t)],
        out_specs=pl.BlockSpec((None, KV_W, n_seg), lambda b: (b, 0, 0)),
        out_shape=jax.ShapeDtypeStruct((bsz, KV_W, n_seg), F32),
        compiler_params=_cparams(("parallel",)),
        name="nsa_compress_prompt",
    )(x_tok, w1c, pe_flat, cmp_w1, cmp_b1, w2c_t)


def _scaled_group_queries(q, g):
    return (_group_queries(q, g).astype(F32) * QK_SCALE).astype(BF16)


def _cmp_branch_fm(qg, cmp_ref, g, valid):
    hd = NSA_HEAD_DIM
    kl, vl = g * hd, (NSA_KV_HEADS + g) * hd
    s_c = _dot(qg, cmp_ref[kl:kl + hd, :].astype(BF16))
    p_c = _masked_softmax(s_c, valid)
    return _dot_nt(p_c.astype(BF16), cmp_ref[vl:vl + hd, :].astype(BF16)), p_c


def _sum_heads(p, rows):
    out = p[0:rows]
    for r in range(1, NSA_GROUP):
        out = out + p[r * rows:(r + 1) * rows]
    return out


ATT_RB = 32


def _nsa_prompt_fm_kernel(q_ref, aux_ref, cmp_ref, kvs_ref, kvw_ref, o_ref,
                          s_scr, bias_scr, e_scr, m_scr, corr_scr, acc_scr, *, seq, nc):
    tq_n, tk_n, hd, grp = NSA_TQ, NSA_TK, NSA_HEAD_DIM, NSA_GROUP
    m_rows = grp * tq_n
    q0 = pl.program_id(1) * tq_n
    q_all = q_ref[...]
    gates = jax.nn.sigmoid(aux_ref[...])
    n_cmp = cmp_ref.shape[1]
    nb = seq // SLC_BLK

    tq_col = q0 + lax.broadcasted_iota(jnp.int32, (tq_n, 1), 0)
    tq_rows = jnp.concatenate([tq_col] * grp, axis=0)
    tq_lane = q0 + lax.broadcasted_iota(jnp.int32, (1, NSA_KV_HEADS * tq_n), 1) % tq_n
    t_end = lax.broadcasted_iota(jnp.int32, (1, n_cmp), 1) * CMP_STRIDE + (CMP_BLK - 1)
    ov_t = _overlap_t(nb, n_cmp, nc)
    n_chunks = (q0 + tq_n + tk_n - 1) // tk_n
    w_len = WINDOW + tq_n
    w0 = pl.multiple_of(jnp.maximum(q0 - WINDOW, 0), tq_n)

    qgs = [_scaled_group_queries(q_all, g) for g in range(NSA_KV_HEADS)]
    o_cs, imps = [], []
    for g in range(NSA_KV_HEADS):
        o_c, p_c = _cmp_branch_fm(qgs[g], cmp_ref, g, t_end <= tq_rows)
        o_cs.append(o_c)
        imps.append(_importance_t(_sum_heads(p_c, tq_n), ov_t))
    sel_all, _ = _topk_rows(_select_scores_t(jnp.concatenate(imps, axis=1), tq_lane), min(SLC_TOP_N, nb))

    tw = w0 + lax.broadcasted_iota(jnp.int32, (1, w_len), 1)
    bias_w = jnp.where((tw <= tq_col) & (tw > tq_col - WINDOW), 0.0, NEG)

    groups = range(NSA_KV_HEADS)

    def exp_rows(g, bias_g, width, online):
        per_head = tq_n // ATT_RB
        for i in range(m_rows // ATT_RB):
            rows = slice(i * ATT_RB, (i + 1) * ATT_RB)
            brows = slice((i % per_head) * ATT_RB, (i % per_head + 1) * ATT_RB)
            s = s_scr[g, rows, :width] + bias_scr[bias_g, brows, :width]
            m_new = jnp.max(s, axis=-1, keepdims=True)
            if online:
                m_old = m_scr[g, rows, :]
                m_new = jnp.maximum(m_old, m_new)
                corr_scr[g, rows, :] = jnp.exp2(m_old - m_new)
                m_scr[g, rows, :] = m_new
            e_scr[g, rows, :width] = jnp.exp2(s - m_new).astype(BF16)

    def keys_log2(ref, g, cols):
        return (ref[g * hd:(g + 1) * hd, cols] * LOG2E).astype(BF16)

    def values_with_ones(ref, g, cols):
        v = ref[(NSA_KV_HEADS + g) * hd:(NSA_KV_HEADS + g + 1) * hd, cols].astype(BF16)
        ones = (lax.broadcasted_iota(jnp.int32, v.shape, 0) == 0).astype(BF16)
        return jnp.concatenate([v, ones], axis=0)

    def normalised(acc):
        return acc[:, :hd] / jnp.maximum(acc[:, hd:hd + 1], 1e-30)

    m_scr[...] = jnp.full(m_scr.shape, NEG, F32)
    acc_scr[...] = jnp.zeros(acc_scr.shape, F32)
    sel_ts = [sel_all[:, g * tq_n:(g + 1) * tq_n].astype(BF16) for g in groups]

    def slc_chunk(ci, carry):
        k0 = pl.multiple_of(ci * tk_n, tk_n)
        cols = pl.ds(k0, tk_n)
        kpos = k0 + lax.broadcasted_iota(jnp.int32, (1, tk_n), 1)
        expand = (lax.broadcasted_iota(jnp.int32, (nb, tk_n), 0) == kpos // SLC_BLK).astype(BF16)
        vs = []
        for g in groups:
            sel_k = _dot_tn(sel_ts[g], expand)
            bias_scr[g, :, :tk_n] = jnp.where((sel_k > 0.5) & (kpos <= tq_col), 0.0, NEG)
            s_scr[g, :, :tk_n] = _dot(qgs[g], keys_log2(kvs_ref, g, cols))
            vs.append(values_with_ones(kvs_ref, g, cols))
        for g in groups:
            exp_rows(g, g, tk_n, True)
        for g in groups:
            acc_scr[g] = corr_scr[g] * acc_scr[g] + _dot_nt(e_scr[g, :, :tk_n], vs[g])
        return carry

    lax.fori_loop(0, n_chunks, slc_chunk, 0)
    o_ss = [normalised(acc_scr[g]) for g in groups]
    wcols = pl.ds(w0, w_len)
    bias_scr[0, :, :w_len] = bias_w
    for g in groups:
        s_scr[g, :, :w_len] = _dot(qgs[g], keys_log2(kvw_ref, g, wcols))
    for g in groups:
        exp_rows(g, 0, w_len, False)
    o_ws = [normalised(_dot_nt(e_scr[g, :, :w_len], values_with_ones(kvw_ref, g, wcols))) for g in groups]
    for g in groups:
        for r in range(grp):
            h = grp * g + r
            rows = slice(r * tq_n, (r + 1) * tq_n)
            gc = GATE_OFF + 3 * h
            o = (gates[:, gc:gc + 1] * o_cs[g][rows] + gates[:, gc + 1:gc + 2] * o_ss[g][rows]
                 + gates[:, gc + 2:gc + 3] * o_ws[g][rows])
            o_ref[:, h * hd:(h + 1) * hd] = o.astype(o_ref.dtype)


def _nsa_prompt_fm(q, aux, cmp_t, kvs_t, kvw_t):
    bsz, seq, _ = q.shape
    n_seg = cmp_t.shape[2]
    kern = functools.partial(_nsa_prompt_fm_kernel, seq=seq, nc=n_seg - CMP_R + 1)
    tile = lambda w: pl.BlockSpec((None, NSA_TQ, w), lambda b, t: (b, t, 0))
    whole = lambda n: pl.BlockSpec((None, KV_W, n), lambda b, t: (b, 0, 0))
    m_rows = NSA_GROUP * NSA_TQ
    width = max(NSA_TK, WINDOW + NSA_TQ)
    ng = NSA_KV_HEADS
    return pl.pallas_call(
        kern,
        grid=(bsz, seq // NSA_TQ),
        in_specs=[tile(NSA_WIDTH), tile(AUX_W), whole(n_seg), whole(seq), whole(seq)],
        out_specs=tile(NSA_WIDTH),
        out_shape=jax.ShapeDtypeStruct((bsz, seq, NSA_WIDTH), BF16),
        scratch_shapes=[pltpu.VMEM((ng, m_rows, width), F32),
                        pltpu.VMEM((ng, NSA_TQ, width), F32),
                        pltpu.VMEM((ng, m_rows, width), BF16),
                        pltpu.VMEM((ng, m_rows, 1), F32),
                        pltpu.VMEM((ng, m_rows, 1), F32),
                        pltpu.VMEM((ng, m_rows, 2 * NSA_HEAD_DIM), F32)],
        compiler_params=_cparams(("parallel", "arbitrary")),
        name="nsa_attn_prompt",
    )(q, aux, cmp_t, kvs_t, kvw_t)


def _cmp_sample_fm_kernel(pt_ref, cache_ref, w1f_ref, pe_ref, w1_ref, b1_ref, w2t_ref, o_ref,
                          xbuf, stage, p_scr, sem, *, steps_per_batch):
    b = pl.program_id(0)
    h = pl.program_id(1)
    step = b * steps_per_batch + h
    n_steps = pl.num_programs(0) * steps_per_batch
    pps = CMP_PAGES_PER_STEP
    segs = pps * SEGS_PER_PAGE

    def page_copy(bb, hh, p, slot):
        return pltpu.make_async_copy(cache_ref.at[pt_ref[bb, hh * pps + p]], xbuf.at[slot, p], sem.at[slot])

    def start_fetch(bb, hh, slot):
        for p in range(pps):
            page_copy(bb, hh, p, slot).start()

    @pl.when(step == 0)
    def _():
        start_fetch(b, h, 0)

    @pl.when(step + 1 < n_steps)
    def _():
        wrap = h + 1 == steps_per_batch
        start_fetch(jnp.where(wrap, b + 1, b), jnp.where(wrap, 0, h + 1), (step + 1) % 2)

    slot = step % 2
    for p in range(pps):
        page_copy(b, h, p, slot).wait()

    pages_per_chunk = CMP_ROW_CHUNK // SEGS_PER_PAGE
    for rc in range(segs // CMP_ROW_CHUNK):
        for p in range(rc * pages_per_chunk, (rc + 1) * pages_per_chunk):
            for c in range(2):
                stage[c, p * PAGE_SIZE:(p + 1) * PAGE_SIZE, :] = jnp.transpose(xbuf[slot, p, c * C_W:(c + 1) * C_W, :])
        for c in range(2):
            p_rows = _cmp_first_layer(stage, c, rc * CMP_ROW_CHUNK * CMP_STRIDE, CMP_ROW_CHUNK, w1f_ref)
            r0 = pl.multiple_of(h * segs + rc * CMP_ROW_CHUNK, CMP_ROW_CHUNK)
            p_scr[c, pl.ds(r0, CMP_ROW_CHUNK), :] = p_rows

    @pl.when(h == steps_per_batch - 1)
    def _():
        for c in range(2):
            o_ref[c * C_W:(c + 1) * C_W, :] = _cmp_second_layer_fm(
                p_scr[c], _cmp_bias_c(pe_ref, w1_ref, b1_ref, c), w2t_ref[c])


def _cmp_sample_fm(page_table, cache_fm, w1_full, pe_flat, cmp_w1, cmp_b1, w2_bd_t):
    bsz, n_pages = page_table.shape
    steps = n_pages // CMP_PAGES_PER_STEP
    n_seg = n_pages * SEGS_PER_PAGE
    const = lambda a: pl.BlockSpec(a.shape, lambda b, h, pt: (0,) * a.ndim, pipeline_mode=pl.Buffered(1))
    grid_spec = pltpu.PrefetchScalarGridSpec(
        num_scalar_prefetch=1,
        grid=(bsz, steps),
        in_specs=[pl.BlockSpec(memory_space=pl.ANY), const(w1_full), const(pe_flat), const(cmp_w1),
                  const(cmp_b1), const(w2_bd_t)],
        out_specs=pl.BlockSpec((None, KV_W, n_seg), lambda b, h, pt: (b, 0, 0)),
        scratch_shapes=[pltpu.VMEM((2, CMP_PAGES_PER_STEP, KV_W, PAGE_SIZE), F32),
                        pltpu.VMEM((2, CMP_PAGES_PER_STEP * PAGE_SIZE, C_W), F32),
                        pltpu.VMEM((2, n_seg, PC_W), F32),
                        pltpu.SemaphoreType.DMA((2,))],
    )
    return pl.pallas_call(
        functools.partial(_cmp_sample_fm_kernel, steps_per_batch=steps),
        grid_spec=grid_spec,
        out_shape=jax.ShapeDtypeStruct((bsz, KV_W, n_seg), F32),
        compiler_params=_cparams(("arbitrary", "arbitrary")),
        name="nsa_compress_sample",
    )(page_table, cache_fm, w1_full, pe_flat, cmp_w1, cmp_b1, w2_bd_t)


def _nsa_sample_select_fm_kernel(q_ref, cmp_ref, oc_ref, idx_ref, *, past_len, nb, nb_pad, nc):
    hd, grp = NSA_HEAD_DIM, NSA_GROUP
    m_rows = grp * SQ_ROWS
    q_all = q_ref[...]
    n_cmp = cmp_ref.shape[1]
    tq_rows = past_len + lax.broadcasted_iota(jnp.int32, (m_rows, 1), 0) % SQ_ROWS
    tq_lane = past_len + lax.broadcasted_iota(jnp.int32, (1, LANES), 1) % SQ_ROWS
    t_end = lax.broadcasted_iota(jnp.int32, (1, n_cmp), 1) * CMP_STRIDE + (CMP_BLK - 1)
    ov_t = _overlap_t(nb_pad, n_cmp, nc)
    psums = []
    for g in range(NSA_KV_HEADS):
        o_c, p_c = _cmp_branch_fm(_scaled_group_queries(q_all, g), cmp_ref, g, t_end <= tq_rows)
        psums.append(_sum_heads(p_c, SQ_ROWS))
        for r in range(grp):
            h = grp * g + r
            oc_ref[:, h * hd:(h + 1) * hd] = o_c[r * SQ_ROWS:(r + 1) * SQ_ROWS]
    psum = jnp.concatenate(psums + [jnp.zeros((LANES - NSA_KV_HEADS * SQ_ROWS, n_cmp), F32)], axis=0)
    score_t = _select_scores_t(_importance_t(psum, ov_t), tq_lane)
    rowid = lax.broadcasted_iota(jnp.int32, score_t.shape, 0)
    _, picks = _topk_rows(jnp.where(rowid < nb, score_t, -jnp.inf), min(SLC_TOP_N, nb))
    idx_ref[...] = jnp.concatenate(picks, axis=0)


def _nsa_sample_select_fm(q_pad, cmp_t, past_len, seq_new):
    bsz = q_pad.shape[0]
    n_seg = cmp_t.shape[2]
    nb = -(-(past_len + seq_new) // SLC_BLK)
    nb_pad = -(-nb // SUBLANES) * SUBLANES
    n_pick = min(SLC_TOP_N, nb)
    kern = functools.partial(_nsa_sample_select_fm_kernel, past_len=past_len, nb=nb, nb_pad=nb_pad,
                             nc=n_seg - CMP_R + 1)
    return pl.pallas_call(
        kern,
        grid=(bsz,),
        in_specs=[pl.BlockSpec((None, SQ_ROWS, NSA_WIDTH), lambda b: (b, 0, 0)),
                  pl.BlockSpec((None, KV_W, n_seg), lambda b: (b, 0, 0))],
        out_specs=[pl.BlockSpec((None, SQ_ROWS, NSA_WIDTH), lambda b: (b, 0, 0)),
                   pl.BlockSpec((None, n_pick, LANES), lambda b: (b, 0, 0))],
        out_shape=[jax.ShapeDtypeStruct((bsz, SQ_ROWS, NSA_WIDTH), F32),
                   jax.ShapeDtypeStruct((bsz, n_pick, LANES), jnp.int32)],
        compiler_params=_cparams(("parallel",)),
        name="nsa_select_sample",
    )(q_pad, cmp_t)


def _nsa_sample_attend_fm_kernel(pt_ref, idx_ref, q_ref, aux_ref, oc_ref, cache_ref, tail_ref, winp_ref,
                                 winn_ref, o_ref, kvbuf, sem, *, past_len, seq_new, n_pick):
    b = pl.program_id(0)
    nbatch = pl.num_programs(0)
    hd, grp = NSA_HEAD_DIM, NSA_GROUP
    n_items = seq_new * NSA_KV_HEADS * n_pick
    n_past_blocks = past_len // SLC_BLK
    blocks_per_page = PAGE_SIZE // SLC_BLK
    key_lanes = n_pick * PAGE_SIZE

    def block_id(bb, item):
        return jnp.minimum(idx_ref[bb * n_items + item], n_past_blocks)

    def copy(bb, item, slot):
        g = (item // n_pick) % NSA_KV_HEADS
        lanes = slice((item % n_pick) * PAGE_SIZE, (item % n_pick + 1) * PAGE_SIZE)
        rows = slice(g * hd, (g + 1) * hd)
        past_blk = jnp.minimum(block_id(bb, item), n_past_blocks - 1)
        page = pt_ref[bb, lax.shift_right_logical(past_blk, blocks_per_page.bit_length() - 1)]
        return pltpu.make_async_copy(cache_ref.at[page, :, rows, :], kvbuf.at[slot, item // n_pick, :, :, lanes],
                                     sem.at[slot])

    def start_fetch(bb, slot):
        for item in range(n_items):
            copy(bb, item, slot).start()

    @pl.when(b == 0)
    def _():
        start_fetch(b, 0)

    @pl.when(b + 1 < nbatch)
    def _():
        start_fetch(b + 1, (b + 1) % 2)

    slot = b % 2

    for item in range(n_items):
        copy(b, item, slot).wait()

    q_all = q_ref[...]
    gates = jax.nn.sigmoid(aux_ref[...])
    m_rows = grp * SQ_ROWS
    row_q = lax.broadcasted_iota(jnp.int32, (m_rows, 1), 0) % SQ_ROWS
    tq_rows = past_len + row_q
    lane = lax.broadcasted_iota(jnp.int32, (1, key_lanes), 1)
    key_slot, key_tok = lane // PAGE_SIZE, lane % PAGE_SIZE
    tok_new = lax.broadcasted_iota(jnp.int32, (1, PAGE_SIZE), 1)
    wb, wn = winp_ref.shape[1], winn_ref.shape[1]
    tw_p = past_len - wb + lax.broadcasted_iota(jnp.int32, (1, wb), 1)
    tw_n = past_len + lax.broadcasted_iota(jnp.int32, (1, wn), 1)
    bias_p = jnp.where((tw_p <= tq_rows) & (tw_p > tq_rows - WINDOW) & (tw_p >= 0), 0.0, NEG)
    bias_n = jnp.where((tw_n <= tq_rows) & (tw_n > tq_rows - WINDOW), 0.0, NEG)

    for g in range(NSA_KV_HEADS):
        qg = _scaled_group_queries(q_all, g)
        kl, vl = g * hd, (NSA_KV_HEADS + g) * hd
        o_s = jnp.zeros((m_rows, hd), F32)
        k_new = tail_ref[0, kl:kl + hd, :].astype(BF16)
        v_new = tail_ref[1, kl:kl + hd, :].astype(BF16)
        for qi in range(seq_new):
            qg_i = qi * NSA_KV_HEADS + g
            blk = jnp.zeros((1, key_lanes), jnp.int32)
            max_blk = jnp.zeros((1, PAGE_SIZE), jnp.int32)
            for kk in range(n_pick):
                blk_k = block_id(b, qg_i * n_pick + kk)
                blk = jnp.where(key_slot == kk, blk_k, blk)
                max_blk = jnp.maximum(max_blk, blk_k)
            mask = ((blk < n_past_blocks) & (key_tok // SLC_BLK == blk % blocks_per_page)
                    & (blk * SLC_BLK + key_tok % SLC_BLK <= past_len + qi))
            mask_new = (max_blk == n_past_blocks) & (tok_new < SLC_BLK) & (past_len + tok_new <= past_len + qi)
            s = _dot(qg, jnp.concatenate([kvbuf[slot, qg_i, 0].astype(BF16), k_new], axis=1))
            p = _masked_softmax(s, jnp.concatenate([mask, mask_new], axis=1))
            vcat = jnp.concatenate([kvbuf[slot, qg_i, 1].astype(BF16), v_new], axis=1)
            o_s = jnp.where(row_q == qi, _dot_nt(p.astype(BF16), vcat), o_s)
        s_p = _dot(qg, winp_ref[kl:kl + hd, :].astype(BF16)) + bias_p
        s_n = _dot(qg, winn_ref[kl:kl + hd, :].astype(BF16)) + bias_n
        m = jnp.maximum(jnp.max(s_p, axis=-1, keepdims=True), jnp.max(s_n, axis=-1, keepdims=True))
        e_p, e_n = jnp.exp(s_p - m), jnp.exp(s_n - m)
        den = jnp.maximum(jnp.sum(e_p, axis=-1, keepdims=True) + jnp.sum(e_n, axis=-1, keepdims=True), 1e-30)
        o_w = (_dot_nt(e_p.astype(BF16), winp_ref[vl:vl + hd, :].astype(BF16))
               + _dot_nt(e_n.astype(BF16), winn_ref[vl:vl + hd, :].astype(BF16))) / den
        for r in range(grp):
            h = grp * g + r
            rows = slice(r * SQ_ROWS, (r + 1) * SQ_ROWS)
            gc = GATE_OFF + 3 * h
            o = (gates[:, gc:gc + 1] * oc_ref[:, h * hd:(h + 1) * hd] + gates[:, gc + 1:gc + 2] * o_s[rows]
                 + gates[:, gc + 2:gc + 3] * o_w[rows])
            o_ref[:, h * hd:(h + 1) * hd] = o.astype(o_ref.dtype)


def _nsa_sample_attend_fm(page_table, idx_flat, q_pad, aux_pad, o_c, cache_fm, tail_fm, win_past, win_new,
                          past_len, seq_new, n_pick):
    bsz = q_pad.shape[0]
    wb, wn = win_past.shape[2], win_new.shape[2]
    kern = functools.partial(_nsa_sample_attend_fm_kernel, past_len=past_len, seq_new=seq_new, n_pick=n_pick)
    per_b = lambda n, w: pl.BlockSpec((None, n, w), lambda b, pt, ix: (b, 0, 0))
    n_qg = seq_new * NSA_KV_HEADS
    buf = pltpu.VMEM((2, n_qg, 2, NSA_HEAD_DIM, n_pick * PAGE_SIZE), F32)
    grid_spec = pltpu.PrefetchScalarGridSpec(
        num_scalar_prefetch=2,
        grid=(bsz,),
        in_specs=[per_b(SQ_ROWS, NSA_WIDTH), per_b(SQ_ROWS, AUX_W), per_b(SQ_ROWS, NSA_WIDTH),
                  pl.BlockSpec(memory_space=pl.ANY),
                  pl.BlockSpec((None, 2, C_W, LANES), lambda b, pt, ix: (b, 0, 0, 0)),
                  per_b(KV_W, wb), per_b(KV_W, wn)],
        out_specs=per_b(SQ_ROWS, NSA_WIDTH),
        scratch_shapes=[buf, pltpu.SemaphoreType.DMA((2,))],
    )
    return pl.pallas_call(
        kern,
        grid_spec=grid_spec,
        out_shape=jax.ShapeDtypeStruct((bsz, SQ_ROWS, NSA_WIDTH), BF16),
        compiler_params=_cparams(("arbitrary",)),
        name="nsa_attend_sample",
    )(page_table, idx_flat, q_pad, aux_pad, o_c, cache_fm, tail_fm, win_past, win_new)


PROMPT_TM = 512
GLA_TL = 512
SAMPLE_GLA_ROWS = 16
GLA_PROMPT_NBB = 2
GLA_SAMPLE_NBB = 8


def _pad_rows(x, n):
    return jnp.pad(x, ((0, 0), (0, n - x.shape[1]), (0, 0)))


def kernel(x_prompt, x_sample, state_gla, cache_cmp_kv, cache_slc_kv, cache_win_kv, page_table, c_prompt,
           c_sample, ln_in_g, ln_in_b, w_ada, b_ada, w_in, gla_w_a2, gla_b_a, gla_norm_g, cmp_pe, cmp_w1,
           cmp_b1, cmp_w2, w_o, ln1_g, ln1_b, w_ffn_in, w_ffn_out, ln2_g, ln2_b):
    assert w_in.shape[0] == DEPTH == 1
    l = 0
    bp, lp, d = x_prompt.shape
    bs, ls, _ = x_sample.shape
    n_pool = cache_cmp_kv.shape[1]
    n_pages = page_table.shape[1]
    past_len = n_pages * PAGE_SIZE
    wb = cache_win_kv.shape[2]
    assert ((past_len + ls) // CMP_STRIDE) * CMP_STRIDE <= past_len and past_len % SLC_BLK == 0
    assert ls <= SQ_ROWS and ls <= SLC_BLK and wb == WINDOW

    w_perm, w_kv_t = _permute_w_in(w_in[l])
    w_o_b, w_fi_b, w_fo_b = w_o[l].astype(BF16), w_ffn_in[l].astype(BF16), w_ffn_out[l].astype(BF16)
    w_a2p = jnp.zeros((AUX_W, GLA_KW), F32).at[:GLA_RANK].set(gla_w_a2[l])
    b_a = gla_b_a[l].reshape(1, GLA_KW)
    norm_g = gla_norm_g[l].reshape(1, GLA_DV)
    w1c, w2c_t, pe_flat = _cmp_weights_fm(cmp_pe[l], cmp_w1[l], cmp_w2[l])
    kvt = (2, NSA_KV_HEADS, NSA_HEAD_DIM)

    def fm_view(a):
        return jnp.transpose(a, (0, 2, 3, 4, 1)).reshape(a.shape[0], KV_W, a.shape[1])

    def tok_view(a_fm):
        n, _, t = a_fm.shape
        return jnp.transpose(a_fm.reshape((n,) + kvt + (t,)), (0, 4, 1, 2, 3))[None]

    mod = _ada(jnp.concatenate([c_prompt, c_sample], axis=0), w_ada[l], b_ada[l])
    mods_p = [m[:bp, None, :] for m in jnp.split(mod, 6, axis=-1)]
    mods_s = [jnp.repeat(m[bp:], ls, axis=0)[None] for m in jnp.split(mod, 6, axis=-1)]

    def out_ffn(x2d, o_g, o_n, mods, tm, rpm):
        sh1, sc1, ga1, sh2, sc2, ga2 = mods
        return _out_ffn(x2d, o_g, o_n, ga1, sc2, sh2, ga2, ln_in_g, ln_in_b, w_o_b, ln1_g[l], ln1_b[l], w_fi_b,
                        w_fo_b, ln2_g[l], ln2_b[l], tm, rpm)

    xp2 = x_prompt.reshape(bp * lp, d)
    rpm = lp // PROMPT_TM
    qk, v, r, qn, kvc, _, _, aux, kvc_t, kvs_t, kvw_t = _inproj(
        xp2, mods_p[1], mods_p[0], ln_in_g, ln_in_b, w_perm, w_kv_t, PROMPT_TM, rpm, seq_per_batch=lp)
    b3 = lambda a: a.reshape(bp, lp, a.shape[-1])
    o_g, s_p = _gla(b3(qk), b3(v), b3(r), b3(aux), w_a2p, b_a, norm_g, jnp.zeros((bp, GLA_KW, GLA_DV), F32),
                    chunk=GLA_CHUNK, sub=GLA_SUB, tl=GLA_TL, l_valid=lp, nbb=GLA_PROMPT_NBB)
    cmp_t_p = _cmp_prompt_fm(kvc, lp, w1c, pe_flat, cmp_w1[l], cmp_b1[l], w2c_t)
    o_n = _nsa_prompt_fm(b3(qn), b3(aux), cmp_t_p, kvs_t, kvw_t)
    y_p = out_ffn(xp2, o_g.reshape(bp * lp, GLA_WIDTH), o_n.reshape(bp * lp, NSA_WIDTH), mods_p, PROMPT_TM, rpm)
    w_keep = min(WINDOW, lp)
    outs_p = (y_p.reshape(bp, lp, d), s_p.reshape(1, bp, GLA_HEADS, GLA_DK, GLA_DV),
              tok_view(kvc_t), tok_view(kvs_t), tok_view(kvw_t[:, :, lp - w_keep:]))

    ts = bs * ls
    xs2 = x_sample.reshape(ts, d)
    qk, v, r, qn, kvc, kvs, kvw, aux = _inproj(xs2, mods_s[1], mods_s[0], ln_in_g, ln_in_b, w_perm, w_kv_t, ts, 1)
    s3 = lambda a: a.reshape(bs, ls, a.shape[-1])
    g16 = lambda a: _pad_rows(s3(a), SAMPLE_GLA_ROWS)
    o_g, s_s = _gla(g16(qk), g16(v), g16(r), g16(aux), w_a2p, b_a, norm_g, state_gla[l].reshape(bs, GLA_KW, GLA_DV),
                    chunk=SAMPLE_GLA_ROWS, sub=SAMPLE_GLA_ROWS, tl=SAMPLE_GLA_ROWS, l_valid=ls,
                    nbb=GLA_SAMPLE_NBB)
    o_g = o_g[:, :ls].reshape(ts, GLA_WIDTH)
    kvc = jnp.concatenate([kvc[0], kvc[1]], axis=1)
    cmp_t_s = _cmp_sample_fm(page_table, fm_view(cache_cmp_kv[l]), w1c, pe_flat, cmp_w1[l], cmp_b1[l], w2c_t)
    q_pad = _pad_rows(s3(qn), SQ_ROWS)
    o_c, idx = _nsa_sample_select_fm(q_pad, cmp_t_s, past_len, ls)
    n_pick = idx.shape[1]
    idx = idx[:, :, :NSA_KV_HEADS * SQ_ROWS].reshape(bs, n_pick, NSA_KV_HEADS, SQ_ROWS)[..., :ls]
    idx_flat = jnp.transpose(idx, (0, 3, 2, 1)).reshape(-1)
    new_fm =lambda a: jnp.pad(jnp.transpose(s3(a), (0, 2, 1)), ((0, 0), (0, 0), (0, LANES - ls)))
    win_past = fm_view(cache_win_kv[l])
    o_n = _nsa_sample_attend_fm(page_table, idx_flat, q_pad, _pad_rows(s3(aux), SQ_ROWS), o_c,
                                fm_view(cache_slc_kv[l]).reshape(n_pool, 2, C_W, PAGE_SIZE),
                                new_fm(kvs).reshape(bs, 2, C_W, LANES), win_past, new_fm(kvw),
                                past_len, ls, n_pick)
    o_n = o_n[:, :ls].reshape(ts, NSA_WIDTH)
    y_s = out_ffn(xs2, o_g, o_n, mods_s, ts, 1)
    win_s = jnp.concatenate([win_past[:, :, ls:], jnp.transpose(s3(kvw), (0, 2, 1))], axis=2)
    outs_s = (y_s.reshape(bs, ls, d), s_s.reshape(1, bs, GLA_HEADS, GLA_DK, GLA_DV),
              kvc.reshape((1, bs, ls) + kvt), kvs.reshape((1, bs, ls) + kvt), tok_view(win_s))

    return (outs_p[0], outs_s[0], outs_p[1], outs_s[1], outs_p[2], outs_s[2], outs_p[3], outs_s[3],
            outs_p[4], outs_s[4])
```

```python
import functools
import math

import numpy as np
import jax
import jax.numpy as jnp
from jax import lax
from jax.experimental import pallas as pl
from jax.experimental.pallas import tpu as pltpu

F32 = jnp.float32
BF16 = jnp.bfloat16

D_MODEL = 1024
DEPTH = 1
PAGE_SIZE = 128
GLA_HEADS = 4
GLA_DV = D_MODEL // (2 * GLA_HEADS)
GLA_DK = GLA_DV // 2
GLA_RANK = 16
GLA_TAU = 16.0
GLA_CHUNK = 64
GLA_SUB = 32
GLA_WIDTH = GLA_HEADS * GLA_DV
GLA_KW = GLA_HEADS * GLA_DK
NSA_HEADS = 8
NSA_KV_HEADS = 2
NSA_GROUP = NSA_HEADS // NSA_KV_HEADS
NSA_HEAD_DIM = D_MODEL // (2 * NSA_HEADS)
NSA_WIDTH = NSA_HEADS * NSA_HEAD_DIM
CMP_BLK = 32
CMP_STRIDE = 16
CMP_HIDDEN = 2 * NSA_HEAD_DIM
SLC_BLK = 64
SLC_TOP_N = 16
WINDOW = 512
FORCE_BONUS = 1e4
NEG = -1e30
D_FF = -(-8 * D_MODEL // (3 * 256)) * 256
ALPHA = (2 * DEPTH) ** 0.25
KV_W = 2 * NSA_KV_HEADS * NSA_HEAD_DIM
N_GATES = 3 * NSA_HEADS
IN_SIZES = (GLA_KW, GLA_KW, GLA_WIDTH, GLA_RANK, GLA_WIDTH, NSA_WIDTH, KV_W, KV_W, KV_W, N_GATES)
IN_WIDTH = sum(IN_SIZES)
LN_EPS = 1e-5

LANES = 128
SUBLANES = 8
VMEM_LIMIT_BYTES = 56 * 1024 * 1024

AUX_W = LANES
IN_GROUPS = (2 * GLA_KW, GLA_WIDTH, GLA_WIDTH, NSA_WIDTH, KV_W, KV_W, KV_W, AUX_W)
IN_PERM_W = sum(IN_GROUPS)
GATE_OFF = GLA_RANK


def _cparams(sem):
    return pltpu.CompilerParams(dimension_semantics=sem, vmem_limit_bytes=VMEM_LIMIT_BYTES)


def _split3(a):
    hi = a.astype(BF16)
    r1 = a - hi.astype(F32)
    mid = r1.astype(BF16)
    lo = (r1 - mid.astype(F32)).astype(BF16)
    return hi, mid, lo


def _dot(a, b):
    return jnp.dot(a, b, preferred_element_type=F32)


def _dot_nt(a, b):
    return lax.dot_general(a, b, (((1,), (1,)), ((), ())), preferred_element_type=F32)


def _dot_tn(a, b):
    return lax.dot_general(a, b, (((0,), (0,)), ((), ())), preferred_element_type=F32)


def _layer_norm(x, g, b):
    mu = jnp.mean(x, axis=-1, keepdims=True)
    xc = x - mu
    var = jnp.mean(xc * xc, axis=-1, keepdims=True)
    return xc * lax.rsqrt(var + LN_EPS) * g + b


def _ada_kernel(c_ref, w_ref, b_ref, o_ref):
    c = c_ref[...]
    a = (c * jax.nn.sigmoid(c)).astype(BF16)
    o_ref[...] = _dot(a, w_ref[...].astype(BF16)) + b_ref[...]


def _ada(c, w_ada, b_ada):
    n, d = c.shape
    m = w_ada.shape[1]
    tn = D_MODEL
    return pl.pallas_call(
        _ada_kernel,
        grid=(m // tn,),
        in_specs=[pl.BlockSpec((n, d), lambda j: (0, 0)),
                  pl.BlockSpec((d, tn), lambda j: (0, j)),
                  pl.BlockSpec((1, tn), lambda j: (0, j))],
        out_specs=pl.BlockSpec((n, tn), lambda j: (0, j)),
        out_shape=jax.ShapeDtypeStruct((n, m), F32),
        compiler_params=_cparams(("parallel",)),
        name="ada_mod",
    )(c, w_ada, b_ada.reshape(1, m))


N_KV_GROUPS = 3
KVC_GROUP = 4


def _inproj_kernel(x_ref, sc_ref, sh_ref, g_ref, b_ref, w_ref, wt_ref, *o_refs, emit_t):
    xn = _layer_norm(x_ref[...], g_ref[...], b_ref[...])
    u = (xn * (1.0 + sc_ref[...]) + sh_ref[...]).astype(BF16)
    lo = 0
    for j, (o_ref, wdt) in enumerate(zip(o_refs[:len(IN_GROUPS)], IN_GROUPS)):
        z = _dot(u, w_ref[:, lo:lo + wdt]).astype(o_ref.dtype)
        if j == KVC_GROUP:
            o_ref[0] = z[:, :wdt // 2]
            o_ref[1] = z[:, wdt // 2:]
        else:
            o_ref[...] = z
        lo += wdt
    if emit_t:
        for j, o_ref in enumerate(o_refs[len(IN_GROUPS):]):
            o_ref[...] = _dot_nt(wt_ref[j * KV_W:(j + 1) * KV_W, :], u)


_IN_OUT_DTYPES = (BF16, BF16, BF16, BF16, F32, F32, F32, F32)


def _inproj(x, sc, sh, ln_g, ln_b, w_perm, w_kv_t, tm, rows_per_mod, seq_per_batch=None):
    t, d = x.shape
    r = sc.shape[1]
    emit_t = seq_per_batch is not None
    mod_spec = pl.BlockSpec((None, r, d), lambda i: (i // rows_per_mod, 0, 0))
    out_specs = [pl.BlockSpec((tm, w), lambda i: (i, 0)) for w in IN_GROUPS]
    out_shape = [jax.ShapeDtypeStruct((t, w), dt) for w, dt in zip(IN_GROUPS, _IN_OUT_DTYPES)]
    out_specs[KVC_GROUP] = pl.BlockSpec((2, tm, KV_W // 2), lambda i: (0, i, 0))
    out_shape[KVC_GROUP] = jax.ShapeDtypeStruct((2, t, KV_W // 2), F32)
    if emit_t:
        tpb = seq_per_batch // tm
        out_specs += [pl.BlockSpec((None, KV_W, tm), lambda i: (i // tpb, 0, i % tpb))] * N_KV_GROUPS
        out_shape += [jax.ShapeDtypeStruct((t // seq_per_batch, KV_W, seq_per_batch), F32)] * N_KV_GROUPS
    return pl.pallas_call(
        functools.partial(_inproj_kernel, emit_t=emit_t),
        grid=(t // tm,),
        in_specs=[pl.BlockSpec((tm, d), lambda i: (i, 0)), mod_spec, mod_spec,
                  pl.BlockSpec((1, d), lambda i: (0, 0)), pl.BlockSpec((1, d), lambda i: (0, 0)),
                  pl.BlockSpec((d, IN_PERM_W), lambda i: (0, 0)),
                  pl.BlockSpec((N_KV_GROUPS * KV_W, d), lambda i: (0, 0))],
        out_specs=out_specs,
        out_shape=out_shape,
        compiler_params=_cparams(("parallel",)),
        name="ln_mod_inproj",
    )(x, sc, sh, ln_g.reshape(1, d), ln_b.reshape(1, d), w_perm, w_kv_t)


def _permute_w_in(w_in):
    q_g, k_g, v_g, a_g, r_g, q_n, kv_c, kv_s, kv_w, g_n = jnp.split(w_in, np.cumsum(IN_SIZES)[:-1], axis=1)
    pad = jnp.zeros((w_in.shape[0], AUX_W - GLA_RANK - N_GATES), w_in.dtype)
    w_perm = jnp.concatenate([q_g, k_g, v_g, r_g, q_n, kv_c, kv_s, kv_w, a_g, g_n, pad], axis=1).astype(BF16)
    w_kv_t = jnp.concatenate([kv_c, kv_s, kv_w], axis=1).T.astype(BF16)
    return w_perm, w_kv_t


FF_CHUNK = 256


def _out_ffn_kernel(x_ref, og_ref, on_ref, ga1_ref, sc2_ref, sh2_ref, ga2_ref, lng_ref, lnb_ref,
                    wo_ref, l1g_ref, l1b_ref, wfi_ref, wfo_ref, l2g_ref, l2b_ref, y_ref):
    x = _layer_norm(x_ref[...], lng_ref[...], lnb_ref[...])
    mix = _dot(og_ref[...], wo_ref[0:GLA_WIDTH, :]) + _dot(on_ref[...], wo_ref[GLA_WIDTH:, :])
    x1 = _layer_norm(ALPHA * x + ga1_ref[...] * mix, l1g_ref[...], l1b_ref[...])
    u2 = (x1 * (1.0 + sc2_ref[...]) + sh2_ref[...]).astype(BF16)
    ffn = jnp.zeros(x1.shape, F32)
    for c in range(D_FF // FF_CHUNK):
        lo = c * FF_CHUNK
        gate = _dot(u2, wfi_ref[:, lo:lo + FF_CHUNK])
        up = _dot(u2, wfi_ref[:, D_FF + lo:D_FF + lo + FF_CHUNK])
        f = (gate * jax.nn.sigmoid(gate) * up).astype(BF16)
        ffn = ffn + _dot(f, wfo_ref[lo:lo + FF_CHUNK, :])
    y_ref[...] = _layer_norm(ALPHA * x1 + ga2_ref[...] * ffn, l2g_ref[...], l2b_ref[...])


def _out_ffn(x, o_g, o_n, ga1, sc2, sh2, ga2, ln_in_g, ln_in_b, w_o, ln1_g, ln1_b, w_fi, w_fo, ln2_g, ln2_b,
             tm, rows_per_mod):
    t, d = x.shape
    r = ga1.shape[1]
    mod_spec = pl.BlockSpec((None, r, d), lambda i: (i // rows_per_mod, 0, 0))
    vec = lambda: pl.BlockSpec((1, d), lambda i: (0, 0))
    const = lambda shp: pl.BlockSpec(shp, lambda i: (0, 0), pipeline_mode=pl.Buffered(1))
    row = lambda a: a.reshape(1, d)
    return pl.pallas_call(
        _out_ffn_kernel,
        grid=(t // tm,),
        in_specs=[pl.BlockSpec((tm, d), lambda i: (i, 0)),
                  pl.BlockSpec((tm, GLA_WIDTH), lambda i: (i, 0)),
                  pl.BlockSpec((tm, NSA_WIDTH), lambda i: (i, 0)),
                  mod_spec, mod_spec, mod_spec, mod_spec, vec(), vec(),
                  const((d, d)), vec(), vec(), const((d, 2 * D_FF)), const((D_FF, d)), vec(), vec()],
        out_specs=pl.BlockSpec((tm, d), lambda i: (i, 0)),
        out_shape=jax.ShapeDtypeStruct((t, d), F32),
        compiler_params=_cparams(("parallel",)),
        name="outproj_ffn",
    )(x, o_g, o_n, ga1, sc2, sh2, ga2, row(ln_in_g), row(ln_in_b), w_o, row(ln1_g), row(ln1_b), w_fi, w_fo,
      row(ln2_g), row(ln2_b))


GLA_EXP_CLAMP = 80.0


def _gla_kernel(qk_ref, v_ref, r_ref, aux_ref, wa_ref, ba_ref, ng_ref, s0_ref, o_ref, sout_ref, s_scr,
                *, chunk, sub, tl, l_valid, l_pad, nbb):
    t = pl.program_id(1)
    c = chunk
    n_sub = c // sub
    hw = GLA_KW

    @pl.when(t == 0)
    def _():
        s_scr[...] = s0_ref[...]

    ri = lax.broadcasted_iota(jnp.int32, (c, c), 0)
    ci = lax.broadcasted_iota(jnp.int32, (c, c), 1)
    causal = ci <= ri
    tril = causal.astype(BF16)
    rowid = lax.broadcasted_iota(jnp.int32, (c, hw), 0)
    head_of_lane = lax.broadcasted_iota(jnp.int32, (c, hw), 1) // GLA_DK
    wa = wa_ref[...]
    wa_hi = wa.astype(BF16)
    wa_mid = (wa - wa_hi.astype(F32)).astype(BF16)
    ba = ba_ref[...]
    ng = ng_ref[...]

    def body(i, carry):
        for e in range(nbb):
            s_scr[e] = one_chunk(i, e, s_scr[e])
        return carry

    def one_chunk(i, e, s_old):
        r0 = pl.multiple_of(i * c, c)
        aux = aux_ref[e, pl.ds(r0, c), :]
        a_hi = aux.astype(BF16)
        a_mid = (aux - a_hi.astype(F32)).astype(BF16)
        z = _dot(a_hi, wa_hi) + _dot(a_mid, wa_hi) + _dot(a_hi, wa_mid) + ba
        g = jax.nn.log_sigmoid(z) / GLA_TAU
        if l_pad != l_valid:
            g = jnp.where(t * tl + r0 + rowid < l_valid, g, 0.0)
        g_hi, g_mid, g_lo = _split3(g)
        b = _dot(tril, g_hi) + _dot(tril, g_mid) + _dot(tril, g_lo)
        qk = qk_ref[e, pl.ds(r0, c), :]
        q = qk[:, :hw].astype(F32) * (GLA_DK ** -0.5)
        k = qk[:, hw:].astype(F32)
        v = v_ref[e, pl.ds(r0, c), :]
        b_last = b[c - 1:c, :]

        def heads_on_rows(x):
            return jnp.concatenate([jnp.where(head_of_lane == h, x, 0.0) for h in range(GLA_HEADS)], axis=0)

        o_inter =_dot(heads_on_rows(q * jnp.exp(b)).astype(BF16), s_old.astype(BF16))

        q_parts, k_parts = [], []
        for s_i in range(n_sub):
            b_ref = b[s_i * sub - 1:s_i * sub, :] if s_i > 0 else jnp.zeros((1, hw), F32)
            in_rows = (rowid >= s_i * sub) & (rowid < (s_i + 1) * sub)
            qt = jnp.where(in_rows, q * jnp.exp(jnp.minimum(b - b_ref, 0.0)), 0.0)
            kt = jnp.where(rowid < (s_i + 1) * sub, k * jnp.exp(jnp.minimum(b_ref - b, GLA_EXP_CLAMP)), 0.0)
            q_parts.append(heads_on_rows(qt).astype(BF16))
            k_parts.append(kt.astype(BF16))
        q_cat = jnp.concatenate(q_parts, axis=1) if n_sub > 1 else q_parts[0]
        k_cat = jnp.concatenate(k_parts, axis=1) if n_sub > 1 else k_parts[0]
        att = _dot_nt(q_cat, k_cat)

        r_t = r_ref[e, pl.ds(r0, c), :].astype(F32)
        for h in range(GLA_HEADS):
            att_h = jnp.where(causal, att[h * c:(h + 1) * c, :], 0.0).astype(BF16)
            o_h = o_inter[h * c:(h + 1) * c, :] + _dot(att_h, v[:, h * GLA_DV:(h + 1) * GLA_DV])
            o_h = o_h * lax.rsqrt(jnp.mean(o_h * o_h, axis=-1, keepdims=True) + 1e-6) * ng
            r_h = r_t[:, h * GLA_DV:(h + 1) * GLA_DV]
            o_ref[e, pl.ds(r0, c), h * GLA_DV:(h + 1) * GLA_DV] = (
                o_h * (r_h * jax.nn.sigmoid(r_h))).astype(o_ref.dtype)

        kd = jnp.concatenate([k * jnp.exp(b_last - b), jnp.zeros((LANES - c, hw), F32)], axis=0)
        kd_t = jnp.transpose(kd).astype(BF16)
        v_pad = jnp.concatenate([v, jnp.zeros((LANES - c, GLA_WIDTH), v.dtype)], axis=0)
        upd = jnp.concatenate([_dot(kd_t[h * GLA_DK:(h + 1) * GLA_DK, :], v_pad[:, h * GLA_DV:(h + 1) * GLA_DV])
                               for h in range(GLA_HEADS)], axis=0)
        decay = jnp.transpose(jnp.broadcast_to(jnp.exp(b_last), (LANES, hw)))
        return decay * s_old + upd

    lax.fori_loop(0, tl // c, body, 0)

    @pl.when(t == pl.num_programs(1) - 1)
    def _():
        sout_ref[...] = s_scr[...]


def _gla(qk, v, r, aux, w_a2p, b_a, norm_g, s0, *, chunk, sub, tl, l_valid, nbb):
    bsz, l_pad, _ = qk.shape
    nt = l_pad // tl
    kern = functools.partial(_gla_kernel, chunk=chunk, sub=sub, tl=tl, l_valid=l_valid, l_pad=l_pad, nbb=nbb)
    tile = lambda w: pl.BlockSpec((nbb, tl, w), lambda b, t: (b, t, 0))
    full = lambda shp: pl.BlockSpec(shp, lambda b, t: (0, 0))
    st = pl.BlockSpec((nbb, GLA_KW, GLA_DV), lambda b, t: (b, 0, 0))
    return pl.pallas_call(
        kern,
        grid=(bsz // nbb, nt),
        in_specs=[tile(2 * GLA_KW), tile(GLA_WIDTH), tile(GLA_WIDTH), tile(AUX_W),
                  full((AUX_W, GLA_KW)), full((1, GLA_KW)), full((1, GLA_DV)), st],
        out_specs=[tile(GLA_WIDTH), st],
        out_shape=[jax.ShapeDtypeStruct((bsz, l_pad, GLA_WIDTH), BF16),
                   jax.ShapeDtypeStruct((bsz, GLA_KW, GLA_DV), F32)],
        scratch_shapes=[pltpu.VMEM((nbb, GLA_KW, GLA_DV), F32)],
        compiler_params=_cparams(("parallel", "arbitrary")),
        name="gla_scan",
    )(qk, v, r, aux, w_a2p, b_a, norm_g, s0)


SEG_W = CMP_STRIDE * KV_W
CMP_R = CMP_BLK // CMP_STRIDE
P_W = CMP_R * 2 * NSA_KV_HEADS * CMP_HIDDEN
H_W = P_W // CMP_R


def _cmp_weights(cmp_pe, cmp_w1, cmp_b1, cmp_w2):
    eye = jnp.eye(2, dtype=F32)
    w1r = cmp_w1.reshape(2, CMP_R, CMP_STRIDE, NSA_HEAD_DIM, CMP_HIDDEN)
    w1_full = jnp.einsum('cmsdh,cC,gG->scgdmCGh', w1r, eye, eye).reshape(SEG_W, P_W).astype(BF16)
    w2_bd = jnp.einsum('chd,cC,gG->cghCGd', cmp_w2, eye, eye).reshape(H_W, KV_W).astype(BF16)
    pe_flat = jnp.transpose(cmp_pe, (1, 0, 2)).reshape(2, CMP_BLK * NSA_HEAD_DIM)
    return w1_full, w2_bd, pe_flat


def _cmp_bias(pe_ref, w1_ref, b1_ref):
    parts = []
    for c in range(2):
        pe = jnp.broadcast_to(pe_ref[c:c + 1, :], (SUBLANES, pe_ref.shape[1]))
        w1c = w1_ref[c]
        pe_hi = pe.astype(BF16)
        pe_mid = (pe - pe_hi.astype(F32)).astype(BF16)
        w_hi = w1c.astype(BF16)
        w_mid = (w1c - w_hi.astype(F32)).astype(BF16)
        pb = _dot(pe_hi, w_hi) + _dot(pe_mid, w_hi) + _dot(pe_hi, w_mid)
        bc = pb[0:1, :] + b1_ref[c:c + 1, :]
        parts += [bc] * NSA_KV_HEADS
    return jnp.concatenate(parts, axis=1)


def _cmp_second_layer(p, bias, w2_ref):
    n = p.shape[0]
    h = p[:, :H_W] + pltpu.roll(p[:, H_W:], n - 1, 0) + bias
    return _dot(jax.nn.gelu(h).astype(BF16), w2_ref[...])


def _cmp_prompt_kernel(x_ref, w1f_ref, pe_ref, w1_ref, b1_ref, w2_ref, o_ref):
    p = _dot(x_ref[...].astype(BF16), w1f_ref[...])
    o_ref[...] = _cmp_second_layer(p, _cmp_bias(pe_ref, w1_ref, b1_ref), w2_ref)


def _cmp_prompt(x_seg, w1_full, pe_flat, cmp_w1, cmp_b1, w2_bd):
    bsz, n_seg, _ = x_seg.shape
    const2 = lambda a: pl.BlockSpec(a.shape, lambda b: (0,) * a.ndim)
    return pl.pallas_call(
        _cmp_prompt_kernel,
        grid=(bsz,),
        in_specs=[pl.BlockSpec((None, n_seg, SEG_W), lambda b: (b, 0, 0)),
                  const2(w1_full), const2(pe_flat), const2(cmp_w1), const2(cmp_b1), const2(w2_bd)],
        out_specs=pl.BlockSpec((None, n_seg, KV_W), lambda b: (b, 0, 0)),
        out_shape=jax.ShapeDtypeStruct((bsz, n_seg, KV_W), F32),
        compiler_params=_cparams(("parallel",)),
        name="nsa_compress_prompt",
    )(x_seg, w1_full, pe_flat, cmp_w1, cmp_b1, w2_bd)


QK_SCALE = NSA_HEAD_DIM ** -0.5
LOG2E = math.log2(math.e)


def _masked_softmax(s, mask):
    s = jnp.where(mask, s, NEG)
    e = jnp.where(mask, jnp.exp(s - jnp.max(s, axis=-1, keepdims=True)), 0.0)
    return e / jnp.maximum(jnp.sum(e, axis=-1, keepdims=True), 1e-30)


def _group_queries(q, g):
    hd = NSA_HEAD_DIM
    return jnp.concatenate([q[:, (NSA_GROUP * g + r) * hd:(NSA_GROUP * g + r + 1) * hd]
                            for r in range(NSA_GROUP)], axis=0)


def _topk_rows(score_t, n_pick):
    nb, nq = score_t.shape
    rowid = lax.broadcasted_iota(jnp.int32, (nb, nq), 0)
    taken = jnp.zeros((nb, nq), jnp.int32)
    picks = []
    for _ in range(n_pick):
        free = taken == 0
        cand = jnp.where(free, score_t, -jnp.inf)
        m = jnp.max(cand, axis=0, keepdims=True)
        hit = free & (cand == m)
        idx = jnp.min(jnp.where(hit, rowid, nb), axis=0, keepdims=True)
        taken = jnp.where(rowid == idx, 1, taken)
        picks.append(idx)
    return taken.astype(F32), picks


def _importance_t(psum, ov_t):
    hi, mid, lo = _split3(psum)
    return _dot_nt(ov_t, hi) + _dot_nt(ov_t, mid) + _dot_nt(ov_t, lo)


def _overlap_t(nb, nc_pad, nc):
    j = lax.broadcasted_iota(jnp.int32, (nb, nc_pad), 0) * SLC_BLK
    i = lax.broadcasted_iota(jnp.int32, (nb, nc_pad), 1) * CMP_STRIDE
    return ((i < j + SLC_BLK) & (i + CMP_BLK > j) & (i < nc * CMP_STRIDE)).astype(BF16)


def _select_scores_t(imp_t, tq_row):
    nb, nq = imp_t.shape
    j = lax.broadcasted_iota(jnp.int32, (nb, nq), 0)
    cur = tq_row // SLC_BLK
    forced = (j == 0) | (j == cur) | (j == cur - 1)
    return jnp.where(j * SLC_BLK <= tq_row, imp_t + FORCE_BONUS * forced.astype(F32), -jnp.inf)


NSA_TQ = 128
NSA_TK = 512


def _nsa_prompt_kernel(q_ref, aux_ref, cmp_ref, kvs_ref, kvw_ref, o_ref, m_scr, l_scr, acc_scr, *, seq, nc):
    tq_n, tk_n, hd, grp = NSA_TQ, NSA_TK, NSA_HEAD_DIM, NSA_GROUP
    m_rows = grp * tq_n
    q0 = pl.program_id(1) * tq_n
    q_all = q_ref[...]
    gates = jax.nn.sigmoid(aux_ref[...])
    n_cmp = cmp_ref.shape[0]
    nb = seq // SLC_BLK

    tq_rows = q0 + lax.broadcasted_iota(jnp.int32, (m_rows, 1), 0) % tq_n
    tq_lane = q0 + lax.broadcasted_iota(jnp.int32, (1, tq_n), 1)
    t_end = lax.broadcasted_iota(jnp.int32, (1, n_cmp), 1) * CMP_STRIDE + (CMP_BLK - 1)
    ov_t = _overlap_t(nb, n_cmp, nc)
    n_chunks = (q0 + tq_n + tk_n - 1) // tk_n
    w_len = WINDOW + tq_n
    w0 = pl.multiple_of(jnp.maximum(q0 - WINDOW, 0), tq_n)

    for g in range(NSA_KV_HEADS):
        qg = _group_queries(q_all, g)
        kl, vl = g * hd, (NSA_KV_HEADS + g) * hd
        cmp = cmp_ref[...]
        s_c = _dot_nt(qg, cmp[:, kl:kl + hd].astype(BF16)) * QK_SCALE
        p_c = _masked_softmax(s_c, t_end <= tq_rows)
        o_c = _dot(p_c.astype(BF16), cmp[:, vl:vl + hd].astype(BF16))
        psum = p_c[0:tq_n]
        for r in range(1, grp):
            psum = psum + p_c[r * tq_n:(r + 1) * tq_n]
        sel_t, _ = _topk_rows(_select_scores_t(_importance_t(psum, ov_t), tq_lane), min(SLC_TOP_N, nb))
        sel_t = sel_t.astype(BF16)
        m_scr[...] = jnp.full((m_rows, 1), NEG, F32)
        l_scr[...] = jnp.zeros((m_rows, 1), F32)
        acc_scr[...] = jnp.zeros((m_rows, hd), F32)

        def slc_chunk(ci, carry):
            k0 = pl.multiple_of(ci * tk_n, tk_n)
            ks = kvs_ref[pl.ds(k0, tk_n), kl:kl + hd].astype(BF16)
            vs = kvs_ref[pl.ds(k0, tk_n), vl:vl + hd].astype(BF16)
            s = _dot_nt(qg, ks) * QK_SCALE
            kpos = k0 + lax.broadcasted_iota(jnp.int32, (1, tk_n), 1)
            blk = lax.broadcasted_iota(jnp.int32, (nb, tk_n), 0)
            expand = (blk == kpos // SLC_BLK).astype(BF16)
            sel_k = _dot_tn(sel_t, expand)
            sel_k = jnp.concatenate([sel_k] * grp, axis=0)
            mask = (sel_k > 0.5) & (kpos <= tq_rows)
            s = jnp.where(mask, s, NEG)
            m_old = m_scr[...]
            m_new = jnp.maximum(m_old, jnp.max(s, axis=-1, keepdims=True))
            e = jnp.where(mask, jnp.exp(s - m_new), 0.0)
            corr = jnp.exp(m_old - m_new)
            l_scr[...] = corr * l_scr[...] + jnp.sum(e, axis=-1, keepdims=True)
            acc_scr[...] = corr * acc_scr[...] + _dot(e.astype(BF16), vs)
            m_scr[...] = m_new
            return carry

        lax.fori_loop(0, n_chunks, slc_chunk, 0)
        o_s = acc_scr[...] / jnp.maximum(l_scr[...], 1e-30)
        kw = kvw_ref[pl.ds(w0, w_len), kl:kl + hd].astype(BF16)
        vw = kvw_ref[pl.ds(w0, w_len), vl:vl + hd].astype(BF16)
        s_w = _dot_nt(qg, kw) * QK_SCALE
        tw = w0 + lax.broadcasted_iota(jnp.int32, (1, w_len), 1)
        p_w = _masked_softmax(s_w, (tw <= tq_rows) & (tw > tq_rows - WINDOW))
        o_w = _dot(p_w.astype(BF16), vw)
        for r in range(grp):
            h = grp * g + r
            rows = slice(r * tq_n, (r + 1) * tq_n)
            gc = GATE_OFF + 3 * h
            o = (gates[:, gc:gc + 1] * o_c[rows] + gates[:, gc + 1:gc + 2] * o_s[rows]
                 + gates[:, gc + 2:gc + 3] * o_w[rows])
            o_ref[:, h * hd:(h + 1) * hd] = o.astype(o_ref.dtype)


def _nsa_prompt(q, aux, cmpkv, kv_s, kv_w):
    bsz, seq, _ = q.shape
    n_seg = cmpkv.shape[1]
    kern = functools.partial(_nsa_prompt_kernel, seq=seq, nc=n_seg - CMP_R + 1)
    tile = lambda w: pl.BlockSpec((None, NSA_TQ, w), lambda b, t: (b, t, 0))
    whole = lambda n, w: pl.BlockSpec((None, n, w), lambda b, t: (b, 0, 0))
    m_rows = NSA_GROUP * NSA_TQ
    return pl.pallas_call(
        kern,
        grid=(bsz, seq // NSA_TQ),
        in_specs=[tile(NSA_WIDTH), tile(AUX_W), whole(n_seg, KV_W), whole(seq, KV_W), whole(seq, KV_W)],
        out_specs=tile(NSA_WIDTH),
        out_shape=jax.ShapeDtypeStruct((bsz, seq, NSA_WIDTH), BF16),
        scratch_shapes=[pltpu.VMEM((m_rows, 1), F32), pltpu.VMEM((m_rows, 1), F32),
                        pltpu.VMEM((m_rows, NSA_HEAD_DIM), F32)],
        compiler_params=_cparams(("parallel", "arbitrary")),
        name="nsa_attn_prompt",
    )(q, aux, cmpkv, kv_s, kv_w)


SEGS_PER_PAGE = PAGE_SIZE // CMP_STRIDE
CMP_PAGES_PER_STEP = 64
CMP_ROW_CHUNK = 256


def _cmp_sample_kernel(pt_ref, cache_ref, w1f_ref, pe_ref, w1_ref, b1_ref, w2_ref, o_ref, xbuf, p_scr, sem,
                       *, steps_per_batch):
    b = pl.program_id(0)
    h = pl.program_id(1)
    step = b * steps_per_batch + h
    n_steps = pl.num_programs(0) * steps_per_batch
    pps = CMP_PAGES_PER_STEP
    rows = pps * SEGS_PER_PAGE

    def page_copy(s, p, slot):
        page = pt_ref[s // steps_per_batch, (s % steps_per_batch) * pps + p]
        return pltpu.make_async_copy(cache_ref.at[page], xbuf.at[slot, pl.ds(p * SEGS_PER_PAGE, SEGS_PER_PAGE), :],
                                     sem.at[slot])

    def start_fetch(s, slot):
        def issue(p, c):
            page_copy(s, p, slot).start()
            return c
        lax.fori_loop(0, pps, issue, 0)

    @pl.when(step == 0)
    def _():
        start_fetch(step, 0)

    @pl.when(step + 1 < n_steps)
    def _():
        start_fetch(step + 1, (step + 1) % 2)

    slot = step % 2

    def wait_one(p, c):
        page_copy(step, p, slot).wait()
        return c
    lax.fori_loop(0, pps, wait_one, 0)

    for rc in range(rows // CMP_ROW_CHUNK):
        x = xbuf[slot, rc * CMP_ROW_CHUNK:(rc + 1) * CMP_ROW_CHUNK, :].astype(BF16)
        r0 = pl.multiple_of(h * rows + rc * CMP_ROW_CHUNK, CMP_ROW_CHUNK)
        p_scr[pl.ds(r0, CMP_ROW_CHUNK), :] = _dot(x, w1f_ref[...])

    @pl.when(h == steps_per_batch - 1)
    def _():
        o_ref[...] = _cmp_second_layer(p_scr[...], _cmp_bias(pe_ref, w1_ref, b1_ref), w2_ref)


def _cmp_sample(page_table, cache_seg, w1_full, pe_flat, cmp_w1, cmp_b1, w2_bd):
    bsz, n_pages = page_table.shape
    steps = n_pages // CMP_PAGES_PER_STEP
    n_seg = n_pages * SEGS_PER_PAGE
    rows = CMP_PAGES_PER_STEP * SEGS_PER_PAGE
    const = lambda a: pl.BlockSpec(a.shape, lambda b, h, pt: (0,) * a.ndim, pipeline_mode=pl.Buffered(1))
    grid_spec = pltpu.PrefetchScalarGridSpec(
        num_scalar_prefetch=1,
        grid=(bsz, steps),
        in_specs=[pl.BlockSpec(memory_space=pl.ANY), const(w1_full), const(pe_flat), const(cmp_w1),
                  const(cmp_b1), const(w2_bd)],
        out_specs=pl.BlockSpec((None, n_seg, KV_W), lambda b, h, pt: (b, 0, 0)),
        scratch_shapes=[pltpu.VMEM((2, rows, SEG_W), F32), pltpu.VMEM((n_seg, P_W), F32),
                        pltpu.SemaphoreType.DMA((2,))],
    )
    return pl.pallas_call(
        functools.partial(_cmp_sample_kernel, steps_per_batch=steps),
        grid_spec=grid_spec,
        out_shape=jax.ShapeDtypeStruct((bsz, n_seg, KV_W), F32),
        compiler_params=_cparams(("arbitrary", "arbitrary")),
        name="nsa_compress_sample",
    )(page_table, cache_seg, w1_full, pe_flat, cmp_w1, cmp_b1, w2_bd)


SQ_ROWS = SUBLANES


def _nsa_sample_select_kernel(q_ref, cmp_ref, oc_ref, idx_ref, *, past_len, nb, nb_pad, nc):
    tq_n, hd, grp = NSA_TQ, NSA_HEAD_DIM, NSA_GROUP
    m_rows = grp * tq_n
    q_all = q_ref[...]
    n_cmp = cmp_ref.shape[0]
    tq_rows = past_len + lax.broadcasted_iota(jnp.int32, (m_rows, 1), 0) % tq_n
    tq_lane = past_len + lax.broadcasted_iota(jnp.int32, (1, tq_n), 1)
    t_end = lax.broadcasted_iota(jnp.int32, (1, n_cmp), 1) * CMP_STRIDE + (CMP_BLK - 1)
    ov_t = _overlap_t(nb_pad, n_cmp, nc)
    cmp = cmp_ref[...]
    for g in range(NSA_KV_HEADS):
        qg = _group_queries(q_all, g)
        kl, vl = g * hd, (NSA_KV_HEADS + g) * hd
        s_c = _dot_nt(qg, cmp[:, kl:kl + hd].astype(BF16)) * QK_SCALE
        p_c = _masked_softmax(s_c, t_end <= tq_rows)
        o_c = _dot(p_c.astype(BF16), cmp[:, vl:vl + hd].astype(BF16))
        psum = p_c[0:tq_n]
        for r in range(1, grp):
            psum = psum + p_c[r * tq_n:(r + 1) * tq_n]
        score_t = _select_scores_t(_importance_t(psum, ov_t), tq_lane)
        rowid = lax.broadcasted_iota(jnp.int32, score_t.shape, 0)
        score_t = jnp.where(rowid < nb, score_t, -jnp.inf)
        _, picks = _topk_rows(score_t, min(SLC_TOP_N, nb))
        idx_ref[g] = jnp.concatenate(picks, axis=0)
        for r in range(grp):
            h = grp * g + r
            oc_ref[:, h * hd:(h + 1) * hd] = o_c[r * tq_n:r * tq_n + SQ_ROWS]


def _nsa_sample_select(q_pad, cmpkv, past_len, seq_new):
    bsz = q_pad.shape[0]
    n_seg = cmpkv.shape[1]
    nb = -(-(past_len + seq_new) // SLC_BLK)
    nb_pad = -(-nb // SUBLANES) * SUBLANES
    n_pick = min(SLC_TOP_N, nb)
    kern = functools.partial(_nsa_sample_select_kernel, past_len=past_len, nb=nb, nb_pad=nb_pad,
                             nc=n_seg - CMP_R + 1)
    return pl.pallas_call(
        kern,
        grid=(bsz,),
        in_specs=[pl.BlockSpec((None, NSA_TQ, NSA_WIDTH), lambda b: (b, 0, 0)),
                  pl.BlockSpec((None, n_seg, KV_W), lambda b: (b, 0, 0))],
        out_specs=[pl.BlockSpec((None, SQ_ROWS, NSA_WIDTH), lambda b: (b, 0, 0)),
                   pl.BlockSpec((None, NSA_KV_HEADS, n_pick, NSA_TQ), lambda b: (b, 0, 0, 0))],
        out_shape=[jax.ShapeDtypeStruct((bsz, SQ_ROWS, NSA_WIDTH), F32),
                   jax.ShapeDtypeStruct((bsz, NSA_KV_HEADS, n_pick, NSA_TQ), jnp.int32)],
        compiler_params=_cparams(("parallel",)),
        name="nsa_select_sample",
    )(q_pad, cmpkv)


def _nsa_sample_attend_kernel(pt_ref, idx_ref, q_ref, aux_ref, oc_ref, cache_ref, tail_ref, winp_ref, winn_ref,
                              o_ref, kbuf, sem, *, past_len, seq_new, n_pick):
    b = pl.program_id(0)
    nbatch = pl.num_programs(0)
    hd, grp = NSA_HEAD_DIM, NSA_GROUP
    n_items = seq_new * NSA_KV_HEADS * n_pick
    n_past_blocks = past_len // SLC_BLK
    blocks_per_page = PAGE_SIZE // SLC_BLK
    key_rows = n_pick * SLC_BLK

    def block_id(bb, item):
        return jnp.minimum(idx_ref[bb * n_items + item], n_past_blocks)

    def dst(slot, item):
        return kbuf.at[slot, item // n_pick, pl.ds((item % n_pick) * SLC_BLK, SLC_BLK), :]

    def start_fetch(bb, slot):
        def issue(item, c):
            blk = block_id(bb, item)

            @pl.when(blk < n_past_blocks)
            def _():
                page = pt_ref[bb, blk // blocks_per_page]
                src = cache_ref.at[page, pl.ds((blk % blocks_per_page) * SLC_BLK, SLC_BLK), :]
                pltpu.make_async_copy(src, dst(slot, item), sem.at[slot]).start()

            @pl.when(blk >= n_past_blocks)
            def _():
                pltpu.make_async_copy(tail_ref.at[bb], dst(slot, item), sem.at[slot]).start()
            return c
        lax.fori_loop(0, n_items, issue, 0)

    @pl.when(b == 0)
    def _():
        start_fetch(b, 0)

    @pl.when(b + 1 < nbatch)
    def _():
        start_fetch(b + 1, (b + 1) % 2)

    slot = b % 2

    def wait_one(item, c):
        pltpu.make_async_copy(tail_ref.at[b], dst(slot, item), sem.at[slot]).wait()
        return c
    lax.fori_loop(0, n_items, wait_one, 0)

    q_all = q_ref[...]
    gates = jax.nn.sigmoid(aux_ref[...])
    m_rows = grp * SQ_ROWS
    row_q = lax.broadcasted_iota(jnp.int32, (m_rows, 1), 0) % SQ_ROWS
    tq_rows = past_len + row_q
    key_slot = lax.broadcasted_iota(jnp.int32, (1, key_rows), 1) // SLC_BLK
    key_off = lax.broadcasted_iota(jnp.int32, (1, key_rows), 1) % SLC_BLK
    wb = winp_ref.shape[0]
    wn = winn_ref.shape[0]
    tw_p = past_len - wb + lax.broadcasted_iota(jnp.int32, (1, wb), 1)
    tw_n = past_len + lax.broadcasted_iota(jnp.int32, (1, wn), 1)

    for g in range(NSA_KV_HEADS):
        qg = _group_queries(q_all, g)
        kl, vl = g * hd, (NSA_KV_HEADS + g) * hd
        o_s = jnp.zeros((m_rows, hd), F32)
        for qi in range(seq_new):
            ks = kbuf[slot, qi * NSA_KV_HEADS + g, :, kl:kl + hd].astype(BF16)
            vs = kbuf[slot, qi * NSA_KV_HEADS + g, :, vl:vl + hd].astype(BF16)
            s = _dot_nt(qg, ks) * QK_SCALE
            base = jnp.zeros((1, key_rows), jnp.int32)
            for kk in range(n_pick):
                blk = block_id(b, (qi * NSA_KV_HEADS + g) * n_pick + kk)
                base = jnp.where(key_slot == kk, blk * SLC_BLK, base)
            p = _masked_softmax(s, base + key_off <= past_len + qi)
            o_s = jnp.where(row_q == qi, _dot(p.astype(BF16), vs), o_s)
        kp = winp_ref[:, kl:kl + hd].astype(BF16)
        vp = winp_ref[:, vl:vl + hd].astype(BF16)
        kn = winn_ref[:, kl:kl + hd].astype(BF16)
        vn = winn_ref[:, vl:vl + hd].astype(BF16)
        s_p = jnp.where((tw_p <= tq_rows) & (tw_p > tq_rows - WINDOW) & (tw_p >= 0),
                        _dot_nt(qg, kp) * QK_SCALE, NEG)
        s_n = jnp.where((tw_n <= tq_rows) & (tw_n > tq_rows - WINDOW) & (tw_n < past_len + seq_new),
                        _dot_nt(qg, kn) * QK_SCALE, NEG)
        m = jnp.maximum(jnp.max(s_p, axis=-1, keepdims=True), jnp.max(s_n, axis=-1, keepdims=True))
        e_p = jnp.where(s_p > 0.5 * NEG, jnp.exp(s_p - m), 0.0)
        e_n = jnp.where(s_n > 0.5 * NEG, jnp.exp(s_n - m), 0.0)
        den = jnp.maximum(jnp.sum(e_p, axis=-1, keepdims=True) + jnp.sum(e_n, axis=-1, keepdims=True), 1e-30)
        o_w = (_dot(e_p.astype(BF16), vp) + _dot(e_n.astype(BF16), vn)) / den
        for r in range(grp):
            h = grp * g + r
            rows = slice(r * SQ_ROWS, (r + 1) * SQ_ROWS)
            gc = GATE_OFF + 3 * h
            o = (gates[:, gc:gc + 1] * oc_ref[:, h * hd:(h + 1) * hd] + gates[:, gc + 1:gc + 2] * o_s[rows]
                 + gates[:, gc + 2:gc + 3] * o_w[rows])
            o_ref[:, h * hd:(h + 1) * hd] = o.astype(o_ref.dtype)


def _nsa_sample_attend(page_table, idx_flat, q_pad, aux_pad, o_c, cache_rows, tail, win_past, win_new,
                       past_len, seq_new, n_pick):
    bsz = q_pad.shape[0]
    wb, wn = win_past.shape[1], win_new.shape[1]
    kern = functools.partial(_nsa_sample_attend_kernel, past_len=past_len, seq_new=seq_new, n_pick=n_pick)
    per_b = lambda n, w: pl.BlockSpec((None, n, w), lambda b, pt, ix: (b, 0, 0))
    grid_spec = pltpu.PrefetchScalarGridSpec(
        num_scalar_prefetch=2,
        grid=(bsz,),
        in_specs=[per_b(SQ_ROWS, NSA_WIDTH), per_b(SQ_ROWS, AUX_W), per_b(SQ_ROWS, NSA_WIDTH),
                  pl.BlockSpec(memory_space=pl.ANY), pl.BlockSpec(memory_space=pl.ANY),
                  per_b(wb, KV_W), per_b(wn, KV_W)],
        out_specs=per_b(SQ_ROWS, NSA_WIDTH),
        scratch_shapes=[pltpu.VMEM((2, seq_new * NSA_KV_HEADS, n_pick * SLC_BLK, KV_W), F32),
                        pltpu.SemaphoreType.DMA((2,))],
    )
    return pl.pallas_call(
        kern,
        grid_spec=grid_spec,
        out_shape=jax.ShapeDtypeStruct((bsz, SQ_ROWS, NSA_WIDTH), BF16),
        compiler_params=_cparams(("arbitrary",)),
        name="nsa_attend_sample",
    )(page_table, idx_flat, q_pad, aux_pad, o_c, cache_rows, tail, win_past, win_new)


C_W = NSA_KV_HEADS * NSA_HEAD_DIM
HC_W = NSA_KV_HEADS * CMP_HIDDEN
PC_W = CMP_R * HC_W
MXU_DEPTH = 256
CMP_S_PER_DOT = MXU_DEPTH // C_W


def _cmp_weights_fm(cmp_pe, cmp_w1, cmp_w2):
    eye = jnp.eye(NSA_KV_HEADS, dtype=F32)
    w1r = cmp_w1.reshape(2, CMP_R, CMP_STRIDE, NSA_HEAD_DIM, CMP_HIDDEN)
    w1c = jnp.einsum('cmsdh,gG->csgdmGh', w1r, eye).reshape(2, CMP_STRIDE * C_W, PC_W).astype(BF16)
    w2c_t = jnp.einsum('chd,gG->cGdgh', cmp_w2, eye).reshape(2, C_W, HC_W).astype(BF16)
    pe_flat = jnp.transpose(cmp_pe, (1, 0, 2)).reshape(2, CMP_BLK * NSA_HEAD_DIM)
    return w1c, w2c_t, pe_flat


def _cmp_bias_c(pe_ref, w1_ref, b1_ref, c):
    pe = jnp.broadcast_to(pe_ref[c:c + 1, :], (SUBLANES, pe_ref.shape[1]))
    w1c = w1_ref[c]
    pe_hi = pe.astype(BF16)
    pe_mid = (pe - pe_hi.astype(F32)).astype(BF16)
    w_hi = w1c.astype(BF16)
    w_mid = (w1c - w_hi.astype(F32)).astype(BF16)
    pb = _dot(pe_hi, w_hi) + _dot(pe_mid, w_hi) + _dot(pe_hi, w_mid)
    bc = pb[0:1, :] + b1_ref[c:c + 1, :]
    return jnp.concatenate([bc] * NSA_KV_HEADS, axis=1)


def _cmp_first_layer(x_ref, c, row0, n_seg, w1c_ref):
    acc = None
    for s in range(0, CMP_STRIDE, CMP_S_PER_DOT):
        xs = jnp.concatenate([x_ref[c, pl.ds(row0 + s + j, n_seg, stride=CMP_STRIDE), :].astype(BF16)
                              for j in range(CMP_S_PER_DOT)], axis=1)
        d = _dot(xs, w1c_ref[c, s * C_W:(s + CMP_S_PER_DOT) * C_W, :])
        acc = d if acc is None else acc + d
    return acc


def _cmp_second_layer_fm(p, bias, w2t):
    n = p.shape[0]
    h = p[:, :HC_W] + pltpu.roll(p[:, HC_W:], n - 1, 0) + bias
    return _dot_nt(w2t, jax.nn.gelu(h).astype(BF16))


def _cmp_prompt_fm_kernel(x_ref, w1c_ref, pe_ref, w1_ref, b1_ref, w2t_ref, o_ref):
    n_seg = o_ref.shape[1]
    for c in range(2):
        p = _cmp_first_layer(x_ref, c, 0, n_seg, w1c_ref)
        o_ref[c * C_W:(c + 1) * C_W, :] = _cmp_second_layer_fm(p, _cmp_bias_c(pe_ref, w1_ref, b1_ref, c), w2t_ref[c])


def _cmp_prompt_fm(x_tok, seq, w1c, pe_flat, cmp_w1, cmp_b1, w2c_t):
    bsz = x_tok.shape[1] // seq
    n_seg = seq // CMP_STRIDE
    const = lambda a: pl.BlockSpec(a.shape, lambda b: (0,) * a.ndim)
    return pl.pallas_call(
        _cmp_prompt_fm_kernel,
        grid=(bsz,),
        in_specs=[pl.BlockSpec((2, seq, C_W), lambda b: (0, b, 0)),
                  const(w1c), const(pe_flat), const(cmp_w1), const(cmp_b1), const(w2c_t)],
        out_specs=pl.BlockSpec((None, KV_W, n_seg), lambda b: (b, 0, 0)),
        out_shape=jax.ShapeDtypeStruct((bsz, KV_W, n_seg), F32),
        compiler_params=_cparams(("parallel",)),
        name="nsa_compress_prompt",
    )(x_tok, w1c, pe_flat, cmp_w1, cmp_b1, w2c_t)


def _scaled_group_queries(q, g):
    return (_group_queries(q, g).astype(F32) * QK_SCALE).astype(BF16)


def _cmp_branch_fm(qg, cmp_ref, g, valid):
    hd = NSA_HEAD_DIM
    kl, vl = g * hd, (NSA_KV_HEADS + g) * hd
    s_c = _dot(qg, cmp_ref[kl:kl + hd, :].astype(BF16))
    p_c = _masked_softmax(s_c, valid)
    return _dot_nt(p_c.astype(BF16), cmp_ref[vl:vl + hd, :].astype(BF16)), p_c


def _sum_heads(p, rows):
    out = p[0:rows]
    for r in range(1, NSA_GROUP):
        out = out + p[r * rows:(r + 1) * rows]
    return out


ATT_RB = 32


def _nsa_prompt_fm_kernel(q_ref, aux_ref, cmp_ref, kvs_ref, kvw_ref, o_ref,
                          s_scr, bias_scr, e_scr, m_scr, corr_scr, acc_scr, *, seq, nc):
    tq_n, tk_n, hd, grp = NSA_TQ, NSA_TK, NSA_HEAD_DIM, NSA_GROUP
    m_rows = grp * tq_n
    q0 = pl.program_id(1) * tq_n
    q_all = q_ref[...]
    gates = jax.nn.sigmoid(aux_ref[...])
    n_cmp = cmp_ref.shape[1]
    nb = seq // SLC_BLK

    tq_col = q0 + lax.broadcasted_iota(jnp.int32, (tq_n, 1), 0)
    tq_rows = jnp.concatenate([tq_col] * grp, axis=0)
    tq_lane = q0 + lax.broadcasted_iota(jnp.int32, (1, NSA_KV_HEADS * tq_n), 1) % tq_n
    t_end = lax.broadcasted_iota(jnp.int32, (1, n_cmp), 1) * CMP_STRIDE + (CMP_BLK - 1)
    ov_t = _overlap_t(nb, n_cmp, nc)
    n_chunks = (q0 + tq_n + tk_n - 1) // tk_n
    w_len = WINDOW + tq_n
    w0 = pl.multiple_of(jnp.maximum(q0 - WINDOW, 0), tq_n)

    qgs = [_scaled_group_queries(q_all, g) for g in range(NSA_KV_HEADS)]
    o_cs, imps = [], []
    for g in range(NSA_KV_HEADS):
        o_c, p_c = _cmp_branch_fm(qgs[g], cmp_ref, g, t_end <= tq_rows)
        o_cs.append(o_c)
        imps.append(_importance_t(_sum_heads(p_c, tq_n), ov_t))
    sel_all, _ = _topk_rows(_select_scores_t(jnp.concatenate(imps, axis=1), tq_lane), min(SLC_TOP_N, nb))

    tw = w0 + lax.broadcasted_iota(jnp.int32, (1, w_len), 1)
    bias_w = jnp.where((tw <= tq_col) & (tw > tq_col - WINDOW), 0.0, NEG)

    groups = range(NSA_KV_HEADS)

    def exp_rows(g, bias_g, width, online):
        per_head = tq_n // ATT_RB
        for i in range(m_rows // ATT_RB):
            rows = slice(i * ATT_RB, (i + 1) * ATT_RB)
            brows = slice((i % per_head) * ATT_RB, (i % per_head + 1) * ATT_RB)
            s = s_scr[g, rows, :width] + bias_scr[bias_g, brows, :width]
            m_new = jnp.max(s, axis=-1, keepdims=True)
            if online:
                m_old = m_scr[g, rows, :]
                m_new = jnp.maximum(m_old, m_new)
                corr_scr[g, rows, :] = jnp.exp2(m_old - m_new)
                m_scr[g, rows, :] = m_new
            e_scr[g, rows, :width] = jnp.exp2(s - m_new).astype(BF16)

    def keys_log2(ref, g, cols):
        return (ref[g * hd:(g + 1) * hd, cols] * LOG2E).astype(BF16)

    def gate_lane(g, r, branch):
        return hd + GATE_OFF + 3 * (grp * g + r) + branch

    def values_with_ones(ref, g, cols, branch):
        v = ref[(NSA_KV_HEADS + g) * hd:(NSA_KV_HEADS + g + 1) * hd, cols].astype(BF16)
        row = lax.broadcasted_iota(jnp.int32, v.shape, 0) + hd
        ones = functools.reduce(jnp.logical_or, [row == gate_lane(g, r, branch) for r in range(grp)])
        return jnp.concatenate([v, ones.astype(BF16)], axis=0)

    m_scr[...] = jnp.full(m_scr.shape, NEG, F32)
    acc_scr[...] = jnp.zeros(acc_scr.shape, F32)
    sel_ts = [sel_all[:, g * tq_n:(g + 1) * tq_n].astype(BF16) for g in groups]

    def slc_chunk(ci, carry):
        k0 = pl.multiple_of(ci * tk_n, tk_n)
        cols = pl.ds(k0, tk_n)
        kpos = k0 + lax.broadcasted_iota(jnp.int32, (1, tk_n), 1)
        expand = (lax.broadcasted_iota(jnp.int32, (nb, tk_n), 0) == kpos // SLC_BLK).astype(BF16)
        for g in groups:
            sel_k = _dot_tn(sel_ts[g], expand)
            bias_scr[g, :, :tk_n] = jnp.where((sel_k > 0.5) & (kpos <= tq_col), 0.0, NEG)
            s_scr[g, :, :tk_n] = _dot(qgs[g], keys_log2(kvs_ref, g, cols))
        for g in groups:
            exp_rows(g, g, tk_n, True)
        for g in groups:
            acc_scr[g] = corr_scr[g] * acc_scr[g] + _dot_nt(e_scr[g, :, :tk_n], values_with_ones(kvs_ref, g, cols, 1))
        return carry

    lax.fori_loop(0, n_chunks, slc_chunk, 0)
    wcols = pl.ds(w0, w_len)
    bias_scr[0, :, :w_len] = bias_w
    for g in groups:
        s_scr[g, :, :w_len] = _dot(qgs[g], keys_log2(kvw_ref, g, wcols))
    for g in groups:
        exp_rows(g, 0, w_len, False)
    acc_ws = [_dot_nt(e_scr[g, :, :w_len], values_with_ones(kvw_ref, g, wcols, 2)) for g in groups]
    gates_rot = pltpu.roll(gates, hd, 1)
    for g in groups:
        for r in range(grp):
            h = grp * g + r
            rows = slice(r * tq_n, (r + 1) * tq_n)
            acc_s, acc_w = acc_scr[g, rows, :], acc_ws[g][rows]
            f_s = gates_rot / jnp.maximum(acc_s, 1e-30)
            f_w = gates_rot / jnp.maximum(acc_w, 1e-30)
            ls, lw = gate_lane(g, r, 1), gate_lane(g, r, 2)
            gc = GATE_OFF + 3 * h
            o = (gates[:, gc:gc + 1] * o_cs[g][rows] + f_s[:, ls:ls + 1] * acc_s[:, :hd]
                 + f_w[:, lw:lw + 1] * acc_w[:, :hd])
            o_ref[:, h * hd:(h + 1) * hd] = o.astype(o_ref.dtype)


def _nsa_prompt_fm(q, aux, cmp_t, kvs_t, kvw_t):
    bsz, seq, _ = q.shape
    n_seg = cmp_t.shape[2]
    kern = functools.partial(_nsa_prompt_fm_kernel, seq=seq, nc=n_seg - CMP_R + 1)
    tile = lambda w: pl.BlockSpec((None, NSA_TQ, w), lambda b, t: (b, t, 0))
    whole = lambda n: pl.BlockSpec((None, KV_W, n), lambda b, t: (b, 0, 0))
    m_rows = NSA_GROUP * NSA_TQ
    width = max(NSA_TK, WINDOW + NSA_TQ)
    ng = NSA_KV_HEADS
    return pl.pallas_call(
        kern,
        grid=(bsz, seq // NSA_TQ),
        in_specs=[tile(NSA_WIDTH), tile(AUX_W), whole(n_seg), whole(seq), whole(seq)],
        out_specs=tile(NSA_WIDTH),
        out_shape=jax.ShapeDtypeStruct((bsz, seq, NSA_WIDTH), BF16),
        scratch_shapes=[pltpu.VMEM((ng, m_rows, width), F32),
                        pltpu.VMEM((ng, NSA_TQ, width), F32),
                        pltpu.VMEM((ng, m_rows, width), BF16),
                        pltpu.VMEM((ng, m_rows, 1), F32),
                        pltpu.VMEM((ng, m_rows, 1), F32),
                        pltpu.VMEM((ng, m_rows, 2 * NSA_HEAD_DIM), F32)],
        compiler_params=_cparams(("parallel", "arbitrary")),
        name="nsa_attn_prompt",
    )(q, aux, cmp_t, kvs_t, kvw_t)


def _cmp_sample_fm_kernel(pt_ref, cache_ref, w1f_ref, pe_ref, w1_ref, b1_ref, w2t_ref, o_ref,
                          xbuf, stage, p_scr, sem, *, steps_per_batch):
    b = pl.program_id(0)
    h = pl.program_id(1)
    step = b * steps_per_batch + h
    n_steps = pl.num_programs(0) * steps_per_batch
    pps = CMP_PAGES_PER_STEP
    segs = pps * SEGS_PER_PAGE

    def page_copy(bb, hh, p, slot):
        return pltpu.make_async_copy(cache_ref.at[pt_ref[bb, hh * pps + p]], xbuf.at[slot, p], sem.at[slot])

    def start_fetch(bb, hh, slot):
        for p in range(pps):
            page_copy(bb, hh, p, slot).start()

    @pl.when(step == 0)
    def _():
        start_fetch(b, h, 0)

    @pl.when(step + 1 < n_steps)
    def _():
        wrap = h + 1 == steps_per_batch
        start_fetch(jnp.where(wrap, b + 1, b), jnp.where(wrap, 0, h + 1), (step + 1) % 2)

    slot = step % 2
    for p in range(pps):
        page_copy(b, h, p, slot).wait()

    pages_per_chunk = CMP_ROW_CHUNK // SEGS_PER_PAGE
    for rc in range(segs // CMP_ROW_CHUNK):
        for p in range(rc * pages_per_chunk, (rc + 1) * pages_per_chunk):
            for c in range(2):
                stage[c, p * PAGE_SIZE:(p + 1) * PAGE_SIZE, :] = jnp.transpose(xbuf[slot, p, c * C_W:(c + 1) * C_W, :])
        for c in range(2):
            p_rows = _cmp_first_layer(stage, c, rc * CMP_ROW_CHUNK * CMP_STRIDE, CMP_ROW_CHUNK, w1f_ref)
            r0 = pl.multiple_of(h * segs + rc * CMP_ROW_CHUNK, CMP_ROW_CHUNK)
            p_scr[c, pl.ds(r0, CMP_ROW_CHUNK), :] = p_rows

    @pl.when(h == steps_per_batch - 1)
    def _():
        for c in range(2):
            o_ref[c * C_W:(c + 1) * C_W, :] = _cmp_second_layer_fm(
                p_scr[c], _cmp_bias_c(pe_ref, w1_ref, b1_ref, c), w2t_ref[c])


def _cmp_sample_fm(page_table, cache_fm, w1_full, pe_flat, cmp_w1, cmp_b1, w2_bd_t):
    bsz, n_pages = page_table.shape
    steps = n_pages // CMP_PAGES_PER_STEP
    n_seg = n_pages * SEGS_PER_PAGE
    const = lambda a: pl.BlockSpec(a.shape, lambda b, h, pt: (0,) * a.ndim, pipeline_mode=pl.Buffered(1))
    grid_spec = pltpu.PrefetchScalarGridSpec(
        num_scalar_prefetch=1,
        grid=(bsz, steps),
        in_specs=[pl.BlockSpec(memory_space=pl.ANY), const(w1_full), const(pe_flat), const(cmp_w1),
                  const(cmp_b1), const(w2_bd_t)],
        out_specs=pl.BlockSpec((None, KV_W, n_seg), lambda b, h, pt: (b, 0, 0)),
        scratch_shapes=[pltpu.VMEM((2, CMP_PAGES_PER_STEP, KV_W, PAGE_SIZE), F32),
                        pltpu.VMEM((2, CMP_PAGES_PER_STEP * PAGE_SIZE, C_W), F32),
                        pltpu.VMEM((2, n_seg, PC_W), F32),
                        pltpu.SemaphoreType.DMA((2,))],
    )
    return pl.pallas_call(
        functools.partial(_cmp_sample_fm_kernel, steps_per_batch=steps),
        grid_spec=grid_spec,
        out_shape=jax.ShapeDtypeStruct((bsz, KV_W, n_seg), F32),
        compiler_params=_cparams(("arbitrary", "arbitrary")),
        name="nsa_compress_sample",
    )(page_table, cache_fm, w1_full, pe_flat, cmp_w1, cmp_b1, w2_bd_t)


def _nsa_sample_select_fm_kernel(q_ref, cmp_ref, oc_ref, idx_ref, *, past_len, nb, nb_pad, nc):
    hd, grp = NSA_HEAD_DIM, NSA_GROUP
    m_rows = grp * SQ_ROWS
    q_all = q_ref[...]
    n_cmp = cmp_ref.shape[1]
    tq_rows = past_len + lax.broadcasted_iota(jnp.int32, (m_rows, 1), 0) % SQ_ROWS
    tq_lane = past_len + lax.broadcasted_iota(jnp.int32, (1, LANES), 1) % SQ_ROWS
    t_end = lax.broadcasted_iota(jnp.int32, (1, n_cmp), 1) * CMP_STRIDE + (CMP_BLK - 1)
    ov_t = _overlap_t(nb_pad, n_cmp, nc)
    psums = []
    for g in range(NSA_KV_HEADS):
        o_c, p_c = _cmp_branch_fm(_scaled_group_queries(q_all, g), cmp_ref, g, t_end <= tq_rows)
        psums.append(_sum_heads(p_c, SQ_ROWS))
        for r in range(grp):
            h = grp * g + r
            oc_ref[:, h * hd:(h + 1) * hd] = o_c[r * SQ_ROWS:(r + 1) * SQ_ROWS]
    psum = jnp.concatenate(psums + [jnp.zeros((LANES - NSA_KV_HEADS * SQ_ROWS, n_cmp), F32)], axis=0)
    score_t = _select_scores_t(_importance_t(psum, ov_t), tq_lane)
    rowid = lax.broadcasted_iota(jnp.int32, score_t.shape, 0)
    _, picks = _topk_rows(jnp.where(rowid < nb, score_t, -jnp.inf), min(SLC_TOP_N, nb))
    idx_ref[...] = jnp.concatenate(picks, axis=0)


def _nsa_sample_select_fm(q_pad, cmp_t, past_len, seq_new):
    bsz = q_pad.shape[0]
    n_seg = cmp_t.shape[2]
    nb = -(-(past_len + seq_new) // SLC_BLK)
    nb_pad = -(-nb // SUBLANES) * SUBLANES
    n_pick = min(SLC_TOP_N, nb)
    kern = functools.partial(_nsa_sample_select_fm_kernel, past_len=past_len, nb=nb, nb_pad=nb_pad,
                             nc=n_seg - CMP_R + 1)
    return pl.pallas_call(
        kern,
        grid=(bsz,),
        in_specs=[pl.BlockSpec((None, SQ_ROWS, NSA_WIDTH), lambda b: (b, 0, 0)),
                  pl.BlockSpec((None, KV_W, n_seg), lambda b: (b, 0, 0))],
        out_specs=[pl.BlockSpec((None, SQ_ROWS, NSA_WIDTH), lambda b: (b, 0, 0)),
                   pl.BlockSpec((None, n_pick, LANES), lambda b: (b, 0, 0))],
        out_shape=[jax.ShapeDtypeStruct((bsz, SQ_ROWS, NSA_WIDTH), F32),
                   jax.ShapeDtypeStruct((bsz, n_pick, LANES), jnp.int32)],
        compiler_params=_cparams(("parallel",)),
        name="nsa_select_sample",
    )(q_pad, cmp_t)


def _nsa_sample_attend_fm_kernel(pt_ref, idx_ref, q_ref, aux_ref, oc_ref, cache_ref, tail_ref, winp_ref,
                                 winn_ref, o_ref, kvbuf, sem, *, past_len, seq_new, n_pick):
    b = pl.program_id(0)
    nbatch = pl.num_programs(0)
    hd, grp = NSA_HEAD_DIM, NSA_GROUP
    n_items = seq_new * NSA_KV_HEADS * n_pick
    n_past_blocks = past_len // SLC_BLK
    blocks_per_page = PAGE_SIZE // SLC_BLK
    key_lanes = n_pick * PAGE_SIZE

    def block_id(bb, item):
        return jnp.minimum(idx_ref[bb * n_items + item], n_past_blocks)

    def copy(bb, item, slot):
        g = (item // n_pick) % NSA_KV_HEADS
        lanes = slice((item % n_pick) * PAGE_SIZE, (item % n_pick + 1) * PAGE_SIZE)
        rows = slice(g * hd, (g + 1) * hd)
        past_blk = jnp.minimum(block_id(bb, item), n_past_blocks - 1)
        page = pt_ref[bb, lax.shift_right_logical(past_blk, blocks_per_page.bit_length() - 1)]
        return pltpu.make_async_copy(cache_ref.at[page, :, rows, :], kvbuf.at[slot, item // n_pick, :, :, lanes],
                                     sem.at[slot])

    def start_fetch(bb, slot):
        for item in range(n_items):
            copy(bb, item, slot).start()

    @pl.when(b == 0)
    def _():
        start_fetch(b, 0)

    @pl.when(b + 1 < nbatch)
    def _():
        start_fetch(b + 1, (b + 1) % 2)

    slot = b % 2

    for item in range(n_items):
        copy(b, item, slot).wait()

    q_all = q_ref[...]
    gates = jax.nn.sigmoid(aux_ref[...])
    m_rows = grp * SQ_ROWS
    row_q = lax.broadcasted_iota(jnp.int32, (m_rows, 1), 0) % SQ_ROWS
    tq_rows = past_len + row_q
    lane = lax.broadcasted_iota(jnp.int32, (1, key_lanes), 1)
    key_slot, key_tok = lane // PAGE_SIZE, lane % PAGE_SIZE
    tok_new = lax.broadcasted_iota(jnp.int32, (1, PAGE_SIZE), 1)
    wb, wn = winp_ref.shape[1], winn_ref.shape[1]
    tw_p = past_len - wb + lax.broadcasted_iota(jnp.int32, (1, wb), 1)
    tw_n = past_len + lax.broadcasted_iota(jnp.int32, (1, wn), 1)
    bias_p = jnp.where((tw_p <= tq_rows) & (tw_p > tq_rows - WINDOW) & (tw_p >= 0), 0.0, NEG)
    bias_n = jnp.where((tw_n <= tq_rows) & (tw_n > tq_rows - WINDOW), 0.0, NEG)

    for g in range(NSA_KV_HEADS):
        qg = _scaled_group_queries(q_all, g)
        kl, vl = g * hd, (NSA_KV_HEADS + g) * hd
        o_s = jnp.zeros((m_rows, hd), F32)
        k_new = tail_ref[0, kl:kl + hd, :].astype(BF16)
        v_new = tail_ref[1, kl:kl + hd, :].astype(BF16)
        for qi in range(seq_new):
            qg_i = qi * NSA_KV_HEADS + g
            blk = jnp.zeros((1, key_lanes), jnp.int32)
            max_blk = jnp.zeros((1, PAGE_SIZE), jnp.int32)
            for kk in range(n_pick):
                blk_k = block_id(b, qg_i * n_pick + kk)
                blk = jnp.where(key_slot == kk, blk_k, blk)
                max_blk = jnp.maximum(max_blk, blk_k)
            mask = ((blk < n_past_blocks) & (key_tok // SLC_BLK == blk % blocks_per_page)
                    & (blk * SLC_BLK + key_tok % SLC_BLK <= past_len + qi))
            mask_new = (max_blk == n_past_blocks) & (tok_new < SLC_BLK) & (past_len + tok_new <= past_len + qi)
            s = _dot(qg, jnp.concatenate([kvbuf[slot, qg_i, 0].astype(BF16), k_new], axis=1))
            p = _masked_softmax(s, jnp.concatenate([mask, mask_new], axis=1))
            vcat = jnp.concatenate([kvbuf[slot, qg_i, 1].astype(BF16), v_new], axis=1)
            o_s = jnp.where(row_q == qi, _dot_nt(p.astype(BF16), vcat), o_s)
        s_p = _dot(qg, winp_ref[kl:kl + hd, :].astype(BF16)) + bias_p
        s_n = _dot(qg, winn_ref[kl:kl + hd, :].astype(BF16)) + bias_n
        m = jnp.maximum(jnp.max(s_p, axis=-1, keepdims=True), jnp.max(s_n, axis=-1, keepdims=True))
        e_p, e_n = jnp.exp(s_p - m), jnp.exp(s_n - m)
        den = jnp.maximum(jnp.sum(e_p, axis=-1, keepdims=True) + jnp.sum(e_n, axis=-1, keepdims=True), 1e-30)
        o_w = (_dot_nt(e_p.astype(BF16), winp_ref[vl:vl + hd, :].astype(BF16))
               + _dot_nt(e_n.astype(BF16), winn_ref[vl:vl + hd, :].astype(BF16))) / den
        for r in range(grp):
            h = grp * g + r
            rows = slice(r * SQ_ROWS, (r + 1) * SQ_ROWS)
            gc = GATE_OFF + 3 * h
            o = (gates[:, gc:gc + 1] * oc_ref[:, h * hd:(h + 1) * hd] + gates[:, gc + 1:gc + 2] * o_s[rows]
                 + gates[:, gc + 2:gc + 3] * o_w[rows])
            o_ref[:, h * hd:(h + 1) * hd] = o.astype(o_ref.dtype)


def _nsa_sample_attend_fm(page_table, idx_flat, q_pad, aux_pad, o_c, cache_fm, tail_fm, win_past, win_new,
                          past_len, seq_new, n_pick):
    bsz = q_pad.shape[0]
    wb, wn = win_past.shape[2], win_new.shape[2]
    kern = functools.partial(_nsa_sample_attend_fm_kernel, past_len=past_len, seq_new=seq_new, n_pick=n_pick)
    per_b = lambda n, w: pl.BlockSpec((None, n, w), lambda b, pt, ix: (b, 0, 0))
    n_qg = seq_new * NSA_KV_HEADS
    buf = pltpu.VMEM((2, n_qg, 2, NSA_HEAD_DIM, n_pick * PAGE_SIZE), F32)
    grid_spec = pltpu.PrefetchScalarGridSpec(
        num_scalar_prefetch=2,
        grid=(bsz,),
        in_specs=[per_b(SQ_ROWS, NSA_WIDTH), per_b(SQ_ROWS, AUX_W), per_b(SQ_ROWS, NSA_WIDTH),
                  pl.BlockSpec(memory_space=pl.ANY),
                  pl.BlockSpec((None, 2, C_W, LANES), lambda b, pt, ix: (b, 0, 0, 0)),
                  per_b(KV_W, wb), per_b(KV_W, wn)],
        out_specs=per_b(SQ_ROWS, NSA_WIDTH),
        scratch_shapes=[buf, pltpu.SemaphoreType.DMA((2,))],
    )
    return pl.pallas_call(
        kern,
        grid_spec=grid_spec,
        out_shape=jax.ShapeDtypeStruct((bsz, SQ_ROWS, NSA_WIDTH), BF16),
        compiler_params=_cparams(("arbitrary",)),
        name="nsa_attend_sample",
    )(page_table, idx_flat, q_pad, aux_pad, o_c, cache_fm, tail_fm, win_past, win_new)


PROMPT_TM = 512
GLA_TL = 512
SAMPLE_GLA_ROWS = 16
GLA_PROMPT_NBB = 2
GLA_SAMPLE_NBB = 8


def _pad_rows(x, n):
    return jnp.pad(x, ((0, 0), (0, n - x.shape[1]), (0, 0)))


def kernel(x_prompt, x_sample, state_gla, cache_cmp_kv, cache_slc_kv, cache_win_kv, page_table, c_prompt,
           c_sample, ln_in_g, ln_in_b, w_ada, b_ada, w_in, gla_w_a2, gla_b_a, gla_norm_g, cmp_pe, cmp_w1,
           cmp_b1, cmp_w2, w_o, ln1_g, ln1_b, w_ffn_in, w_ffn_out, ln2_g, ln2_b):
    assert w_in.shape[0] == DEPTH == 1
    l = 0
    bp, lp, d = x_prompt.shape
    bs, ls, _ = x_sample.shape
    n_pool = cache_cmp_kv.shape[1]
    n_pages = page_table.shape[1]
    past_len = n_pages * PAGE_SIZE
    wb = cache_win_kv.shape[2]
    assert ((past_len + ls) // CMP_STRIDE) * CMP_STRIDE <= past_len and past_len % SLC_BLK == 0
    assert ls <= SQ_ROWS and ls <= SLC_BLK and wb == WINDOW

    w_perm, w_kv_t = _permute_w_in(w_in[l])
    w_o_b, w_fi_b, w_fo_b = w_o[l].astype(BF16), w_ffn_in[l].astype(BF16), w_ffn_out[l].astype(BF16)
    w_a2p = jnp.zeros((AUX_W, GLA_KW), F32).at[:GLA_RANK].set(gla_w_a2[l])
    b_a = gla_b_a[l].reshape(1, GLA_KW)
    norm_g = gla_norm_g[l].reshape(1, GLA_DV)
    w1c, w2c_t, pe_flat = _cmp_weights_fm(cmp_pe[l], cmp_w1[l], cmp_w2[l])
    kvt = (2, NSA_KV_HEADS, NSA_HEAD_DIM)

    def fm_view(a):
        return jnp.transpose(a, (0, 2, 3, 4, 1)).reshape(a.shape[0], KV_W, a.shape[1])

    def tok_view(a_fm):
        n, _, t = a_fm.shape
        return jnp.transpose(a_fm.reshape((n,) + kvt + (t,)), (0, 4, 1, 2, 3))[None]

    mod = _ada(jnp.concatenate([c_prompt, c_sample], axis=0), w_ada[l], b_ada[l])
    mods_p = [m[:bp, None, :] for m in jnp.split(mod, 6, axis=-1)]
    mods_s = [jnp.repeat(m[bp:], ls, axis=0)[None] for m in jnp.split(mod, 6, axis=-1)]

    def out_ffn(x2d, o_g, o_n, mods, tm, rpm):
        sh1, sc1, ga1, sh2, sc2, ga2 = mods
        return _out_ffn(x2d, o_g, o_n, ga1, sc2, sh2, ga2, ln_in_g, ln_in_b, w_o_b, ln1_g[l], ln1_b[l], w_fi_b,
                        w_fo_b, ln2_g[l], ln2_b[l], tm, rpm)

    xp2 = x_prompt.reshape(bp * lp, d)
    rpm = lp // PROMPT_TM
    qk, v, r, qn, kvc, _, _, aux, kvc_t, kvs_t, kvw_t = _inproj(
        xp2, mods_p[1], mods_p[0], ln_in_g, ln_in_b, w_perm, w_kv_t, PROMPT_TM, rpm, seq_per_batch=lp)
    b3 = lambda a: a.reshape(bp, lp, a.shape[-1])
    o_g, s_p = _gla(b3(qk), b3(v), b3(r), b3(aux), w_a2p, b_a, norm_g, jnp.zeros((bp, GLA_KW, GLA_DV), F32),
                    chunk=GLA_CHUNK, sub=GLA_SUB, tl=GLA_TL, l_valid=lp, nbb=GLA_PROMPT_NBB)
    cmp_t_p = _cmp_prompt_fm(kvc, lp, w1c, pe_flat, cmp_w1[l], cmp_b1[l], w2c_t)
    o_n = _nsa_prompt_fm(b3(qn), b3(aux), cmp_t_p, kvs_t, kvw_t)
    y_p = out_ffn(xp2, o_g.reshape(bp * lp, GLA_WIDTH), o_n.reshape(bp * lp, NSA_WIDTH), mods_p, PROMPT_TM, rpm)
    w_keep = min(WINDOW, lp)
    outs_p = (y_p.reshape(bp, lp, d), s_p.reshape(1, bp, GLA_HEADS, GLA_DK, GLA_DV),
              tok_view(kvc_t), tok_view(kvs_t), tok_view(kvw_t[:, :, lp - w_keep:]))

    ts = bs * ls
    xs2 = x_sample.reshape(ts, d)
    qk, v, r, qn, kvc, kvs, kvw, aux = _inproj(xs2, mods_s[1], mods_s[0], ln_in_g, ln_in_b, w_perm, w_kv_t, ts, 1)
    s3 = lambda a: a.reshape(bs, ls, a.shape[-1])
    g16 = lambda a: _pad_rows(s3(a), SAMPLE_GLA_ROWS)
    o_g, s_s = _gla(g16(qk), g16(v), g16(r), g16(aux), w_a2p, b_a, norm_g, state_gla[l].reshape(bs, GLA_KW, GLA_DV),
                    chunk=SAMPLE_GLA_ROWS, sub=SAMPLE_GLA_ROWS, tl=SAMPLE_GLA_ROWS, l_valid=ls,
                    nbb=GLA_SAMPLE_NBB)
    o_g = o_g[:, :ls].reshape(ts, GLA_WIDTH)
    kvc = jnp.concatenate([kvc[0], kvc[1]], axis=1)
    cmp_t_s = _cmp_sample_fm(page_table, fm_view(cache_cmp_kv[l]), w1c, pe_flat, cmp_w1[l], cmp_b1[l], w2c_t)
    q_pad = _pad_rows(s3(qn), SQ_ROWS)
    o_c, idx = _nsa_sample_select_fm(q_pad, cmp_t_s, past_len, ls)
    n_pick = idx.shape[1]
    idx = idx[:, :, :NSA_KV_HEADS * SQ_ROWS].reshape(bs, n_pick, NSA_KV_HEADS, SQ_ROWS)[..., :ls]
    idx_flat = jnp.transpose(idx, (0, 3, 2, 1)).reshape(-1)
    new_fm =lambda a: jnp.pad(jnp.transpose(s3(a), (0, 2, 1)), ((0, 0), (0, 0), (0, LANES - ls)))
    win_past = fm_view(cache_win_kv[l])
    o_n = _nsa_sample_attend_fm(page_table, idx_flat, q_pad, _pad_rows(s3(aux), SQ_ROWS), o_c,
                                fm_view(cache_slc_kv[l]).reshape(n_pool, 2, C_W, PAGE_SIZE),
                                new_fm(kvs).reshape(bs, 2, C_W, LANES), win_past, new_fm(kvw),
                                past_len, ls, n_pick)
    o_n = o_n[:, :ls].reshape(ts, NSA_WIDTH)
    y_s = out_ffn(xs2, o_g, o_n, mods_s, ts, 1)
    win_s = jnp.concatenate([win_past[:, :, ls:], jnp.transpose(s3(kvw), (0, 2, 1))], axis=2)
    outs_s = (y_s.reshape(bs, ls, d), s_s.reshape(1, bs, GLA_HEADS, GLA_DK, GLA_DV),
              kvc.reshape((1, bs, ls) + kvt), kvs.reshape((1, bs, ls) + kvt), tok_view(win_s))

    return (outs_p[0], outs_s[0], outs_p[1], outs_s[1], outs_p[2], outs_s[2], outs_p[3], outs_s[3],
            outs_p[4], outs_s[4])
```

```python
import functools
import math

import numpy as np
import jax
import jax.numpy as jnp
from jax import lax
from jax.experimental import pallas as pl
from jax.experimental.pallas import tpu as pltpu

F32 = jnp.float32
BF16 = jnp.bfloat16

D_MODEL = 1024
DEPTH = 1
PAGE_SIZE = 128
GLA_HEADS = 4
GLA_DV = D_MODEL // (2 * GLA_HEADS)
GLA_DK = GLA_DV // 2
GLA_RANK = 16
GLA_TAU = 16.0
GLA_CHUNK = 64
GLA_SUB = 32
GLA_WIDTH = GLA_HEADS * GLA_DV
GLA_KW = GLA_HEADS * GLA_DK
NSA_HEADS = 8
NSA_KV_HEADS = 2
NSA_GROUP = NSA_HEADS // NSA_KV_HEADS
NSA_HEAD_DIM = D_MODEL // (2 * NSA_HEADS)
NSA_WIDTH = NSA_HEADS * NSA_HEAD_DIM
CMP_BLK = 32
CMP_STRIDE = 16
CMP_HIDDEN = 2 * NSA_HEAD_DIM
SLC_BLK = 64
SLC_TOP_N = 16
WINDOW = 512
FORCE_BONUS = 1e4
NEG = -1e30
D_FF = -(-8 * D_MODEL // (3 * 256)) * 256
ALPHA = (2 * DEPTH) ** 0.25
KV_W = 2 * NSA_KV_HEADS * NSA_HEAD_DIM
N_GATES = 3 * NSA_HEADS
IN_SIZES = (GLA_KW, GLA_KW, GLA_WIDTH, GLA_RANK, GLA_WIDTH, NSA_WIDTH, KV_W, KV_W, KV_W, N_GATES)
IN_WIDTH = sum(IN_SIZES)
LN_EPS = 1e-5

LANES = 128
SUBLANES = 8
VMEM_LIMIT_BYTES = 56 * 1024 * 1024

AUX_W = LANES
IN_GROUPS = (2 * GLA_KW, GLA_WIDTH, GLA_WIDTH, NSA_WIDTH, KV_W, KV_W, KV_W, AUX_W)
IN_PERM_W = sum(IN_GROUPS)
GATE_OFF = GLA_RANK


def _cparams(sem):
    return pltpu.CompilerParams(dimension_semantics=sem, vmem_limit_bytes=VMEM_LIMIT_BYTES)


def _split3(a):
    hi = a.astype(BF16)
    r1 = a - hi.astype(F32)
    mid = r1.astype(BF16)
    lo = (r1 - mid.astype(F32)).astype(BF16)
    return hi, mid, lo


def _dot(a, b):
    return jnp.dot(a, b, preferred_element_type=F32)


def _dot_nt(a, b):
    return lax.dot_general(a, b, (((1,), (1,)), ((), ())), preferred_element_type=F32)


def _dot_tn(a, b):
    return lax.dot_general(a, b, (((0,), (0,)), ((), ())), preferred_element_type=F32)


def _layer_norm(x, g, b):
    mu = jnp.mean(x, axis=-1, keepdims=True)
    xc = x - mu
    var = jnp.mean(xc * xc, axis=-1, keepdims=True)
    return xc * lax.rsqrt(var + LN_EPS) * g + b


def _ada_kernel(c_ref, w_ref, b_ref, o_ref):
    c = c_ref[...]
    a = (c * jax.nn.sigmoid(c)).astype(BF16)
    o_ref[...] = _dot(a, w_ref[...].astype(BF16)) + b_ref[...]


def _ada(c, w_ada, b_ada):
    n, d = c.shape
    m = w_ada.shape[1]
    tn = D_MODEL
    return pl.pallas_call(
        _ada_kernel,
        grid=(m // tn,),
        in_specs=[pl.BlockSpec((n, d), lambda j: (0, 0)),
                  pl.BlockSpec((d, tn), lambda j: (0, j)),
                  pl.BlockSpec((1, tn), lambda j: (0, j))],
        out_specs=pl.BlockSpec((n, tn), lambda j: (0, j)),
        out_shape=jax.ShapeDtypeStruct((n, m), F32),
        compiler_params=_cparams(("parallel",)),
        name="ada_mod",
    )(c, w_ada, b_ada.reshape(1, m))


N_KV_GROUPS = 3
KVC_GROUP = 4


def _inproj_kernel(x_ref, sc_ref, sh_ref, g_ref, b_ref, w_ref, wt_ref, *o_refs, emit_t):
    xn = _layer_norm(x_ref[...], g_ref[...], b_ref[...])
    u = (xn * (1.0 + sc_ref[...]) + sh_ref[...]).astype(BF16)
    lo = 0
    for j, (o_ref, wdt) in enumerate(zip(o_refs[:len(IN_GROUPS)], IN_GROUPS)):
        z = _dot(u, w_ref[:, lo:lo + wdt]).astype(o_ref.dtype)
        if j == KVC_GROUP:
            o_ref[0] = z[:, :wdt // 2]
            o_ref[1] = z[:, wdt // 2:]
        else:
            o_ref[...] = z
        lo += wdt
    if emit_t:
        for j, o_ref in enumerate(o_refs[len(IN_GROUPS):]):
            o_ref[...] = _dot_nt(wt_ref[j * KV_W:(j + 1) * KV_W, :], u)


_IN_OUT_DTYPES = (BF16, BF16, BF16, BF16, F32, F32, F32, F32)


def _inproj(x, sc, sh, ln_g, ln_b, w_perm, w_kv_t, tm, rows_per_mod, seq_per_batch=None):
    t, d = x.shape
    r = sc.shape[1]
    emit_t = seq_per_batch is not None
    mod_spec = pl.BlockSpec((None, r, d), lambda i: (i // rows_per_mod, 0, 0))
    out_specs = [pl.BlockSpec((tm, w), lambda i: (i, 0)) for w in IN_GROUPS]
    out_shape = [jax.ShapeDtypeStruct((t, w), dt) for w, dt in zip(IN_GROUPS, _IN_OUT_DTYPES)]
    out_specs[KVC_GROUP] = pl.BlockSpec((2, tm, KV_W // 2), lambda i: (0, i, 0))
    out_shape[KVC_GROUP] = jax.ShapeDtypeStruct((2, t, KV_W // 2), F32)
    if emit_t:
        tpb = seq_per_batch // tm
        out_specs += [pl.BlockSpec((None, KV_W, tm), lambda i: (i // tpb, 0, i % tpb))] * N_KV_GROUPS
        out_shape += [jax.ShapeDtypeStruct((t // seq_per_batch, KV_W, seq_per_batch), F32)] * N_KV_GROUPS
    return pl.pallas_call(
        functools.partial(_inproj_kernel, emit_t=emit_t),
        grid=(t // tm,),
        in_specs=[pl.BlockSpec((tm, d), lambda i: (i, 0)), mod_spec, mod_spec,
                  pl.BlockSpec((1, d), lambda i: (0, 0)), pl.BlockSpec((1, d), lambda i: (0, 0)),
                  pl.BlockSpec((d, IN_PERM_W), lambda i: (0, 0)),
                  pl.BlockSpec((N_KV_GROUPS * KV_W, d), lambda i: (0, 0))],
        out_specs=out_specs,
        out_shape=out_shape,
        compiler_params=_cparams(("parallel",)),
        name="ln_mod_inproj",
    )(x, sc, sh, ln_g.reshape(1, d), ln_b.reshape(1, d), w_perm, w_kv_t)


def _permute_w_in(w_in):
    q_g, k_g, v_g, a_g, r_g, q_n, kv_c, kv_s, kv_w, g_n = jnp.split(w_in, np.cumsum(IN_SIZES)[:-1], axis=1)
    pad = jnp.zeros((w_in.shape[0], AUX_W - GLA_RANK - N_GATES), w_in.dtype)
    w_perm = jnp.concatenate([q_g, k_g, v_g, r_g, q_n, kv_c, kv_s, kv_w, a_g, g_n, pad], axis=1).astype(BF16)
    w_kv_t = jnp.concatenate([kv_c, kv_s, kv_w], axis=1).T.astype(BF16)
    return w_perm, w_kv_t


FF_CHUNK = 256


def _out_ffn_kernel(x_ref, og_ref, on_ref, ga1_ref, sc2_ref, sh2_ref, ga2_ref, lng_ref, lnb_ref,
                    wo_ref, l1g_ref, l1b_ref, wfi_ref, wfo_ref, l2g_ref, l2b_ref, y_ref):
    x = _layer_norm(x_ref[...], lng_ref[...], lnb_ref[...])
    mix = _dot(og_ref[...], wo_ref[0:GLA_WIDTH, :]) + _dot(on_ref[...], wo_ref[GLA_WIDTH:, :])
    x1 = _layer_norm(ALPHA * x + ga1_ref[...] * mix, l1g_ref[...], l1b_ref[...])
    u2 = (x1 * (1.0 + sc2_ref[...]) + sh2_ref[...]).astype(BF16)
    ffn = jnp.zeros(x1.shape, F32)
    for c in range(D_FF // FF_CHUNK):
        lo = c * FF_CHUNK
        gate = _dot(u2, wfi_ref[:, lo:lo + FF_CHUNK])
        up = _dot(u2, wfi_ref[:, D_FF + lo:D_FF + lo + FF_CHUNK])
        f = (gate * jax.nn.sigmoid(gate) * up).astype(BF16)
        ffn = ffn + _dot(f, wfo_ref[lo:lo + FF_CHUNK, :])
    y_ref[...] = _layer_norm(ALPHA * x1 + ga2_ref[...] * ffn, l2g_ref[...], l2b_ref[...])


def _out_ffn(x, o_g, o_n, ga1, sc2, sh2, ga2, ln_in_g, ln_in_b, w_o, ln1_g, ln1_b, w_fi, w_fo, ln2_g, ln2_b,
             tm, rows_per_mod):
    t, d = x.shape
    r = ga1.shape[1]
    mod_spec = pl.BlockSpec((None, r, d), lambda i: (i // rows_per_mod, 0, 0))
    vec = lambda: pl.BlockSpec((1, d), lambda i: (0, 0))
    const = lambda shp: pl.BlockSpec(shp, lambda i: (0, 0), pipeline_mode=pl.Buffered(1))
    row = lambda a: a.reshape(1, d)
    return pl.pallas_call(
        _out_ffn_kernel,
        grid=(t // tm,),
        in_specs=[pl.BlockSpec((tm, d), lambda i: (i, 0)),
                  pl.BlockSpec((tm, GLA_WIDTH), lambda i: (i, 0)),
                  pl.BlockSpec((tm, NSA_WIDTH), lambda i: (i, 0)),
                  mod_spec, mod_spec, mod_spec, mod_spec, vec(), vec(),
                  const((d, d)), vec(), vec(), const((d, 2 * D_FF)), const((D_FF, d)), vec(), vec()],
        out_specs=pl.BlockSpec((tm, d), lambda i: (i, 0)),
        out_shape=jax.ShapeDtypeStruct((t, d), F32),
        compiler_params=_cparams(("parallel",)),
        name="outproj_ffn",
    )(x, o_g, o_n, ga1, sc2, sh2, ga2, row(ln_in_g), row(ln_in_b), w_o, row(ln1_g), row(ln1_b), w_fi, w_fo,
      row(ln2_g), row(ln2_b))


GLA_EXP_CLAMP = 80.0


def _gla_kernel(qk_ref, v_ref, r_ref, aux_ref, wa_ref, ba_ref, ng_ref, s0_ref, o_ref, sout_ref, s_scr,
                *, chunk, sub, tl, l_valid, l_pad, nbb):
    t = pl.program_id(1)
    c = chunk
    n_sub = c // sub
    hw = GLA_KW

    @pl.when(t == 0)
    def _():
        s_scr[...] = s0_ref[...]

    ri = lax.broadcasted_iota(jnp.int32, (c, c), 0)
    ci = lax.broadcasted_iota(jnp.int32, (c, c), 1)
    causal = ci <= ri
    tril = causal.astype(BF16)
    rowid = lax.broadcasted_iota(jnp.int32, (c, hw), 0)
    head_of_lane = lax.broadcasted_iota(jnp.int32, (c, hw), 1) // GLA_DK
    wa = wa_ref[...]
    wa_hi = wa.astype(BF16)
    wa_mid = (wa - wa_hi.astype(F32)).astype(BF16)
    ba = ba_ref[...]
    ng = ng_ref[...]

    def body(i, carry):
        for e in range(nbb):
            s_scr[e] = one_chunk(i, e, s_scr[e])
        return carry

    def one_chunk(i, e, s_old):
        r0 = pl.multiple_of(i * c, c)
        aux = aux_ref[e, pl.ds(r0, c), :]
        a_hi = aux.astype(BF16)
        a_mid = (aux - a_hi.astype(F32)).astype(BF16)
        z = _dot(a_hi, wa_hi) + _dot(a_mid, wa_hi) + _dot(a_hi, wa_mid) + ba
        g = jax.nn.log_sigmoid(z) / GLA_TAU
        if l_pad != l_valid:
            g = jnp.where(t * tl + r0 + rowid < l_valid, g, 0.0)
        g_hi, g_mid, g_lo = _split3(g)
        b = _dot(tril, g_hi) + _dot(tril, g_mid) + _dot(tril, g_lo)
        qk = qk_ref[e, pl.ds(r0, c), :]
        q = qk[:, :hw].astype(F32) * (GLA_DK ** -0.5)
        k = qk[:, hw:].astype(F32)
        v = v_ref[e, pl.ds(r0, c), :]
        b_last = b[c - 1:c, :]

        def heads_on_rows(x):
            return jnp.concatenate([jnp.where(head_of_lane == h, x, 0.0) for h in range(GLA_HEADS)], axis=0)

        o_inter =_dot(heads_on_rows(q * jnp.exp(b)).astype(BF16), s_old.astype(BF16))

        q_parts, k_parts = [], []
        for s_i in range(n_sub):
            b_ref = b[s_i * sub - 1:s_i * sub, :] if s_i > 0 else jnp.zeros((1, hw), F32)
            in_rows = (rowid >= s_i * sub) & (rowid < (s_i + 1) * sub)
            qt = jnp.where(in_rows, q * jnp.exp(jnp.minimum(b - b_ref, 0.0)), 0.0)
            kt = jnp.where(rowid < (s_i + 1) * sub, k * jnp.exp(jnp.minimum(b_ref - b, GLA_EXP_CLAMP)), 0.0)
            q_parts.append(heads_on_rows(qt).astype(BF16))
            k_parts.append(kt.astype(BF16))
        q_cat = jnp.concatenate(q_parts, axis=1) if n_sub > 1 else q_parts[0]
        k_cat = jnp.concatenate(k_parts, axis=1) if n_sub > 1 else k_parts[0]
        att = _dot_nt(q_cat, k_cat)

        r_t = r_ref[e, pl.ds(r0, c), :].astype(F32)
        for h in range(GLA_HEADS):
            att_h = jnp.where(causal, att[h * c:(h + 1) * c, :], 0.0).astype(BF16)
            o_h = o_inter[h * c:(h + 1) * c, :] + _dot(att_h, v[:, h * GLA_DV:(h + 1) * GLA_DV])
            o_h = o_h * lax.rsqrt(jnp.mean(o_h * o_h, axis=-1, keepdims=True) + 1e-6) * ng
            r_h = r_t[:, h * GLA_DV:(h + 1) * GLA_DV]
            o_ref[e, pl.ds(r0, c), h * GLA_DV:(h + 1) * GLA_DV] = (
                o_h * (r_h * jax.nn.sigmoid(r_h))).astype(o_ref.dtype)

        kd = jnp.concatenate([k * jnp.exp(b_last - b), jnp.zeros((LANES - c, hw), F32)], axis=0)
        kd_t = jnp.transpose(kd).astype(BF16)
        v_pad = jnp.concatenate([v, jnp.zeros((LANES - c, GLA_WIDTH), v.dtype)], axis=0)
        upd = jnp.concatenate([_dot(kd_t[h * GLA_DK:(h + 1) * GLA_DK, :], v_pad[:, h * GLA_DV:(h + 1) * GLA_DV])
                               for h in range(GLA_HEADS)], axis=0)
        decay = jnp.transpose(jnp.broadcast_to(jnp.exp(b_last), (LANES, hw)))
        return decay * s_old + upd

    lax.fori_loop(0, tl // c, body, 0)

    @pl.when(t == pl.num_programs(1) - 1)
    def _():
        sout_ref[...] = s_scr[...]


def _gla(qk, v, r, aux, w_a2p, b_a, norm_g, s0, *, chunk, sub, tl, l_valid, nbb):
    bsz, l_pad, _ = qk.shape
    nt = l_pad // tl
    kern = functools.partial(_gla_kernel, chunk=chunk, sub=sub, tl=tl, l_valid=l_valid, l_pad=l_pad, nbb=nbb)
    tile = lambda w: pl.BlockSpec((nbb, tl, w), lambda b, t: (b, t, 0))
    full = lambda shp: pl.BlockSpec(shp, lambda b, t: (0, 0))
    st = pl.BlockSpec((nbb, GLA_KW, GLA_DV), lambda b, t: (b, 0, 0))
    return pl.pallas_call(
        kern,
        grid=(bsz // nbb, nt),
        in_specs=[tile(2 * GLA_KW), tile(GLA_WIDTH), tile(GLA_WIDTH), tile(AUX_W),
                  full((AUX_W, GLA_KW)), full((1, GLA_KW)), full((1, GLA_DV)), st],
        out_specs=[tile(GLA_WIDTH), st],
        out_shape=[jax.ShapeDtypeStruct((bsz, l_pad, GLA_WIDTH), BF16),
                   jax.ShapeDtypeStruct((bsz, GLA_KW, GLA_DV), F32)],
        scratch_shapes=[pltpu.VMEM((nbb, GLA_KW, GLA_DV), F32)],
        compiler_params=_cparams(("parallel", "arbitrary")),
        name="gla_scan",
    )(qk, v, r, aux, w_a2p, b_a, norm_g, s0)


CMP_R = CMP_BLK // CMP_STRIDE


QK_SCALE = NSA_HEAD_DIM ** -0.5
LOG2E = math.log2(math.e)


def _masked_softmax(s, mask):
    s = jnp.where(mask, s, NEG)
    e = jnp.where(mask, jnp.exp(s - jnp.max(s, axis=-1, keepdims=True)), 0.0)
    return e / jnp.maximum(jnp.sum(e, axis=-1, keepdims=True), 1e-30)


def _group_queries(q, g):
    hd = NSA_HEAD_DIM
    return jnp.concatenate([q[:, (NSA_GROUP * g + r) * hd:(NSA_GROUP * g + r + 1) * hd]
                            for r in range(NSA_GROUP)], axis=0)


def _topk_rows(score_t, n_pick):
    nb, nq = score_t.shape
    rowid = lax.broadcasted_iota(jnp.int32, (nb, nq), 0)
    taken = jnp.zeros((nb, nq), jnp.int32)
    picks = []
    for _ in range(n_pick):
        free = taken == 0
        cand = jnp.where(free, score_t, -jnp.inf)
        m = jnp.max(cand, axis=0, keepdims=True)
        hit = free & (cand == m)
        idx = jnp.min(jnp.where(hit, rowid, nb), axis=0, keepdims=True)
        taken = jnp.where(rowid == idx, 1, taken)
        picks.append(idx)
    return taken.astype(F32), picks


def _importance_t(psum, ov_t):
    hi, mid, lo = _split3(psum)
    return _dot_nt(ov_t, hi) + _dot_nt(ov_t, mid) + _dot_nt(ov_t, lo)


def _overlap_t(nb, nc_pad, nc):
    j = lax.broadcasted_iota(jnp.int32, (nb, nc_pad), 0) * SLC_BLK
    i = lax.broadcasted_iota(jnp.int32, (nb, nc_pad), 1) * CMP_STRIDE
    return ((i < j + SLC_BLK) & (i + CMP_BLK > j) & (i < nc * CMP_STRIDE)).astype(BF16)


def _select_scores_t(imp_t, tq_row):
    nb, nq = imp_t.shape
    j = lax.broadcasted_iota(jnp.int32, (nb, nq), 0)
    cur = tq_row // SLC_BLK
    forced = (j == 0) | (j == cur) | (j == cur - 1)
    return jnp.where(j * SLC_BLK <= tq_row, imp_t + FORCE_BONUS * forced.astype(F32), -jnp.inf)


NSA_TQ = 128
NSA_TK = 512


SEGS_PER_PAGE = PAGE_SIZE // CMP_STRIDE
CMP_PAGES_PER_STEP = 64
CMP_ROW_CHUNK = 256
STAGE_PITCH = 24


SQ_ROWS = SUBLANES


C_W = NSA_KV_HEADS * NSA_HEAD_DIM
HC_W = NSA_KV_HEADS * CMP_HIDDEN
PC_W = CMP_R * HC_W
MXU_DEPTH = 256
CMP_S_PER_DOT = MXU_DEPTH // C_W


def _cmp_weights_fm(cmp_pe, cmp_w1, cmp_w2):
    eye = jnp.eye(NSA_KV_HEADS, dtype=F32)
    w1r = cmp_w1.reshape(2, CMP_R, CMP_STRIDE, NSA_HEAD_DIM, CMP_HIDDEN)
    w1c = jnp.einsum('cmsdh,gG->csgdmGh', w1r, eye).reshape(2, CMP_STRIDE * C_W, PC_W).astype(BF16)
    w2c_t = jnp.einsum('chd,gG->cGdgh', cmp_w2, eye).reshape(2, C_W, HC_W).astype(BF16)
    pe_flat = jnp.transpose(cmp_pe, (1, 0, 2)).reshape(2, CMP_BLK * NSA_HEAD_DIM)
    return w1c, w2c_t, pe_flat


def _cmp_bias_c(pe_ref, w1_ref, b1_ref, c):
    pe = jnp.broadcast_to(pe_ref[c:c + 1, :], (SUBLANES, pe_ref.shape[1]))
    w1c = w1_ref[c]
    pe_hi = pe.astype(BF16)
    pe_mid = (pe - pe_hi.astype(F32)).astype(BF16)
    w_hi = w1c.astype(BF16)
    w_mid = (w1c - w_hi.astype(F32)).astype(BF16)
    pb = _dot(pe_hi, w_hi) + _dot(pe_mid, w_hi) + _dot(pe_hi, w_mid)
    bc = pb[0:1, :] + b1_ref[c:c + 1, :]
    return jnp.concatenate([bc] * NSA_KV_HEADS, axis=1)


def _cmp_first_layer(x_ref, c, row0, n_seg, w1c_ref, pitch=CMP_STRIDE):
    acc = None
    for s in range(0, CMP_STRIDE, CMP_S_PER_DOT):
        xs = jnp.concatenate([x_ref[c, pl.ds(row0 + s + j, n_seg, stride=pitch), :].astype(BF16)
                              for j in range(CMP_S_PER_DOT)], axis=1)
        d = _dot(xs, w1c_ref[c, s * C_W:(s + CMP_S_PER_DOT) * C_W, :])
        acc = d if acc is None else acc + d
    return acc


def _cmp_second_layer_fm(p, bias, w2t):
    n = p.shape[0]
    h = p[:, :HC_W] + pltpu.roll(p[:, HC_W:], n - 1, 0) + bias
    return _dot_nt(w2t, jax.nn.gelu(h).astype(BF16))


def _cmp_prompt_fm_kernel(x_ref, w1c_ref, pe_ref, w1_ref, b1_ref, w2t_ref, o_ref):
    n_seg = o_ref.shape[1]
    for c in range(2):
        p = _cmp_first_layer(x_ref, c, 0, n_seg, w1c_ref)
        o_ref[c * C_W:(c + 1) * C_W, :] = _cmp_second_layer_fm(p, _cmp_bias_c(pe_ref, w1_ref, b1_ref, c), w2t_ref[c])


def _cmp_prompt_fm(x_tok, seq, w1c, pe_flat, cmp_w1, cmp_b1, w2c_t):
    bsz = x_tok.shape[1] // seq
    n_seg = seq // CMP_STRIDE
    const = lambda a: pl.BlockSpec(a.shape, lambda b: (0,) * a.ndim)
    return pl.pallas_call(
        _cmp_prompt_fm_kernel,
        grid=(bsz,),
        in_specs=[pl.BlockSpec((2, seq, C_W), lambda b: (0, b, 0)),
                  const(w1c), const(pe_flat), const(cmp_w1), const(cmp_b1), const(w2c_t)],
        out_specs=pl.BlockSpec((None, KV_W, n_seg), lambda b: (b, 0, 0)),
        out_shape=jax.ShapeDtypeStruct((bsz, KV_W, n_seg), F32),
        compiler_params=_cparams(("parallel",)),
        name="nsa_compress_prompt",
    )(x_tok, w1c, pe_flat, cmp_w1, cmp_b1, w2c_t)


def _scaled_group_queries(q, g):
    return (_group_queries(q, g).astype(F32) * QK_SCALE).astype(BF16)


def _cmp_branch_fm(qg, cmp_ref, g, valid):
    hd = NSA_HEAD_DIM
    kl, vl = g * hd, (NSA_KV_HEADS + g) * hd
    s_c = _dot(qg, cmp_ref[kl:kl + hd, :].astype(BF16))
    p_c = _masked_softmax(s_c, valid)
    return _dot_nt(p_c.astype(BF16), cmp_ref[vl:vl + hd, :].astype(BF16)), p_c


def _sum_heads(p, rows):
    out = p[0:rows]
    for r in range(1, NSA_GROUP):
        out = out + p[r * rows:(r + 1) * rows]
    return out


ATT_RB = 32


def _nsa_prompt_fm_kernel(q_ref, aux_ref, cmp_ref, kvs_ref, kvw_ref, o_ref,
                          s_scr, bias_scr, e_scr, m_scr, corr_scr, acc_scr, *, seq, nc):
    tq_n, tk_n, hd, grp = NSA_TQ, NSA_TK, NSA_HEAD_DIM, NSA_GROUP
    m_rows = grp * tq_n
    q0 = pl.program_id(1) * tq_n
    q_all = q_ref[...]
    gates = jax.nn.sigmoid(aux_ref[...])
    n_cmp = cmp_ref.shape[1]
    nb = seq // SLC_BLK

    tq_col = q0 + lax.broadcasted_iota(jnp.int32, (tq_n, 1), 0)
    tq_rows = jnp.concatenate([tq_col] * grp, axis=0)
    tq_lane = q0 + lax.broadcasted_iota(jnp.int32, (1, NSA_KV_HEADS * tq_n), 1) % tq_n
    t_end = lax.broadcasted_iota(jnp.int32, (1, n_cmp), 1) * CMP_STRIDE + (CMP_BLK - 1)
    ov_t = _overlap_t(nb, n_cmp, nc)
    n_chunks = (q0 + tq_n + tk_n - 1) // tk_n
    w_len = WINDOW + tq_n
    w0 = pl.multiple_of(jnp.maximum(q0 - WINDOW, 0), tq_n)

    qgs = [_scaled_group_queries(q_all, g) for g in range(NSA_KV_HEADS)]
    o_cs, imps = [], []
    for g in range(NSA_KV_HEADS):
        o_c, p_c = _cmp_branch_fm(qgs[g], cmp_ref, g, t_end <= tq_rows)
        o_cs.append(o_c)
        imps.append(_importance_t(_sum_heads(p_c, tq_n), ov_t))
    sel_all, _ = _topk_rows(_select_scores_t(jnp.concatenate(imps, axis=1), tq_lane), min(SLC_TOP_N, nb))

    tw = w0 + lax.broadcasted_iota(jnp.int32, (1, w_len), 1)
    bias_w = jnp.where((tw <= tq_col) & (tw > tq_col - WINDOW), 0.0, NEG)

    groups = range(NSA_KV_HEADS)

    def exp_rows(g, bias_g, width, online):
        per_head = tq_n // ATT_RB
        for i in range(m_rows // ATT_RB):
            rows = slice(i * ATT_RB, (i + 1) * ATT_RB)
            brows = slice((i % per_head) * ATT_RB, (i % per_head + 1) * ATT_RB)
            s = s_scr[g, rows, :width] + bias_scr[bias_g, brows, :width]
            m_new = jnp.max(s, axis=-1, keepdims=True)
            if online:
                m_old = m_scr[g, rows, :]
                m_new = jnp.maximum(m_old, m_new)
                corr_scr[g, rows, :] = jnp.exp2(m_old - m_new)
                m_scr[g, rows, :] = m_new
            e_scr[g, rows, :width] = jnp.exp2(s - m_new).astype(BF16)

    def keys_log2(ref, g, cols):
        return (ref[g * hd:(g + 1) * hd, cols] * LOG2E).astype(BF16)

    def gate_lane(g, r, branch):
        return hd + GATE_OFF + 3 * (grp * g + r) + branch

    def values_with_ones(ref, g, cols, branch):
        v = ref[(NSA_KV_HEADS + g) * hd:(NSA_KV_HEADS + g + 1) * hd, cols].astype(BF16)
        row = lax.broadcasted_iota(jnp.int32, v.shape, 0) + hd
        ones = functools.reduce(jnp.logical_or, [row == gate_lane(g, r, branch) for r in range(grp)])
        return jnp.concatenate([v, ones.astype(BF16)], axis=0)

    m_scr[...] = jnp.full(m_scr.shape, NEG, F32)
    acc_scr[...] = jnp.zeros(acc_scr.shape, F32)
    sel_ts = [sel_all[:, g * tq_n:(g + 1) * tq_n].astype(BF16) for g in groups]

    def slc_chunk(ci, carry):
        k0 = pl.multiple_of(ci * tk_n, tk_n)
        cols = pl.ds(k0, tk_n)
        kpos = k0 + lax.broadcasted_iota(jnp.int32, (1, tk_n), 1)
        expand = (lax.broadcasted_iota(jnp.int32, (nb, tk_n), 0) == kpos // SLC_BLK).astype(BF16)
        for g in groups:
            sel_k = _dot_tn(sel_ts[g], expand)
            bias_scr[g, :, :tk_n] = jnp.where((sel_k > 0.5) & (kpos <= tq_col), 0.0, NEG)
            s_scr[g, :, :tk_n] = _dot(qgs[g], keys_log2(kvs_ref, g, cols))
        for g in groups:
            exp_rows(g, g, tk_n, True)
        for g in groups:
            acc_scr[g] = corr_scr[g] * acc_scr[g] + _dot_nt(e_scr[g, :, :tk_n], values_with_ones(kvs_ref, g, cols, 1))
        return carry

    lax.fori_loop(0, n_chunks, slc_chunk, 0)
    wcols = pl.ds(w0, w_len)
    bias_scr[0, :, :w_len] = bias_w
    for g in groups:
        s_scr[g, :, :w_len] = _dot(qgs[g], keys_log2(kvw_ref, g, wcols))
    for g in groups:
        exp_rows(g, 0, w_len, False)
    acc_ws = [_dot_nt(e_scr[g, :, :w_len], values_with_ones(kvw_ref, g, wcols, 2)) for g in groups]
    gates_rot = pltpu.roll(gates, hd, 1)
    for g in groups:
        for r in range(grp):
            h = grp * g + r
            rows = slice(r * tq_n, (r + 1) * tq_n)
            acc_s, acc_w = acc_scr[g, rows, :], acc_ws[g][rows]
            f_s = gates_rot / jnp.maximum(acc_s, 1e-30)
            f_w = gates_rot / jnp.maximum(acc_w, 1e-30)
            ls, lw = gate_lane(g, r, 1), gate_lane(g, r, 2)
            gc = GATE_OFF + 3 * h
            o = (gates[:, gc:gc + 1] * o_cs[g][rows] + f_s[:, ls:ls + 1] * acc_s[:, :hd]
                 + f_w[:, lw:lw + 1] * acc_w[:, :hd])
            o_ref[:, h * hd:(h + 1) * hd] = o.astype(o_ref.dtype)


def _nsa_prompt_fm(q, aux, cmp_t, kvs_t, kvw_t):
    bsz, seq, _ = q.shape
    n_seg = cmp_t.shape[2]
    kern = functools.partial(_nsa_prompt_fm_kernel, seq=seq, nc=n_seg - CMP_R + 1)
    tile = lambda w: pl.BlockSpec((None, NSA_TQ, w), lambda b, t: (b, t, 0))
    whole = lambda n: pl.BlockSpec((None, KV_W, n), lambda b, t: (b, 0, 0))
    m_rows = NSA_GROUP * NSA_TQ
    width = max(NSA_TK, WINDOW + NSA_TQ)
    ng = NSA_KV_HEADS
    return pl.pallas_call(
        kern,
        grid=(bsz, seq // NSA_TQ),
        in_specs=[tile(NSA_WIDTH), tile(AUX_W), whole(n_seg), whole(seq), whole(seq)],
        out_specs=tile(NSA_WIDTH),
        out_shape=jax.ShapeDtypeStruct((bsz, seq, NSA_WIDTH), BF16),
        scratch_shapes=[pltpu.VMEM((ng, m_rows, width), F32),
                        pltpu.VMEM((ng, NSA_TQ, width), F32),
                        pltpu.VMEM((ng, m_rows, width), BF16),
                        pltpu.VMEM((ng, m_rows, 1), F32),
                        pltpu.VMEM((ng, m_rows, 1), F32),
                        pltpu.VMEM((ng, m_rows, 2 * NSA_HEAD_DIM), F32)],
        compiler_params=_cparams(("parallel", "arbitrary")),
        name="nsa_attn_prompt",
    )(q, aux, cmp_t, kvs_t, kvw_t)


def _cmp_sample_fm_kernel(pt_ref, cache_ref, w1f_ref, pe_ref, w1_ref, b1_ref, w2t_ref, o_ref,
                          xbuf, stage, p_scr, sem, *, steps_per_batch):
    b = pl.program_id(0)
    h = pl.program_id(1)
    step = b * steps_per_batch + h
    n_steps = pl.num_programs(0) * steps_per_batch
    pps = CMP_PAGES_PER_STEP
    segs = pps * SEGS_PER_PAGE

    def page_copy(bb, hh, p, slot):
        return pltpu.make_async_copy(cache_ref.at[pt_ref[bb, hh * pps + p]], xbuf.at[slot, p], sem.at[slot])

    def start_fetch(bb, hh, slot):
        for p in range(pps):
            page_copy(bb, hh, p, slot).start()

    @pl.when(step == 0)
    def _():
        start_fetch(b, h, 0)

    @pl.when(step + 1 < n_steps)
    def _():
        wrap = h + 1 == steps_per_batch
        start_fetch(jnp.where(wrap, b + 1, b), jnp.where(wrap, 0, h + 1), (step + 1) % 2)

    slot = step % 2
    for p in range(pps):
        page_copy(b, h, p, slot).wait()

    pages_per_chunk = CMP_ROW_CHUNK // SEGS_PER_PAGE
    for rc in range(segs // CMP_ROW_CHUNK):
        for p in range(rc * pages_per_chunk, (rc + 1) * pages_per_chunk):
            for c in range(2):
                tok = jnp.transpose(xbuf[slot, p, c * C_W:(c + 1) * C_W, :])
                for n in range(SEGS_PER_PAGE):
                    r0 = (p * SEGS_PER_PAGE + n) * STAGE_PITCH
                    stage[c, r0:r0 + CMP_STRIDE, :] = tok[n * CMP_STRIDE:(n + 1) * CMP_STRIDE]
        for c in range(2):
            p_rows = _cmp_first_layer(stage, c, rc * CMP_ROW_CHUNK * STAGE_PITCH, CMP_ROW_CHUNK, w1f_ref,
                                      pitch=STAGE_PITCH)
            r0 = pl.multiple_of(h * segs + rc * CMP_ROW_CHUNK, CMP_ROW_CHUNK)
            p_scr[c, pl.ds(r0, CMP_ROW_CHUNK), :] = p_rows

    @pl.when(h == steps_per_batch - 1)
    def _():
        for c in range(2):
            o_ref[c * C_W:(c + 1) * C_W, :] = _cmp_second_layer_fm(
                p_scr[c], _cmp_bias_c(pe_ref, w1_ref, b1_ref, c), w2t_ref[c])


def _cmp_sample_fm(page_table, cache_fm, w1_full, pe_flat, cmp_w1, cmp_b1, w2_bd_t):
    bsz, n_pages = page_table.shape
    steps = n_pages // CMP_PAGES_PER_STEP
    n_seg = n_pages * SEGS_PER_PAGE
    const = lambda a: pl.BlockSpec(a.shape, lambda b, h, pt: (0,) * a.ndim, pipeline_mode=pl.Buffered(1))
    grid_spec = pltpu.PrefetchScalarGridSpec(
        num_scalar_prefetch=1,
        grid=(bsz, steps),
        in_specs=[pl.BlockSpec(memory_space=pl.ANY), const(w1_full), const(pe_flat), const(cmp_w1),
                  const(cmp_b1), const(w2_bd_t)],
        out_specs=pl.BlockSpec((None, KV_W, n_seg), lambda b, h, pt: (b, 0, 0)),
        scratch_shapes=[pltpu.VMEM((2, CMP_PAGES_PER_STEP, KV_W, PAGE_SIZE), F32),
                        pltpu.VMEM((2, CMP_PAGES_PER_STEP * SEGS_PER_PAGE * STAGE_PITCH, C_W), F32),
                        pltpu.VMEM((2, n_seg, PC_W), F32),
                        pltpu.SemaphoreType.DMA((2,))],
    )
    return pl.pallas_call(
        functools.partial(_cmp_sample_fm_kernel, steps_per_batch=steps),
        grid_spec=grid_spec,
        out_shape=jax.ShapeDtypeStruct((bsz, KV_W, n_seg), F32),
        compiler_params=_cparams(("arbitrary", "arbitrary")),
        name="nsa_compress_sample",
    )(page_table, cache_fm, w1_full, pe_flat, cmp_w1, cmp_b1, w2_bd_t)


def _nsa_sample_select_fm_kernel(q_ref, cmp_ref, oc_ref, idx_ref, *, past_len, nb, nb_pad, nc):
    hd, grp = NSA_HEAD_DIM, NSA_GROUP
    m_rows = grp * SQ_ROWS
    q_all = q_ref[...]
    n_cmp = cmp_ref.shape[1]
    tq_rows = past_len + lax.broadcasted_iota(jnp.int32, (m_rows, 1), 0) % SQ_ROWS
    tq_lane = past_len + lax.broadcasted_iota(jnp.int32, (1, LANES), 1) % SQ_ROWS
    t_end = lax.broadcasted_iota(jnp.int32, (1, n_cmp), 1) * CMP_STRIDE + (CMP_BLK - 1)
    ov_t = _overlap_t(nb_pad, n_cmp, nc)
    psums = []
    for g in range(NSA_KV_HEADS):
        o_c, p_c = _cmp_branch_fm(_scaled_group_queries(q_all, g), cmp_ref, g, t_end <= tq_rows)
        psums.append(_sum_heads(p_c, SQ_ROWS))
        for r in range(grp):
            h = grp * g + r
            oc_ref[:, h * hd:(h + 1) * hd] = o_c[r * SQ_ROWS:(r + 1) * SQ_ROWS]
    psum = jnp.concatenate(psums + [jnp.zeros((LANES - NSA_KV_HEADS * SQ_ROWS, n_cmp), F32)], axis=0)
    score_t = _select_scores_t(_importance_t(psum, ov_t), tq_lane)
    rowid = lax.broadcasted_iota(jnp.int32, score_t.shape, 0)
    _, picks = _topk_rows(jnp.where(rowid < nb, score_t, -jnp.inf), min(SLC_TOP_N, nb))
    idx_ref[...] = jnp.concatenate(picks, axis=0)


def _nsa_sample_select_fm(q_pad, cmp_t, past_len, seq_new):
    bsz = q_pad.shape[0]
    n_seg = cmp_t.shape[2]
    nb = -(-(past_len + seq_new) // SLC_BLK)
    nb_pad = -(-nb // SUBLANES) * SUBLANES
    n_pick = min(SLC_TOP_N, nb)
    kern = functools.partial(_nsa_sample_select_fm_kernel, past_len=past_len, nb=nb, nb_pad=nb_pad,
                             nc=n_seg - CMP_R + 1)
    return pl.pallas_call(
        kern,
        grid=(bsz,),
        in_specs=[pl.BlockSpec((None, SQ_ROWS, NSA_WIDTH), lambda b: (b, 0, 0)),
                  pl.BlockSpec((None, KV_W, n_seg), lambda b: (b, 0, 0))],
        out_specs=[pl.BlockSpec((None, SQ_ROWS, NSA_WIDTH), lambda b: (b, 0, 0)),
                   pl.BlockSpec((None, n_pick, LANES), lambda b: (b, 0, 0))],
        out_shape=[jax.ShapeDtypeStruct((bsz, SQ_ROWS, NSA_WIDTH), F32),
                   jax.ShapeDtypeStruct((bsz, n_pick, LANES), jnp.int32)],
        compiler_params=_cparams(("parallel",)),
        name="nsa_select_sample",
    )(q_pad, cmp_t)


def _nsa_sample_attend_fm_kernel(pt_ref, idx_ref, q_ref, gate_ref, oc_ref, cache_ref, tail_ref, winp_ref,
                                 winn_ref, o_ref, kvbuf, sem, *, past_len, seq_new, n_pick):
    b = pl.program_id(0)
    nbatch = pl.num_programs(0)
    hd, grp = NSA_HEAD_DIM, NSA_GROUP
    n_items = seq_new * NSA_KV_HEADS * n_pick
    n_past_blocks = past_len // SLC_BLK
    blocks_per_page = PAGE_SIZE // SLC_BLK

    def block_id(bb, item):
        return jnp.minimum(idx_ref[bb * n_items + item], n_past_blocks)

    def copy(bb, item, slot):
        g = (item // n_pick) % NSA_KV_HEADS
        lanes = slice((item % n_pick) * PAGE_SIZE, (item % n_pick + 1) * PAGE_SIZE)
        rows = slice(g * hd, (g + 1) * hd)
        past_blk = jnp.minimum(block_id(bb, item), n_past_blocks - 1)
        page = pt_ref[bb, lax.shift_right_logical(past_blk, blocks_per_page.bit_length() - 1)]
        return pltpu.make_async_copy(cache_ref.at[page, :, rows, :], kvbuf.at[slot, item // n_pick, :, :, lanes],
                                     sem.at[slot])

    def start_fetch(bb, slot):
        for item in range(n_items):
            copy(bb, item, slot).start()

    @pl.when(b == 0)
    def _():
        start_fetch(b, 0)

    @pl.when(b + 1 < nbatch)
    def _():
        start_fetch(b + 1, (b + 1) % 2)

    slot = b % 2

    for item in range(n_items):
        copy(b, item, slot).wait()

    head_row = lax.broadcasted_iota(jnp.int32, (NSA_HEADS, 1), 0)
    tok = lax.broadcasted_iota(jnp.int32, (1, PAGE_SIZE), 1)
    wb, wn = winp_ref.shape[1], winn_ref.shape[1]
    tw_p = past_len - wb + lax.broadcasted_iota(jnp.int32, (1, wb), 1)
    tw_n = past_len + lax.broadcasted_iota(jnp.int32, (1, wn), 1)

    def heads16(x):
        return jnp.concatenate([x.astype(BF16), jnp.zeros_like(x, dtype=BF16)], axis=0)

    def merge_groups(per_group):
        out = per_group[0]
        for g in range(1, NSA_KV_HEADS):
            out = jnp.where(head_row >= g * grp, per_group[g], out)
        return out

    for qi in range(seq_new):
        tq = past_len + qi
        q16 = heads16(q_ref[qi] * QK_SCALE)
        gates = jax.nn.sigmoid(gate_ref[qi])
        o_s_g, o_w_g = [], []
        for g in range(NSA_KV_HEADS):
            qg_i = qi * NSA_KV_HEADS + g
            kl = g * hd
            pieces, new_picked = [], jnp.zeros((1, PAGE_SIZE), jnp.int32)
            for kk in range(n_pick):
                blk_k = block_id(b, qg_i * n_pick + kk)
                pieces.append((blk_k < n_past_blocks) & (tok // SLC_BLK == blk_k % blocks_per_page)
                              & (blk_k * SLC_BLK + tok % SLC_BLK <= tq))
                new_picked = jnp.maximum(new_picked, (blk_k == n_past_blocks).astype(jnp.int32))
            pieces.append((new_picked > 0) & (tok < SLC_BLK) & (past_len + tok <= tq))
            mask = jnp.concatenate(pieces, axis=1)
            kcat = jnp.concatenate([kvbuf[slot, qg_i, 0].astype(BF16), tail_ref[0, kl:kl + hd, :].astype(BF16)], axis=1)
            vcat = jnp.concatenate([kvbuf[slot, qg_i, 1].astype(BF16), tail_ref[1, kl:kl + hd, :].astype(BF16)], axis=1)
            p = _masked_softmax(_dot(q16, kcat)[:NSA_HEADS], mask)
            o_s_g.append(_dot_nt(heads16(p), vcat)[:NSA_HEADS])
            vl = (NSA_KV_HEADS + g) * hd
            bias_p = jnp.where((tw_p <= tq) & (tw_p > tq - WINDOW) & (tw_p >= 0), 0.0, NEG)
            bias_n = jnp.where((tw_n <= tq) & (tw_n > tq - WINDOW), 0.0, NEG)
            s_p = _dot(q16, winp_ref[kl:kl + hd, :].astype(BF16))[:NSA_HEADS] + bias_p
            s_n = _dot(q16, winn_ref[kl:kl + hd, :].astype(BF16))[:NSA_HEADS] + bias_n
            m = jnp.maximum(jnp.max(s_p, axis=-1, keepdims=True), jnp.max(s_n, axis=-1, keepdims=True))
            e_p, e_n = jnp.exp(s_p - m), jnp.exp(s_n - m)
            den = jnp.maximum(jnp.sum(e_p, axis=-1, keepdims=True) + jnp.sum(e_n, axis=-1, keepdims=True), 1e-30)
            o_w_g.append((_dot_nt(heads16(e_p), winp_ref[vl:vl + hd, :].astype(BF16))
                          + _dot_nt(heads16(e_n), winn_ref[vl:vl + hd, :].astype(BF16)))[:NSA_HEADS] / den)
        o_ref[qi] = (gates[:, 0:1] * oc_ref[qi] + gates[:, 1:2] * merge_groups(o_s_g)
                     + gates[:, 2:3] * merge_groups(o_w_g))


def _nsa_sample_attend_fm(page_table, idx_flat, q_heads, gate_logits, o_c, cache_fm, tail_fm, win_past, win_new,
                          past_len, seq_new, n_pick):
    bsz = q_heads.shape[0]
    wb, wn = win_past.shape[2], win_new.shape[2]
    kern = functools.partial(_nsa_sample_attend_fm_kernel, past_len=past_len, seq_new=seq_new, n_pick=n_pick)
    per_b = lambda n, w: pl.BlockSpec((None, n, w), lambda b, pt, ix: (b, 0, 0))
    per_bq = lambda w: pl.BlockSpec((None, seq_new, NSA_HEADS, w), lambda b, pt, ix: (b, 0, 0, 0))
    n_qg = seq_new * NSA_KV_HEADS
    buf = pltpu.VMEM((2, n_qg, 2, NSA_HEAD_DIM, n_pick * PAGE_SIZE), F32)
    grid_spec = pltpu.PrefetchScalarGridSpec(
        num_scalar_prefetch=2,
        grid=(bsz,),
        in_specs=[per_bq(NSA_HEAD_DIM), per_bq(LANES), per_bq(NSA_HEAD_DIM),
                  pl.BlockSpec(memory_space=pl.ANY),
                  pl.BlockSpec((None, 2, C_W, LANES), lambda b, pt, ix: (b, 0, 0, 0)),
                  per_b(KV_W, wb), per_b(KV_W, wn)],
        out_specs=per_bq(NSA_HEAD_DIM),
        scratch_shapes=[buf, pltpu.SemaphoreType.DMA((2,))],
    )
    return pl.pallas_call(
        kern,
        grid_spec=grid_spec,
        out_shape=jax.ShapeDtypeStruct((bsz, seq_new, NSA_HEADS, NSA_HEAD_DIM), F32),
        compiler_params=_cparams(("arbitrary",)),
        name="nsa_attend_sample",
    )(page_table, idx_flat, q_heads, gate_logits, o_c, cache_fm, tail_fm, win_past, win_new)


PROMPT_TM = 512
GLA_TL = 512
SAMPLE_GLA_ROWS = 16
GLA_PROMPT_NBB = 2
GLA_SAMPLE_NBB = 8


def _pad_rows(x, n):
    return jnp.pad(x, ((0, 0), (0, n - x.shape[1]), (0, 0)))


def kernel(x_prompt, x_sample, state_gla, cache_cmp_kv, cache_slc_kv, cache_win_kv, page_table, c_prompt,
           c_sample, ln_in_g, ln_in_b, w_ada, b_ada, w_in, gla_w_a2, gla_b_a, gla_norm_g, cmp_pe, cmp_w1,
           cmp_b1, cmp_w2, w_o, ln1_g, ln1_b, w_ffn_in, w_ffn_out, ln2_g, ln2_b):
    assert w_in.shape[0] == DEPTH == 1
    l = 0
    bp, lp, d = x_prompt.shape
    bs, ls, _ = x_sample.shape
    n_pool = cache_cmp_kv.shape[1]
    n_pages = page_table.shape[1]
    past_len = n_pages * PAGE_SIZE
    wb = cache_win_kv.shape[2]
    assert ((past_len + ls) // CMP_STRIDE) * CMP_STRIDE <= past_len and past_len % SLC_BLK == 0
    assert ls <= SQ_ROWS and ls <= SLC_BLK and wb == WINDOW

    w_perm, w_kv_t = _permute_w_in(w_in[l])
    w_o_b, w_fi_b, w_fo_b = w_o[l].astype(BF16), w_ffn_in[l].astype(BF16), w_ffn_out[l].astype(BF16)
    w_a2p = jnp.zeros((AUX_W, GLA_KW), F32).at[:GLA_RANK].set(gla_w_a2[l])
    b_a = gla_b_a[l].reshape(1, GLA_KW)
    norm_g = gla_norm_g[l].reshape(1, GLA_DV)
    w1c, w2c_t, pe_flat = _cmp_weights_fm(cmp_pe[l], cmp_w1[l], cmp_w2[l])
    kvt = (2, NSA_KV_HEADS, NSA_HEAD_DIM)

    def fm_view(a):
        return jnp.transpose(a, (0, 2, 3, 4, 1)).reshape(a.shape[0], KV_W, a.shape[1])

    def tok_view(a_fm):
        n, _, t = a_fm.shape
        return jnp.transpose(a_fm.reshape((n,) + kvt + (t,)), (0, 4, 1, 2, 3))[None]

    mod = _ada(jnp.concatenate([c_prompt, c_sample], axis=0), w_ada[l], b_ada[l])
    mods_p = [m[:bp, None, :] for m in jnp.split(mod, 6, axis=-1)]
    mods_s = [jnp.repeat(m[bp:], ls, axis=0)[None] for m in jnp.split(mod, 6, axis=-1)]

    def out_ffn(x2d, o_g, o_n, mods, tm, rpm):
        sh1, sc1, ga1, sh2, sc2, ga2 = mods
        return _out_ffn(x2d, o_g, o_n, ga1, sc2, sh2, ga2, ln_in_g, ln_in_b, w_o_b, ln1_g[l], ln1_b[l], w_fi_b,
                        w_fo_b, ln2_g[l], ln2_b[l], tm, rpm)

    xp2 = x_prompt.reshape(bp * lp, d)
    rpm = lp // PROMPT_TM
    qk, v, r, qn, kvc, _, _, aux, kvc_t, kvs_t, kvw_t = _inproj(
        xp2, mods_p[1], mods_p[0], ln_in_g, ln_in_b, w_perm, w_kv_t, PROMPT_TM, rpm, seq_per_batch=lp)
    b3 = lambda a: a.reshape(bp, lp, a.shape[-1])
    o_g, s_p = _gla(b3(qk), b3(v), b3(r), b3(aux), w_a2p, b_a, norm_g, jnp.zeros((bp, GLA_KW, GLA_DV), F32),
                    chunk=GLA_CHUNK, sub=GLA_SUB, tl=GLA_TL, l_valid=lp, nbb=GLA_PROMPT_NBB)
    cmp_t_p = _cmp_prompt_fm(kvc, lp, w1c, pe_flat, cmp_w1[l], cmp_b1[l], w2c_t)
    o_n = _nsa_prompt_fm(b3(qn), b3(aux), cmp_t_p, kvs_t, kvw_t)
    y_p = out_ffn(xp2, o_g.reshape(bp * lp, GLA_WIDTH), o_n.reshape(bp * lp, NSA_WIDTH), mods_p, PROMPT_TM, rpm)
    w_keep = min(WINDOW, lp)
    outs_p = (y_p.reshape(bp, lp, d), s_p.reshape(1, bp, GLA_HEADS, GLA_DK, GLA_DV),
              tok_view(kvc_t), tok_view(kvs_t), tok_view(kvw_t[:, :, lp - w_keep:]))

    ts = bs * ls
    xs2 = x_sample.reshape(ts, d)
    qk, v, r, qn, kvc, kvs, kvw, aux = _inproj(xs2, mods_s[1], mods_s[0], ln_in_g, ln_in_b, w_perm, w_kv_t, ts, 1)
    s3 = lambda a: a.reshape(bs, ls, a.shape[-1])
    g16 = lambda a: _pad_rows(s3(a), SAMPLE_GLA_ROWS)
    o_g, s_s = _gla(g16(qk), g16(v), g16(r), g16(aux), w_a2p, b_a, norm_g, state_gla[l].reshape(bs, GLA_KW, GLA_DV),
                    chunk=SAMPLE_GLA_ROWS, sub=SAMPLE_GLA_ROWS, tl=SAMPLE_GLA_ROWS, l_valid=ls,
                    nbb=GLA_SAMPLE_NBB)
    o_g = o_g[:, :ls].reshape(ts, GLA_WIDTH)
    kvc = jnp.concatenate([kvc[0], kvc[1]], axis=1)
    cmp_t_s = _cmp_sample_fm(page_table, fm_view(cache_cmp_kv[l]), w1c, pe_flat, cmp_w1[l], cmp_b1[l], w2c_t)
    q_pad = _pad_rows(s3(qn), SQ_ROWS)
    o_c, idx = _nsa_sample_select_fm(q_pad, cmp_t_s, past_len, ls)
    n_pick = idx.shape[1]
    idx = idx[:, :, :NSA_KV_HEADS * SQ_ROWS].reshape(bs, n_pick, NSA_KV_HEADS, SQ_ROWS)[..., :ls]
    idx_flat = jnp.transpose(idx, (0, 3, 2, 1)).reshape(-1)
    new_fm =lambda a: jnp.pad(jnp.transpose(s3(a), (0, 2, 1)), ((0, 0), (0, 0), (0, LANES - ls)))
    win_past = fm_view(cache_win_kv[l])
    per_head = lambda a, w: a.reshape(bs, -1, NSA_HEADS, w)[:, :ls]
    gate_logits = jnp.pad(per_head(aux[:, GATE_OFF:GATE_OFF + N_GATES], 3), ((0, 0),) * 3 + ((0, LANES - 3),))
    o_n = _nsa_sample_attend_fm(page_table, idx_flat, per_head(qn.astype(F32), NSA_HEAD_DIM), gate_logits,
                                per_head(o_c, NSA_HEAD_DIM),
                                fm_view(cache_slc_kv[l]).reshape(n_pool, 2, C_W, PAGE_SIZE),
                                new_fm(kvs).reshape(bs, 2, C_W, LANES), win_past, new_fm(kvw),
                                past_len, ls, n_pick)
    o_n = o_n.reshape(ts, NSA_WIDTH).astype(BF16)
    y_s = out_ffn(xs2, o_g, o_n, mods_s, ts, 1)
    win_s = jnp.concatenate([win_past[:, :, ls:], jnp.transpose(s3(kvw), (0, 2, 1))], axis=2)
    outs_s = (y_s.reshape(bs, ls, d), s_s.reshape(1, bs, GLA_HEADS, GLA_DK, GLA_DV),
              kvc.reshape((1, bs, ls) + kvt), kvs.reshape((1, bs, ls) + kvt), tok_view(win_s))

    return (outs_p[0], outs_s[0], outs_p[1], outs_s[1], outs_p[2], outs_s[2], outs_p[3], outs_s[3],
            outs_p[4], outs_s[4])
```

```python
import functools
import math

import numpy as np
import jax
import jax.numpy as jnp
from jax import lax
from jax.experimental import pallas as pl
from jax.experimental.pallas import tpu as pltpu

F32 = jnp.float32
BF16 = jnp.bfloat16

D_MODEL = 1024
DEPTH = 1
PAGE_SIZE = 128
GLA_HEADS = 4
GLA_DV = D_MODEL // (2 * GLA_HEADS)
GLA_DK = GLA_DV // 2
GLA_RANK = 16
GLA_TAU = 16.0
GLA_CHUNK = 64
GLA_SUB = 32
GLA_WIDTH = GLA_HEADS * GLA_DV
GLA_KW = GLA_HEADS * GLA_DK
NSA_HEADS = 8
NSA_KV_HEADS = 2
NSA_GROUP = NSA_HEADS // NSA_KV_HEADS
NSA_HEAD_DIM = D_MODEL // (2 * NSA_HEADS)
NSA_WIDTH = NSA_HEADS * NSA_HEAD_DIM
CMP_BLK = 32
CMP_STRIDE = 16
CMP_HIDDEN = 2 * NSA_HEAD_DIM
SLC_BLK = 64
SLC_TOP_N = 16
WINDOW = 512
FORCE_BONUS = 1e4
NEG = -1e30
D_FF = -(-8 * D_MODEL // (3 * 256)) * 256
ALPHA = (2 * DEPTH) ** 0.25
KV_W = 2 * NSA_KV_HEADS * NSA_HEAD_DIM
N_GATES = 3 * NSA_HEADS
IN_SIZES = (GLA_KW, GLA_KW, GLA_WIDTH, GLA_RANK, GLA_WIDTH, NSA_WIDTH, KV_W, KV_W, KV_W, N_GATES)
IN_WIDTH = sum(IN_SIZES)
LN_EPS = 1e-5

LANES = 128
SUBLANES = 8
VMEM_LIMIT_BYTES = 56 * 1024 * 1024

AUX_W = LANES
IN_GROUPS = (2 * GLA_KW, GLA_WIDTH, GLA_WIDTH, NSA_WIDTH, KV_W, KV_W, KV_W, AUX_W)
IN_PERM_W = sum(IN_GROUPS)
GATE_OFF = GLA_RANK


def _cparams(sem):
    return pltpu.CompilerParams(dimension_semantics=sem, vmem_limit_bytes=VMEM_LIMIT_BYTES)


def _split3(a):
    hi = a.astype(BF16)
    r1 = a - hi.astype(F32)
    mid = r1.astype(BF16)
    lo = (r1 - mid.astype(F32)).astype(BF16)
    return hi, mid, lo


def _dot(a, b):
    return jnp.dot(a, b, preferred_element_type=F32)


def _dot_nt(a, b):
    return lax.dot_general(a, b, (((1,), (1,)), ((), ())), preferred_element_type=F32)


def _dot_tn(a, b):
    return lax.dot_general(a, b, (((0,), (0,)), ((), ())), preferred_element_type=F32)


def _layer_norm(x, g, b):
    mu = jnp.mean(x, axis=-1, keepdims=True)
    xc = x - mu
    var = jnp.mean(xc * xc, axis=-1, keepdims=True)
    return xc * lax.rsqrt(var + LN_EPS) * g + b


def _ada_kernel(c_ref, w_ref, b_ref, o_ref):
    c = c_ref[...]
    a = (c * jax.nn.sigmoid(c)).astype(BF16)
    o_ref[...] = _dot(a, w_ref[...].astype(BF16)) + b_ref[...]


def _ada(c, w_ada, b_ada):
    n, d = c.shape
    m = w_ada.shape[1]
    tn = D_MODEL
    return pl.pallas_call(
        _ada_kernel,
        grid=(m // tn,),
        in_specs=[pl.BlockSpec((n, d), lambda j: (0, 0)),
                  pl.BlockSpec((d, tn), lambda j: (0, j)),
                  pl.BlockSpec((1, tn), lambda j: (0, j))],
        out_specs=pl.BlockSpec((n, tn), lambda j: (0, j)),
        out_shape=jax.ShapeDtypeStruct((n, m), F32),
        compiler_params=_cparams(("parallel",)),
        name="ada_mod",
    )(c, w_ada, b_ada.reshape(1, m))


N_KV_GROUPS = 3


def _inproj_kernel(x_ref, sc_ref, sh_ref, g_ref, b_ref, w_ref, *o_refs, feature_major):
    xn = _layer_norm(x_ref[...], g_ref[...], b_ref[...])
    u = (xn * (1.0 + sc_ref[...]) + sh_ref[...]).astype(BF16)
    plain = dict(zip(("qk", "v", "r", "qn", "kvc", "aux"), o_refs[:6]))
    extra = dict(zip(("kvc", "kvs", "kvw") if feature_major else ("kvs", "kvw"), o_refs[6:]))
    lo = 0
    for name, wdt in zip(IN_GROUP_NAMES, IN_GROUPS):
        z = _dot(u, w_ref[:, lo:lo + wdt])
        lo += wdt
        if name == "kvc":
            plain[name][0] = z[:, :KV_W // 2]
            plain[name][1] = z[:, KV_W // 2:]
        elif name in plain:
            plain[name][...] = z.astype(plain[name].dtype)
        if name in extra:
            extra[name][...] = jnp.transpose(z) if feature_major else z


IN_GROUP_NAMES = ("qk", "v", "r", "qn", "kvc", "kvs", "kvw", "aux")


def _inproj(x, sc, sh, ln_g, ln_b, w_perm, tm, rows_per_mod, seq_per_batch=None):
    t, d = x.shape
    r = sc.shape[1]
    feature_major = seq_per_batch is not None
    mod_spec = pl.BlockSpec((None, r, d), lambda i: (i // rows_per_mod, 0, 0))
    tok = lambda w, dt: (pl.BlockSpec((tm, w), lambda i: (i, 0)), jax.ShapeDtypeStruct((t, w), dt))
    outs = [tok(2 * GLA_KW, BF16), tok(GLA_WIDTH, BF16), tok(GLA_WIDTH, BF16), tok(NSA_WIDTH, BF16),
            (pl.BlockSpec((2, tm, KV_W // 2), lambda i: (0, i, 0)), jax.ShapeDtypeStruct((2, t, KV_W // 2), F32)),
            tok(AUX_W, F32)]
    if feature_major:
        tpb = seq_per_batch // tm
        outs += [(pl.BlockSpec((None, KV_W, tm), lambda i: (i // tpb, 0, i % tpb)),
                  jax.ShapeDtypeStruct((t // seq_per_batch, KV_W, seq_per_batch), F32))] * N_KV_GROUPS
    else:
        outs += [tok(KV_W, F32)] * 2
    return pl.pallas_call(
        functools.partial(_inproj_kernel, feature_major=feature_major),
        grid=(t // tm,),
        in_specs=[pl.BlockSpec((tm, d), lambda i: (i, 0)), mod_spec, mod_spec,
                  pl.BlockSpec((1, d), lambda i: (0, 0)), pl.BlockSpec((1, d), lambda i: (0, 0)),
                  pl.BlockSpec((d, IN_PERM_W), lambda i: (0, 0))],
        out_specs=[o[0] for o in outs],
        out_shape=[o[1] for o in outs],
        compiler_params=_cparams(("parallel",)),
        name="ln_mod_inproj",
    )(x, sc, sh, ln_g.reshape(1, d), ln_b.reshape(1, d), w_perm)


def _permute_w_in(w_in):
    q_g, k_g, v_g, a_g, r_g, q_n, kv_c, kv_s, kv_w, g_n = jnp.split(w_in, np.cumsum(IN_SIZES)[:-1], axis=1)
    pad = jnp.zeros((w_in.shape[0], AUX_W - GLA_RANK - N_GATES), w_in.dtype)
    return jnp.concatenate([q_g, k_g, v_g, r_g, q_n, kv_c, kv_s, kv_w, a_g, g_n, pad], axis=1).astype(BF16)


FF_CHUNK = 256


def _out_ffn_kernel(x_ref, og_ref, on_ref, ga1_ref, sc2_ref, sh2_ref, ga2_ref, lng_ref, lnb_ref,
                    wo_ref, l1g_ref, l1b_ref, wfi_ref, wfo_ref, l2g_ref, l2b_ref, y_ref):
    x = _layer_norm(x_ref[...], lng_ref[...], lnb_ref[...])
    mix = _dot(og_ref[...], wo_ref[0:GLA_WIDTH, :]) + _dot(on_ref[...], wo_ref[GLA_WIDTH:, :])
    x1 = _layer_norm(ALPHA * x + ga1_ref[...] * mix, l1g_ref[...], l1b_ref[...])
    u2 = (x1 * (1.0 + sc2_ref[...]) + sh2_ref[...]).astype(BF16)
    ffn = jnp.zeros(x1.shape, F32)
    for c in range(D_FF // FF_CHUNK):
        lo = c * FF_CHUNK
        gate = _dot(u2, wfi_ref[:, lo:lo + FF_CHUNK])
        up = _dot(u2, wfi_ref[:, D_FF + lo:D_FF + lo + FF_CHUNK])
        f = (gate * jax.nn.sigmoid(gate) * up).astype(BF16)
        ffn = ffn + _dot(f, wfo_ref[lo:lo + FF_CHUNK, :])
    y_ref[...] = _layer_norm(ALPHA * x1 + ga2_ref[...] * ffn, l2g_ref[...], l2b_ref[...])


def _out_ffn(x, o_g, o_n, ga1, sc2, sh2, ga2, ln_in_g, ln_in_b, w_o, ln1_g, ln1_b, w_fi, w_fo, ln2_g, ln2_b,
             tm, rows_per_mod):
    t, d = x.shape
    r = ga1.shape[1]
    mod_spec = pl.BlockSpec((None, r, d), lambda i: (i // rows_per_mod, 0, 0))
    vec = lambda: pl.BlockSpec((1, d), lambda i: (0, 0))
    const = lambda shp: pl.BlockSpec(shp, lambda i: (0, 0), pipeline_mode=pl.Buffered(1))
    row = lambda a: a.reshape(1, d)
    return pl.pallas_call(
        _out_ffn_kernel,
        grid=(t // tm,),
        in_specs=[pl.BlockSpec((tm, d), lambda i: (i, 0)),
                  pl.BlockSpec((tm, GLA_WIDTH), lambda i: (i, 0)),
                  pl.BlockSpec((tm, NSA_WIDTH), lambda i: (i, 0)),
                  mod_spec, mod_spec, mod_spec, mod_spec, vec(), vec(),
                  const((d, d)), vec(), vec(), const((d, 2 * D_FF)), const((D_FF, d)), vec(), vec()],
        out_specs=pl.BlockSpec((tm, d), lambda i: (i, 0)),
        out_shape=jax.ShapeDtypeStruct((t, d), F32),
        compiler_params=_cparams(("parallel",)),
        name="outproj_ffn",
    )(x, o_g, o_n, ga1, sc2, sh2, ga2, row(ln_in_g), row(ln_in_b), w_o, row(ln1_g), row(ln1_b), w_fi, w_fo,
      row(ln2_g), row(ln2_b))


GLA_EXP_CLAMP = 80.0


def _gla_kernel(qk_ref, v_ref, r_ref, aux_ref, wa_ref, ba_ref, ng_ref, s0_ref, o_ref, sout_ref, s_scr,
                *, chunk, sub, tl, l_valid, l_pad, nbb):
    t = pl.program_id(1)
    c = chunk
    n_sub = c // sub
    hw = GLA_KW

    @pl.when(t == 0)
    def _():
        s_scr[...] = s0_ref[...]

    ri = lax.broadcasted_iota(jnp.int32, (c, c), 0)
    ci = lax.broadcasted_iota(jnp.int32, (c, c), 1)
    causal = ci <= ri
    tril = causal.astype(BF16)
    rowid = lax.broadcasted_iota(jnp.int32, (c, hw), 0)
    head_of_lane = lax.broadcasted_iota(jnp.int32, (c, hw), 1) // GLA_DK
    wa = wa_ref[...]
    wa_hi = wa.astype(BF16)
    wa_mid = (wa - wa_hi.astype(F32)).astype(BF16)
    ba = ba_ref[...]
    ng = ng_ref[...]

    def body(i, carry):
        for e in range(nbb):
            s_scr[e] = one_chunk(i, e, s_scr[e])
        return carry

    def one_chunk(i, e, s_old):
        r0 = pl.multiple_of(i * c, c)
        aux = aux_ref[e, pl.ds(r0, c), :]
        a_hi = aux.astype(BF16)
        a_mid = (aux - a_hi.astype(F32)).astype(BF16)
        z = _dot(a_hi, wa_hi) + _dot(a_mid, wa_hi) + _dot(a_hi, wa_mid) + ba
        g = jax.nn.log_sigmoid(z) / GLA_TAU
        if l_pad != l_valid:
            g = jnp.where(t * tl + r0 + rowid < l_valid, g, 0.0)
        g_hi, g_mid, g_lo = _split3(g)
        b = _dot(tril, g_hi) + _dot(tril, g_mid) + _dot(tril, g_lo)
        qk = qk_ref[e, pl.ds(r0, c), :]
        q = qk[:, :hw].astype(F32) * (GLA_DK ** -0.5)
        k = qk[:, hw:].astype(F32)
        v = v_ref[e, pl.ds(r0, c), :]
        b_last = b[c - 1:c, :]

        def heads_on_rows(x):
            return jnp.concatenate([jnp.where(head_of_lane == h, x, 0.0) for h in range(GLA_HEADS)], axis=0)

        o_inter =_dot(heads_on_rows(q * jnp.exp(b)).astype(BF16), s_old.astype(BF16))

        q_parts, k_parts = [], []
        for s_i in range(n_sub):
            b_ref = b[s_i * sub - 1:s_i * sub, :] if s_i > 0 else jnp.zeros((1, hw), F32)
            in_rows = (rowid >= s_i * sub) & (rowid < (s_i + 1) * sub)
            qt = jnp.where(in_rows, q * jnp.exp(jnp.minimum(b - b_ref, 0.0)), 0.0)
            kt = jnp.where(rowid < (s_i + 1) * sub, k * jnp.exp(jnp.minimum(b_ref - b, GLA_EXP_CLAMP)), 0.0)
            q_parts.append(heads_on_rows(qt).astype(BF16))
            k_parts.append(kt.astype(BF16))
        q_cat = jnp.concatenate(q_parts, axis=1) if n_sub > 1 else q_parts[0]
        k_cat = jnp.concatenate(k_parts, axis=1) if n_sub > 1 else k_parts[0]
        att = _dot_nt(q_cat, k_cat)

        r_t = r_ref[e, pl.ds(r0, c), :].astype(F32)
        for h in range(GLA_HEADS):
            att_h = jnp.where(causal, att[h * c:(h + 1) * c, :], 0.0).astype(BF16)
            o_h = o_inter[h * c:(h + 1) * c, :] + _dot(att_h, v[:, h * GLA_DV:(h + 1) * GLA_DV])
            o_h = o_h * lax.rsqrt(jnp.mean(o_h * o_h, axis=-1, keepdims=True) + 1e-6) * ng
            r_h = r_t[:, h * GLA_DV:(h + 1) * GLA_DV]
            o_ref[e, pl.ds(r0, c), h * GLA_DV:(h + 1) * GLA_DV] = (
                o_h * (r_h * jax.nn.sigmoid(r_h))).astype(o_ref.dtype)

        kd = jnp.concatenate([k * jnp.exp(b_last - b), jnp.zeros((LANES - c, hw), F32)], axis=0)
        kd_t = jnp.transpose(kd).astype(BF16)
        v_pad = jnp.concatenate([v, jnp.zeros((LANES - c, GLA_WIDTH), v.dtype)], axis=0)
        upd = jnp.concatenate([_dot(kd_t[h * GLA_DK:(h + 1) * GLA_DK, :], v_pad[:, h * GLA_DV:(h + 1) * GLA_DV])
                               for h in range(GLA_HEADS)], axis=0)
        decay = jnp.transpose(jnp.broadcast_to(jnp.exp(b_last), (LANES, hw)))
        return decay * s_old + upd

    lax.fori_loop(0, tl // c, body, 0)

    @pl.when(t == pl.num_programs(1) - 1)
    def _():
        sout_ref[...] = s_scr[...]


def _gla(qk, v, r, aux, w_a2p, b_a, norm_g, s0, *, chunk, sub, tl, l_valid, nbb):
    bsz, l_pad, _ = qk.shape
    nt = l_pad // tl
    kern = functools.partial(_gla_kernel, chunk=chunk, sub=sub, tl=tl, l_valid=l_valid, l_pad=l_pad, nbb=nbb)
    tile = lambda w: pl.BlockSpec((nbb, tl, w), lambda b, t: (b, t, 0))
    full = lambda shp: pl.BlockSpec(shp, lambda b, t: (0, 0))
    st = pl.BlockSpec((nbb, GLA_KW, GLA_DV), lambda b, t: (b, 0, 0))
    return pl.pallas_call(
        kern,
        grid=(bsz // nbb, nt),
        in_specs=[tile(2 * GLA_KW), tile(GLA_WIDTH), tile(GLA_WIDTH), tile(AUX_W),
                  full((AUX_W, GLA_KW)), full((1, GLA_KW)), full((1, GLA_DV)), st],
        out_specs=[tile(GLA_WIDTH), st],
        out_shape=[jax.ShapeDtypeStruct((bsz, l_pad, GLA_WIDTH), BF16),
                   jax.ShapeDtypeStruct((bsz, GLA_KW, GLA_DV), F32)],
        scratch_shapes=[pltpu.VMEM((nbb, GLA_KW, GLA_DV), F32)],
        compiler_params=_cparams(("parallel", "arbitrary")),
        name="gla_scan",
    )(qk, v, r, aux, w_a2p, b_a, norm_g, s0)


CMP_R = CMP_BLK // CMP_STRIDE


QK_SCALE = NSA_HEAD_DIM ** -0.5
LOG2E = math.log2(math.e)


def _masked_softmax(s, mask):
    s = jnp.where(mask, s, NEG)
    e = jnp.where(mask, jnp.exp(s - jnp.max(s, axis=-1, keepdims=True)), 0.0)
    return e / jnp.maximum(jnp.sum(e, axis=-1, keepdims=True), 1e-30)


def _group_queries(q, g):
    hd = NSA_HEAD_DIM
    return jnp.concatenate([q[:, (NSA_GROUP * g + r) * hd:(NSA_GROUP * g + r + 1) * hd]
                            for r in range(NSA_GROUP)], axis=0)


def _topk_rows(score_t, n_pick):
    nb, nq = score_t.shape
    rowid = lax.broadcasted_iota(jnp.int32, (nb, nq), 0)
    taken = jnp.zeros((nb, nq), jnp.int32)
    picks = []
    for _ in range(n_pick):
        free = taken == 0
        cand = jnp.where(free, score_t, -jnp.inf)
        m = jnp.max(cand, axis=0, keepdims=True)
        hit = free & (cand == m)
        idx = jnp.min(jnp.where(hit, rowid, nb), axis=0, keepdims=True)
        taken = jnp.where(rowid == idx, 1, taken)
        picks.append(idx)
    return taken.astype(F32), picks


def _importance_t(psum, ov_t):
    hi, mid, lo = _split3(psum)
    return _dot_nt(ov_t, hi) + _dot_nt(ov_t, mid) + _dot_nt(ov_t, lo)


def _overlap_t(nb, nc_pad, nc):
    j = lax.broadcasted_iota(jnp.int32, (nb, nc_pad), 0) * SLC_BLK
    i = lax.broadcasted_iota(jnp.int32, (nb, nc_pad), 1) * CMP_STRIDE
    return ((i < j + SLC_BLK) & (i + CMP_BLK > j) & (i < nc * CMP_STRIDE)).astype(BF16)


def _select_scores_t(imp_t, tq_row):
    nb, nq = imp_t.shape
    j = lax.broadcasted_iota(jnp.int32, (nb, nq), 0)
    cur = tq_row // SLC_BLK
    forced = (j == 0) | (j == cur) | (j == cur - 1)
    return jnp.where(j * SLC_BLK <= tq_row, imp_t + FORCE_BONUS * forced.astype(F32), -jnp.inf)


NSA_TQ = 128
NSA_TK = 512


SEGS_PER_PAGE = PAGE_SIZE // CMP_STRIDE
CMP_PAGES_PER_STEP = 64
CMP_ROW_CHUNK = 256
STAGE_PITCH = 24


SQ_ROWS = SUBLANES


C_W = NSA_KV_HEADS * NSA_HEAD_DIM
HC_W = NSA_KV_HEADS * CMP_HIDDEN
PC_W = CMP_R * HC_W
MXU_DEPTH = 256
CMP_S_PER_DOT = MXU_DEPTH // C_W


def _cmp_weights_fm(cmp_pe, cmp_w1, cmp_w2):
    eye = jnp.eye(NSA_KV_HEADS, dtype=F32)
    w1r = cmp_w1.reshape(2, CMP_R, CMP_STRIDE, NSA_HEAD_DIM, CMP_HIDDEN)
    w1c = jnp.einsum('cmsdh,gG->csgdmGh', w1r, eye).reshape(2, CMP_STRIDE * C_W, PC_W).astype(BF16)
    w2c_t = jnp.einsum('chd,gG->cGdgh', cmp_w2, eye).reshape(2, C_W, HC_W).astype(BF16)
    pe_flat = jnp.transpose(cmp_pe, (1, 0, 2)).reshape(2, CMP_BLK * NSA_HEAD_DIM)
    return w1c, w2c_t, pe_flat


def _cmp_bias_c(pe_ref, w1_ref, b1_ref, c):
    pe = jnp.broadcast_to(pe_ref[c:c + 1, :], (SUBLANES, pe_ref.shape[1]))
    w1c = w1_ref[c]
    pe_hi = pe.astype(BF16)
    pe_mid = (pe - pe_hi.astype(F32)).astype(BF16)
    w_hi = w1c.astype(BF16)
    w_mid = (w1c - w_hi.astype(F32)).astype(BF16)
    pb = _dot(pe_hi, w_hi) + _dot(pe_mid, w_hi) + _dot(pe_hi, w_mid)
    bc = pb[0:1, :] + b1_ref[c:c + 1, :]
    return jnp.concatenate([bc] * NSA_KV_HEADS, axis=1)


def _cmp_first_layer(x_ref, c, row0, n_seg, w1c_ref, pitch=CMP_STRIDE):
    acc = None
    for s in range(0, CMP_STRIDE, CMP_S_PER_DOT):
        xs = jnp.concatenate([x_ref[c, pl.ds(row0 + s + j, n_seg, stride=pitch), :].astype(BF16)
                              for j in range(CMP_S_PER_DOT)], axis=1)
        d = _dot(xs, w1c_ref[c, s * C_W:(s + CMP_S_PER_DOT) * C_W, :])
        acc = d if acc is None else acc + d
    return acc


def _cmp_second_layer_fm(p, bias, w2t):
    n = p.shape[0]
    h = p[:, :HC_W] + pltpu.roll(p[:, HC_W:], n - 1, 0) + bias
    return _dot_nt(w2t, jax.nn.gelu(h).astype(BF16))


def _cmp_prompt_fm_kernel(x_ref, w1c_ref, pe_ref, w1_ref, b1_ref, w2t_ref, o_ref):
    n_seg = o_ref.shape[1]
    for c in range(2):
        p = _cmp_first_layer(x_ref, c, 0, n_seg, w1c_ref)
        o_ref[c * C_W:(c + 1) * C_W, :] = _cmp_second_layer_fm(p, _cmp_bias_c(pe_ref, w1_ref, b1_ref, c), w2t_ref[c])


def _cmp_prompt_fm(x_tok, seq, w1c, pe_flat, cmp_w1, cmp_b1, w2c_t):
    bsz = x_tok.shape[1] // seq
    n_seg = seq // CMP_STRIDE
    const = lambda a: pl.BlockSpec(a.shape, lambda b: (0,) * a.ndim)
    return pl.pallas_call(
        _cmp_prompt_fm_kernel,
        grid=(bsz,),
        in_specs=[pl.BlockSpec((2, seq, C_W), lambda b: (0, b, 0)),
                  const(w1c), const(pe_flat), const(cmp_w1), const(cmp_b1), const(w2c_t)],
        out_specs=pl.BlockSpec((None, KV_W, n_seg), lambda b: (b, 0, 0)),
        out_shape=jax.ShapeDtypeStruct((bsz, KV_W, n_seg), F32),
        compiler_params=_cparams(("parallel",)),
        name="nsa_compress_prompt",
    )(x_tok, w1c, pe_flat, cmp_w1, cmp_b1, w2c_t)


def _scaled_group_queries(q, g):
    return (_group_queries(q, g).astype(F32) * QK_SCALE).astype(BF16)


def _cmp_branch_fm(qg, cmp_ref, g, valid):
    hd = NSA_HEAD_DIM
    kl, vl = g * hd, (NSA_KV_HEADS + g) * hd
    s_c = _dot(qg, cmp_ref[kl:kl + hd, :].astype(BF16))
    p_c = _masked_softmax(s_c, valid)
    return _dot_nt(p_c.astype(BF16), cmp_ref[vl:vl + hd, :].astype(BF16)), p_c


def _sum_heads(p, rows):
    out = p[0:rows]
    for r in range(1, NSA_GROUP):
        out = out + p[r * rows:(r + 1) * rows]
    return out


ATT_RB = 32


def _nsa_prompt_fm_kernel(q_ref, aux_ref, cmp_ref, kvs_ref, kvw_ref, o_ref,
                          s_scr, bias_scr, e_scr, m_scr, corr_scr, acc_scr, *, seq, nc):
    tq_n, tk_n, hd, grp = NSA_TQ, NSA_TK, NSA_HEAD_DIM, NSA_GROUP
    m_rows = grp * tq_n
    q0 = pl.program_id(1) * tq_n
    q_all = q_ref[...]
    gates = jax.nn.sigmoid(aux_ref[...])
    n_cmp = cmp_ref.shape[1]
    nb = seq // SLC_BLK

    tq_col = q0 + lax.broadcasted_iota(jnp.int32, (tq_n, 1), 0)
    tq_rows = jnp.concatenate([tq_col] * grp, axis=0)
    tq_lane = q0 + lax.broadcasted_iota(jnp.int32, (1, NSA_KV_HEADS * tq_n), 1) % tq_n
    t_end = lax.broadcasted_iota(jnp.int32, (1, n_cmp), 1) * CMP_STRIDE + (CMP_BLK - 1)
    ov_t = _overlap_t(nb, n_cmp, nc)
    n_chunks = (q0 + tq_n + tk_n - 1) // tk_n
    w_len = WINDOW + tq_n
    w0 = pl.multiple_of(jnp.maximum(q0 - WINDOW, 0), tq_n)

    qgs = [_scaled_group_queries(q_all, g) for g in range(NSA_KV_HEADS)]
    o_cs, imps = [], []
    for g in range(NSA_KV_HEADS):
        o_c, p_c = _cmp_branch_fm(qgs[g], cmp_ref, g, t_end <= tq_rows)
        o_cs.append(o_c)
        imps.append(_importance_t(_sum_heads(p_c, tq_n), ov_t))
    sel_all, _ = _topk_rows(_select_scores_t(jnp.concatenate(imps, axis=1), tq_lane), min(SLC_TOP_N, nb))

    tw = w0 + lax.broadcasted_iota(jnp.int32, (1, w_len), 1)
    bias_w = jnp.where((tw <= tq_col) & (tw > tq_col - WINDOW), 0.0, NEG)

    groups = range(NSA_KV_HEADS)

    def exp_rows(g, bias_g, width, online):
        per_head = tq_n // ATT_RB
        for i in range(m_rows // ATT_RB):
            rows = slice(i * ATT_RB, (i + 1) * ATT_RB)
            brows = slice((i % per_head) * ATT_RB, (i % per_head + 1) * ATT_RB)
            s = s_scr[g, rows, :width] + bias_scr[bias_g, brows, :width]
            m_new = jnp.max(s, axis=-1, keepdims=True)
            if online:
                m_old = m_scr[g, rows, :]
                m_new = jnp.maximum(m_old, m_new)
                corr_scr[g, rows, :] = jnp.exp2(m_old - m_new)
                m_scr[g, rows, :] = m_new
            e_scr[g, rows, :width] = jnp.exp2(s - m_new).astype(BF16)

    def keys_log2(ref, g, cols):
        return (ref[g * hd:(g + 1) * hd, cols] * LOG2E).astype(BF16)

    def gate_lane(g, r, branch):
        return hd + GATE_OFF + 3 * (grp * g + r) + branch

    def values_with_ones(ref, g, cols, branch):
        v = ref[(NSA_KV_HEADS + g) * hd:(NSA_KV_HEADS + g + 1) * hd, cols].astype(BF16)
        row = lax.broadcasted_iota(jnp.int32, v.shape, 0) + hd
        ones = functools.reduce(jnp.logical_or, [row == gate_lane(g, r, branch) for r in range(grp)])
        return jnp.concatenate([v, ones.astype(BF16)], axis=0)

    m_scr[...] = jnp.full(m_scr.shape, NEG, F32)
    acc_scr[...] = jnp.zeros(acc_scr.shape, F32)
    sel_ts = [sel_all[:, g * tq_n:(g + 1) * tq_n].astype(BF16) for g in groups]

    def slc_chunk(ci, carry):
        k0 = pl.multiple_of(ci * tk_n, tk_n)
        cols = pl.ds(k0, tk_n)
        kpos = k0 + lax.broadcasted_iota(jnp.int32, (1, tk_n), 1)
        expand = (lax.broadcasted_iota(jnp.int32, (nb, tk_n), 0) == kpos // SLC_BLK).astype(BF16)
        for g in groups:
            sel_k = _dot_tn(sel_ts[g], expand)
            bias_scr[g, :, :tk_n] = jnp.where((sel_k > 0.5) & (kpos <= tq_col), 0.0, NEG)
            s_scr[g, :, :tk_n] = _dot(qgs[g], keys_log2(kvs_ref, g, cols))
        for g in groups:
            exp_rows(g, g, tk_n, True)
        for g in groups:
            acc_scr[g] = corr_scr[g] * acc_scr[g] + _dot_nt(e_scr[g, :, :tk_n], values_with_ones(kvs_ref, g, cols, 1))
        return carry

    lax.fori_loop(0, n_chunks, slc_chunk, 0)
    wcols = pl.ds(w0, w_len)
    bias_scr[0, :, :w_len] = bias_w
    for g in groups:
        s_scr[g, :, :w_len] = _dot(qgs[g], keys_log2(kvw_ref, g, wcols))
    for g in groups:
        exp_rows(g, 0, w_len, False)
    acc_ws = [_dot_nt(e_scr[g, :, :w_len], values_with_ones(kvw_ref, g, wcols, 2)) for g in groups]
    gates_rot = pltpu.roll(gates, hd, 1)
    for g in groups:
        for r in range(grp):
            h = grp * g + r
            rows = slice(r * tq_n, (r + 1) * tq_n)
            acc_s, acc_w = acc_scr[g, rows, :], acc_ws[g][rows]
            f_s = gates_rot / jnp.maximum(acc_s, 1e-30)
            f_w = gates_rot / jnp.maximum(acc_w, 1e-30)
            ls, lw = gate_lane(g, r, 1), gate_lane(g, r, 2)
            gc = GATE_OFF + 3 * h
            o = (gates[:, gc:gc + 1] * o_cs[g][rows] + f_s[:, ls:ls + 1] * acc_s[:, :hd]
                 + f_w[:, lw:lw + 1] * acc_w[:, :hd])
            o_ref[:, h * hd:(h + 1) * hd] = o.astype(o_ref.dtype)


def _nsa_prompt_fm(q, aux, cmp_t, kvs_t, kvw_t):
    bsz, seq, _ = q.shape
    n_seg = cmp_t.shape[2]
    kern = functools.partial(_nsa_prompt_fm_kernel, seq=seq, nc=n_seg - CMP_R + 1)
    tile = lambda w: pl.BlockSpec((None, NSA_TQ, w), lambda b, t: (b, t, 0))
    whole = lambda n: pl.BlockSpec((None, KV_W, n), lambda b, t: (b, 0, 0))
    m_rows = NSA_GROUP * NSA_TQ
    width = max(NSA_TK, WINDOW + NSA_TQ)
    ng = NSA_KV_HEADS
    return pl.pallas_call(
        kern,
        grid=(bsz, seq // NSA_TQ),
        in_specs=[tile(NSA_WIDTH), tile(AUX_W), whole(n_seg), whole(seq), whole(seq)],
        out_specs=tile(NSA_WIDTH),
        out_shape=jax.ShapeDtypeStruct((bsz, seq, NSA_WIDTH), BF16),
        scratch_shapes=[pltpu.VMEM((ng, m_rows, width), F32),
                        pltpu.VMEM((ng, NSA_TQ, width), F32),
                        pltpu.VMEM((ng, m_rows, width), BF16),
                        pltpu.VMEM((ng, m_rows, 1), F32),
                        pltpu.VMEM((ng, m_rows, 1), F32),
                        pltpu.VMEM((ng, m_rows, 2 * NSA_HEAD_DIM), F32)],
        compiler_params=_cparams(("parallel", "arbitrary")),
        name="nsa_attn_prompt",
    )(q, aux, cmp_t, kvs_t, kvw_t)


def _cmp_sample_fm_kernel(pt_ref, cache_ref, w1f_ref, pe_ref, w1_ref, b1_ref, w2t_ref, o_ref,
                          xbuf, stage_a, stage_b, p_scr, sem, *, steps_per_batch):
    b = pl.program_id(0)
    h = pl.program_id(1)
    step = b * steps_per_batch + h
    n_steps = pl.num_programs(0) * steps_per_batch
    pps = CMP_PAGES_PER_STEP
    segs = pps * SEGS_PER_PAGE

    def page_copy(bb, hh, p, slot):
        return pltpu.make_async_copy(cache_ref.at[pt_ref[bb, hh * pps + p]], xbuf.at[slot, p], sem.at[slot])

    def start_fetch(bb, hh, slot):
        for p in range(pps):
            page_copy(bb, hh, p, slot).start(priority=p % 2)

    @pl.when(step == 0)
    def _():
        start_fetch(b, h, 0)

    @pl.when(step + 1 < n_steps)
    def _():
        wrap = h + 1 == steps_per_batch
        start_fetch(jnp.where(wrap, b + 1, b), jnp.where(wrap, 0, h + 1), (step + 1) % 2)

    slot = step % 2
    for p in range(pps):
        page_copy(b, h, p, slot).wait()

    pages_per_chunk = CMP_ROW_CHUNK // SEGS_PER_PAGE
    n_chunks = segs // CMP_ROW_CHUNK
    stages = (stage_a, stage_b)
    assert n_chunks == len(stages)

    def transpose_page(rc, lp):
        p = rc * pages_per_chunk + lp
        for c in range(2):
            tok = jnp.transpose(xbuf[slot, p, c * C_W:(c + 1) * C_W, :].astype(BF16)).astype(F32)
            for n in range(SEGS_PER_PAGE):
                r0 = (lp * SEGS_PER_PAGE + n) * STAGE_PITCH
                stages[rc][c, r0:r0 + CMP_STRIDE, :] = tok[n * CMP_STRIDE:(n + 1) * CMP_STRIDE]

    def first_layer(rc, between=()):
        between = list(between)
        dots = [(c, s) for c in range(2) for s in range(0, CMP_STRIDE, CMP_S_PER_DOT)]
        per_dot = -(-len(between) // len(dots))
        accs = [None, None]
        for c, s in dots:
            xs = jnp.concatenate([stages[rc][c, pl.ds(s + j, CMP_ROW_CHUNK, stride=STAGE_PITCH), :].astype(BF16)
                                  for j in range(CMP_S_PER_DOT)], axis=1)
            d = _dot(xs, w1f_ref[c, s * C_W:(s + CMP_S_PER_DOT) * C_W, :])
            accs[c] = d if accs[c] is None else accs[c] + d
            for thunk in between[:per_dot]:
                thunk()
            between = between[per_dot:]
        for c in range(2):
            r0 = pl.multiple_of(h * segs + rc * CMP_ROW_CHUNK, CMP_ROW_CHUNK)
            p_scr[c, pl.ds(r0, CMP_ROW_CHUNK), :] = accs[c]

    for lp in range(pages_per_chunk):
        transpose_page(0, lp)
    for rc in range(n_chunks):
        nxt = [functools.partial(transpose_page, rc + 1, lp) for lp in range(pages_per_chunk)] if rc + 1 < n_chunks else []
        first_layer(rc, nxt)

    @pl.when(h == steps_per_batch - 1)
    def _():
        for c in range(2):
            o_ref[c * C_W:(c + 1) * C_W, :] = _cmp_second_layer_fm(
                p_scr[c], _cmp_bias_c(pe_ref, w1_ref, b1_ref, c), w2t_ref[c])


def _cmp_sample_fm(page_table, cache_fm, w1_full, pe_flat, cmp_w1, cmp_b1, w2_bd_t):
    bsz, n_pages = page_table.shape
    steps = n_pages // CMP_PAGES_PER_STEP
    n_seg = n_pages * SEGS_PER_PAGE
    const = lambda a: pl.BlockSpec(a.shape, lambda b, h, pt: (0,) * a.ndim, pipeline_mode=pl.Buffered(1))
    grid_spec = pltpu.PrefetchScalarGridSpec(
        num_scalar_prefetch=1,
        grid=(bsz, steps),
        in_specs=[pl.BlockSpec(memory_space=pl.ANY), const(w1_full), const(pe_flat), const(cmp_w1),
                  const(cmp_b1), const(w2_bd_t)],
        out_specs=pl.BlockSpec((None, KV_W, n_seg), lambda b, h, pt: (b, 0, 0)),
        scratch_shapes=[pltpu.VMEM((2, CMP_PAGES_PER_STEP, KV_W, PAGE_SIZE), F32),
                        pltpu.VMEM((2, CMP_ROW_CHUNK * STAGE_PITCH, C_W), F32),
                        pltpu.VMEM((2, CMP_ROW_CHUNK * STAGE_PITCH, C_W), F32),
                        pltpu.VMEM((2, n_seg, PC_W), F32),
                        pltpu.SemaphoreType.DMA((2,))],
    )
    return pl.pallas_call(
        functools.partial(_cmp_sample_fm_kernel, steps_per_batch=steps),
        grid_spec=grid_spec,
        out_shape=jax.ShapeDtypeStruct((bsz, KV_W, n_seg), F32),
        compiler_params=_cparams(("arbitrary", "arbitrary")),
        name="nsa_compress_sample",
    )(page_table, cache_fm, w1_full, pe_flat, cmp_w1, cmp_b1, w2_bd_t)


def _nsa_sample_select_fm_kernel(q_ref, cmp_ref, oc_ref, idx_ref, *, past_len, nb, nb_pad, nc):
    hd, grp = NSA_HEAD_DIM, NSA_GROUP
    m_rows = grp * SQ_ROWS
    q_all = q_ref[...]
    n_cmp = cmp_ref.shape[1]
    tq_rows = past_len + lax.broadcasted_iota(jnp.int32, (m_rows, 1), 0) % SQ_ROWS
    tq_lane = past_len + lax.broadcasted_iota(jnp.int32, (1, LANES), 1) % SQ_ROWS
    t_end = lax.broadcasted_iota(jnp.int32, (1, n_cmp), 1) * CMP_STRIDE + (CMP_BLK - 1)
    ov_t = _overlap_t(nb_pad, n_cmp, nc)
    psums = []
    for g in range(NSA_KV_HEADS):
        o_c, p_c = _cmp_branch_fm(_scaled_group_queries(q_all, g), cmp_ref, g, t_end <= tq_rows)
        psums.append(_sum_heads(p_c, SQ_ROWS))
        for r in range(grp):
            h = grp * g + r
            oc_ref[:, h * hd:(h + 1) * hd] = o_c[r * SQ_ROWS:(r + 1) * SQ_ROWS]
    psum = jnp.concatenate(psums + [jnp.zeros((LANES - NSA_KV_HEADS * SQ_ROWS, n_cmp), F32)], axis=0)
    score_t = _select_scores_t(_importance_t(psum, ov_t), tq_lane)
    rowid = lax.broadcasted_iota(jnp.int32, score_t.shape, 0)
    _, picks = _topk_rows(jnp.where(rowid < nb, score_t, -jnp.inf), min(SLC_TOP_N, nb))
    idx_ref[...] = jnp.concatenate(picks, axis=0)


def _nsa_sample_select_fm(q_pad, cmp_t, past_len, seq_new):
    bsz = q_pad.shape[0]
    n_seg = cmp_t.shape[2]
    nb = -(-(past_len + seq_new) // SLC_BLK)
    nb_pad = -(-nb // SUBLANES) * SUBLANES
    n_pick = min(SLC_TOP_N, nb)
    kern = functools.partial(_nsa_sample_select_fm_kernel, past_len=past_len, nb=nb, nb_pad=nb_pad,
                             nc=n_seg - CMP_R + 1)
    return pl.pallas_call(
        kern,
        grid=(bsz,),
        in_specs=[pl.BlockSpec((None, SQ_ROWS, NSA_WIDTH), lambda b: (b, 0, 0)),
                  pl.BlockSpec((None, KV_W, n_seg), lambda b: (b, 0, 0))],
        out_specs=[pl.BlockSpec((None, SQ_ROWS, NSA_WIDTH), lambda b: (b, 0, 0)),
                   pl.BlockSpec((None, n_pick, LANES), lambda b: (b, 0, 0))],
        out_shape=[jax.ShapeDtypeStruct((bsz, SQ_ROWS, NSA_WIDTH), F32),
                   jax.ShapeDtypeStruct((bsz, n_pick, LANES), jnp.int32)],
        compiler_params=_cparams(("parallel",)),
        name="nsa_select_sample",
    )(q_pad, cmp_t)


def _nsa_sample_attend_fm_kernel(pt_ref, idx_ref, q_ref, gate_ref, oc_ref, cache_ref, tail_ref, winp_ref,
                                 winn_ref, o_ref, kvbuf, sem, *, past_len, seq_new, n_pick):
    b = pl.program_id(0)
    nbatch = pl.num_programs(0)
    hd, grp = NSA_HEAD_DIM, NSA_GROUP
    n_items = seq_new * NSA_KV_HEADS * n_pick
    n_past_blocks = past_len // SLC_BLK
    blocks_per_page = PAGE_SIZE // SLC_BLK

    def block_id(bb, item):
        return jnp.minimum(idx_ref[bb * n_items + item], n_past_blocks)

    def copy(bb, item, slot):
        g = (item // n_pick) % NSA_KV_HEADS
        lanes = slice((item % n_pick) * PAGE_SIZE, (item % n_pick + 1) * PAGE_SIZE)
        rows = slice(g * hd, (g + 1) * hd)
        past_blk = jnp.minimum(block_id(bb, item), n_past_blocks - 1)
        page = pt_ref[bb, lax.shift_right_logical(past_blk, blocks_per_page.bit_length() - 1)]
        return pltpu.make_async_copy(cache_ref.at[page, :, rows, :], kvbuf.at[slot, item // n_pick, :, :, lanes],
                                     sem.at[slot])

    def start_fetch(bb, slot):
        for item in range(n_items):
            copy(bb, item, slot).start(priority=item % 2)

    @pl.when(b == 0)
    def _():
        start_fetch(b, 0)

    @pl.when(b + 1 < nbatch)
    def _():
        start_fetch(b + 1, (b + 1) % 2)

    slot = b % 2

    for item in range(n_items):
        copy(b, item, slot).wait()

    head_row = lax.broadcasted_iota(jnp.int32, (NSA_HEADS, 1), 0)
    tok = lax.broadcasted_iota(jnp.int32, (1, PAGE_SIZE), 1)
    wb, wn = winp_ref.shape[1], winn_ref.shape[1]
    tw_p = past_len - wb + lax.broadcasted_iota(jnp.int32, (1, wb), 1)
    tw_n = past_len + lax.broadcasted_iota(jnp.int32, (1, wn), 1)

    def heads16(x):
        return jnp.concatenate([x.astype(BF16), jnp.zeros_like(x, dtype=BF16)], axis=0)

    def merge_groups(per_group):
        out = per_group[0]
        for g in range(1, NSA_KV_HEADS):
            out = jnp.where(head_row >= g * grp, per_group[g], out)
        return out

    for qi in range(seq_new):
        tq = past_len + qi
        q16 = heads16(q_ref[qi] * QK_SCALE)
        gates = jax.nn.sigmoid(gate_ref[qi])
        o_s_g, o_w_g = [], []
        for g in range(NSA_KV_HEADS):
            qg_i = qi * NSA_KV_HEADS + g
            kl = g * hd
            pieces, new_picked = [], jnp.zeros((1, PAGE_SIZE), jnp.int32)
            for kk in range(n_pick):
                blk_k = block_id(b, qg_i * n_pick + kk)
                pieces.append((blk_k < n_past_blocks) & (tok // SLC_BLK == blk_k % blocks_per_page)
                              & (blk_k * SLC_BLK + tok % SLC_BLK <= tq))
                new_picked = jnp.maximum(new_picked, (blk_k == n_past_blocks).astype(jnp.int32))
            pieces.append((new_picked > 0) & (tok < SLC_BLK) & (past_len + tok <= tq))
            mask = jnp.concatenate(pieces, axis=1)
            kcat = jnp.concatenate([kvbuf[slot, qg_i, 0].astype(BF16), tail_ref[0, kl:kl + hd, :].astype(BF16)], axis=1)
            vcat = jnp.concatenate([kvbuf[slot, qg_i, 1].astype(BF16), tail_ref[1, kl:kl + hd, :].astype(BF16)], axis=1)
            p = _masked_softmax(_dot(q16, kcat)[:NSA_HEADS], mask)
            o_s_g.append(_dot_nt(heads16(p), vcat)[:NSA_HEADS])
            vl = (NSA_KV_HEADS + g) * hd
            bias_p = jnp.where((tw_p <= tq) & (tw_p > tq - WINDOW) & (tw_p >= 0), 0.0, NEG)
            bias_n = jnp.where((tw_n <= tq) & (tw_n > tq - WINDOW), 0.0, NEG)
            s_p = _dot(q16, winp_ref[kl:kl + hd, :].astype(BF16))[:NSA_HEADS] + bias_p
            s_n = _dot(q16, winn_ref[kl:kl + hd, :].astype(BF16))[:NSA_HEADS] + bias_n
            m = jnp.maximum(jnp.max(s_p, axis=-1, keepdims=True), jnp.max(s_n, axis=-1, keepdims=True))
            e_p, e_n = jnp.exp(s_p - m), jnp.exp(s_n - m)
            den = jnp.maximum(jnp.sum(e_p, axis=-1, keepdims=True) + jnp.sum(e_n, axis=-1, keepdims=True), 1e-30)
            o_w_g.append((_dot_nt(heads16(e_p), winp_ref[vl:vl + hd, :].astype(BF16))
                          + _dot_nt(heads16(e_n), winn_ref[vl:vl + hd, :].astype(BF16)))[:NSA_HEADS] / den)
        o_ref[qi] = (gates[:, 0:1] * oc_ref[qi] + gates[:, 1:2] * merge_groups(o_s_g)
                     + gates[:, 2:3] * merge_groups(o_w_g))


def _nsa_sample_attend_fm(page_table, idx_flat, q_heads, gate_logits, o_c, cache_fm, tail_fm, win_past, win_new,
                          past_len, seq_new, n_pick):
    bsz = q_heads.shape[0]
    wb, wn = win_past.shape[2], win_new.shape[2]
    kern = functools.partial(_nsa_sample_attend_fm_kernel, past_len=past_len, seq_new=seq_new, n_pick=n_pick)
    per_b = lambda n, w: pl.BlockSpec((None, n, w), lambda b, pt, ix: (b, 0, 0))
    per_bq = lambda w: pl.BlockSpec((None, seq_new, NSA_HEADS, w), lambda b, pt, ix: (b, 0, 0, 0))
    n_qg = seq_new * NSA_KV_HEADS
    buf = pltpu.VMEM((2, n_qg, 2, NSA_HEAD_DIM, n_pick * PAGE_SIZE), F32)
    grid_spec = pltpu.PrefetchScalarGridSpec(
        num_scalar_prefetch=2,
        grid=(bsz,),
        in_specs=[per_bq(NSA_HEAD_DIM), per_bq(LANES), per_bq(NSA_HEAD_DIM),
                  pl.BlockSpec(memory_space=pl.ANY),
                  pl.BlockSpec((None, 2, C_W, LANES), lambda b, pt, ix: (b, 0, 0, 0)),
                  per_b(KV_W, wb), per_b(KV_W, wn)],
        out_specs=per_bq(NSA_HEAD_DIM),
        scratch_shapes=[buf, pltpu.SemaphoreType.DMA((2,))],
    )
    return pl.pallas_call(
        kern,
        grid_spec=grid_spec,
        out_shape=jax.ShapeDtypeStruct((bsz, seq_new, NSA_HEADS, NSA_HEAD_DIM), F32),
        compiler_params=_cparams(("arbitrary",)),
        name="nsa_attend_sample",
    )(page_table, idx_flat, q_heads, gate_logits, o_c, cache_fm, tail_fm, win_past, win_new)


PROMPT_TM = 512
FFN_TM = 512
GLA_TL = 512
SAMPLE_GLA_ROWS = 16
GLA_PROMPT_NBB = 4
GLA_SAMPLE_NBB = 8


def _pad_rows(x, n):
    return jnp.pad(x, ((0, 0), (0, n - x.shape[1]), (0, 0)))


def kernel(x_prompt, x_sample, state_gla, cache_cmp_kv, cache_slc_kv, cache_win_kv, page_table, c_prompt,
           c_sample, ln_in_g, ln_in_b, w_ada, b_ada, w_in, gla_w_a2, gla_b_a, gla_norm_g, cmp_pe, cmp_w1,
           cmp_b1, cmp_w2, w_o, ln1_g, ln1_b, w_ffn_in, w_ffn_out, ln2_g, ln2_b):
    assert w_in.shape[0] == DEPTH == 1
    l = 0
    bp, lp, d = x_prompt.shape
    bs, ls, _ = x_sample.shape
    n_pool = cache_cmp_kv.shape[1]
    n_pages = page_table.shape[1]
    past_len = n_pages * PAGE_SIZE
    wb = cache_win_kv.shape[2]
    assert ((past_len + ls) // CMP_STRIDE) * CMP_STRIDE <= past_len and past_len % SLC_BLK == 0
    assert ls <= SQ_ROWS and ls <= SLC_BLK and wb == WINDOW

    w_perm = _permute_w_in(w_in[l])
    w_o_b, w_fi_b, w_fo_b = w_o[l].astype(BF16), w_ffn_in[l].astype(BF16), w_ffn_out[l].astype(BF16)
    w_a2p = jnp.zeros((AUX_W, GLA_KW), F32).at[:GLA_RANK].set(gla_w_a2[l])
    b_a = gla_b_a[l].reshape(1, GLA_KW)
    norm_g = gla_norm_g[l].reshape(1, GLA_DV)
    w1c, w2c_t, pe_flat = _cmp_weights_fm(cmp_pe[l], cmp_w1[l], cmp_w2[l])
    kvt = (2, NSA_KV_HEADS, NSA_HEAD_DIM)

    def fm_view(a):
        return jnp.transpose(a, (0, 2, 3, 4, 1)).reshape(a.shape[0], KV_W, a.shape[1])

    def tok_view(a_fm):
        n, _, t = a_fm.shape
        return jnp.transpose(a_fm.reshape((n,) + kvt + (t,)), (0, 4, 1, 2, 3))[None]

    mod = _ada(jnp.concatenate([c_prompt, c_sample], axis=0), w_ada[l], b_ada[l])
    mods_p = [m[:bp, None, :] for m in jnp.split(mod, 6, axis=-1)]
    mods_s = [jnp.repeat(m[bp:], ls, axis=0)[None] for m in jnp.split(mod, 6, axis=-1)]

    def out_ffn(x2d, o_g, o_n, mods, tm, rpm):
        sh1, sc1, ga1, sh2, sc2, ga2 = mods
        return _out_ffn(x2d, o_g, o_n, ga1, sc2, sh2, ga2, ln_in_g, ln_in_b, w_o_b, ln1_g[l], ln1_b[l], w_fi_b,
                        w_fo_b, ln2_g[l], ln2_b[l], tm, rpm)

    xp2 = x_prompt.reshape(bp * lp, d)
    rpm = lp // PROMPT_TM
    qk, v, r, qn, kvc, aux, kvc_t, kvs_t, kvw_t = _inproj(
        xp2, mods_p[1], mods_p[0], ln_in_g, ln_in_b, w_perm, PROMPT_TM, rpm, seq_per_batch=lp)
    b3 = lambda a: a.reshape(bp, lp, a.shape[-1])
    o_g, s_p = _gla(b3(qk), b3(v), b3(r), b3(aux), w_a2p, b_a, norm_g, jnp.zeros((bp, GLA_KW, GLA_DV), F32),
                    chunk=GLA_CHUNK, sub=GLA_SUB, tl=GLA_TL, l_valid=lp, nbb=GLA_PROMPT_NBB)
    cmp_t_p = _cmp_prompt_fm(kvc, lp, w1c, pe_flat, cmp_w1[l], cmp_b1[l], w2c_t)
    o_n = _nsa_prompt_fm(b3(qn), b3(aux), cmp_t_p, kvs_t, kvw_t)
    y_p = out_ffn(xp2, o_g.reshape(bp * lp, GLA_WIDTH), o_n.reshape(bp * lp, NSA_WIDTH), mods_p, FFN_TM,
                  lp // FFN_TM)
    w_keep = min(WINDOW, lp)
    outs_p = (y_p.reshape(bp, lp, d), s_p.reshape(1, bp, GLA_HEADS, GLA_DK, GLA_DV),
              tok_view(kvc_t), tok_view(kvs_t), tok_view(kvw_t[:, :, lp - w_keep:]))

    ts = bs * ls
    xs2 = x_sample.reshape(ts, d)
    qk, v, r, qn, kvc, aux, kvs, kvw = _inproj(xs2, mods_s[1], mods_s[0], ln_in_g, ln_in_b, w_perm, ts, 1)
    s3 = lambda a: a.reshape(bs, ls, a.shape[-1])
    g16 = lambda a: _pad_rows(s3(a), SAMPLE_GLA_ROWS)
    o_g, s_s = _gla(g16(qk), g16(v), g16(r), g16(aux), w_a2p, b_a, norm_g, state_gla[l].reshape(bs, GLA_KW, GLA_DV),
                    chunk=SAMPLE_GLA_ROWS, sub=SAMPLE_GLA_ROWS, tl=SAMPLE_GLA_ROWS, l_valid=ls,
                    nbb=GLA_SAMPLE_NBB)
    o_g = o_g[:, :ls].reshape(ts, GLA_WIDTH)
    kvc = jnp.concatenate([kvc[0], kvc[1]], axis=1)
    cmp_t_s = _cmp_sample_fm(page_table, fm_view(cache_cmp_kv[l]), w1c, pe_flat, cmp_w1[l], cmp_b1[l], w2c_t)
    q_pad = _pad_rows(s3(qn), SQ_ROWS)
    o_c, idx = _nsa_sample_select_fm(q_pad, cmp_t_s, past_len, ls)
    n_pick = idx.shape[1]
    idx = idx[:, :, :NSA_KV_HEADS * SQ_ROWS].reshape(bs, n_pick, NSA_KV_HEADS, SQ_ROWS)[..., :ls]
    idx_flat = jnp.transpose(idx, (0, 3, 2, 1)).reshape(-1)
    new_fm =lambda a: jnp.pad(jnp.transpose(s3(a), (0, 2, 1)), ((0, 0), (0, 0), (0, LANES - ls)))
    win_past = fm_view(cache_win_kv[l])
    per_head = lambda a, w: a.reshape(bs, -1, NSA_HEADS, w)[:, :ls]
    gate_logits = jnp.pad(per_head(aux[:, GATE_OFF:GATE_OFF + N_GATES], 3), ((0, 0),) * 3 + ((0, LANES - 3),))
    o_n = _nsa_sample_attend_fm(page_table, idx_flat, per_head(qn.astype(F32), NSA_HEAD_DIM), gate_logits,
                                per_head(o_c, NSA_HEAD_DIM),
                                fm_view(cache_slc_kv[l]).reshape(n_pool, 2, C_W, PAGE_SIZE),
                                new_fm(kvs).reshape(bs, 2, C_W, LANES), win_past, new_fm(kvw),
                                past_len, ls, n_pick)
    o_n = o_n.reshape(ts, NSA_WIDTH).astype(BF16)
    y_s = out_ffn(xs2, o_g, o_n, mods_s, ts, 1)
    win_s = jnp.concatenate([win_past[:, :, ls:], jnp.transpose(s3(kvw), (0, 2, 1))], axis=2)
    outs_s = (y_s.reshape(bs, ls, d), s_s.reshape(1, bs, GLA_HEADS, GLA_DK, GLA_DV),
              kvc.reshape((1, bs, ls) + kvt), kvs.reshape((1, bs, ls) + kvt), tok_view(win_s))

    return (outs_p[0], outs_s[0], outs_p[1], outs_s[1], outs_p[2], outs_s[2], outs_p[3], outs_s[3],
            outs_p[4], outs_s[4])
```

```python
import functools
import math

import numpy as np
import jax
import jax.numpy as jnp
from jax import lax
from jax.experimental import pallas as pl
from jax.experimental.pallas import tpu as pltpu

F32 = jnp.float32
BF16 = jnp.bfloat16

D_MODEL = 1024
DEPTH = 1
PAGE_SIZE = 128
GLA_HEADS = 4
GLA_DV = D_MODEL // (2 * GLA_HEADS)
GLA_DK = GLA_DV // 2
GLA_RANK = 16
GLA_TAU = 16.0
GLA_CHUNK = 64
GLA_SUB = 32
GLA_WIDTH = GLA_HEADS * GLA_DV
GLA_KW = GLA_HEADS * GLA_DK
NSA_HEADS = 8
NSA_KV_HEADS = 2
NSA_GROUP = NSA_HEADS // NSA_KV_HEADS
NSA_HEAD_DIM = D_MODEL // (2 * NSA_HEADS)
NSA_WIDTH = NSA_HEADS * NSA_HEAD_DIM
CMP_BLK = 32
CMP_STRIDE = 16
CMP_HIDDEN = 2 * NSA_HEAD_DIM
SLC_BLK = 64
SLC_TOP_N = 16
WINDOW = 512
FORCE_BONUS = 1e4
NEG = -1e30
D_FF = -(-8 * D_MODEL // (3 * 256)) * 256
ALPHA = (2 * DEPTH) ** 0.25
KV_W = 2 * NSA_KV_HEADS * NSA_HEAD_DIM
N_GATES = 3 * NSA_HEADS
IN_SIZES = (GLA_KW, GLA_KW, GLA_WIDTH, GLA_RANK, GLA_WIDTH, NSA_WIDTH, KV_W, KV_W, KV_W, N_GATES)
IN_WIDTH = sum(IN_SIZES)
LN_EPS = 1e-5

LANES = 128
SUBLANES = 8
VMEM_LIMIT_BYTES = 56 * 1024 * 1024

AUX_W = LANES
IN_GROUPS = (2 * GLA_KW, GLA_WIDTH, GLA_WIDTH, NSA_WIDTH, KV_W, KV_W, KV_W, AUX_W)
IN_PERM_W = sum(IN_GROUPS)
GATE_OFF = GLA_RANK


def _cparams(sem):
    return pltpu.CompilerParams(dimension_semantics=sem, vmem_limit_bytes=VMEM_LIMIT_BYTES)


def _split3(a):
    hi = a.astype(BF16)
    r1 = a - hi.astype(F32)
    mid = r1.astype(BF16)
    lo = (r1 - mid.astype(F32)).astype(BF16)
    return hi, mid, lo


def _dot(a, b):
    return jnp.dot(a, b, preferred_element_type=F32)


def _dot_nt(a, b):
    return lax.dot_general(a, b, (((1,), (1,)), ((), ())), preferred_element_type=F32)


def _dot_tn(a, b):
    return lax.dot_general(a, b, (((0,), (0,)), ((), ())), preferred_element_type=F32)


def _layer_norm(x, g, b):
    mu = jnp.mean(x, axis=-1, keepdims=True)
    xc = x - mu
    var = jnp.mean(xc * xc, axis=-1, keepdims=True)
    return xc * lax.rsqrt(var + LN_EPS) * g + b


def _ada_kernel(c_ref, w_ref, b_ref, o_ref):
    c = c_ref[...]
    a = (c * jax.nn.sigmoid(c)).astype(BF16)
    o_ref[...] = _dot(a, w_ref[...].astype(BF16)) + b_ref[...]


def _ada(c, w_ada, b_ada):
    n, d = c.shape
    m = w_ada.shape[1]
    tn = D_MODEL
    return pl.pallas_call(
        _ada_kernel,
        grid=(m // tn,),
        in_specs=[pl.BlockSpec((n, d), lambda j: (0, 0)),
                  pl.BlockSpec((d, tn), lambda j: (0, j)),
                  pl.BlockSpec((1, tn), lambda j: (0, j))],
        out_specs=pl.BlockSpec((n, tn), lambda j: (0, j)),
        out_shape=jax.ShapeDtypeStruct((n, m), F32),
        compiler_params=_cparams(("parallel",)),
        name="ada_mod",
    )(c, w_ada, b_ada.reshape(1, m))


N_KV_GROUPS = 3


def _inproj_kernel(x_ref, sc_ref, sh_ref, g_ref, b_ref, w_ref, *o_refs, feature_major):
    xn = _layer_norm(x_ref[...], g_ref[...], b_ref[...])
    u = (xn * (1.0 + sc_ref[...]) + sh_ref[...]).astype(BF16)
    plain = dict(zip(("qk", "v", "r", "qn", "kvc", "aux"), o_refs[:6]))
    extra = dict(zip(("kvc", "kvs", "kvw") if feature_major else ("kvs", "kvw"), o_refs[6:]))
    lo = 0
    for name, wdt in zip(IN_GROUP_NAMES, IN_GROUPS):
        z = _dot(u, w_ref[:, lo:lo + wdt])
        lo += wdt
        if name == "kvc":
            plain[name][0] = z[:, :KV_W // 2]
            plain[name][1] = z[:, KV_W // 2:]
        elif name in plain:
            plain[name][...] = z.astype(plain[name].dtype)
        if name in extra:
            extra[name][...] = jnp.transpose(z) if feature_major else z


IN_GROUP_NAMES = ("qk", "v", "r", "qn", "kvc", "kvs", "kvw", "aux")


def _inproj(x, sc, sh, ln_g, ln_b, w_perm, tm, rows_per_mod, seq_per_batch=None):
    t, d = x.shape
    r = sc.shape[1]
    feature_major = seq_per_batch is not None
    mod_spec = pl.BlockSpec((None, r, d), lambda i: (i // rows_per_mod, 0, 0))
    tok = lambda w, dt: (pl.BlockSpec((tm, w), lambda i: (i, 0)), jax.ShapeDtypeStruct((t, w), dt))
    outs = [tok(2 * GLA_KW, BF16), tok(GLA_WIDTH, BF16), tok(GLA_WIDTH, BF16), tok(NSA_WIDTH, BF16),
            (pl.BlockSpec((2, tm, KV_W // 2), lambda i: (0, i, 0)), jax.ShapeDtypeStruct((2, t, KV_W // 2), F32)),
            tok(AUX_W, F32)]
    if feature_major:
        tpb = seq_per_batch // tm
        outs += [(pl.BlockSpec((None, KV_W, tm), lambda i: (i // tpb, 0, i % tpb)),
                  jax.ShapeDtypeStruct((t // seq_per_batch, KV_W, seq_per_batch), F32))] * N_KV_GROUPS
    else:
        outs += [tok(KV_W, F32)] * 2
    return pl.pallas_call(
        functools.partial(_inproj_kernel, feature_major=feature_major),
        grid=(t // tm,),
        in_specs=[pl.BlockSpec((tm, d), lambda i: (i, 0)), mod_spec, mod_spec,
                  pl.BlockSpec((1, d), lambda i: (0, 0)), pl.BlockSpec((1, d), lambda i: (0, 0)),
                  pl.BlockSpec((d, IN_PERM_W), lambda i: (0, 0))],
        out_specs=[o[0] for o in outs],
        out_shape=[o[1] for o in outs],
        compiler_params=_cparams(("parallel",)),
        name="ln_mod_inproj",
    )(x, sc, sh, ln_g.reshape(1, d), ln_b.reshape(1, d), w_perm)


def _permute_w_in(w_in):
    q_g, k_g, v_g, a_g, r_g, q_n, kv_c, kv_s, kv_w, g_n = jnp.split(w_in, np.cumsum(IN_SIZES)[:-1], axis=1)
    pad = jnp.zeros((w_in.shape[0], AUX_W - GLA_RANK - N_GATES), w_in.dtype)
    return jnp.concatenate([q_g, k_g, v_g, r_g, q_n, kv_c, kv_s, kv_w, a_g, g_n, pad], axis=1).astype(BF16)


FF_CHUNK = 256


def _out_ffn_kernel(x_ref, og_ref, on_ref, ga1_ref, sc2_ref, sh2_ref, ga2_ref, lng_ref, lnb_ref,
                    wo_ref, l1g_ref, l1b_ref, wfi_ref, wfo_ref, l2g_ref, l2b_ref, y_ref):
    x = _layer_norm(x_ref[...], lng_ref[...], lnb_ref[...])
    mix = _dot(og_ref[...], wo_ref[0:GLA_WIDTH, :]) + _dot(on_ref[...], wo_ref[GLA_WIDTH:, :])
    x1 = _layer_norm(ALPHA * x + ga1_ref[...] * mix, l1g_ref[...], l1b_ref[...])
    u2 = (x1 * (1.0 + sc2_ref[...]) + sh2_ref[...]).astype(BF16)
    ffn = jnp.zeros(x1.shape, F32)
    for c in range(D_FF // FF_CHUNK):
        lo = c * FF_CHUNK
        gate = _dot(u2, wfi_ref[:, lo:lo + FF_CHUNK])
        up = _dot(u2, wfi_ref[:, D_FF + lo:D_FF + lo + FF_CHUNK])
        f = (gate * jax.nn.sigmoid(gate) * up).astype(BF16)
        ffn = ffn + _dot(f, wfo_ref[lo:lo + FF_CHUNK, :])
    y_ref[...] = _layer_norm(ALPHA * x1 + ga2_ref[...] * ffn, l2g_ref[...], l2b_ref[...])


def _out_ffn(x, o_g, o_n, ga1, sc2, sh2, ga2, ln_in_g, ln_in_b, w_o, ln1_g, ln1_b, w_fi, w_fo, ln2_g, ln2_b,
             tm, rows_per_mod):
    t, d = x.shape
    r = ga1.shape[1]
    mod_spec = pl.BlockSpec((None, r, d), lambda i: (i // rows_per_mod, 0, 0))
    vec = lambda: pl.BlockSpec((1, d), lambda i: (0, 0))
    const = lambda shp: pl.BlockSpec(shp, lambda i: (0, 0), pipeline_mode=pl.Buffered(1))
    row = lambda a: a.reshape(1, d)
    return pl.pallas_call(
        _out_ffn_kernel,
        grid=(t // tm,),
        in_specs=[pl.BlockSpec((tm, d), lambda i: (i, 0)),
                  pl.BlockSpec((tm, GLA_WIDTH), lambda i: (i, 0)),
                  pl.BlockSpec((tm, NSA_WIDTH), lambda i: (i, 0)),
                  mod_spec, mod_spec, mod_spec, mod_spec, vec(), vec(),
                  const((d, d)), vec(), vec(), const((d, 2 * D_FF)), const((D_FF, d)), vec(), vec()],
        out_specs=pl.BlockSpec((tm, d), lambda i: (i, 0)),
        out_shape=jax.ShapeDtypeStruct((t, d), F32),
        compiler_params=_cparams(("parallel",)),
        name="outproj_ffn",
    )(x, o_g, o_n, ga1, sc2, sh2, ga2, row(ln_in_g), row(ln_in_b), w_o, row(ln1_g), row(ln1_b), w_fi, w_fo,
      row(ln2_g), row(ln2_b))


GLA_EXP_CLAMP = 80.0


def _gla_kernel(qk_ref, v_ref, r_ref, aux_ref, wa_ref, ba_ref, ng_ref, s0_ref, o_ref, sout_ref, s_scr,
                *, chunk, sub, tl, l_valid, l_pad, nbb):
    t = pl.program_id(1)
    c = chunk
    n_sub = c // sub
    hw = GLA_KW

    @pl.when(t == 0)
    def _():
        s_scr[...] = s0_ref[...]

    ri = lax.broadcasted_iota(jnp.int32, (c, c), 0)
    ci = lax.broadcasted_iota(jnp.int32, (c, c), 1)
    causal = ci <= ri
    tril = causal.astype(BF16)
    rowid = lax.broadcasted_iota(jnp.int32, (c, hw), 0)
    head_of_lane = lax.broadcasted_iota(jnp.int32, (c, hw), 1) // GLA_DK
    wa = wa_ref[...]
    wa_hi = wa.astype(BF16)
    wa_mid = (wa - wa_hi.astype(F32)).astype(BF16)
    ba = ba_ref[...]
    ng = ng_ref[...]

    def body(i, carry):
        for e in range(nbb):
            s_scr[e] = one_chunk(i, e, s_scr[e])
        return carry

    def one_chunk(i, e, s_old):
        r0 = pl.multiple_of(i * c, c)
        aux = aux_ref[e, pl.ds(r0, c), :]
        a_hi = aux.astype(BF16)
        a_mid = (aux - a_hi.astype(F32)).astype(BF16)
        z = _dot(a_hi, wa_hi) + _dot(a_mid, wa_hi) + _dot(a_hi, wa_mid) + ba
        g = jax.nn.log_sigmoid(z) / GLA_TAU
        if l_pad != l_valid:
            g = jnp.where(t * tl + r0 + rowid < l_valid, g, 0.0)
        g_hi, g_mid, g_lo = _split3(g)
        b = _dot(tril, g_hi) + _dot(tril, g_mid) + _dot(tril, g_lo)
        qk = qk_ref[e, pl.ds(r0, c), :]
        q = qk[:, :hw].astype(F32) * (GLA_DK ** -0.5)
        k = qk[:, hw:].astype(F32)
        v = v_ref[e, pl.ds(r0, c), :]
        b_last = b[c - 1:c, :]

        def heads_on_rows(x):
            return jnp.concatenate([jnp.where(head_of_lane == h, x, 0.0) for h in range(GLA_HEADS)], axis=0)

        o_inter =_dot(heads_on_rows(q * jnp.exp(b)).astype(BF16), s_old.astype(BF16))

        q_parts, k_parts = [], []
        for s_i in range(n_sub):
            b_ref = b[s_i * sub - 1:s_i * sub, :] if s_i > 0 else jnp.zeros((1, hw), F32)
            in_rows = (rowid >= s_i * sub) & (rowid < (s_i + 1) * sub)
            qt = jnp.where(in_rows, q * jnp.exp(jnp.minimum(b - b_ref, 0.0)), 0.0)
            kt = jnp.where(rowid < (s_i + 1) * sub, k * jnp.exp(jnp.minimum(b_ref - b, GLA_EXP_CLAMP)), 0.0)
            q_parts.append(heads_on_rows(qt).astype(BF16))
            k_parts.append(kt.astype(BF16))
        q_cat = jnp.concatenate(q_parts, axis=1) if n_sub > 1 else q_parts[0]
        k_cat = jnp.concatenate(k_parts, axis=1) if n_sub > 1 else k_parts[0]
        att = _dot_nt(q_cat, k_cat)

        r_t = r_ref[e, pl.ds(r0, c), :].astype(F32)
        for h in range(GLA_HEADS):
            att_h = jnp.where(causal, att[h * c:(h + 1) * c, :], 0.0).astype(BF16)
            o_h = o_inter[h * c:(h + 1) * c, :] + _dot(att_h, v[:, h * GLA_DV:(h + 1) * GLA_DV])
            o_h = o_h * lax.rsqrt(jnp.mean(o_h * o_h, axis=-1, keepdims=True) + 1e-6) * ng
            r_h = r_t[:, h * GLA_DV:(h + 1) * GLA_DV]
            o_ref[e, pl.ds(r0, c), h * GLA_DV:(h + 1) * GLA_DV] = (
                o_h * (r_h * jax.nn.sigmoid(r_h))).astype(o_ref.dtype)

        kd = jnp.concatenate([k * jnp.exp(b_last - b), jnp.zeros((LANES - c, hw), F32)], axis=0)
        kd_t = jnp.transpose(kd).astype(BF16)
        v_pad = jnp.concatenate([v, jnp.zeros((LANES - c, GLA_WIDTH), v.dtype)], axis=0)
        upd = jnp.concatenate([_dot(kd_t[h * GLA_DK:(h + 1) * GLA_DK, :], v_pad[:, h * GLA_DV:(h + 1) * GLA_DV])
                               for h in range(GLA_HEADS)], axis=0)
        decay = jnp.transpose(jnp.broadcast_to(jnp.exp(b_last), (LANES, hw)))
        return decay * s_old + upd

    lax.fori_loop(0, tl // c, body, 0)

    @pl.when(t == pl.num_programs(1) - 1)
    def _():
        sout_ref[...] = s_scr[...]


def _gla(qk, v, r, aux, w_a2p, b_a, norm_g, s0, *, chunk, sub, tl, l_valid, nbb):
    bsz, l_pad, _ = qk.shape
    nt = l_pad // tl
    kern = functools.partial(_gla_kernel, chunk=chunk, sub=sub, tl=tl, l_valid=l_valid, l_pad=l_pad, nbb=nbb)
    tile = lambda w: pl.BlockSpec((nbb, tl, w), lambda b, t: (b, t, 0))
    full = lambda shp: pl.BlockSpec(shp, lambda b, t: (0, 0))
    st = pl.BlockSpec((nbb, GLA_KW, GLA_DV), lambda b, t: (b, 0, 0))
    return pl.pallas_call(
        kern,
        grid=(bsz // nbb, nt),
        in_specs=[tile(2 * GLA_KW), tile(GLA_WIDTH), tile(GLA_WIDTH), tile(AUX_W),
                  full((AUX_W, GLA_KW)), full((1, GLA_KW)), full((1, GLA_DV)), st],
        out_specs=[tile(GLA_WIDTH), st],
        out_shape=[jax.ShapeDtypeStruct((bsz, l_pad, GLA_WIDTH), BF16),
                   jax.ShapeDtypeStruct((bsz, GLA_KW, GLA_DV), F32)],
        scratch_shapes=[pltpu.VMEM((nbb, GLA_KW, GLA_DV), F32)],
        compiler_params=_cparams(("parallel", "arbitrary")),
        name="gla_scan",
    )(qk, v, r, aux, w_a2p, b_a, norm_g, s0)


CMP_R = CMP_BLK // CMP_STRIDE


QK_SCALE = NSA_HEAD_DIM ** -0.5
LOG2E = math.log2(math.e)


def _masked_softmax(s, mask):
    s = jnp.where(mask, s, NEG)
    e = jnp.where(mask, jnp.exp(s - jnp.max(s, axis=-1, keepdims=True)), 0.0)
    return e / jnp.maximum(jnp.sum(e, axis=-1, keepdims=True), 1e-30)


def _group_queries(q, g):
    hd = NSA_HEAD_DIM
    return jnp.concatenate([q[:, (NSA_GROUP * g + r) * hd:(NSA_GROUP * g + r + 1) * hd]
                            for r in range(NSA_GROUP)], axis=0)


def _topk_rows(score_t, n_pick):
    nb, nq = score_t.shape
    rowid = lax.broadcasted_iota(jnp.int32, (nb, nq), 0)
    taken = jnp.zeros((nb, nq), jnp.int32)
    picks = []
    for _ in range(n_pick):
        free = taken == 0
        cand = jnp.where(free, score_t, -jnp.inf)
        m = jnp.max(cand, axis=0, keepdims=True)
        hit = free & (cand == m)
        idx = jnp.min(jnp.where(hit, rowid, nb), axis=0, keepdims=True)
        taken = jnp.where(rowid == idx, 1, taken)
        picks.append(idx)
    return taken.astype(F32), picks


def _importance_t(psum, ov_t):
    hi, mid, _ = _split3(psum)
    return _dot_nt(ov_t, hi) + _dot_nt(ov_t, mid)


def _overlap_t(nb, nc_pad, nc):
    j = lax.broadcasted_iota(jnp.int32, (nb, nc_pad), 0) * SLC_BLK
    i = lax.broadcasted_iota(jnp.int32, (nb, nc_pad), 1) * CMP_STRIDE
    return ((i < j + SLC_BLK) & (i + CMP_BLK > j) & (i < nc * CMP_STRIDE)).astype(BF16)


def _select_scores_t(imp_t, tq_row):
    nb, nq = imp_t.shape
    j = lax.broadcasted_iota(jnp.int32, (nb, nq), 0)
    cur = tq_row // SLC_BLK
    forced = (j == 0) | (j == cur) | (j == cur - 1)
    return jnp.where(j * SLC_BLK <= tq_row, imp_t + FORCE_BONUS * forced.astype(F32), -jnp.inf)


NSA_TQ = 256
NSA_TK = 512


SEGS_PER_PAGE = PAGE_SIZE // CMP_STRIDE
CMP_PAGES_PER_STEP = 64
CMP_ROW_CHUNK = 256
STAGE_PITCH = 24


SQ_ROWS = SUBLANES


C_W = NSA_KV_HEADS * NSA_HEAD_DIM
HC_W = NSA_KV_HEADS * CMP_HIDDEN
PC_W = CMP_R * HC_W
MXU_DEPTH = 256
CMP_S_PER_DOT = MXU_DEPTH // C_W


def _cmp_weights_fm(cmp_pe, cmp_w1, cmp_w2):
    eye = jnp.eye(NSA_KV_HEADS, dtype=F32)
    w1r = cmp_w1.reshape(2, CMP_R, CMP_STRIDE, NSA_HEAD_DIM, CMP_HIDDEN)
    w1c = jnp.einsum('cmsdh,gG->csgdmGh', w1r, eye).reshape(2, CMP_STRIDE * C_W, PC_W).astype(BF16)
    w2c_t = jnp.einsum('chd,gG->cGdgh', cmp_w2, eye).reshape(2, C_W, HC_W).astype(BF16)
    pe_flat = jnp.transpose(cmp_pe, (1, 0, 2)).reshape(2, CMP_BLK * NSA_HEAD_DIM)
    return w1c, w2c_t, pe_flat


def _cmp_bias_c(pe_ref, w1_ref, b1_ref, c):
    pe = jnp.broadcast_to(pe_ref[c:c + 1, :], (SUBLANES, pe_ref.shape[1]))
    w1c = w1_ref[c]
    pe_hi = pe.astype(BF16)
    pe_mid = (pe - pe_hi.astype(F32)).astype(BF16)
    w_hi = w1c.astype(BF16)
    w_mid = (w1c - w_hi.astype(F32)).astype(BF16)
    pb = _dot(pe_hi, w_hi) + _dot(pe_mid, w_hi) + _dot(pe_hi, w_mid)
    bc = pb[0:1, :] + b1_ref[c:c + 1, :]
    return jnp.concatenate([bc] * NSA_KV_HEADS, axis=1)


def _cmp_first_layer(x_ref, c, row0, n_seg, w1c_ref, pitch=CMP_STRIDE):
    acc = None
    for s in range(0, CMP_STRIDE, CMP_S_PER_DOT):
        xs = jnp.concatenate([x_ref[c, pl.ds(row0 + s + j, n_seg, stride=pitch), :].astype(BF16)
                              for j in range(CMP_S_PER_DOT)], axis=1)
        d = _dot(xs, w1c_ref[c, s * C_W:(s + CMP_S_PER_DOT) * C_W, :])
        acc = d if acc is None else acc + d
    return acc


def _cmp_second_layer_fm(p, bias, w2t):
    n = p.shape[0]
    h = p[:, :HC_W] + pltpu.roll(p[:, HC_W:], n - 1, 0) + bias
    return _dot_nt(w2t, jax.nn.gelu(h).astype(BF16))


def _cmp_prompt_fm_kernel(x_ref, w1c_ref, pe_ref, w1_ref, b1_ref, w2t_ref, o_ref):
    n_seg = o_ref.shape[1]
    for c in range(2):
        p = _cmp_first_layer(x_ref, c, 0, n_seg, w1c_ref)
        o_ref[c * C_W:(c + 1) * C_W, :] = _cmp_second_layer_fm(p, _cmp_bias_c(pe_ref, w1_ref, b1_ref, c), w2t_ref[c])


def _cmp_prompt_fm(x_tok, seq, w1c, pe_flat, cmp_w1, cmp_b1, w2c_t):
    bsz = x_tok.shape[1] // seq
    n_seg = seq // CMP_STRIDE
    const = lambda a: pl.BlockSpec(a.shape, lambda b: (0,) * a.ndim)
    return pl.pallas_call(
        _cmp_prompt_fm_kernel,
        grid=(bsz,),
        in_specs=[pl.BlockSpec((2, seq, C_W), lambda b: (0, b, 0)),
                  const(w1c), const(pe_flat), const(cmp_w1), const(cmp_b1), const(w2c_t)],
        out_specs=pl.BlockSpec((None, KV_W, n_seg), lambda b: (b, 0, 0)),
        out_shape=jax.ShapeDtypeStruct((bsz, KV_W, n_seg), F32),
        compiler_params=_cparams(("parallel",)),
        name="nsa_compress_prompt",
    )(x_tok, w1c, pe_flat, cmp_w1, cmp_b1, w2c_t)


def _scaled_group_queries(q, g):
    return (_group_queries(q, g).astype(F32) * QK_SCALE).astype(BF16)


def _cmp_branch_fm(qg, cmp_ref, g, valid):
    hd = NSA_HEAD_DIM
    kl, vl = g * hd, (NSA_KV_HEADS + g) * hd
    s_c = _dot(qg, cmp_ref[kl:kl + hd, :].astype(BF16))
    p_c = _masked_softmax(s_c, valid)
    return _dot_nt(p_c.astype(BF16), cmp_ref[vl:vl + hd, :].astype(BF16)), p_c


def _sum_heads(p, rows):
    out = p[0:rows]
    for r in range(1, NSA_GROUP):
        out = out + p[r * rows:(r + 1) * rows]
    return out


ATT_RB = 32


def _nsa_prompt_fm_kernel(q_ref, aux_ref, cmp_ref, kvs_ref, kvw_ref, o_ref,
                          s_scr, bias_scr, e_scr, m_scr, corr_scr, acc_scr, *, seq, nc):
    tq_n, tk_n, hd, grp = NSA_TQ, NSA_TK, NSA_HEAD_DIM, NSA_GROUP
    m_rows = grp * tq_n
    q0 = pl.program_id(1) * tq_n
    q_all = q_ref[...]
    gates = jax.nn.sigmoid(aux_ref[...])
    n_cmp = cmp_ref.shape[1]
    nb = seq // SLC_BLK

    tq_col = q0 + lax.broadcasted_iota(jnp.int32, (tq_n, 1), 0)
    tq_rows = jnp.concatenate([tq_col] * grp, axis=0)
    tq_lane = q0 + lax.broadcasted_iota(jnp.int32, (1, NSA_KV_HEADS * tq_n), 1) % tq_n
    t_end = lax.broadcasted_iota(jnp.int32, (1, n_cmp), 1) * CMP_STRIDE + (CMP_BLK - 1)
    ov_t = _overlap_t(nb, n_cmp, nc)
    n_chunks = (q0 + tq_n + tk_n - 1) // tk_n
    w_len = WINDOW + tq_n
    w0 = pl.multiple_of(jnp.maximum(q0 - WINDOW, 0), tq_n)

    qgs = [_scaled_group_queries(q_all, g) for g in range(NSA_KV_HEADS)]
    o_cs, imps = [], []
    for g in range(NSA_KV_HEADS):
        o_c, p_c = _cmp_branch_fm(qgs[g], cmp_ref, g, t_end <= tq_rows)
        o_cs.append(o_c)
        imps.append(_importance_t(_sum_heads(p_c, tq_n), ov_t))
    sel_all, _ = _topk_rows(_select_scores_t(jnp.concatenate(imps, axis=1), tq_lane), min(SLC_TOP_N, nb))

    tw = w0 + lax.broadcasted_iota(jnp.int32, (1, w_len), 1)
    bias_w = jnp.where((tw <= tq_col) & (tw > tq_col - WINDOW), 0.0, NEG)

    groups = range(NSA_KV_HEADS)

    def exp_rows(g, bias_g, width, online):
        per_head = tq_n // ATT_RB
        for i in range(m_rows // ATT_RB):
            rows = slice(i * ATT_RB, (i + 1) * ATT_RB)
            brows = slice((i % per_head) * ATT_RB, (i % per_head + 1) * ATT_RB)
            s = s_scr[g, rows, :width] + bias_scr[bias_g, brows, :width]
            m_new = jnp.max(s, axis=-1, keepdims=True)
            if online:
                m_old = m_scr[g, rows, :]
                m_new = jnp.maximum(m_old, m_new)
                corr_scr[g, rows, :] = jnp.exp2(m_old - m_new)
                m_scr[g, rows, :] = m_new
            e_scr[g, rows, :width] = jnp.exp2(s - m_new).astype(BF16)

    def keys_log2(ref, g, cols):
        return (ref[g * hd:(g + 1) * hd, cols] * LOG2E).astype(BF16)

    def gate_lane(g, r, branch):
        return hd + GATE_OFF + 3 * (grp * g + r) + branch

    def values_with_ones(ref, g, cols, branch):
        v = ref[(NSA_KV_HEADS + g) * hd:(NSA_KV_HEADS + g + 1) * hd, cols].astype(BF16)
        row = lax.broadcasted_iota(jnp.int32, v.shape, 0) + hd
        ones = functools.reduce(jnp.logical_or, [row == gate_lane(g, r, branch) for r in range(grp)])
        return jnp.concatenate([v, ones.astype(BF16)], axis=0)

    m_scr[...] = jnp.full(m_scr.shape, NEG, F32)
    acc_scr[...] = jnp.zeros(acc_scr.shape, F32)
    sel_ts = [sel_all[:, g * tq_n:(g + 1) * tq_n].astype(BF16) for g in groups]

    def slc_chunk(ci, carry):
        k0 = pl.multiple_of(ci * tk_n, tk_n)
        cols = pl.ds(k0, tk_n)
        kpos = k0 + lax.broadcasted_iota(jnp.int32, (1, tk_n), 1)
        expand = (lax.broadcasted_iota(jnp.int32, (nb, tk_n), 0) == kpos // SLC_BLK).astype(BF16)
        for g in groups:
            sel_k = _dot_tn(sel_ts[g], expand)
            bias_scr[g, :, :tk_n] = jnp.where((sel_k > 0.5) & (kpos <= tq_col), 0.0, NEG)
            s_scr[g, :, :tk_n] = _dot(qgs[g], keys_log2(kvs_ref, g, cols))
        for g in groups:
            exp_rows(g, g, tk_n, True)
        for g in groups:
            acc_scr[g] = corr_scr[g] * acc_scr[g] + _dot_nt(e_scr[g, :, :tk_n], values_with_ones(kvs_ref, g, cols, 1))
        return carry

    lax.fori_loop(0, n_chunks, slc_chunk, 0)
    wcols = pl.ds(w0, w_len)
    bias_scr[0, :, :w_len] = bias_w
    for g in groups:
        s_scr[g, :, :w_len] = _dot(qgs[g], keys_log2(kvw_ref, g, wcols))
    for g in groups:
        exp_rows(g, 0, w_len, False)
    acc_ws = [_dot_nt(e_scr[g, :, :w_len], values_with_ones(kvw_ref, g, wcols, 2)) for g in groups]
    gates_rot = pltpu.roll(gates, hd, 1)
    for g in groups:
        for r in range(grp):
            h = grp * g + r
            rows = slice(r * tq_n, (r + 1) * tq_n)
            acc_s, acc_w = acc_scr[g, rows, :], acc_ws[g][rows]
            f_s = gates_rot / jnp.maximum(acc_s, 1e-30)
            f_w = gates_rot / jnp.maximum(acc_w, 1e-30)
            ls, lw = gate_lane(g, r, 1), gate_lane(g, r, 2)
            gc = GATE_OFF + 3 * h
            o = (gates[:, gc:gc + 1] * o_cs[g][rows] + f_s[:, ls:ls + 1] * acc_s[:, :hd]
                 + f_w[:, lw:lw + 1] * acc_w[:, :hd])
            o_ref[:, h * hd:(h + 1) * hd] = o.astype(o_ref.dtype)


def _nsa_prompt_fm(q, aux, cmp_t, kvs_t, kvw_t):
    bsz, seq, _ = q.shape
    n_seg = cmp_t.shape[2]
    kern = functools.partial(_nsa_prompt_fm_kernel, seq=seq, nc=n_seg - CMP_R + 1)
    tile = lambda w: pl.BlockSpec((None, NSA_TQ, w), lambda b, t: (b, t, 0))
    whole = lambda n: pl.BlockSpec((None, KV_W, n), lambda b, t: (b, 0, 0))
    m_rows = NSA_GROUP * NSA_TQ
    width = max(NSA_TK, WINDOW + NSA_TQ)
    ng = NSA_KV_HEADS
    return pl.pallas_call(
        kern,
        grid=(bsz, seq // NSA_TQ),
        in_specs=[tile(NSA_WIDTH), tile(AUX_W), whole(n_seg), whole(seq), whole(seq)],
        out_specs=tile(NSA_WIDTH),
        out_shape=jax.ShapeDtypeStruct((bsz, seq, NSA_WIDTH), BF16),
        scratch_shapes=[pltpu.VMEM((ng, m_rows, width), F32),
                        pltpu.VMEM((ng, NSA_TQ, width), F32),
                        pltpu.VMEM((ng, m_rows, width), BF16),
                        pltpu.VMEM((ng, m_rows, 1), F32),
                        pltpu.VMEM((ng, m_rows, 1), F32),
                        pltpu.VMEM((ng, m_rows, 2 * NSA_HEAD_DIM), F32)],
        compiler_params=_cparams(("parallel", "arbitrary")),
        name="nsa_attn_prompt",
    )(q, aux, cmp_t, kvs_t, kvw_t)


def _cmp_sample_fm_kernel(pt_ref, cache_ref, w1f_ref, pe_ref, w1_ref, b1_ref, w2t_ref, o_ref,
                          xbuf, stage_a, stage_b, p_scr, sem, *, steps_per_batch):
    b = pl.program_id(0)
    h = pl.program_id(1)
    step = b * steps_per_batch + h
    n_steps = pl.num_programs(0) * steps_per_batch
    pps = CMP_PAGES_PER_STEP
    segs = pps * SEGS_PER_PAGE

    def page_copy(bb, hh, p, slot):
        return pltpu.make_async_copy(cache_ref.at[pt_ref[bb, hh * pps + p]], xbuf.at[slot, p], sem.at[slot])

    def start_fetch(bb, hh, slot):
        for p in range(pps):
            page_copy(bb, hh, p, slot).start(priority=p % 2)

    @pl.when(step == 0)
    def _():
        start_fetch(b, h, 0)

    @pl.when(step + 1 < n_steps)
    def _():
        wrap = h + 1 == steps_per_batch
        start_fetch(jnp.where(wrap, b + 1, b), jnp.where(wrap, 0, h + 1), (step + 1) % 2)

    slot = step % 2
    for p in range(pps):
        page_copy(b, h, p, slot).wait()

    pages_per_chunk = CMP_ROW_CHUNK // SEGS_PER_PAGE
    n_chunks = segs // CMP_ROW_CHUNK
    stages = (stage_a, stage_b)
    assert n_chunks == len(stages)

    def transpose_page(rc, lp):
        p = rc * pages_per_chunk + lp
        for c in range(2):
            tok = jnp.transpose(xbuf[slot, p, c * C_W:(c + 1) * C_W, :].astype(BF16)).astype(F32)
            for n in range(SEGS_PER_PAGE):
                r0 = (lp * SEGS_PER_PAGE + n) * STAGE_PITCH
                stages[rc][c, r0:r0 + CMP_STRIDE, :] = tok[n * CMP_STRIDE:(n + 1) * CMP_STRIDE]

    def first_layer(rc, between=()):
        between = list(between)
        dots = [(c, s) for c in range(2) for s in range(0, CMP_STRIDE, CMP_S_PER_DOT)]
        per_dot = -(-len(between) // len(dots))
        accs = [None, None]
        for c, s in dots:
            xs = jnp.concatenate([stages[rc][c, pl.ds(s + j, CMP_ROW_CHUNK, stride=STAGE_PITCH), :].astype(BF16)
                                  for j in range(CMP_S_PER_DOT)], axis=1)
            d = _dot(xs, w1f_ref[c, s * C_W:(s + CMP_S_PER_DOT) * C_W, :])
            accs[c] = d if accs[c] is None else accs[c] + d
            for thunk in between[:per_dot]:
                thunk()
            between = between[per_dot:]
        for c in range(2):
            r0 = pl.multiple_of(h * segs + rc * CMP_ROW_CHUNK, CMP_ROW_CHUNK)
            p_scr[c, pl.ds(r0, CMP_ROW_CHUNK), :] = accs[c]

    for lp in range(pages_per_chunk):
        transpose_page(0, lp)
    for rc in range(n_chunks):
        nxt = [functools.partial(transpose_page, rc + 1, lp) for lp in range(pages_per_chunk)] if rc + 1 < n_chunks else []
        first_layer(rc, nxt)

    @pl.when(h == steps_per_batch - 1)
    def _():
        for c in range(2):
            o_ref[c * C_W:(c + 1) * C_W, :] = _cmp_second_layer_fm(
                p_scr[c], _cmp_bias_c(pe_ref, w1_ref, b1_ref, c), w2t_ref[c])


def _cmp_sample_fm(page_table, cache_fm, w1_full, pe_flat, cmp_w1, cmp_b1, w2_bd_t):
    bsz, n_pages = page_table.shape
    steps = n_pages // CMP_PAGES_PER_STEP
    n_seg = n_pages * SEGS_PER_PAGE
    const = lambda a: pl.BlockSpec(a.shape, lambda b, h, pt: (0,) * a.ndim, pipeline_mode=pl.Buffered(1))
    grid_spec = pltpu.PrefetchScalarGridSpec(
        num_scalar_prefetch=1,
        grid=(bsz, steps),
        in_specs=[pl.BlockSpec(memory_space=pl.ANY), const(w1_full), const(pe_flat), const(cmp_w1),
                  const(cmp_b1), const(w2_bd_t)],
        out_specs=pl.BlockSpec((None, KV_W, n_seg), lambda b, h, pt: (b, 0, 0)),
        scratch_shapes=[pltpu.VMEM((2, CMP_PAGES_PER_STEP, KV_W, PAGE_SIZE), F32),
                        pltpu.VMEM((2, CMP_ROW_CHUNK * STAGE_PITCH, C_W), F32),
                        pltpu.VMEM((2, CMP_ROW_CHUNK * STAGE_PITCH, C_W), F32),
                        pltpu.VMEM((2, n_seg, PC_W), F32),
                        pltpu.SemaphoreType.DMA((2,))],
    )
    return pl.pallas_call(
        functools.partial(_cmp_sample_fm_kernel, steps_per_batch=steps),
        grid_spec=grid_spec,
        out_shape=jax.ShapeDtypeStruct((bsz, KV_W, n_seg), F32),
        compiler_params=_cparams(("arbitrary", "arbitrary")),
        name="nsa_compress_sample",
    )(page_table, cache_fm, w1_full, pe_flat, cmp_w1, cmp_b1, w2_bd_t)


def _nsa_sample_select_fm_kernel(q_ref, cmp_ref, oc_ref, idx_ref, *, past_len, nb, nb_pad, nc):
    hd, grp = NSA_HEAD_DIM, NSA_GROUP
    m_rows = grp * SQ_ROWS
    q_all = q_ref[...]
    n_cmp = cmp_ref.shape[1]
    tq_rows = past_len + lax.broadcasted_iota(jnp.int32, (m_rows, 1), 0) % SQ_ROWS
    tq_lane = past_len + lax.broadcasted_iota(jnp.int32, (1, LANES), 1) % SQ_ROWS
    t_end = lax.broadcasted_iota(jnp.int32, (1, n_cmp), 1) * CMP_STRIDE + (CMP_BLK - 1)
    ov_t = _overlap_t(nb_pad, n_cmp, nc)
    psums = []
    for g in range(NSA_KV_HEADS):
        o_c, p_c = _cmp_branch_fm(_scaled_group_queries(q_all, g), cmp_ref, g, t_end <= tq_rows)
        psums.append(_sum_heads(p_c, SQ_ROWS))
        for r in range(grp):
            h = grp * g + r
            oc_ref[:, h * hd:(h + 1) * hd] = o_c[r * SQ_ROWS:(r + 1) * SQ_ROWS]
    psum = jnp.concatenate(psums + [jnp.zeros((LANES - NSA_KV_HEADS * SQ_ROWS, n_cmp), F32)], axis=0)
    score_t = _select_scores_t(_importance_t(psum, ov_t), tq_lane)
    rowid = lax.broadcasted_iota(jnp.int32, score_t.shape, 0)
    _, picks = _topk_rows(jnp.where(rowid < nb, score_t, -jnp.inf), min(SLC_TOP_N, nb))
    idx_ref[...] = jnp.concatenate(picks, axis=0)


def _nsa_sample_select_fm(q_pad, cmp_t, past_len, seq_new):
    bsz = q_pad.shape[0]
    n_seg = cmp_t.shape[2]
    nb = -(-(past_len + seq_new) // SLC_BLK)
    nb_pad = -(-nb // SUBLANES) * SUBLANES
    n_pick = min(SLC_TOP_N, nb)
    kern = functools.partial(_nsa_sample_select_fm_kernel, past_len=past_len, nb=nb, nb_pad=nb_pad,
                             nc=n_seg - CMP_R + 1)
    return pl.pallas_call(
        kern,
        grid=(bsz,),
        in_specs=[pl.BlockSpec((None, SQ_ROWS, NSA_WIDTH), lambda b: (b, 0, 0)),
                  pl.BlockSpec((None, KV_W, n_seg), lambda b: (b, 0, 0))],
        out_specs=[pl.BlockSpec((None, SQ_ROWS, NSA_WIDTH), lambda b: (b, 0, 0)),
                   pl.BlockSpec((None, n_pick, LANES), lambda b: (b, 0, 0))],
        out_shape=[jax.ShapeDtypeStruct((bsz, SQ_ROWS, NSA_WIDTH), F32),
                   jax.ShapeDtypeStruct((bsz, n_pick, LANES), jnp.int32)],
        compiler_params=_cparams(("parallel",)),
        name="nsa_select_sample",
    )(q_pad, cmp_t)


def _nsa_sample_attend_fm_kernel(pt_ref, idx_ref, q_ref, gate_ref, oc_ref, cache_ref, tail_ref, winp_ref,
                                 winn_ref, o_ref, kvbuf, sem, *, past_len, seq_new, n_pick):
    b = pl.program_id(0)
    nbatch = pl.num_programs(0)
    hd, grp = NSA_HEAD_DIM, NSA_GROUP
    n_items = seq_new * NSA_KV_HEADS * n_pick
    n_past_blocks = past_len // SLC_BLK
    blocks_per_page = PAGE_SIZE // SLC_BLK

    def block_id(bb, item):
        return jnp.minimum(idx_ref[bb * n_items + item], n_past_blocks)

    def copy(bb, item, slot):
        g = (item // n_pick) % NSA_KV_HEADS
        lanes = slice((item % n_pick) * PAGE_SIZE, (item % n_pick + 1) * PAGE_SIZE)
        rows = slice(g * hd, (g + 1) * hd)
        past_blk = jnp.minimum(block_id(bb, item), n_past_blocks - 1)
        page = pt_ref[bb, lax.shift_right_logical(past_blk, blocks_per_page.bit_length() - 1)]
        return pltpu.make_async_copy(cache_ref.at[page, :, rows, :], kvbuf.at[slot, item // n_pick, :, :, lanes],
                                     sem.at[slot])

    def start_fetch(bb, slot):
        for item in range(n_items):
            copy(bb, item, slot).start(priority=item % 2)

    @pl.when(b == 0)
    def _():
        start_fetch(b, 0)

    @pl.when(b + 1 < nbatch)
    def _():
        start_fetch(b + 1, (b + 1) % 2)

    slot = b % 2

    for item in range(n_items):
        copy(b, item, slot).wait()

    head_row = lax.broadcasted_iota(jnp.int32, (NSA_HEADS, 1), 0)
    tok = lax.broadcasted_iota(jnp.int32, (1, PAGE_SIZE), 1)
    wb, wn = winp_ref.shape[1], winn_ref.shape[1]
    tw_p = past_len - wb + lax.broadcasted_iota(jnp.int32, (1, wb), 1)
    tw_n = past_len + lax.broadcasted_iota(jnp.int32, (1, wn), 1)

    def heads16(x):
        return jnp.concatenate([x.astype(BF16), jnp.zeros_like(x, dtype=BF16)], axis=0)

    def merge_groups(per_group):
        out = per_group[0]
        for g in range(1, NSA_KV_HEADS):
            out = jnp.where(head_row >= g * grp, per_group[g], out)
        return out

    for qi in range(seq_new):
        tq = past_len + qi
        q16 = heads16(q_ref[qi] * QK_SCALE)
        gates = jax.nn.sigmoid(gate_ref[qi])
        o_s_g, o_w_g = [], []
        for g in range(NSA_KV_HEADS):
            qg_i = qi * NSA_KV_HEADS + g
            kl = g * hd
            pieces, new_picked = [], jnp.zeros((1, PAGE_SIZE), jnp.int32)
            for kk in range(n_pick):
                blk_k = block_id(b, qg_i * n_pick + kk)
                pieces.append((blk_k < n_past_blocks) & (tok // SLC_BLK == blk_k % blocks_per_page)
                              & (blk_k * SLC_BLK + tok % SLC_BLK <= tq))
                new_picked = jnp.maximum(new_picked, (blk_k == n_past_blocks).astype(jnp.int32))
            pieces.append((new_picked > 0) & (tok < SLC_BLK) & (past_len + tok <= tq))
            mask = jnp.concatenate(pieces, axis=1)
            kcat = jnp.concatenate([kvbuf[slot, qg_i, 0].astype(BF16), tail_ref[0, kl:kl + hd, :].astype(BF16)], axis=1)
            vcat = jnp.concatenate([kvbuf[slot, qg_i, 1].astype(BF16), tail_ref[1, kl:kl + hd, :].astype(BF16)], axis=1)
            p = _masked_softmax(_dot(q16, kcat)[:NSA_HEADS], mask)
            o_s_g.append(_dot_nt(heads16(p), vcat)[:NSA_HEADS])
            vl = (NSA_KV_HEADS + g) * hd
            bias_p = jnp.where((tw_p <= tq) & (tw_p > tq - WINDOW) & (tw_p >= 0), 0.0, NEG)
            bias_n = jnp.where((tw_n <= tq) & (tw_n > tq - WINDOW), 0.0, NEG)
            s_p = _dot(q16, winp_ref[kl:kl + hd, :].astype(BF16))[:NSA_HEADS] + bias_p
            s_n = _dot(q16, winn_ref[kl:kl + hd, :].astype(BF16))[:NSA_HEADS] + bias_n
            m = jnp.maximum(jnp.max(s_p, axis=-1, keepdims=True), jnp.max(s_n, axis=-1, keepdims=True))
            e_p, e_n = jnp.exp(s_p - m), jnp.exp(s_n - m)
            den = jnp.maximum(jnp.sum(e_p, axis=-1, keepdims=True) + jnp.sum(e_n, axis=-1, keepdims=True), 1e-30)
            o_w_g.append((_dot_nt(heads16(e_p), winp_ref[vl:vl + hd, :].astype(BF16))
                          + _dot_nt(heads16(e_n), winn_ref[vl:vl + hd, :].astype(BF16)))[:NSA_HEADS] / den)
        o_ref[qi] = (gates[:, 0:1] * oc_ref[qi] + gates[:, 1:2] * merge_groups(o_s_g)
                     + gates[:, 2:3] * merge_groups(o_w_g))


def _nsa_sample_attend_fm(page_table, idx_flat, q_heads, gate_logits, o_c, cache_fm, tail_fm, win_past, win_new,
                          past_len, seq_new, n_pick):
    bsz = q_heads.shape[0]
    wb, wn = win_past.shape[2], win_new.shape[2]
    kern = functools.partial(_nsa_sample_attend_fm_kernel, past_len=past_len, seq_new=seq_new, n_pick=n_pick)
    per_b = lambda n, w: pl.BlockSpec((None, n, w), lambda b, pt, ix: (b, 0, 0))
    per_bq = lambda w: pl.BlockSpec((None, seq_new, NSA_HEADS, w), lambda b, pt, ix: (b, 0, 0, 0))
    n_qg = seq_new * NSA_KV_HEADS
    buf = pltpu.VMEM((2, n_qg, 2, NSA_HEAD_DIM, n_pick * PAGE_SIZE), F32)
    grid_spec = pltpu.PrefetchScalarGridSpec(
        num_scalar_prefetch=2,
        grid=(bsz,),
        in_specs=[per_bq(NSA_HEAD_DIM), per_bq(LANES), per_bq(NSA_HEAD_DIM),
                  pl.BlockSpec(memory_space=pl.ANY),
                  pl.BlockSpec((None, 2, C_W, LANES), lambda b, pt, ix: (b, 0, 0, 0)),
                  per_b(KV_W, wb), per_b(KV_W, wn)],
        out_specs=per_bq(NSA_HEAD_DIM),
        scratch_shapes=[buf, pltpu.SemaphoreType.DMA((2,))],
    )
    return pl.pallas_call(
        kern,
        grid_spec=grid_spec,
        out_shape=jax.ShapeDtypeStruct((bsz, seq_new, NSA_HEADS, NSA_HEAD_DIM), F32),
        compiler_params=_cparams(("arbitrary",)),
        name="nsa_attend_sample",
    )(page_table, idx_flat, q_heads, gate_logits, o_c, cache_fm, tail_fm, win_past, win_new)


PROMPT_TM = 512
FFN_TM = 512
GLA_TL = 512
SAMPLE_GLA_ROWS = 16
GLA_PROMPT_NBB = 8
GLA_SAMPLE_NBB = 8


def _pad_rows(x, n):
    return jnp.pad(x, ((0, 0), (0, n - x.shape[1]), (0, 0)))


def kernel(x_prompt, x_sample, state_gla, cache_cmp_kv, cache_slc_kv, cache_win_kv, page_table, c_prompt,
           c_sample, ln_in_g, ln_in_b, w_ada, b_ada, w_in, gla_w_a2, gla_b_a, gla_norm_g, cmp_pe, cmp_w1,
           cmp_b1, cmp_w2, w_o, ln1_g, ln1_b, w_ffn_in, w_ffn_out, ln2_g, ln2_b):
    assert w_in.shape[0] == DEPTH == 1
    l = 0
    bp, lp, d = x_prompt.shape
    bs, ls, _ = x_sample.shape
    n_pool = cache_cmp_kv.shape[1]
    n_pages = page_table.shape[1]
    past_len = n_pages * PAGE_SIZE
    wb = cache_win_kv.shape[2]
    assert ((past_len + ls) // CMP_STRIDE) * CMP_STRIDE <= past_len and past_len % SLC_BLK == 0
    assert ls <= SQ_ROWS and ls <= SLC_BLK and wb == WINDOW

    w_perm = _permute_w_in(w_in[l])
    w_o_b, w_fi_b, w_fo_b = w_o[l].astype(BF16), w_ffn_in[l].astype(BF16), w_ffn_out[l].astype(BF16)
    w_a2p = jnp.zeros((AUX_W, GLA_KW), F32).at[:GLA_RANK].set(gla_w_a2[l])
    b_a = gla_b_a[l].reshape(1, GLA_KW)
    norm_g = gla_norm_g[l].reshape(1, GLA_DV)
    w1c, w2c_t, pe_flat = _cmp_weights_fm(cmp_pe[l], cmp_w1[l], cmp_w2[l])
    kvt = (2, NSA_KV_HEADS, NSA_HEAD_DIM)

    def fm_view(a):
        return jnp.transpose(a, (0, 2, 3, 4, 1)).reshape(a.shape[0], KV_W, a.shape[1])

    def tok_view(a_fm):
        n, _, t = a_fm.shape
        return jnp.transpose(a_fm.reshape((n,) + kvt + (t,)), (0, 4, 1, 2, 3))[None]

    mod = _ada(jnp.concatenate([c_prompt, c_sample], axis=0), w_ada[l], b_ada[l])
    mods_p = [m[:bp, None, :] for m in jnp.split(mod, 6, axis=-1)]
    mods_s = [jnp.repeat(m[bp:], ls, axis=0)[None] for m in jnp.split(mod, 6, axis=-1)]

    def out_ffn(x2d, o_g, o_n, mods, tm, rpm):
        sh1, sc1, ga1, sh2, sc2, ga2 = mods
        return _out_ffn(x2d, o_g, o_n, ga1, sc2, sh2, ga2, ln_in_g, ln_in_b, w_o_b, ln1_g[l], ln1_b[l], w_fi_b,
                        w_fo_b, ln2_g[l], ln2_b[l], tm, rpm)

    xp2 = x_prompt.reshape(bp * lp, d)
    rpm = lp // PROMPT_TM
    qk, v, r, qn, kvc, aux, kvc_t, kvs_t, kvw_t = _inproj(
        xp2, mods_p[1], mods_p[0], ln_in_g, ln_in_b, w_perm, PROMPT_TM, rpm, seq_per_batch=lp)
    b3 = lambda a: a.reshape(bp, lp, a.shape[-1])
    o_g, s_p = _gla(b3(qk), b3(v), b3(r), b3(aux), w_a2p, b_a, norm_g, jnp.zeros((bp, GLA_KW, GLA_DV), F32),
                    chunk=GLA_CHUNK, sub=GLA_SUB, tl=GLA_TL, l_valid=lp, nbb=GLA_PROMPT_NBB)
    cmp_t_p = _cmp_prompt_fm(kvc, lp, w1c, pe_flat, cmp_w1[l], cmp_b1[l], w2c_t)
    o_n = _nsa_prompt_fm(b3(qn), b3(aux), cmp_t_p, kvs_t, kvw_t)
    y_p = out_ffn(xp2, o_g.reshape(bp * lp, GLA_WIDTH), o_n.reshape(bp * lp, NSA_WIDTH), mods_p, FFN_TM,
                  lp // FFN_TM)
    w_keep = min(WINDOW, lp)
    outs_p = (y_p.reshape(bp, lp, d), s_p.reshape(1, bp, GLA_HEADS, GLA_DK, GLA_DV),
              tok_view(kvc_t), tok_view(kvs_t), tok_view(kvw_t[:, :, lp - w_keep:]))

    ts = bs * ls
    xs2 = x_sample.reshape(ts, d)
    qk, v, r, qn, kvc, aux, kvs, kvw = _inproj(xs2, mods_s[1], mods_s[0], ln_in_g, ln_in_b, w_perm, ts, 1)
    s3 = lambda a: a.reshape(bs, ls, a.shape[-1])
    g16 = lambda a: _pad_rows(s3(a), SAMPLE_GLA_ROWS)
    o_g, s_s = _gla(g16(qk), g16(v), g16(r), g16(aux), w_a2p, b_a, norm_g, state_gla[l].reshape(bs, GLA_KW, GLA_DV),
                    chunk=SAMPLE_GLA_ROWS, sub=SAMPLE_GLA_ROWS, tl=SAMPLE_GLA_ROWS, l_valid=ls,
                    nbb=GLA_SAMPLE_NBB)
    o_g = o_g[:, :ls].reshape(ts, GLA_WIDTH)
    kvc = jnp.concatenate([kvc[0], kvc[1]], axis=1)
    cmp_t_s = _cmp_sample_fm(page_table, fm_view(cache_cmp_kv[l]), w1c, pe_flat, cmp_w1[l], cmp_b1[l], w2c_t)
    q_pad = _pad_rows(s3(qn), SQ_ROWS)
    o_c, idx = _nsa_sample_select_fm(q_pad, cmp_t_s, past_len, ls)
    n_pick = idx.shape[1]
    idx = idx[:, :, :NSA_KV_HEADS * SQ_ROWS].reshape(bs, n_pick, NSA_KV_HEADS, SQ_ROWS)[..., :ls]
    idx_flat = jnp.transpose(idx, (0, 3, 2, 1)).reshape(-1)
    new_fm =lambda a: jnp.pad(jnp.transpose(s3(a), (0, 2, 1)), ((0, 0), (0, 0), (0, LANES - ls)))
    win_past = fm_view(cache_win_kv[l])
    per_head = lambda a, w: a.reshape(bs, -1, NSA_HEADS, w)[:, :ls]
    gate_logits = jnp.pad(per_head(aux[:, GATE_OFF:GATE_OFF + N_GATES], 3), ((0, 0),) * 3 + ((0, LANES - 3),))
    o_n = _nsa_sample_attend_fm(page_table, idx_flat, per_head(qn.astype(F32), NSA_HEAD_DIM), gate_logits,
                                per_head(o_c, NSA_HEAD_DIM),
                                fm_view(cache_slc_kv[l]).reshape(n_pool, 2, C_W, PAGE_SIZE),
                                new_fm(kvs).reshape(bs, 2, C_W, LANES), win_past, new_fm(kvw),
                                past_len, ls, n_pick)
    o_n = o_n.reshape(ts, NSA_WIDTH).astype(BF16)
    y_s = out_ffn(xs2, o_g, o_n, mods_s, ts, 1)
    win_s = jnp.concatenate([win_past[:, :, ls:], jnp.transpose(s3(kvw), (0, 2, 1))], axis=2)
    outs_s = (y_s.reshape(bs, ls, d), s_s.reshape(1, bs, GLA_HEADS, GLA_DK, GLA_DV),
              kvc.reshape((1, bs, ls) + kvt), kvs.reshape((1, bs, ls) + kvt), tok_view(win_s))

    return (outs_p[0], outs_s[0], outs_p[1], outs_s[1], outs_p[2], outs_s[2], outs_p[3], outs_s[3],
            outs_p[4], outs_s[4])
```

```python
import functools
import math

import numpy as np
import jax
import jax.numpy as jnp
from jax import lax
from jax.experimental import pallas as pl
from jax.experimental.pallas import tpu as pltpu

F32 = jnp.float32
BF16 = jnp.bfloat16

D_MODEL = 1024
DEPTH = 1
PAGE_SIZE = 128
GLA_HEADS = 4
GLA_DV = D_MODEL // (2 * GLA_HEADS)
GLA_DK = GLA_DV // 2
GLA_RANK = 16
GLA_TAU = 16.0
GLA_CHUNK = 64
GLA_SUB = 32
GLA_WIDTH = GLA_HEADS * GLA_DV
GLA_KW = GLA_HEADS * GLA_DK
NSA_HEADS = 8
NSA_KV_HEADS = 2
NSA_GROUP = NSA_HEADS // NSA_KV_HEADS
NSA_HEAD_DIM = D_MODEL // (2 * NSA_HEADS)
NSA_WIDTH = NSA_HEADS * NSA_HEAD_DIM
CMP_BLK = 32
CMP_STRIDE = 16
CMP_HIDDEN = 2 * NSA_HEAD_DIM
SLC_BLK = 64
SLC_TOP_N = 16
WINDOW = 512
FORCE_BONUS = 1e4
NEG = -1e30
D_FF = -(-8 * D_MODEL // (3 * 256)) * 256
ALPHA = (2 * DEPTH) ** 0.25
KV_W = 2 * NSA_KV_HEADS * NSA_HEAD_DIM
N_GATES = 3 * NSA_HEADS
IN_SIZES = (GLA_KW, GLA_KW, GLA_WIDTH, GLA_RANK, GLA_WIDTH, NSA_WIDTH, KV_W, KV_W, KV_W, N_GATES)
IN_WIDTH = sum(IN_SIZES)
LN_EPS = 1e-5

LANES = 128
SUBLANES = 8
VMEM_LIMIT_BYTES = 56 * 1024 * 1024

AUX_W = LANES
IN_GROUPS = (2 * GLA_KW, GLA_WIDTH, GLA_WIDTH, NSA_WIDTH, KV_W, KV_W, KV_W, AUX_W)
IN_PERM_W = sum(IN_GROUPS)
GATE_OFF = GLA_RANK


def _cparams(sem):
    return pltpu.CompilerParams(dimension_semantics=sem, vmem_limit_bytes=VMEM_LIMIT_BYTES)


def _split3(a):
    hi = a.astype(BF16)
    r1 = a - hi.astype(F32)
    mid = r1.astype(BF16)
    lo = (r1 - mid.astype(F32)).astype(BF16)
    return hi, mid, lo


def _dot(a, b):
    return jnp.dot(a, b, preferred_element_type=F32)


def _dot_nt(a, b):
    return lax.dot_general(a, b, (((1,), (1,)), ((), ())), preferred_element_type=F32)


def _dot_tn(a, b):
    return lax.dot_general(a, b, (((0,), (0,)), ((), ())), preferred_element_type=F32)


def _layer_norm(x, g, b):
    mu = jnp.mean(x, axis=-1, keepdims=True)
    xc = x - mu
    var = jnp.mean(xc * xc, axis=-1, keepdims=True)
    return xc * lax.rsqrt(var + LN_EPS) * g + b


def _ada_kernel(c_ref, w_ref, b_ref, o_ref):
    c = c_ref[...]
    a = (c * jax.nn.sigmoid(c)).astype(BF16)
    o_ref[...] = _dot(a, w_ref[...].astype(BF16)) + b_ref[...]


def _ada(c, w_ada, b_ada):
    n, d = c.shape
    m = w_ada.shape[1]
    tn = D_MODEL
    return pl.pallas_call(
        _ada_kernel,
        grid=(m // tn,),
        in_specs=[pl.BlockSpec((n, d), lambda j: (0, 0)),
                  pl.BlockSpec((d, tn), lambda j: (0, j)),
                  pl.BlockSpec((1, tn), lambda j: (0, j))],
        out_specs=pl.BlockSpec((n, tn), lambda j: (0, j)),
        out_shape=jax.ShapeDtypeStruct((n, m), F32),
        compiler_params=_cparams(("parallel",)),
        name="ada_mod",
    )(c, w_ada, b_ada.reshape(1, m))


N_KV_GROUPS = 3


def _inproj_kernel(x_ref, sc_ref, sh_ref, g_ref, b_ref, w_ref, *o_refs, feature_major):
    xn = _layer_norm(x_ref[...], g_ref[...], b_ref[...])
    u = (xn * (1.0 + sc_ref[...]) + sh_ref[...]).astype(BF16)
    plain = dict(zip(("qk", "v", "r", "qn", "kvc", "aux"), o_refs[:6]))
    extra = dict(zip(("kvc", "kvs", "kvw") if feature_major else ("kvs", "kvw"), o_refs[6:]))
    lo = 0
    for name, wdt in zip(IN_GROUP_NAMES, IN_GROUPS):
        z = _dot(u, w_ref[:, lo:lo + wdt])
        lo += wdt
        if name == "kvc":
            plain[name][0] = z[:, :KV_W // 2]
            plain[name][1] = z[:, KV_W // 2:]
        elif name in plain:
            plain[name][...] = z.astype(plain[name].dtype)
        if name in extra:
            extra[name][...] = jnp.transpose(z) if feature_major else z


IN_GROUP_NAMES = ("qk", "v", "r", "qn", "kvc", "kvs", "kvw", "aux")


def _inproj(x, sc, sh, ln_g, ln_b, w_perm, tm, rows_per_mod, seq_per_batch=None):
    t, d = x.shape
    r = sc.shape[1]
    feature_major = seq_per_batch is not None
    mod_spec = pl.BlockSpec((None, r, d), lambda i: (i // rows_per_mod, 0, 0))
    tok = lambda w, dt: (pl.BlockSpec((tm, w), lambda i: (i, 0)), jax.ShapeDtypeStruct((t, w), dt))
    outs = [tok(2 * GLA_KW, BF16), tok(GLA_WIDTH, BF16), tok(GLA_WIDTH, BF16), tok(NSA_WIDTH, BF16),
            (pl.BlockSpec((2, tm, KV_W // 2), lambda i: (0, i, 0)), jax.ShapeDtypeStruct((2, t, KV_W // 2), F32)),
            tok(AUX_W, F32)]
    if feature_major:
        tpb = seq_per_batch // tm
        outs += [(pl.BlockSpec((None, KV_W, tm), lambda i: (i // tpb, 0, i % tpb)),
                  jax.ShapeDtypeStruct((t // seq_per_batch, KV_W, seq_per_batch), F32))] * N_KV_GROUPS
    else:
        outs += [tok(KV_W, F32)] * 2
    return pl.pallas_call(
        functools.partial(_inproj_kernel, feature_major=feature_major),
        grid=(t // tm,),
        in_specs=[pl.BlockSpec((tm, d), lambda i: (i, 0)), mod_spec, mod_spec,
                  pl.BlockSpec((1, d), lambda i: (0, 0)), pl.BlockSpec((1, d), lambda i: (0, 0)),
                  pl.BlockSpec((d, IN_PERM_W), lambda i: (0, 0))],
        out_specs=[o[0] for o in outs],
        out_shape=[o[1] for o in outs],
        compiler_params=_cparams(("parallel",)),
        name="ln_mod_inproj",
    )(x, sc, sh, ln_g.reshape(1, d), ln_b.reshape(1, d), w_perm)


def _permute_w_in(w_in):
    q_g, k_g, v_g, a_g, r_g, q_n, kv_c, kv_s, kv_w, g_n = jnp.split(w_in, np.cumsum(IN_SIZES)[:-1], axis=1)
    pad = jnp.zeros((w_in.shape[0], AUX_W - GLA_RANK - N_GATES), w_in.dtype)
    return jnp.concatenate([q_g, k_g, v_g, r_g, q_n, kv_c, kv_s, kv_w, a_g, g_n, pad], axis=1).astype(BF16)


FF_CHUNK = 256


def _out_ffn_kernel(x_ref, og_ref, on_ref, ga1_ref, sc2_ref, sh2_ref, ga2_ref, lng_ref, lnb_ref,
                    wo_ref, l1g_ref, l1b_ref, wfi_ref, wfo_ref, l2g_ref, l2b_ref, y_ref):
    x = _layer_norm(x_ref[...], lng_ref[...], lnb_ref[...])
    mix = _dot(og_ref[...], wo_ref[0:GLA_WIDTH, :]) + _dot(on_ref[...], wo_ref[GLA_WIDTH:, :])
    x1 = _layer_norm(ALPHA * x + ga1_ref[...] * mix, l1g_ref[...], l1b_ref[...])
    u2 = (x1 * (1.0 + sc2_ref[...]) + sh2_ref[...]).astype(BF16)
    ffn = jnp.zeros(x1.shape, F32)
    for c in range(D_FF // FF_CHUNK):
        lo = c * FF_CHUNK
        gate = _dot(u2, wfi_ref[:, lo:lo + FF_CHUNK])
        up = _dot(u2, wfi_ref[:, D_FF + lo:D_FF + lo + FF_CHUNK])
        f = (gate * jax.nn.sigmoid(gate) * up).astype(BF16)
        ffn = ffn + _dot(f, wfo_ref[lo:lo + FF_CHUNK, :])
    y_ref[...] = _layer_norm(ALPHA * x1 + ga2_ref[...] * ffn, l2g_ref[...], l2b_ref[...])


def _out_ffn(x, o_g, o_n, ga1, sc2, sh2, ga2, ln_in_g, ln_in_b, w_o, ln1_g, ln1_b, w_fi, w_fo, ln2_g, ln2_b,
             tm, rows_per_mod):
    t, d = x.shape
    r = ga1.shape[1]
    mod_spec = pl.BlockSpec((None, r, d), lambda i: (i // rows_per_mod, 0, 0))
    vec = lambda: pl.BlockSpec((1, d), lambda i: (0, 0))
    const = lambda shp: pl.BlockSpec(shp, lambda i: (0, 0), pipeline_mode=pl.Buffered(1))
    row = lambda a: a.reshape(1, d)
    return pl.pallas_call(
        _out_ffn_kernel,
        grid=(t // tm,),
        in_specs=[pl.BlockSpec((tm, d), lambda i: (i, 0)),
                  pl.BlockSpec((tm, GLA_WIDTH), lambda i: (i, 0)),
                  pl.BlockSpec((tm, NSA_WIDTH), lambda i: (i, 0)),
                  mod_spec, mod_spec, mod_spec, mod_spec, vec(), vec(),
                  const((d, d)), vec(), vec(), const((d, 2 * D_FF)), const((D_FF, d)), vec(), vec()],
        out_specs=pl.BlockSpec((tm, d), lambda i: (i, 0)),
        out_shape=jax.ShapeDtypeStruct((t, d), F32),
        compiler_params=_cparams(("parallel",)),
        name="outproj_ffn",
    )(x, o_g, o_n, ga1, sc2, sh2, ga2, row(ln_in_g), row(ln_in_b), w_o, row(ln1_g), row(ln1_b), w_fi, w_fo,
      row(ln2_g), row(ln2_b))


GLA_EXP_CLAMP = 80.0


def _gla_kernel(qk_ref, v_ref, r_ref, aux_ref, wa_ref, ba_ref, ng_ref, s0_ref, o_ref, sout_ref, s_scr,
                *, chunk, sub, tl, l_valid, l_pad, nbb):
    t = pl.program_id(1)
    c = chunk
    n_sub = c // sub
    hw = GLA_KW

    @pl.when(t == 0)
    def _():
        s_scr[...] = s0_ref[...]

    ri = lax.broadcasted_iota(jnp.int32, (c, c), 0)
    ci = lax.broadcasted_iota(jnp.int32, (c, c), 1)
    causal = ci <= ri
    tril = causal.astype(BF16)
    rowid = lax.broadcasted_iota(jnp.int32, (c, hw), 0)
    head_of_lane = lax.broadcasted_iota(jnp.int32, (c, hw), 1) // GLA_DK
    wa = wa_ref[...]
    wa_hi = wa.astype(BF16)
    wa_mid = (wa - wa_hi.astype(F32)).astype(BF16)
    ba = ba_ref[...]
    ng = ng_ref[...]

    def body(i, carry):
        for e in range(nbb):
            s_scr[e] = one_chunk(i, e, s_scr[e])
        return carry

    def one_chunk(i, e, s_old):
        r0 = pl.multiple_of(i * c, c)
        aux = aux_ref[e, pl.ds(r0, c), :]
        a_hi = aux.astype(BF16)
        a_mid = (aux - a_hi.astype(F32)).astype(BF16)
        z = _dot(a_hi, wa_hi) + _dot(a_mid, wa_hi) + _dot(a_hi, wa_mid) + ba
        g = jax.nn.log_sigmoid(z) / GLA_TAU
        if l_pad != l_valid:
            g = jnp.where(t * tl + r0 + rowid < l_valid, g, 0.0)
        g_hi, g_mid, g_lo = _split3(g)
        b = _dot(tril, g_hi) + _dot(tril, g_mid) + _dot(tril, g_lo)
        qk = qk_ref[e, pl.ds(r0, c), :]
        q = qk[:, :hw].astype(F32) * (GLA_DK ** -0.5)
        k = qk[:, hw:].astype(F32)
        v = v_ref[e, pl.ds(r0, c), :]
        b_last = b[c - 1:c, :]

        def heads_on_rows(x):
            return jnp.concatenate([jnp.where(head_of_lane == h, x, 0.0) for h in range(GLA_HEADS)], axis=0)

        o_inter =_dot(heads_on_rows(q * jnp.exp(b)).astype(BF16), s_old.astype(BF16))

        q_parts, k_parts = [], []
        for s_i in range(n_sub):
            b_ref = b[s_i * sub - 1:s_i * sub, :] if s_i > 0 else jnp.zeros((1, hw), F32)
            in_rows = (rowid >= s_i * sub) & (rowid < (s_i + 1) * sub)
            qt = jnp.where(in_rows, q * jnp.exp(jnp.minimum(b - b_ref, 0.0)), 0.0)
            kt = jnp.where(rowid < (s_i + 1) * sub, k * jnp.exp(jnp.minimum(b_ref - b, GLA_EXP_CLAMP)), 0.0)
            q_parts.append(heads_on_rows(qt).astype(BF16))
            k_parts.append(kt.astype(BF16))
        q_cat = jnp.concatenate(q_parts, axis=1) if n_sub > 1 else q_parts[0]
        k_cat = jnp.concatenate(k_parts, axis=1) if n_sub > 1 else k_parts[0]
        att = _dot_nt(q_cat, k_cat)

        r_t = r_ref[e, pl.ds(r0, c), :].astype(F32)
        for h in range(GLA_HEADS):
            att_h = jnp.where(causal, att[h * c:(h + 1) * c, :], 0.0).astype(BF16)
            o_h = o_inter[h * c:(h + 1) * c, :] + _dot(att_h, v[:, h * GLA_DV:(h + 1) * GLA_DV])
            o_h = o_h * lax.rsqrt(jnp.mean(o_h * o_h, axis=-1, keepdims=True) + 1e-6) * ng
            r_h = r_t[:, h * GLA_DV:(h + 1) * GLA_DV]
            o_ref[e, pl.ds(r0, c), h * GLA_DV:(h + 1) * GLA_DV] = (
                o_h * (r_h * jax.nn.sigmoid(r_h))).astype(o_ref.dtype)

        kd = jnp.concatenate([k * jnp.exp(b_last - b), jnp.zeros((LANES - c, hw), F32)], axis=0)
        kd_t = jnp.transpose(kd).astype(BF16)
        v_pad = jnp.concatenate([v, jnp.zeros((LANES - c, GLA_WIDTH), v.dtype)], axis=0)
        upd = jnp.concatenate([_dot(kd_t[h * GLA_DK:(h + 1) * GLA_DK, :], v_pad[:, h * GLA_DV:(h + 1) * GLA_DV])
                               for h in range(GLA_HEADS)], axis=0)
        decay = jnp.transpose(jnp.broadcast_to(jnp.exp(b_last), (LANES, hw)))
        return decay * s_old + upd

    lax.fori_loop(0, tl // c, body, 0)

    @pl.when(t == pl.num_programs(1) - 1)
    def _():
        sout_ref[...] = s_scr[...]


def _gla(qk, v, r, aux, w_a2p, b_a, norm_g, s0, *, chunk, sub, tl, l_valid, nbb):
    bsz, l_pad, _ = qk.shape
    nt = l_pad // tl
    kern = functools.partial(_gla_kernel, chunk=chunk, sub=sub, tl=tl, l_valid=l_valid, l_pad=l_pad, nbb=nbb)
    tile = lambda w: pl.BlockSpec((nbb, tl, w), lambda b, t: (b, t, 0))
    full = lambda shp: pl.BlockSpec(shp, lambda b, t: (0, 0))
    st = pl.BlockSpec((nbb, GLA_KW, GLA_DV), lambda b, t: (b, 0, 0))
    return pl.pallas_call(
        kern,
        grid=(bsz // nbb, nt),
        in_specs=[tile(2 * GLA_KW), tile(GLA_WIDTH), tile(GLA_WIDTH), tile(AUX_W),
                  full((AUX_W, GLA_KW)), full((1, GLA_KW)), full((1, GLA_DV)), st],
        out_specs=[tile(GLA_WIDTH), st],
        out_shape=[jax.ShapeDtypeStruct((bsz, l_pad, GLA_WIDTH), BF16),
                   jax.ShapeDtypeStruct((bsz, GLA_KW, GLA_DV), F32)],
        scratch_shapes=[pltpu.VMEM((nbb, GLA_KW, GLA_DV), F32)],
        compiler_params=_cparams(("parallel", "arbitrary")),
        name="gla_scan",
    )(qk, v, r, aux, w_a2p, b_a, norm_g, s0)


CMP_R = CMP_BLK // CMP_STRIDE


QK_SCALE = NSA_HEAD_DIM ** -0.5
LOG2E = math.log2(math.e)


def _masked_softmax(s, mask):
    s = jnp.where(mask, s, NEG)
    e = jnp.where(mask, jnp.exp(s - jnp.max(s, axis=-1, keepdims=True)), 0.0)
    return e / jnp.maximum(jnp.sum(e, axis=-1, keepdims=True), 1e-30)


def _group_queries(q, g):
    hd = NSA_HEAD_DIM
    return jnp.concatenate([q[:, (NSA_GROUP * g + r) * hd:(NSA_GROUP * g + r + 1) * hd]
                            for r in range(NSA_GROUP)], axis=0)


def _topk_rows(score_t, n_pick):
    nb, nq = score_t.shape
    rowid = lax.broadcasted_iota(jnp.int32, (nb, nq), 0)
    taken = jnp.zeros((nb, nq), jnp.int32)
    picks = []
    for _ in range(n_pick):
        free = taken == 0
        cand = jnp.where(free, score_t, -jnp.inf)
        m = jnp.max(cand, axis=0, keepdims=True)
        hit = free & (cand == m)
        idx = jnp.min(jnp.where(hit, rowid, nb), axis=0, keepdims=True)
        taken = jnp.where(rowid == idx, 1, taken)
        picks.append(idx)
    return taken.astype(F32), picks


def _importance_t(psum, ov_t):
    hi, mid, _ = _split3(psum)
    return _dot_nt(ov_t, hi) + _dot_nt(ov_t, mid)


def _overlap_t(nb, nc_pad, nc):
    j = lax.broadcasted_iota(jnp.int32, (nb, nc_pad), 0) * SLC_BLK
    i = lax.broadcasted_iota(jnp.int32, (nb, nc_pad), 1) * CMP_STRIDE
    return ((i < j + SLC_BLK) & (i + CMP_BLK > j) & (i < nc * CMP_STRIDE)).astype(BF16)


def _select_scores_t(imp_t, tq_row):
    nb, nq = imp_t.shape
    j = lax.broadcasted_iota(jnp.int32, (nb, nq), 0)
    cur = tq_row // SLC_BLK
    forced = (j == 0) | (j == cur) | (j == cur - 1)
    return jnp.where(j * SLC_BLK <= tq_row, imp_t + FORCE_BONUS * forced.astype(F32), -jnp.inf)


NSA_TQ = 256
NSA_TK = 512


SEGS_PER_PAGE = PAGE_SIZE // CMP_STRIDE
CMP_PAGES_PER_STEP = 64
CMP_ROW_CHUNK = 256
STAGE_PITCH = 24


SQ_ROWS = SUBLANES


C_W = NSA_KV_HEADS * NSA_HEAD_DIM
HC_W = NSA_KV_HEADS * CMP_HIDDEN
PC_W = CMP_R * HC_W
MXU_DEPTH = 256
CMP_S_PER_DOT = MXU_DEPTH // C_W


def _cmp_weights_fm(cmp_pe, cmp_w1, cmp_w2):
    eye = jnp.eye(NSA_KV_HEADS, dtype=F32)
    w1r = cmp_w1.reshape(2, CMP_R, CMP_STRIDE, NSA_HEAD_DIM, CMP_HIDDEN)
    w1c = jnp.einsum('cmsdh,gG->csgdmGh', w1r, eye).reshape(2, CMP_STRIDE * C_W, PC_W).astype(BF16)
    w2c_t = jnp.einsum('chd,gG->cGdgh', cmp_w2, eye).reshape(2, C_W, HC_W).astype(BF16)
    pe_flat = jnp.transpose(cmp_pe, (1, 0, 2)).reshape(2, CMP_BLK * NSA_HEAD_DIM)
    return w1c, w2c_t, pe_flat


def _cmp_bias_c(pe_ref, w1_ref, b1_ref, c):
    pe = jnp.broadcast_to(pe_ref[c:c + 1, :], (SUBLANES, pe_ref.shape[1]))
    w1c = w1_ref[c]
    pe_hi = pe.astype(BF16)
    pe_mid = (pe - pe_hi.astype(F32)).astype(BF16)
    w_hi = w1c.astype(BF16)
    w_mid = (w1c - w_hi.astype(F32)).astype(BF16)
    pb = _dot(pe_hi, w_hi) + _dot(pe_mid, w_hi) + _dot(pe_hi, w_mid)
    bc = pb[0:1, :] + b1_ref[c:c + 1, :]
    return jnp.concatenate([bc] * NSA_KV_HEADS, axis=1)


def _cmp_first_layer(x_ref, c, row0, n_seg, w1c_ref, pitch=CMP_STRIDE):
    acc = None
    for s in range(0, CMP_STRIDE, CMP_S_PER_DOT):
        xs = jnp.concatenate([x_ref[c, pl.ds(row0 + s + j, n_seg, stride=pitch), :].astype(BF16)
                              for j in range(CMP_S_PER_DOT)], axis=1)
        d = _dot(xs, w1c_ref[c, s * C_W:(s + CMP_S_PER_DOT) * C_W, :])
        acc = d if acc is None else acc + d
    return acc


def _cmp_second_layer_fm(p, bias, w2t):
    n = p.shape[0]
    h = p[:, :HC_W] + pltpu.roll(p[:, HC_W:], n - 1, 0) + bias
    return _dot_nt(w2t, jax.nn.gelu(h).astype(BF16))


def _cmp_prompt_fm_kernel(x_ref, w1c_ref, pe_ref, w1_ref, b1_ref, w2t_ref, o_ref):
    n_seg = o_ref.shape[1]
    for c in range(2):
        p = _cmp_first_layer(x_ref, c, 0, n_seg, w1c_ref)
        o_ref[c * C_W:(c + 1) * C_W, :] = _cmp_second_layer_fm(p, _cmp_bias_c(pe_ref, w1_ref, b1_ref, c), w2t_ref[c])


def _cmp_prompt_fm(x_tok, seq, w1c, pe_flat, cmp_w1, cmp_b1, w2c_t):
    bsz = x_tok.shape[1] // seq
    n_seg = seq // CMP_STRIDE
    const = lambda a: pl.BlockSpec(a.shape, lambda b: (0,) * a.ndim)
    return pl.pallas_call(
        _cmp_prompt_fm_kernel,
        grid=(bsz,),
        in_specs=[pl.BlockSpec((2, seq, C_W), lambda b: (0, b, 0)),
                  const(w1c), const(pe_flat), const(cmp_w1), const(cmp_b1), const(w2c_t)],
        out_specs=pl.BlockSpec((None, KV_W, n_seg), lambda b: (b, 0, 0)),
        out_shape=jax.ShapeDtypeStruct((bsz, KV_W, n_seg), F32),
        compiler_params=_cparams(("parallel",)),
        name="nsa_compress_prompt",
    )(x_tok, w1c, pe_flat, cmp_w1, cmp_b1, w2c_t)


def _scaled_group_queries(q, g):
    return (_group_queries(q, g).astype(F32) * QK_SCALE).astype(BF16)


def _cmp_branch_fm(qg, cmp_ref, g, valid):
    hd = NSA_HEAD_DIM
    kl, vl = g * hd, (NSA_KV_HEADS + g) * hd
    s_c = _dot(qg, cmp_ref[kl:kl + hd, :].astype(BF16))
    p_c = _masked_softmax(s_c, valid)
    return _dot_nt(p_c.astype(BF16), cmp_ref[vl:vl + hd, :].astype(BF16)), p_c


def _sum_heads(p, rows):
    out = p[0:rows]
    for r in range(1, NSA_GROUP):
        out = out + p[r * rows:(r + 1) * rows]
    return out


ATT_RB = 32


def _nsa_prompt_fm_kernel(q_ref, aux_ref, cmp_ref, kvs_ref, kvw_ref, o_ref,
                          s_scr, bias_scr, e_scr, m_scr, corr_scr, acc_scr, *, seq, nc):
    tq_n, tk_n, hd, grp = NSA_TQ, NSA_TK, NSA_HEAD_DIM, NSA_GROUP
    m_rows = grp * tq_n
    q0 = pl.program_id(1) * tq_n
    q_all = q_ref[...]
    gates = jax.nn.sigmoid(aux_ref[...])
    n_cmp = cmp_ref.shape[1]
    nb = seq // SLC_BLK

    tq_col = q0 + lax.broadcasted_iota(jnp.int32, (tq_n, 1), 0)
    tq_rows = jnp.concatenate([tq_col] * grp, axis=0)
    tq_lane = q0 + lax.broadcasted_iota(jnp.int32, (1, NSA_KV_HEADS * tq_n), 1) % tq_n
    t_end = lax.broadcasted_iota(jnp.int32, (1, n_cmp), 1) * CMP_STRIDE + (CMP_BLK - 1)
    ov_t = _overlap_t(nb, n_cmp, nc)
    n_chunks = (q0 + tq_n + tk_n - 1) // tk_n
    w_len = WINDOW + tq_n
    w0 = pl.multiple_of(jnp.maximum(q0 - WINDOW, 0), tq_n)

    qgs = [_scaled_group_queries(q_all, g) for g in range(NSA_KV_HEADS)]
    o_cs, p_cs = [], []
    for g in range(NSA_KV_HEADS):
        o_c, p_c = _cmp_branch_fm(qgs[g], cmp_ref, g, t_end <= tq_rows)
        o_cs.append(o_c)
        p_cs.append(p_c)
    n_pick = min(SLC_TOP_N, nb)

    def ranked_blocks():
        imps = [_importance_t(_sum_heads(p_c, tq_n), ov_t) for p_c in p_cs]
        return _topk_rows(_select_scores_t(jnp.concatenate(imps, axis=1), tq_lane), n_pick)[0]

    def all_visible_blocks():
        j = lax.broadcasted_iota(jnp.int32, (nb, NSA_KV_HEADS * tq_n), 0)
        return (j * SLC_BLK <= tq_lane).astype(F32)

    sel_all = lax.cond(q0 + tq_n <= n_pick * SLC_BLK, all_visible_blocks, ranked_blocks)

    tw = w0 + lax.broadcasted_iota(jnp.int32, (1, w_len), 1)
    bias_w = jnp.where((tw <= tq_col) & (tw > tq_col - WINDOW), 0.0, NEG)

    groups = range(NSA_KV_HEADS)

    def exp_rows(g, bias_g, width, online):
        per_head = tq_n // ATT_RB
        for i in range(m_rows // ATT_RB):
            rows = slice(i * ATT_RB, (i + 1) * ATT_RB)
            brows = slice((i % per_head) * ATT_RB, (i % per_head + 1) * ATT_RB)
            s = s_scr[g, rows, :width] + bias_scr[bias_g, brows, :width]
            m_new = jnp.max(s, axis=-1, keepdims=True)
            if online:
                m_old = m_scr[g, rows, :]
                m_new = jnp.maximum(m_old, m_new)
                corr_scr[g, rows, :] = jnp.exp2(m_old - m_new)
                m_scr[g, rows, :] = m_new
            e_scr[g, rows, :width] = jnp.exp2(s - m_new).astype(BF16)

    def keys_log2(ref, g, cols):
        return (ref[g * hd:(g + 1) * hd, cols] * LOG2E).astype(BF16)

    def gate_lane(g, r, branch):
        return hd + GATE_OFF + 3 * (grp * g + r) + branch

    def values_with_ones(ref, g, cols, branch):
        v = ref[(NSA_KV_HEADS + g) * hd:(NSA_KV_HEADS + g + 1) * hd, cols].astype(BF16)
        row = lax.broadcasted_iota(jnp.int32, v.shape, 0) + hd
        ones = functools.reduce(jnp.logical_or, [row == gate_lane(g, r, branch) for r in range(grp)])
        return jnp.concatenate([v, ones.astype(BF16)], axis=0)

    m_scr[...] = jnp.full(m_scr.shape, NEG, F32)
    acc_scr[...] = jnp.zeros(acc_scr.shape, F32)
    sel_ts = [sel_all[:, g * tq_n:(g + 1) * tq_n].astype(BF16) for g in groups]

    def slc_chunk(ci, carry):
        k0 = pl.multiple_of(ci * tk_n, tk_n)
        cols = pl.ds(k0, tk_n)
        kpos = k0 + lax.broadcasted_iota(jnp.int32, (1, tk_n), 1)
        expand = (lax.broadcasted_iota(jnp.int32, (nb, tk_n), 0) == kpos // SLC_BLK).astype(BF16)
        for g in groups:
            sel_k = _dot_tn(sel_ts[g], expand)
            bias_scr[g, :, :tk_n] = jnp.where((sel_k > 0.5) & (kpos <= tq_col), 0.0, NEG)
            s_scr[g, :, :tk_n] = _dot(qgs[g], keys_log2(kvs_ref, g, cols))
        for g in groups:
            exp_rows(g, g, tk_n, True)
        for g in groups:
            acc_scr[g] = corr_scr[g] * acc_scr[g] + _dot_nt(e_scr[g, :, :tk_n], values_with_ones(kvs_ref, g, cols, 1))
        return carry

    lax.fori_loop(0, n_chunks, slc_chunk, 0)
    wcols = pl.ds(w0, w_len)
    bias_scr[0, :, :w_len] = bias_w
    for g in groups:
        s_scr[g, :, :w_len] = _dot(qgs[g], keys_log2(kvw_ref, g, wcols))
    for g in groups:
        exp_rows(g, 0, w_len, False)
    acc_ws = [_dot_nt(e_scr[g, :, :w_len], values_with_ones(kvw_ref, g, wcols, 2)) for g in groups]
    gates_rot = pltpu.roll(gates, hd, 1)
    for g in groups:
        for r in range(grp):
            h = grp * g + r
            rows = slice(r * tq_n, (r + 1) * tq_n)
            acc_s, acc_w = acc_scr[g, rows, :], acc_ws[g][rows]
            f_s = gates_rot / jnp.maximum(acc_s, 1e-30)
            f_w = gates_rot / jnp.maximum(acc_w, 1e-30)
            ls, lw = gate_lane(g, r, 1), gate_lane(g, r, 2)
            gc = GATE_OFF + 3 * h
            o = (gates[:, gc:gc + 1] * o_cs[g][rows] + f_s[:, ls:ls + 1] * acc_s[:, :hd]
                 + f_w[:, lw:lw + 1] * acc_w[:, :hd])
            o_ref[:, h * hd:(h + 1) * hd] = o.astype(o_ref.dtype)


def _nsa_prompt_fm(q, aux, cmp_t, kvs_t, kvw_t):
    bsz, seq, _ = q.shape
    n_seg = cmp_t.shape[2]
    kern = functools.partial(_nsa_prompt_fm_kernel, seq=seq, nc=n_seg - CMP_R + 1)
    tile = lambda w: pl.BlockSpec((None, NSA_TQ, w), lambda b, t: (b, t, 0))
    whole = lambda n: pl.BlockSpec((None, KV_W, n), lambda b, t: (b, 0, 0))
    m_rows = NSA_GROUP * NSA_TQ
    width = max(NSA_TK, WINDOW + NSA_TQ)
    ng = NSA_KV_HEADS
    return pl.pallas_call(
        kern,
        grid=(bsz, seq // NSA_TQ),
        in_specs=[tile(NSA_WIDTH), tile(AUX_W), whole(n_seg), whole(seq), whole(seq)],
        out_specs=tile(NSA_WIDTH),
        out_shape=jax.ShapeDtypeStruct((bsz, seq, NSA_WIDTH), BF16),
        scratch_shapes=[pltpu.VMEM((ng, m_rows, width), F32),
                        pltpu.VMEM((ng, NSA_TQ, width), F32),
                        pltpu.VMEM((ng, m_rows, width), BF16),
                        pltpu.VMEM((ng, m_rows, 1), F32),
                        pltpu.VMEM((ng, m_rows, 1), F32),
                        pltpu.VMEM((ng, m_rows, 2 * NSA_HEAD_DIM), F32)],
        compiler_params=_cparams(("parallel", "arbitrary")),
        name="nsa_attn_prompt",
    )(q, aux, cmp_t, kvs_t, kvw_t)


def _cmp_sample_fm_kernel(pt_ref, cache_ref, w1f_ref, pe_ref, w1_ref, b1_ref, w2t_ref, o_ref,
                          xbuf, stage_a, stage_b, p_scr, bias_scr, sem, *, steps_per_batch):
    b = pl.program_id(0)
    h = pl.program_id(1)
    step = b * steps_per_batch + h
    n_steps = pl.num_programs(0) * steps_per_batch
    pps = CMP_PAGES_PER_STEP
    segs = pps * SEGS_PER_PAGE

    def page_copy(bb, hh, p, slot):
        return pltpu.make_async_copy(cache_ref.at[pt_ref[bb, hh * pps + p]], xbuf.at[slot, p], sem.at[slot])

    def start_fetch(bb, hh, slot):
        for p in range(pps):
            page_copy(bb, hh, p, slot).start(priority=p % 2)

    @pl.when(step == 0)
    def _():
        start_fetch(b, h, 0)

    @pl.when(step + 1 < n_steps)
    def _():
        wrap = h + 1 == steps_per_batch
        start_fetch(jnp.where(wrap, b + 1, b), jnp.where(wrap, 0, h + 1), (step + 1) % 2)

    slot = step % 2
    for p in range(pps):
        page_copy(b, h, p, slot).wait()

    pages_per_chunk = CMP_ROW_CHUNK // SEGS_PER_PAGE
    n_chunks = segs // CMP_ROW_CHUNK
    stages = (stage_a, stage_b)
    assert n_chunks == len(stages)

    def transpose_page(rc, lp):
        p = rc * pages_per_chunk + lp
        for c in range(2):
            tok = jnp.transpose(xbuf[slot, p, c * C_W:(c + 1) * C_W, :].astype(BF16)).astype(F32)
            for n in range(SEGS_PER_PAGE):
                r0 = (lp * SEGS_PER_PAGE + n) * STAGE_PITCH
                stages[rc][c, r0:r0 + CMP_STRIDE, :] = tok[n * CMP_STRIDE:(n + 1) * CMP_STRIDE]

    def first_layer(rc, between=()):
        between = list(between)
        dots = [(c, s) for c in range(2) for s in range(0, CMP_STRIDE, CMP_S_PER_DOT)]
        per_dot = -(-len(between) // len(dots))
        accs = [None, None]
        for c, s in dots:
            xs = jnp.concatenate([stages[rc][c, pl.ds(s + j, CMP_ROW_CHUNK, stride=STAGE_PITCH), :].astype(BF16)
                                  for j in range(CMP_S_PER_DOT)], axis=1)
            d = _dot(xs, w1f_ref[c, s * C_W:(s + CMP_S_PER_DOT) * C_W, :])
            accs[c] = d if accs[c] is None else accs[c] + d
            for thunk in between[:per_dot]:
                thunk()
            between = between[per_dot:]
        for c in range(2):
            r0 = pl.multiple_of(h * segs + rc * CMP_ROW_CHUNK, CMP_ROW_CHUNK)
            p_scr[c, pl.ds(r0, CMP_ROW_CHUNK), :] = accs[c]

    for lp in range(pages_per_chunk):
        transpose_page(0, lp)
    for rc in range(n_chunks):
        nxt = [functools.partial(transpose_page, rc + 1, lp) for lp in range(pages_per_chunk)] if rc + 1 < n_chunks else []
        first_layer(rc, nxt)

    @pl.when(step == 0)
    def _():
        for c in range(2):
            bias_scr[c] = jnp.broadcast_to(_cmp_bias_c(pe_ref, w1_ref, b1_ref, c), (SUBLANES, HC_W))

    @pl.when(h == steps_per_batch - 1)
    def _():
        for c in range(2):
            o_ref[c * C_W:(c + 1) * C_W, :] = _cmp_second_layer_fm(p_scr[c], bias_scr[c, 0:1, :], w2t_ref[c])


def _cmp_sample_fm(page_table, cache_fm, w1_full, pe_flat, cmp_w1, cmp_b1, w2_bd_t):
    bsz, n_pages = page_table.shape
    steps = n_pages // CMP_PAGES_PER_STEP
    n_seg = n_pages * SEGS_PER_PAGE
    const = lambda a: pl.BlockSpec(a.shape, lambda b, h, pt: (0,) * a.ndim, pipeline_mode=pl.Buffered(1))
    grid_spec = pltpu.PrefetchScalarGridSpec(
        num_scalar_prefetch=1,
        grid=(bsz, steps),
        in_specs=[pl.BlockSpec(memory_space=pl.ANY), const(w1_full), const(pe_flat), const(cmp_w1),
                  const(cmp_b1), const(w2_bd_t)],
        out_specs=pl.BlockSpec((None, KV_W, n_seg), lambda b, h, pt: (b, 0, 0)),
        scratch_shapes=[pltpu.VMEM((2, CMP_PAGES_PER_STEP, KV_W, PAGE_SIZE), F32),
                        pltpu.VMEM((2, CMP_ROW_CHUNK * STAGE_PITCH, C_W), F32),
                        pltpu.VMEM((2, CMP_ROW_CHUNK * STAGE_PITCH, C_W), F32),
                        pltpu.VMEM((2, n_seg, PC_W), F32),
                        pltpu.VMEM((2, SUBLANES, HC_W), F32),
                        pltpu.SemaphoreType.DMA((2,))],
    )
    return pl.pallas_call(
        functools.partial(_cmp_sample_fm_kernel, steps_per_batch=steps),
        grid_spec=grid_spec,
        out_shape=jax.ShapeDtypeStruct((bsz, KV_W, n_seg), F32),
        compiler_params=_cparams(("arbitrary", "arbitrary")),
        name="nsa_compress_sample",
    )(page_table, cache_fm, w1_full, pe_flat, cmp_w1, cmp_b1, w2_bd_t)


def _nsa_sample_select_fm_kernel(q_ref, cmp_ref, oc_ref, idx_ref, *, past_len, nb, nb_pad, nc):
    hd, grp = NSA_HEAD_DIM, NSA_GROUP
    m_rows = grp * SQ_ROWS
    q_all = q_ref[...]
    n_cmp = cmp_ref.shape[1]
    tq_rows = past_len + lax.broadcasted_iota(jnp.int32, (m_rows, 1), 0) % SQ_ROWS
    tq_lane = past_len + lax.broadcasted_iota(jnp.int32, (1, LANES), 1) % SQ_ROWS
    t_end = lax.broadcasted_iota(jnp.int32, (1, n_cmp), 1) * CMP_STRIDE + (CMP_BLK - 1)
    ov_t = _overlap_t(nb_pad, n_cmp, nc)
    psums = []
    for g in range(NSA_KV_HEADS):
        o_c, p_c = _cmp_branch_fm(_scaled_group_queries(q_all, g), cmp_ref, g, t_end <= tq_rows)
        psums.append(_sum_heads(p_c, SQ_ROWS))
        for r in range(grp):
            h = grp * g + r
            oc_ref[:, h * hd:(h + 1) * hd] = o_c[r * SQ_ROWS:(r + 1) * SQ_ROWS]
    psum = jnp.concatenate(psums + [jnp.zeros((LANES - NSA_KV_HEADS * SQ_ROWS, n_cmp), F32)], axis=0)
    score_t = _select_scores_t(_importance_t(psum, ov_t), tq_lane)
    rowid = lax.broadcasted_iota(jnp.int32, score_t.shape, 0)
    _, picks = _topk_rows(jnp.where(rowid < nb, score_t, -jnp.inf), min(SLC_TOP_N, nb))
    idx_ref[...] = jnp.concatenate(picks, axis=0)


def _nsa_sample_select_fm(q_pad, cmp_t, past_len, seq_new):
    bsz = q_pad.shape[0]
    n_seg = cmp_t.shape[2]
    nb = -(-(past_len + seq_new) // SLC_BLK)
    nb_pad = -(-nb // SUBLANES) * SUBLANES
    n_pick = min(SLC_TOP_N, nb)
    kern = functools.partial(_nsa_sample_select_fm_kernel, past_len=past_len, nb=nb, nb_pad=nb_pad,
                             nc=n_seg - CMP_R + 1)
    return pl.pallas_call(
        kern,
        grid=(bsz,),
        in_specs=[pl.BlockSpec((None, SQ_ROWS, NSA_WIDTH), lambda b: (b, 0, 0)),
                  pl.BlockSpec((None, KV_W, n_seg), lambda b: (b, 0, 0))],
        out_specs=[pl.BlockSpec((None, SQ_ROWS, NSA_WIDTH), lambda b: (b, 0, 0)),
                   pl.BlockSpec((None, n_pick, LANES), lambda b: (b, 0, 0))],
        out_shape=[jax.ShapeDtypeStruct((bsz, SQ_ROWS, NSA_WIDTH), F32),
                   jax.ShapeDtypeStruct((bsz, n_pick, LANES), jnp.int32)],
        compiler_params=_cparams(("parallel",)),
        name="nsa_select_sample",
    )(q_pad, cmp_t)


def _nsa_sample_attend_fm_kernel(pt_ref, idx_ref, q_ref, gate_ref, oc_ref, cache_ref, tail_ref, winp_ref,
                                 winn_ref, o_ref, kvbuf, sem, *, past_len, seq_new, n_pick):
    b = pl.program_id(0)
    nbatch = pl.num_programs(0)
    hd, grp = NSA_HEAD_DIM, NSA_GROUP
    n_items = seq_new * NSA_KV_HEADS * n_pick
    n_past_blocks = past_len // SLC_BLK
    blocks_per_page = PAGE_SIZE // SLC_BLK

    def block_id(bb, item):
        return jnp.minimum(idx_ref[bb * n_items + item], n_past_blocks)

    def copy(bb, item, slot):
        g = (item // n_pick) % NSA_KV_HEADS
        lanes = slice((item % n_pick) * PAGE_SIZE, (item % n_pick + 1) * PAGE_SIZE)
        rows = slice(g * hd, (g + 1) * hd)
        past_blk = jnp.minimum(block_id(bb, item), n_past_blocks - 1)
        page = pt_ref[bb, lax.shift_right_logical(past_blk, blocks_per_page.bit_length() - 1)]
        return pltpu.make_async_copy(cache_ref.at[page, :, rows, :], kvbuf.at[slot, item // n_pick, :, :, lanes],
                                     sem.at[slot])

    def start_fetch(bb, slot):
        for item in range(n_items):
            copy(bb, item, slot).start(priority=item % 2)

    @pl.when(b == 0)
    def _():
        start_fetch(b, 0)

    @pl.when(b + 1 < nbatch)
    def _():
        start_fetch(b + 1, (b + 1) % 2)

    slot = b % 2

    for item in range(n_items):
        copy(b, item, slot).wait()

    head_row = lax.broadcasted_iota(jnp.int32, (NSA_HEADS, 1), 0)
    tok = lax.broadcasted_iota(jnp.int32, (1, PAGE_SIZE), 1)
    key_lane = lax.broadcasted_iota(jnp.int32, (1, n_pick * PAGE_SIZE), 1)
    wb, wn = winp_ref.shape[1], winn_ref.shape[1]
    tw_p = past_len - wb + lax.broadcasted_iota(jnp.int32, (1, wb), 1)
    tw_n = past_len + lax.broadcasted_iota(jnp.int32, (1, wn), 1)

    def heads16(x):
        return jnp.concatenate([x.astype(BF16), jnp.zeros_like(x, dtype=BF16)], axis=0)

    def merge_groups(per_group):
        out = per_group[0]
        for g in range(1, NSA_KV_HEADS):
            out = jnp.where(head_row >= g * grp, per_group[g], out)
        return out

    for qi in range(seq_new):
        tq = past_len + qi
        q16 = heads16(q_ref[qi] * QK_SCALE)
        gates = jax.nn.sigmoid(gate_ref[qi])
        o_s_g, o_w_g = [], []
        for g in range(NSA_KV_HEADS):
            qg_i = qi * NSA_KV_HEADS + g
            kl = g * hd
            blk = jnp.zeros_like(key_lane)
            for kk in range(n_pick):
                blk = jnp.where(key_lane // PAGE_SIZE == kk, block_id(b, qg_i * n_pick + kk), blk)
            new_picked = jnp.max(blk, axis=1, keepdims=True) == n_past_blocks
            in_block = (key_lane % PAGE_SIZE) // SLC_BLK == blk % blocks_per_page
            mask = jnp.concatenate(
                [(blk < n_past_blocks) & in_block & (blk * SLC_BLK + key_lane % SLC_BLK <= tq),
                 new_picked & (tok < SLC_BLK) & (past_len + tok <= tq)], axis=1)
            kcat = jnp.concatenate([kvbuf[slot, qg_i, 0].astype(BF16), tail_ref[0, kl:kl + hd, :].astype(BF16)], axis=1)
            vcat = jnp.concatenate([kvbuf[slot, qg_i, 1].astype(BF16), tail_ref[1, kl:kl + hd, :].astype(BF16)], axis=1)
            p = _masked_softmax(_dot(q16, kcat)[:NSA_HEADS], mask)
            o_s_g.append(_dot_nt(heads16(p), vcat)[:NSA_HEADS])
            vl = (NSA_KV_HEADS + g) * hd
            bias_p = jnp.where((tw_p <= tq) & (tw_p > tq - WINDOW) & (tw_p >= 0), 0.0, NEG)
            bias_n = jnp.where((tw_n <= tq) & (tw_n > tq - WINDOW), 0.0, NEG)
            s_p = _dot(q16, winp_ref[kl:kl + hd, :].astype(BF16))[:NSA_HEADS] + bias_p
            s_n = _dot(q16, winn_ref[kl:kl + hd, :].astype(BF16))[:NSA_HEADS] + bias_n
            m = jnp.maximum(jnp.max(s_p, axis=-1, keepdims=True), jnp.max(s_n, axis=-1, keepdims=True))
            e_p, e_n = jnp.exp(s_p - m), jnp.exp(s_n - m)
            den = jnp.maximum(jnp.sum(e_p, axis=-1, keepdims=True) + jnp.sum(e_n, axis=-1, keepdims=True), 1e-30)
            o_w_g.append((_dot_nt(heads16(e_p), winp_ref[vl:vl + hd, :].astype(BF16))
                          + _dot_nt(heads16(e_n), winn_ref[vl:vl + hd, :].astype(BF16)))[:NSA_HEADS] / den)
        o_ref[qi] = (gates[:, 0:1] * oc_ref[qi] + gates[:, 1:2] * merge_groups(o_s_g)
                     + gates[:, 2:3] * merge_groups(o_w_g))


def _nsa_sample_attend_fm(page_table, idx_flat, q_heads, gate_logits, o_c, cache_fm, tail_fm, win_past, win_new,
                          past_len, seq_new, n_pick):
    bsz = q_heads.shape[0]
    wb, wn = win_past.shape[2], win_new.shape[2]
    kern = functools.partial(_nsa_sample_attend_fm_kernel, past_len=past_len, seq_new=seq_new, n_pick=n_pick)
    per_b = lambda n, w: pl.BlockSpec((None, n, w), lambda b, pt, ix: (b, 0, 0))
    per_bq = lambda w: pl.BlockSpec((None, seq_new, NSA_HEADS, w), lambda b, pt, ix: (b, 0, 0, 0))
    n_qg = seq_new * NSA_KV_HEADS
    buf = pltpu.VMEM((2, n_qg, 2, NSA_HEAD_DIM, n_pick * PAGE_SIZE), F32)
    grid_spec = pltpu.PrefetchScalarGridSpec(
        num_scalar_prefetch=2,
        grid=(bsz,),
        in_specs=[per_bq(NSA_HEAD_DIM), per_bq(LANES), per_bq(NSA_HEAD_DIM),
                  pl.BlockSpec(memory_space=pl.ANY),
                  pl.BlockSpec((None, 2, C_W, LANES), lambda b, pt, ix: (b, 0, 0, 0)),
                  per_b(KV_W, wb), per_b(KV_W, wn)],
        out_specs=per_bq(NSA_HEAD_DIM),
        scratch_shapes=[buf, pltpu.SemaphoreType.DMA((2,))],
    )
    return pl.pallas_call(
        kern,
        grid_spec=grid_spec,
        out_shape=jax.ShapeDtypeStruct((bsz, seq_new, NSA_HEADS, NSA_HEAD_DIM), F32),
        compiler_params=_cparams(("arbitrary",)),
        name="nsa_attend_sample",
    )(page_table, idx_flat, q_heads, gate_logits, o_c, cache_fm, tail_fm, win_past, win_new)


PROMPT_TM = 512
FFN_TM = 512
GLA_TL = 512
SAMPLE_GLA_ROWS = 16
GLA_PROMPT_NBB = 8
GLA_SAMPLE_NBB = 8


def _pad_rows(x, n):
    return jnp.pad(x, ((0, 0), (0, n - x.shape[1]), (0, 0)))


def kernel(x_prompt, x_sample, state_gla, cache_cmp_kv, cache_slc_kv, cache_win_kv, page_table, c_prompt,
           c_sample, ln_in_g, ln_in_b, w_ada, b_ada, w_in, gla_w_a2, gla_b_a, gla_norm_g, cmp_pe, cmp_w1,
           cmp_b1, cmp_w2, w_o, ln1_g, ln1_b, w_ffn_in, w_ffn_out, ln2_g, ln2_b):
    assert w_in.shape[0] == DEPTH == 1
    l = 0
    bp, lp, d = x_prompt.shape
    bs, ls, _ = x_sample.shape
    n_pool = cache_cmp_kv.shape[1]
    n_pages = page_table.shape[1]
    past_len = n_pages * PAGE_SIZE
    wb = cache_win_kv.shape[2]
    assert ((past_len + ls) // CMP_STRIDE) * CMP_STRIDE <= past_len and past_len % SLC_BLK == 0
    assert ls <= SQ_ROWS and ls <= SLC_BLK and wb == WINDOW

    w_perm = _permute_w_in(w_in[l])
    w_o_b, w_fi_b, w_fo_b = w_o[l].astype(BF16), w_ffn_in[l].astype(BF16), w_ffn_out[l].astype(BF16)
    w_a2p = jnp.zeros((AUX_W, GLA_KW), F32).at[:GLA_RANK].set(gla_w_a2[l])
    b_a = gla_b_a[l].reshape(1, GLA_KW)
    norm_g = gla_norm_g[l].reshape(1, GLA_DV)
    w1c, w2c_t, pe_flat = _cmp_weights_fm(cmp_pe[l], cmp_w1[l], cmp_w2[l])
    kvt = (2, NSA_KV_HEADS, NSA_HEAD_DIM)

    def fm_view(a):
        return jnp.transpose(a, (0, 2, 3, 4, 1)).reshape(a.shape[0], KV_W, a.shape[1])

    def tok_view(a_fm):
        n, _, t = a_fm.shape
        return jnp.transpose(a_fm.reshape((n,) + kvt + (t,)), (0, 4, 1, 2, 3))[None]

    mod = _ada(jnp.concatenate([c_prompt, c_sample], axis=0), w_ada[l], b_ada[l])
    mods_p = [m[:bp, None, :] for m in jnp.split(mod, 6, axis=-1)]
    mods_s = [jnp.repeat(m[bp:], ls, axis=0)[None] for m in jnp.split(mod, 6, axis=-1)]

    def out_ffn(x2d, o_g, o_n, mods, tm, rpm):
        sh1, sc1, ga1, sh2, sc2, ga2 = mods
        return _out_ffn(x2d, o_g, o_n, ga1, sc2, sh2, ga2, ln_in_g, ln_in_b, w_o_b, ln1_g[l], ln1_b[l], w_fi_b,
                        w_fo_b, ln2_g[l], ln2_b[l], tm, rpm)

    xp2 = x_prompt.reshape(bp * lp, d)
    rpm = lp // PROMPT_TM
    qk, v, r, qn, kvc, aux, kvc_t, kvs_t, kvw_t = _inproj(
        xp2, mods_p[1], mods_p[0], ln_in_g, ln_in_b, w_perm, PROMPT_TM, rpm, seq_per_batch=lp)
    b3 = lambda a: a.reshape(bp, lp, a.shape[-1])
    o_g, s_p = _gla(b3(qk), b3(v), b3(r), b3(aux), w_a2p, b_a, norm_g, jnp.zeros((bp, GLA_KW, GLA_DV), F32),
                    chunk=GLA_CHUNK, sub=GLA_SUB, tl=GLA_TL, l_valid=lp, nbb=GLA_PROMPT_NBB)
    cmp_t_p = _cmp_prompt_fm(kvc, lp, w1c, pe_flat, cmp_w1[l], cmp_b1[l], w2c_t)
    o_n = _nsa_prompt_fm(b3(qn), b3(aux), cmp_t_p, kvs_t, kvw_t)
    y_p = out_ffn(xp2, o_g.reshape(bp * lp, GLA_WIDTH), o_n.reshape(bp * lp, NSA_WIDTH), mods_p, FFN_TM,
                  lp // FFN_TM)
    w_keep = min(WINDOW, lp)
    outs_p = (y_p.reshape(bp, lp, d), s_p.reshape(1, bp, GLA_HEADS, GLA_DK, GLA_DV),
              tok_view(kvc_t), tok_view(kvs_t), tok_view(kvw_t[:, :, lp - w_keep:]))

    ts = bs * ls
    xs2 = x_sample.reshape(ts, d)
    qk, v, r, qn, kvc, aux, kvs, kvw = _inproj(xs2, mods_s[1], mods_s[0], ln_in_g, ln_in_b, w_perm, ts, 1)
    s3 = lambda a: a.reshape(bs, ls, a.shape[-1])
    g16 = lambda a: _pad_rows(s3(a), SAMPLE_GLA_ROWS)
    o_g, s_s = _gla(g16(qk), g16(v), g16(r), g16(aux), w_a2p, b_a, norm_g, state_gla[l].reshape(bs, GLA_KW, GLA_DV),
                    chunk=SAMPLE_GLA_ROWS, sub=SAMPLE_GLA_ROWS, tl=SAMPLE_GLA_ROWS, l_valid=ls,
                    nbb=GLA_SAMPLE_NBB)
    o_g = o_g[:, :ls].reshape(ts, GLA_WIDTH)
    kvc = jnp.concatenate([kvc[0], kvc[1]], axis=1)
    cmp_t_s = _cmp_sample_fm(page_table, fm_view(cache_cmp_kv[l]), w1c, pe_flat, cmp_w1[l], cmp_b1[l], w2c_t)
    q_pad = _pad_rows(s3(qn), SQ_ROWS)
    o_c, idx = _nsa_sample_select_fm(q_pad, cmp_t_s, past_len, ls)
    n_pick = idx.shape[1]
    idx = idx[:, :, :NSA_KV_HEADS * SQ_ROWS].reshape(bs, n_pick, NSA_KV_HEADS, SQ_ROWS)[..., :ls]
    idx_flat = jnp.transpose(idx, (0, 3, 2, 1)).reshape(-1)
    new_fm =lambda a: jnp.pad(jnp.transpose(s3(a), (0, 2, 1)), ((0, 0), (0, 0), (0, LANES - ls)))
    win_past = fm_view(cache_win_kv[l])
    per_head = lambda a, w: a.reshape(bs, -1, NSA_HEADS, w)[:, :ls]
    gate_logits = jnp.pad(per_head(aux[:, GATE_OFF:GATE_OFF + N_GATES], 3), ((0, 0),) * 3 + ((0, LANES - 3),))
    o_n = _nsa_sample_attend_fm(page_table, idx_flat, per_head(qn.astype(F32), NSA_HEAD_DIM), gate_logits,
                                per_head(o_c, NSA_HEAD_DIM),
                                fm_view(cache_slc_kv[l]).reshape(n_pool, 2, C_W, PAGE_SIZE),
                                new_fm(kvs).reshape(bs, 2, C_W, LANES), win_past, new_fm(kvw),
                                past_len, ls, n_pick)
    o_n = o_n.reshape(ts, NSA_WIDTH).astype(BF16)
    y_s = out_ffn(xs2, o_g, o_n, mods_s, ts, 1)
    win_s = jnp.concatenate([win_past[:, :, ls:], jnp.transpose(s3(kvw), (0, 2, 1))], axis=2)
    outs_s = (y_s.reshape(bs, ls, d), s_s.reshape(1, bs, GLA_HEADS, GLA_DK, GLA_DV),
              kvc.reshape((1, bs, ls) + kvt), kvs.reshape((1, bs, ls) + kvt), tok_view(win_s))

    return (outs_p[0], outs_s[0], outs_p[1], outs_s[1], outs_p[2], outs_s[2], outs_p[3], outs_s[3],
            outs_p[4], outs_s[4])
```

```python
import functools
import math

import numpy as np
import jax
import jax.numpy as jnp
from jax import lax
from jax.experimental import pallas as pl
from jax.experimental.pallas import tpu as pltpu

F32 = jnp.float32
BF16 = jnp.bfloat16

D_MODEL = 1024
DEPTH = 1
PAGE_SIZE = 128
GLA_HEADS = 4
GLA_DV = D_MODEL // (2 * GLA_HEADS)
GLA_DK = GLA_DV // 2
GLA_RANK = 16
GLA_TAU = 16.0
GLA_CHUNK = 64
GLA_SUB = 32
GLA_WIDTH = GLA_HEADS * GLA_DV
GLA_KW = GLA_HEADS * GLA_DK
NSA_HEADS = 8
NSA_KV_HEADS = 2
NSA_GROUP = NSA_HEADS // NSA_KV_HEADS
NSA_HEAD_DIM = D_MODEL // (2 * NSA_HEADS)
NSA_WIDTH = NSA_HEADS * NSA_HEAD_DIM
CMP_BLK = 32
CMP_STRIDE = 16
CMP_HIDDEN = 2 * NSA_HEAD_DIM
SLC_BLK = 64
SLC_TOP_N = 16
WINDOW = 512
FORCE_BONUS = 1e4
NEG = -1e30
D_FF = -(-8 * D_MODEL // (3 * 256)) * 256
ALPHA = (2 * DEPTH) ** 0.25
KV_W = 2 * NSA_KV_HEADS * NSA_HEAD_DIM
N_GATES = 3 * NSA_HEADS
IN_SIZES = (GLA_KW, GLA_KW, GLA_WIDTH, GLA_RANK, GLA_WIDTH, NSA_WIDTH, KV_W, KV_W, KV_W, N_GATES)
IN_WIDTH = sum(IN_SIZES)
LN_EPS = 1e-5

LANES = 128
SUBLANES = 8
VMEM_LIMIT_BYTES = 56 * 1024 * 1024

AUX_W = LANES
IN_GROUPS = (2 * GLA_KW, GLA_WIDTH, GLA_WIDTH, NSA_WIDTH, KV_W, KV_W, KV_W, AUX_W)
IN_PERM_W = sum(IN_GROUPS)
GATE_OFF = GLA_RANK


def _cparams(sem):
    return pltpu.CompilerParams(dimension_semantics=sem, vmem_limit_bytes=VMEM_LIMIT_BYTES)


def _split3(a):
    hi = a.astype(BF16)
    r1 = a - hi.astype(F32)
    mid = r1.astype(BF16)
    lo = (r1 - mid.astype(F32)).astype(BF16)
    return hi, mid, lo


def _dot(a, b):
    return jnp.dot(a, b, preferred_element_type=F32)


def _dot_nt(a, b):
    return lax.dot_general(a, b, (((1,), (1,)), ((), ())), preferred_element_type=F32)


def _dot_tn(a, b):
    return lax.dot_general(a, b, (((0,), (0,)), ((), ())), preferred_element_type=F32)


def _layer_norm(x, g, b):
    mu = jnp.mean(x, axis=-1, keepdims=True)
    xc = x - mu
    var = jnp.mean(xc * xc, axis=-1, keepdims=True)
    return xc * lax.rsqrt(var + LN_EPS) * g + b


def _ada_kernel(c_ref, w_ref, b_ref, o_ref):
    c = c_ref[...]
    a = (c * jax.nn.sigmoid(c)).astype(BF16)
    o_ref[...] = _dot(a, w_ref[...].astype(BF16)) + b_ref[...]


def _ada(c, w_ada, b_ada):
    n, d = c.shape
    m = w_ada.shape[1]
    tn = D_MODEL
    return pl.pallas_call(
        _ada_kernel,
        grid=(m // tn,),
        in_specs=[pl.BlockSpec((n, d), lambda j: (0, 0)),
                  pl.BlockSpec((d, tn), lambda j: (0, j)),
                  pl.BlockSpec((1, tn), lambda j: (0, j))],
        out_specs=pl.BlockSpec((n, tn), lambda j: (0, j)),
        out_shape=jax.ShapeDtypeStruct((n, m), F32),
        compiler_params=_cparams(("parallel",)),
        name="ada_mod",
    )(c, w_ada, b_ada.reshape(1, m))


N_KV_GROUPS = 3


def _inproj_kernel(x_ref, sc_ref, sh_ref, g_ref, b_ref, w_ref, *o_refs, feature_major):
    xn = _layer_norm(x_ref[...], g_ref[...], b_ref[...])
    u = (xn * (1.0 + sc_ref[...]) + sh_ref[...]).astype(BF16)
    plain = dict(zip(("qk", "v", "r", "qn", "kvc", "aux"), o_refs[:6]))
    extra = dict(zip(("kvc", "kvs", "kvw") if feature_major else ("kvs", "kvw"), o_refs[6:]))
    lo = 0
    for name, wdt in zip(IN_GROUP_NAMES, IN_GROUPS):
        z = _dot(u, w_ref[:, lo:lo + wdt])
        lo += wdt
        if name == "kvc":
            plain[name][0] = z[:, :KV_W // 2]
            plain[name][1] = z[:, KV_W // 2:]
        elif name in plain:
            plain[name][...] = z.astype(plain[name].dtype)
        if name in extra:
            extra[name][...] = jnp.transpose(z) if feature_major else z


IN_GROUP_NAMES = ("qk", "v", "r", "qn", "kvc", "kvs", "kvw", "aux")


def _inproj(x, sc, sh, ln_g, ln_b, w_perm, tm, rows_per_mod, seq_per_batch=None):
    t, d = x.shape
    r = sc.shape[1]
    feature_major = seq_per_batch is not None
    mod_spec = pl.BlockSpec((None, r, d), lambda i: (i // rows_per_mod, 0, 0))
    tok = lambda w, dt: (pl.BlockSpec((tm, w), lambda i: (i, 0)), jax.ShapeDtypeStruct((t, w), dt))
    outs = [tok(2 * GLA_KW, BF16), tok(GLA_WIDTH, BF16), tok(GLA_WIDTH, BF16), tok(NSA_WIDTH, BF16),
            (pl.BlockSpec((2, tm, KV_W // 2), lambda i: (0, i, 0)), jax.ShapeDtypeStruct((2, t, KV_W // 2), F32)),
            tok(AUX_W, F32)]
    if feature_major:
        tpb = seq_per_batch // tm
        outs += [(pl.BlockSpec((None, KV_W, tm), lambda i: (i // tpb, 0, i % tpb)),
                  jax.ShapeDtypeStruct((t // seq_per_batch, KV_W, seq_per_batch), F32))] * N_KV_GROUPS
    else:
        outs += [tok(KV_W, F32)] * 2
    return pl.pallas_call(
        functools.partial(_inproj_kernel, feature_major=feature_major),
        grid=(t // tm,),
        in_specs=[pl.BlockSpec((tm, d), lambda i: (i, 0)), mod_spec, mod_spec,
                  pl.BlockSpec((1, d), lambda i: (0, 0)), pl.BlockSpec((1, d), lambda i: (0, 0)),
                  pl.BlockSpec((d, IN_PERM_W), lambda i: (0, 0))],
        out_specs=[o[0] for o in outs],
        out_shape=[o[1] for o in outs],
        compiler_params=_cparams(("parallel",)),
        name="ln_mod_inproj",
    )(x, sc, sh, ln_g.reshape(1, d), ln_b.reshape(1, d), w_perm)


def _permute_w_in(w_in):
    q_g, k_g, v_g, a_g, r_g, q_n, kv_c, kv_s, kv_w, g_n = jnp.split(w_in, np.cumsum(IN_SIZES)[:-1], axis=1)
    pad = jnp.zeros((w_in.shape[0], AUX_W - GLA_RANK - N_GATES), w_in.dtype)
    return jnp.concatenate([q_g, k_g, v_g, r_g, q_n, kv_c, kv_s, kv_w, a_g, g_n, pad], axis=1).astype(BF16)


FF_CHUNK = 256


def _out_ffn_kernel(x_ref, og_ref, on_ref, ga1_ref, sc2_ref, sh2_ref, ga2_ref, lng_ref, lnb_ref,
                    wo_ref, l1g_ref, l1b_ref, wfi_ref, wfo_ref, l2g_ref, l2b_ref, y_ref):
    x = _layer_norm(x_ref[...], lng_ref[...], lnb_ref[...])
    mix = _dot(og_ref[...], wo_ref[0:GLA_WIDTH, :]) + _dot(on_ref[...], wo_ref[GLA_WIDTH:, :])
    x1 = _layer_norm(ALPHA * x + ga1_ref[...] * mix, l1g_ref[...], l1b_ref[...])
    u2 = (x1 * (1.0 + sc2_ref[...]) + sh2_ref[...]).astype(BF16)
    ffn = jnp.zeros(x1.shape, F32)
    for c in range(D_FF // FF_CHUNK):
        lo = c * FF_CHUNK
        gate = _dot(u2, wfi_ref[:, lo:lo + FF_CHUNK])
        up = _dot(u2, wfi_ref[:, D_FF + lo:D_FF + lo + FF_CHUNK])
        f = (gate * jax.nn.sigmoid(gate) * up).astype(BF16)
        ffn = ffn + _dot(f, wfo_ref[lo:lo + FF_CHUNK, :])
    y_ref[...] = _layer_norm(ALPHA * x1 + ga2_ref[...] * ffn, l2g_ref[...], l2b_ref[...])


def _out_ffn(x, o_g, o_n, ga1, sc2, sh2, ga2, ln_in_g, ln_in_b, w_o, ln1_g, ln1_b, w_fi, w_fo, ln2_g, ln2_b,
             tm, rows_per_mod):
    t, d = x.shape
    r = ga1.shape[1]
    mod_spec = pl.BlockSpec((None, r, d), lambda i: (i // rows_per_mod, 0, 0))
    vec = lambda: pl.BlockSpec((1, d), lambda i: (0, 0))
    const = lambda shp: pl.BlockSpec(shp, lambda i: (0, 0), pipeline_mode=pl.Buffered(1))
    row = lambda a: a.reshape(1, d)
    return pl.pallas_call(
        _out_ffn_kernel,
        grid=(t // tm,),
        in_specs=[pl.BlockSpec((tm, d), lambda i: (i, 0)),
                  pl.BlockSpec((tm, GLA_WIDTH), lambda i: (i, 0)),
                  pl.BlockSpec((tm, NSA_WIDTH), lambda i: (i, 0)),
                  mod_spec, mod_spec, mod_spec, mod_spec, vec(), vec(),
                  const((d, d)), vec(), vec(), const((d, 2 * D_FF)), const((D_FF, d)), vec(), vec()],
        out_specs=pl.BlockSpec((tm, d), lambda i: (i, 0)),
        out_shape=jax.ShapeDtypeStruct((t, d), F32),
        compiler_params=_cparams(("parallel",)),
        name="outproj_ffn",
    )(x, o_g, o_n, ga1, sc2, sh2, ga2, row(ln_in_g), row(ln_in_b), w_o, row(ln1_g), row(ln1_b), w_fi, w_fo,
      row(ln2_g), row(ln2_b))


GLA_EXP_CLAMP = 80.0


def _gla_kernel(qk_ref, v_ref, r_ref, aux_ref, wa_ref, ba_ref, ng_ref, s0_ref, o_ref, sout_ref, s_scr,
                *, chunk, sub, tl, l_valid, l_pad, nbb):
    t = pl.program_id(1)
    c = chunk
    n_sub = c // sub
    hw = GLA_KW

    @pl.when(t == 0)
    def _():
        s_scr[...] = s0_ref[...]

    ri = lax.broadcasted_iota(jnp.int32, (c, c), 0)
    ci = lax.broadcasted_iota(jnp.int32, (c, c), 1)
    causal = ci <= ri
    tril = causal.astype(BF16)
    rowid = lax.broadcasted_iota(jnp.int32, (c, hw), 0)
    head_of_lane = lax.broadcasted_iota(jnp.int32, (c, hw), 1) // GLA_DK
    wa = wa_ref[...]
    wa_hi = wa.astype(BF16)
    wa_mid = (wa - wa_hi.astype(F32)).astype(BF16)
    ba = ba_ref[...]
    ng = ng_ref[...]

    def body(i, carry):
        for e in range(nbb):
            s_scr[e] = one_chunk(i, e, s_scr[e])
        return carry

    def one_chunk(i, e, s_old):
        r0 = pl.multiple_of(i * c, c)
        aux = aux_ref[e, pl.ds(r0, c), :]
        a_hi = aux.astype(BF16)
        a_mid = (aux - a_hi.astype(F32)).astype(BF16)
        z = _dot(a_hi, wa_hi) + _dot(a_mid, wa_hi) + _dot(a_hi, wa_mid) + ba
        g = jax.nn.log_sigmoid(z) / GLA_TAU
        if l_pad != l_valid:
            g = jnp.where(t * tl + r0 + rowid < l_valid, g, 0.0)
        g_hi, g_mid, g_lo = _split3(g)
        b = _dot(tril, g_hi) + _dot(tril, g_mid) + _dot(tril, g_lo)
        qk = qk_ref[e, pl.ds(r0, c), :]
        q = qk[:, :hw].astype(F32) * (GLA_DK ** -0.5)
        k = qk[:, hw:].astype(F32)
        v = v_ref[e, pl.ds(r0, c), :]
        b_last = b[c - 1:c, :]

        def heads_on_rows(x):
            return jnp.concatenate([jnp.where(head_of_lane == h, x, 0.0) for h in range(GLA_HEADS)], axis=0)

        o_inter =_dot(heads_on_rows(q * jnp.exp(b)).astype(BF16), s_old.astype(BF16))

        q_parts, k_parts = [], []
        for s_i in range(n_sub):
            b_ref = b[s_i * sub - 1:s_i * sub, :] if s_i > 0 else jnp.zeros((1, hw), F32)
            in_rows = (rowid >= s_i * sub) & (rowid < (s_i + 1) * sub)
            qt = jnp.where(in_rows, q * jnp.exp(jnp.minimum(b - b_ref, 0.0)), 0.0)
            kt = jnp.where(rowid < (s_i + 1) * sub, k * jnp.exp(jnp.minimum(b_ref - b, GLA_EXP_CLAMP)), 0.0)
            q_parts.append(heads_on_rows(qt).astype(BF16))
            k_parts.append(kt.astype(BF16))
        q_cat = jnp.concatenate(q_parts, axis=1) if n_sub > 1 else q_parts[0]
        k_cat = jnp.concatenate(k_parts, axis=1) if n_sub > 1 else k_parts[0]
        att = _dot_nt(q_cat, k_cat)

        r_t = r_ref[e, pl.ds(r0, c), :].astype(F32)
        for h in range(GLA_HEADS):
            att_h = jnp.where(causal, att[h * c:(h + 1) * c, :], 0.0).astype(BF16)
            o_h = o_inter[h * c:(h + 1) * c, :] + _dot(att_h, v[:, h * GLA_DV:(h + 1) * GLA_DV])
            o_h = o_h * lax.rsqrt(jnp.mean(o_h * o_h, axis=-1, keepdims=True) + 1e-6) * ng
            r_h = r_t[:, h * GLA_DV:(h + 1) * GLA_DV]
            o_ref[e, pl.ds(r0, c), h * GLA_DV:(h + 1) * GLA_DV] = (
                o_h * (r_h * jax.nn.sigmoid(r_h))).astype(o_ref.dtype)

        kd = jnp.concatenate([k * jnp.exp(b_last - b), jnp.zeros((LANES - c, hw), F32)], axis=0)
        kd_t = jnp.transpose(kd).astype(BF16)
        v_pad = jnp.concatenate([v, jnp.zeros((LANES - c, GLA_WIDTH), v.dtype)], axis=0)
        upd = jnp.concatenate([_dot(kd_t[h * GLA_DK:(h + 1) * GLA_DK, :], v_pad[:, h * GLA_DV:(h + 1) * GLA_DV])
                               for h in range(GLA_HEADS)], axis=0)
        decay = jnp.transpose(jnp.broadcast_to(jnp.exp(b_last), (LANES, hw)))
        return decay * s_old + upd

    lax.fori_loop(0, tl // c, body, 0)

    @pl.when(t == pl.num_programs(1) - 1)
    def _():
        sout_ref[...] = s_scr[...]


def _gla(qk, v, r, aux, w_a2p, b_a, norm_g, s0, *, chunk, sub, tl, l_valid, nbb):
    bsz, l_pad, _ = qk.shape
    nt = l_pad // tl
    kern = functools.partial(_gla_kernel, chunk=chunk, sub=sub, tl=tl, l_valid=l_valid, l_pad=l_pad, nbb=nbb)
    tile = lambda w: pl.BlockSpec((nbb, tl, w), lambda b, t: (b, t, 0))
    full = lambda shp: pl.BlockSpec(shp, lambda b, t: (0, 0))
    st = pl.BlockSpec((nbb, GLA_KW, GLA_DV), lambda b, t: (b, 0, 0))
    return pl.pallas_call(
        kern,
        grid=(bsz // nbb, nt),
        in_specs=[tile(2 * GLA_KW), tile(GLA_WIDTH), tile(GLA_WIDTH), tile(AUX_W),
                  full((AUX_W, GLA_KW)), full((1, GLA_KW)), full((1, GLA_DV)), st],
        out_specs=[tile(GLA_WIDTH), st],
        out_shape=[jax.ShapeDtypeStruct((bsz, l_pad, GLA_WIDTH), BF16),
                   jax.ShapeDtypeStruct((bsz, GLA_KW, GLA_DV), F32)],
        scratch_shapes=[pltpu.VMEM((nbb, GLA_KW, GLA_DV), F32)],
        compiler_params=_cparams(("parallel", "arbitrary")),
        name="gla_scan",
    )(qk, v, r, aux, w_a2p, b_a, norm_g, s0)


CMP_R = CMP_BLK // CMP_STRIDE


QK_SCALE = NSA_HEAD_DIM ** -0.5
LOG2E = math.log2(math.e)


def _masked_softmax(s, mask):
    s = jnp.where(mask, s, NEG)
    e = jnp.where(mask, jnp.exp(s - jnp.max(s, axis=-1, keepdims=True)), 0.0)
    return e / jnp.maximum(jnp.sum(e, axis=-1, keepdims=True), 1e-30)


def _group_queries(q, g):
    hd = NSA_HEAD_DIM
    return jnp.concatenate([q[:, (NSA_GROUP * g + r) * hd:(NSA_GROUP * g + r + 1) * hd]
                            for r in range(NSA_GROUP)], axis=0)


def _topk_rows(score_t, n_pick):
    nb, nq = score_t.shape
    rowid = lax.broadcasted_iota(jnp.int32, (nb, nq), 0)
    taken = jnp.zeros((nb, nq), jnp.int32)
    picks = []
    for _ in range(n_pick):
        free = taken == 0
        cand = jnp.where(free, score_t, -jnp.inf)
        m = jnp.max(cand, axis=0, keepdims=True)
        hit = free & (cand == m)
        idx = jnp.min(jnp.where(hit, rowid, nb), axis=0, keepdims=True)
        taken = jnp.where(rowid == idx, 1, taken)
        picks.append(idx)
    return taken.astype(F32), picks


def _importance_t(psum, ov_t):
    hi, mid, _ = _split3(psum)
    return _dot_nt(ov_t, hi) + _dot_nt(ov_t, mid)


def _overlap_t(nb, nc_pad, nc):
    j = lax.broadcasted_iota(jnp.int32, (nb, nc_pad), 0) * SLC_BLK
    i = lax.broadcasted_iota(jnp.int32, (nb, nc_pad), 1) * CMP_STRIDE
    return ((i < j + SLC_BLK) & (i + CMP_BLK > j) & (i < nc * CMP_STRIDE)).astype(BF16)


def _select_scores_t(imp_t, tq_row):
    nb, nq = imp_t.shape
    j = lax.broadcasted_iota(jnp.int32, (nb, nq), 0)
    cur = tq_row // SLC_BLK
    forced = (j == 0) | (j == cur) | (j == cur - 1)
    return jnp.where(j * SLC_BLK <= tq_row, imp_t + FORCE_BONUS * forced.astype(F32), -jnp.inf)


NSA_TQ = 256
NSA_TK = 512


SEGS_PER_PAGE = PAGE_SIZE // CMP_STRIDE
CMP_PAGES_PER_STEP = 64
CMP_ROW_CHUNK = 256
STAGE_PITCH = 24


SQ_ROWS = SUBLANES


C_W = NSA_KV_HEADS * NSA_HEAD_DIM
HC_W = NSA_KV_HEADS * CMP_HIDDEN
PC_W = CMP_R * HC_W
MXU_DEPTH = 256
CMP_S_PER_DOT = MXU_DEPTH // C_W


def _cmp_weights_fm(cmp_pe, cmp_w1, cmp_w2):
    eye = jnp.eye(NSA_KV_HEADS, dtype=F32)
    w1r = cmp_w1.reshape(2, CMP_R, CMP_STRIDE, NSA_HEAD_DIM, CMP_HIDDEN)
    w1c = jnp.einsum('cmsdh,gG->csgdmGh', w1r, eye).reshape(2, CMP_STRIDE * C_W, PC_W).astype(BF16)
    w2c_t = jnp.einsum('chd,gG->cGdgh', cmp_w2, eye).reshape(2, C_W, HC_W).astype(BF16)
    pe_flat = jnp.transpose(cmp_pe, (1, 0, 2)).reshape(2, CMP_BLK * NSA_HEAD_DIM)
    return w1c, w2c_t, pe_flat


def _cmp_bias_c(pe_ref, w1_ref, b1_ref, c):
    pe = jnp.broadcast_to(pe_ref[c:c + 1, :], (SUBLANES, pe_ref.shape[1]))
    w1c = w1_ref[c]
    pe_hi = pe.astype(BF16)
    pe_mid = (pe - pe_hi.astype(F32)).astype(BF16)
    w_hi = w1c.astype(BF16)
    w_mid = (w1c - w_hi.astype(F32)).astype(BF16)
    pb = _dot(pe_hi, w_hi) + _dot(pe_mid, w_hi) + _dot(pe_hi, w_mid)
    bc = pb[0:1, :] + b1_ref[c:c + 1, :]
    return jnp.concatenate([bc] * NSA_KV_HEADS, axis=1)


def _cmp_first_layer(x_ref, c, row0, n_seg, w1c_ref, pitch=CMP_STRIDE):
    acc = None
    for s in range(0, CMP_STRIDE, CMP_S_PER_DOT):
        xs = jnp.concatenate([x_ref[c, pl.ds(row0 + s + j, n_seg, stride=pitch), :].astype(BF16)
                              for j in range(CMP_S_PER_DOT)], axis=1)
        d = _dot(xs, w1c_ref[c, s * C_W:(s + CMP_S_PER_DOT) * C_W, :])
        acc = d if acc is None else acc + d
    return acc


def _cmp_second_layer_fm(p, bias, w2t):
    n = p.shape[0]
    h = p[:, :HC_W] + pltpu.roll(p[:, HC_W:], n - 1, 0) + bias
    return _dot_nt(w2t, jax.nn.gelu(h).astype(BF16))


def _cmp_prompt_fm_kernel(x_ref, w1c_ref, pe_ref, w1_ref, b1_ref, w2t_ref, o_ref):
    n_seg = o_ref.shape[1]
    for c in range(2):
        p = _cmp_first_layer(x_ref, c, 0, n_seg, w1c_ref)
        o_ref[c * C_W:(c + 1) * C_W, :] = _cmp_second_layer_fm(p, _cmp_bias_c(pe_ref, w1_ref, b1_ref, c), w2t_ref[c])


def _cmp_prompt_fm(x_tok, seq, w1c, pe_flat, cmp_w1, cmp_b1, w2c_t):
    bsz = x_tok.shape[1] // seq
    n_seg = seq // CMP_STRIDE
    const = lambda a: pl.BlockSpec(a.shape, lambda b: (0,) * a.ndim)
    return pl.pallas_call(
        _cmp_prompt_fm_kernel,
        grid=(bsz,),
        in_specs=[pl.BlockSpec((2, seq, C_W), lambda b: (0, b, 0)),
                  const(w1c), const(pe_flat), const(cmp_w1), const(cmp_b1), const(w2c_t)],
        out_specs=pl.BlockSpec((None, KV_W, n_seg), lambda b: (b, 0, 0)),
        out_shape=jax.ShapeDtypeStruct((bsz, KV_W, n_seg), F32),
        compiler_params=_cparams(("parallel",)),
        name="nsa_compress_prompt",
    )(x_tok, w1c, pe_flat, cmp_w1, cmp_b1, w2c_t)


def _scaled_group_queries(q, g):
    return (_group_queries(q, g).astype(F32) * QK_SCALE).astype(BF16)


def _cmp_branch_fm(qg, cmp_ref, g, valid):
    hd = NSA_HEAD_DIM
    kl, vl = g * hd, (NSA_KV_HEADS + g) * hd
    s_c = _dot(qg, cmp_ref[kl:kl + hd, :].astype(BF16))
    p_c = _masked_softmax(s_c, valid)
    return _dot_nt(p_c.astype(BF16), cmp_ref[vl:vl + hd, :].astype(BF16)), p_c


def _sum_heads(p, rows):
    out = p[0:rows]
    for r in range(1, NSA_GROUP):
        out = out + p[r * rows:(r + 1) * rows]
    return out


ATT_RB = 32


def _nsa_prompt_fm_kernel(q_ref, aux_ref, cmp_ref, kvs_ref, kvw_ref, o_ref,
                          s_scr, bias_scr, e_scr, m_scr, corr_scr, acc_scr, *, seq, nc):
    tq_n, tk_n, hd, grp = NSA_TQ, NSA_TK, NSA_HEAD_DIM, NSA_GROUP
    m_rows = grp * tq_n
    q0 = pl.program_id(1) * tq_n
    q_all = q_ref[...]
    gates = jax.nn.sigmoid(aux_ref[...])
    n_cmp = cmp_ref.shape[1]
    nb = seq // SLC_BLK

    tq_col = q0 + lax.broadcasted_iota(jnp.int32, (tq_n, 1), 0)
    tq_rows = jnp.concatenate([tq_col] * grp, axis=0)
    tq_lane = q0 + lax.broadcasted_iota(jnp.int32, (1, NSA_KV_HEADS * tq_n), 1) % tq_n
    t_end = lax.broadcasted_iota(jnp.int32, (1, n_cmp), 1) * CMP_STRIDE + (CMP_BLK - 1)
    ov_t = _overlap_t(nb, n_cmp, nc)
    n_chunks = (q0 + tq_n + tk_n - 1) // tk_n
    w_len = WINDOW + tq_n
    w0 = pl.multiple_of(jnp.maximum(q0 - WINDOW, 0), tq_n)

    qgs = [_scaled_group_queries(q_all, g) for g in range(NSA_KV_HEADS)]
    o_cs, p_cs = [], []
    for g in range(NSA_KV_HEADS):
        o_c, p_c = _cmp_branch_fm(qgs[g], cmp_ref, g, t_end <= tq_rows)
        o_cs.append(o_c)
        p_cs.append(p_c)
    n_pick = min(SLC_TOP_N, nb)

    def ranked_blocks():
        imps = [_importance_t(_sum_heads(p_c, tq_n), ov_t) for p_c in p_cs]
        return _topk_rows(_select_scores_t(jnp.concatenate(imps, axis=1), tq_lane), n_pick)[0]

    def all_visible_blocks():
        j = lax.broadcasted_iota(jnp.int32, (nb, NSA_KV_HEADS * tq_n), 0)
        return (j * SLC_BLK <= tq_lane).astype(F32)

    sel_all = lax.cond(q0 + tq_n <= n_pick * SLC_BLK, all_visible_blocks, ranked_blocks)

    tw = w0 + lax.broadcasted_iota(jnp.int32, (1, w_len), 1)
    bias_w = jnp.where((tw <= tq_col) & (tw > tq_col - WINDOW), 0.0, NEG)

    groups = range(NSA_KV_HEADS)

    def exp_rows(g, bias_g, width, online):
        per_head = tq_n // ATT_RB
        for i in range(m_rows // ATT_RB):
            rows = slice(i * ATT_RB, (i + 1) * ATT_RB)
            brows = slice((i % per_head) * ATT_RB, (i % per_head + 1) * ATT_RB)
            s = s_scr[g, rows, :width] + bias_scr[bias_g, brows, :width]
            m_new = jnp.max(s, axis=-1, keepdims=True)
            if online:
                m_old = m_scr[g, rows, :]
                m_new = jnp.maximum(m_old, m_new)
                corr_scr[g, rows, :] = jnp.exp2(m_old - m_new)
                m_scr[g, rows, :] = m_new
            e_scr[g, rows, :width] = jnp.exp2(s - m_new).astype(BF16)

    def keys_log2(ref, g, cols):
        return (ref[g * hd:(g + 1) * hd, cols] * LOG2E).astype(BF16)

    def gate_lane(g, r, branch):
        return hd + GATE_OFF + 3 * (grp * g + r) + branch

    def values_with_ones(ref, g, cols, branch):
        v = ref[(NSA_KV_HEADS + g) * hd:(NSA_KV_HEADS + g + 1) * hd, cols].astype(BF16)
        row = lax.broadcasted_iota(jnp.int32, v.shape, 0) + hd
        ones = functools.reduce(jnp.logical_or, [row == gate_lane(g, r, branch) for r in range(grp)])
        return jnp.concatenate([v, ones.astype(BF16)], axis=0)

    m_scr[...] = jnp.full(m_scr.shape, NEG, F32)
    acc_scr[...] = jnp.zeros(acc_scr.shape, F32)
    sel_ts = [sel_all[:, g * tq_n:(g + 1) * tq_n].astype(BF16) for g in groups]

    def slc_chunk(ci, carry):
        k0 = pl.multiple_of(ci * tk_n, tk_n)
        cols = pl.ds(k0, tk_n)
        kpos = k0 + lax.broadcasted_iota(jnp.int32, (1, tk_n), 1)
        expand = (lax.broadcasted_iota(jnp.int32, (nb, tk_n), 0) == kpos // SLC_BLK).astype(BF16)
        for g in groups:
            sel_k = _dot_tn(sel_ts[g], expand)
            bias_scr[g, :, :tk_n] = jnp.where((sel_k > 0.5) & (kpos <= tq_col), 0.0, NEG)
            s_scr[g, :, :tk_n] = _dot(qgs[g], keys_log2(kvs_ref, g, cols))
        for g in groups:
            exp_rows(g, g, tk_n, True)
        for g in groups:
            acc_scr[g] = corr_scr[g] * acc_scr[g] + _dot_nt(e_scr[g, :, :tk_n], values_with_ones(kvs_ref, g, cols, 1))
        return carry

    lax.fori_loop(0, n_chunks, slc_chunk, 0)
    wcols = pl.ds(w0, w_len)
    bias_scr[0, :, :w_len] = bias_w
    for g in groups:
        s_scr[g, :, :w_len] = _dot(qgs[g], keys_log2(kvw_ref, g, wcols))
    for g in groups:
        exp_rows(g, 0, w_len, False)
    acc_ws = [_dot_nt(e_scr[g, :, :w_len], values_with_ones(kvw_ref, g, wcols, 2)) for g in groups]
    gates_rot = pltpu.roll(gates, hd, 1)
    for g in groups:
        for r in range(grp):
            h = grp * g + r
            rows = slice(r * tq_n, (r + 1) * tq_n)
            acc_s, acc_w = acc_scr[g, rows, :], acc_ws[g][rows]
            f_s = gates_rot / jnp.maximum(acc_s, 1e-30)
            f_w = gates_rot / jnp.maximum(acc_w, 1e-30)
            ls, lw = gate_lane(g, r, 1), gate_lane(g, r, 2)
            gc = GATE_OFF + 3 * h
            o = (gates[:, gc:gc + 1] * o_cs[g][rows] + f_s[:, ls:ls + 1] * acc_s[:, :hd]
                 + f_w[:, lw:lw + 1] * acc_w[:, :hd])
            o_ref[:, h * hd:(h + 1) * hd] = o.astype(o_ref.dtype)


def _nsa_prompt_fm(q, aux, cmp_t, kvs_t, kvw_t):
    bsz, seq, _ = q.shape
    n_seg = cmp_t.shape[2]
    kern = functools.partial(_nsa_prompt_fm_kernel, seq=seq, nc=n_seg - CMP_R + 1)
    tile = lambda w: pl.BlockSpec((None, NSA_TQ, w), lambda b, t: (b, t, 0))
    whole = lambda n: pl.BlockSpec((None, KV_W, n), lambda b, t: (b, 0, 0))
    m_rows = NSA_GROUP * NSA_TQ
    width = max(NSA_TK, WINDOW + NSA_TQ)
    ng = NSA_KV_HEADS
    return pl.pallas_call(
        kern,
        grid=(bsz, seq // NSA_TQ),
        in_specs=[tile(NSA_WIDTH), tile(AUX_W), whole(n_seg), whole(seq), whole(seq)],
        out_specs=tile(NSA_WIDTH),
        out_shape=jax.ShapeDtypeStruct((bsz, seq, NSA_WIDTH), BF16),
        scratch_shapes=[pltpu.VMEM((ng, m_rows, width), F32),
                        pltpu.VMEM((ng, NSA_TQ, width), F32),
                        pltpu.VMEM((ng, m_rows, width), BF16),
                        pltpu.VMEM((ng, m_rows, 1), F32),
                        pltpu.VMEM((ng, m_rows, 1), F32),
                        pltpu.VMEM((ng, m_rows, 2 * NSA_HEAD_DIM), F32)],
        compiler_params=_cparams(("parallel", "arbitrary")),
        name="nsa_attn_prompt",
    )(q, aux, cmp_t, kvs_t, kvw_t)


def _cmp_sample_fm_kernel(pt_ref, cache_ref, w1f_ref, pe_ref, w1_ref, b1_ref, w2t_ref, o_ref,
                          xbuf, stage_a, stage_b, p_scr, bias_scr, sem, *, steps_per_batch):
    b = pl.program_id(0)
    h = pl.program_id(1)
    step = b * steps_per_batch + h
    n_steps = pl.num_programs(0) * steps_per_batch
    pps = CMP_PAGES_PER_STEP
    segs = pps * SEGS_PER_PAGE

    def page_copy(bb, hh, p, slot):
        return pltpu.make_async_copy(cache_ref.at[pt_ref[bb, hh * pps + p]], xbuf.at[slot, p], sem.at[slot])

    def start_fetch(bb, hh, slot):
        for p in range(pps):
            page_copy(bb, hh, p, slot).start(priority=p % 2)

    @pl.when(step == 0)
    def _():
        start_fetch(b, h, 0)

    @pl.when(step + 1 < n_steps)
    def _():
        wrap = h + 1 == steps_per_batch
        start_fetch(jnp.where(wrap, b + 1, b), jnp.where(wrap, 0, h + 1), (step + 1) % 2)

    slot = step % 2
    for p in range(pps):
        page_copy(b, h, p, slot).wait()

    pages_per_chunk = CMP_ROW_CHUNK // SEGS_PER_PAGE
    n_chunks = segs // CMP_ROW_CHUNK
    stages = (stage_a, stage_b)
    assert n_chunks == len(stages)

    def transpose_page(rc, lp):
        p = rc * pages_per_chunk + lp
        for c in range(2):
            tok = jnp.transpose(xbuf[slot, p, c * C_W:(c + 1) * C_W, :].astype(BF16)).astype(F32)
            for n in range(SEGS_PER_PAGE):
                r0 = (lp * SEGS_PER_PAGE + n) * STAGE_PITCH
                stages[rc][c, r0:r0 + CMP_STRIDE, :] = tok[n * CMP_STRIDE:(n + 1) * CMP_STRIDE]

    def first_layer(rc, between=()):
        between = list(between)
        dots = [(c, s) for c in range(2) for s in range(0, CMP_STRIDE, CMP_S_PER_DOT)]
        per_dot = -(-len(between) // len(dots))
        accs = [None, None]
        for c, s in dots:
            xs = jnp.concatenate([stages[rc][c, pl.ds(s + j, CMP_ROW_CHUNK, stride=STAGE_PITCH), :].astype(BF16)
                                  for j in range(CMP_S_PER_DOT)], axis=1)
            d = _dot(xs, w1f_ref[c, s * C_W:(s + CMP_S_PER_DOT) * C_W, :])
            accs[c] = d if accs[c] is None else accs[c] + d
            for thunk in between[:per_dot]:
                thunk()
            between = between[per_dot:]
        for c in range(2):
            r0 = pl.multiple_of(h * segs + rc * CMP_ROW_CHUNK, CMP_ROW_CHUNK)
            p_scr[c, pl.ds(r0, CMP_ROW_CHUNK), :] = accs[c]

    for lp in range(pages_per_chunk):
        transpose_page(0, lp)
    for rc in range(n_chunks):
        nxt = [functools.partial(transpose_page, rc + 1, lp) for lp in range(pages_per_chunk)] if rc + 1 < n_chunks else []
        first_layer(rc, nxt)

    @pl.when(step == 0)
    def _():
        for c in range(2):
            bias_scr[c] = jnp.broadcast_to(_cmp_bias_c(pe_ref, w1_ref, b1_ref, c), (SUBLANES, HC_W))

    @pl.when(h == steps_per_batch - 1)
    def _():
        for c in range(2):
            o_ref[c * C_W:(c + 1) * C_W, :] = _cmp_second_layer_fm(p_scr[c], bias_scr[c, 0:1, :], w2t_ref[c])


def _cmp_sample_fm(page_table, cache_fm, w1_full, pe_flat, cmp_w1, cmp_b1, w2_bd_t):
    bsz, n_pages = page_table.shape
    steps = n_pages // CMP_PAGES_PER_STEP
    n_seg = n_pages * SEGS_PER_PAGE
    const = lambda a: pl.BlockSpec(a.shape, lambda b, h, pt: (0,) * a.ndim, pipeline_mode=pl.Buffered(1))
    grid_spec = pltpu.PrefetchScalarGridSpec(
        num_scalar_prefetch=1,
        grid=(bsz, steps),
        in_specs=[pl.BlockSpec(memory_space=pl.ANY), const(w1_full), const(pe_flat), const(cmp_w1),
                  const(cmp_b1), const(w2_bd_t)],
        out_specs=pl.BlockSpec((None, KV_W, n_seg), lambda b, h, pt: (b, 0, 0)),
        scratch_shapes=[pltpu.VMEM((2, CMP_PAGES_PER_STEP, KV_W, PAGE_SIZE), F32),
                        pltpu.VMEM((2, CMP_ROW_CHUNK * STAGE_PITCH, C_W), F32),
                        pltpu.VMEM((2, CMP_ROW_CHUNK * STAGE_PITCH, C_W), F32),
                        pltpu.VMEM((2, n_seg, PC_W), F32),
                        pltpu.VMEM((2, SUBLANES, HC_W), F32),
                        pltpu.SemaphoreType.DMA((2,))],
    )
    return pl.pallas_call(
        functools.partial(_cmp_sample_fm_kernel, steps_per_batch=steps),
        grid_spec=grid_spec,
        out_shape=jax.ShapeDtypeStruct((bsz, KV_W, n_seg), F32),
        compiler_params=_cparams(("arbitrary", "arbitrary")),
        name="nsa_compress_sample",
    )(page_table, cache_fm, w1_full, pe_flat, cmp_w1, cmp_b1, w2_bd_t)


SEL_COLS = NSA_KV_HEADS * SQ_ROWS
SEL_NBB = LANES // SEL_COLS


def _nsa_sample_select_fm_kernel(q_ref, cmp_ref, oc_ref, idx_ref, *, past_len, nb, nb_pad, nc):
    hd, grp = NSA_HEAD_DIM, NSA_GROUP
    m_rows = grp * SQ_ROWS
    n_cmp = cmp_ref.shape[2]
    tq_rows = past_len + lax.broadcasted_iota(jnp.int32, (m_rows, 1), 0) % SQ_ROWS
    tq_lane = past_len + lax.broadcasted_iota(jnp.int32, (1, LANES), 1) % SQ_ROWS
    t_end = lax.broadcasted_iota(jnp.int32, (1, n_cmp), 1) * CMP_STRIDE + (CMP_BLK - 1)
    ov_t = _overlap_t(nb_pad, n_cmp, nc)
    psums = []
    for e in range(SEL_NBB):
        q_all = q_ref[e]
        for g in range(NSA_KV_HEADS):
            o_c, p_c = _cmp_branch_fm(_scaled_group_queries(q_all, g), cmp_ref.at[e], g, t_end <= tq_rows)
            psums.append(_sum_heads(p_c, SQ_ROWS))
            for r in range(grp):
                h = grp * g + r
                oc_ref[e, :, h * hd:(h + 1) * hd] = o_c[r * SQ_ROWS:(r + 1) * SQ_ROWS]
    score_t = _select_scores_t(_importance_t(jnp.concatenate(psums, axis=0), ov_t), tq_lane)
    rowid = lax.broadcasted_iota(jnp.int32, score_t.shape, 0)
    _, picks = _topk_rows(jnp.where(rowid < nb, score_t, -jnp.inf), min(SLC_TOP_N, nb))
    idx_ref[...] = jnp.concatenate(picks, axis=0)


def _nsa_sample_select_fm(q_pad, cmp_t, past_len, seq_new):
    bsz = q_pad.shape[0]
    n_seg = cmp_t.shape[2]
    nb = -(-(past_len + seq_new) // SLC_BLK)
    nb_pad = -(-nb // SUBLANES) * SUBLANES
    n_pick = min(SLC_TOP_N, nb)
    kern = functools.partial(_nsa_sample_select_fm_kernel, past_len=past_len, nb=nb, nb_pad=nb_pad,
                             nc=n_seg - CMP_R + 1)
    return pl.pallas_call(
        kern,
        grid=(bsz // SEL_NBB,),
        in_specs=[pl.BlockSpec((SEL_NBB, SQ_ROWS, NSA_WIDTH), lambda b: (b, 0, 0)),
                  pl.BlockSpec((SEL_NBB, KV_W, n_seg), lambda b: (b, 0, 0))],
        out_specs=[pl.BlockSpec((SEL_NBB, SQ_ROWS, NSA_WIDTH), lambda b: (b, 0, 0)),
                   pl.BlockSpec((None, n_pick, LANES), lambda b: (b, 0, 0))],
        out_shape=[jax.ShapeDtypeStruct((bsz, SQ_ROWS, NSA_WIDTH), F32),
                   jax.ShapeDtypeStruct((bsz // SEL_NBB, n_pick, LANES), jnp.int32)],
        compiler_params=_cparams(("parallel",)),
        name="nsa_select_sample",
    )(q_pad, cmp_t)


def _nsa_sample_attend_fm_kernel(pt_ref, idx_ref, q_ref, gate_ref, oc_ref, cache_ref, tail_ref, winp_ref,
                                 winn_ref, o_ref, kvbuf, sem, *, past_len, seq_new, n_pick):
    b = pl.program_id(0)
    nbatch = pl.num_programs(0)
    hd, grp = NSA_HEAD_DIM, NSA_GROUP
    n_items = seq_new * NSA_KV_HEADS * n_pick
    n_past_blocks = past_len // SLC_BLK
    blocks_per_page = PAGE_SIZE // SLC_BLK

    def block_id(bb, item):
        return jnp.minimum(idx_ref[bb * n_items + item], n_past_blocks)

    def copy(bb, item, slot):
        g = (item // n_pick) % NSA_KV_HEADS
        lanes = slice((item % n_pick) * PAGE_SIZE, (item % n_pick + 1) * PAGE_SIZE)
        rows = slice(g * hd, (g + 1) * hd)
        past_blk = jnp.minimum(block_id(bb, item), n_past_blocks - 1)
        page = pt_ref[bb, lax.shift_right_logical(past_blk, blocks_per_page.bit_length() - 1)]
        return pltpu.make_async_copy(cache_ref.at[page, :, rows, :], kvbuf.at[slot, item // n_pick, :, :, lanes],
                                     sem.at[slot])

    def start_fetch(bb, slot):
        for item in range(n_items):
            copy(bb, item, slot).start(priority=item % 2)

    @pl.when(b == 0)
    def _():
        start_fetch(b, 0)

    @pl.when(b + 1 < nbatch)
    def _():
        start_fetch(b + 1, (b + 1) % 2)

    slot = b % 2

    for item in range(n_items):
        copy(b, item, slot).wait()

    head_row = lax.broadcasted_iota(jnp.int32, (NSA_HEADS, 1), 0)
    tok = lax.broadcasted_iota(jnp.int32, (1, PAGE_SIZE), 1)
    key_lane = lax.broadcasted_iota(jnp.int32, (1, n_pick * PAGE_SIZE), 1)
    wb, wn = winp_ref.shape[1], winn_ref.shape[1]
    tw_p = past_len - wb + lax.broadcasted_iota(jnp.int32, (1, wb), 1)
    tw_n = past_len + lax.broadcasted_iota(jnp.int32, (1, wn), 1)

    def heads16(x):
        return jnp.concatenate([x.astype(BF16), jnp.zeros_like(x, dtype=BF16)], axis=0)

    def merge_groups(per_group):
        out = per_group[0]
        for g in range(1, NSA_KV_HEADS):
            out = jnp.where(head_row >= g * grp, per_group[g], out)
        return out

    for qi in range(seq_new):
        tq = past_len + qi
        q16 = heads16(q_ref[qi] * QK_SCALE)
        gates = jax.nn.sigmoid(gate_ref[qi])
        o_s_g, o_w_g = [], []
        for g in range(NSA_KV_HEADS):
            qg_i = qi * NSA_KV_HEADS + g
            kl = g * hd
            blk = jnp.zeros_like(key_lane)
            for kk in range(n_pick):
                blk = jnp.where(key_lane // PAGE_SIZE == kk, block_id(b, qg_i * n_pick + kk), blk)
            new_picked = jnp.max(blk, axis=1, keepdims=True) == n_past_blocks
            in_block = (key_lane % PAGE_SIZE) // SLC_BLK == blk % blocks_per_page
            mask = jnp.concatenate(
                [(blk < n_past_blocks) & in_block & (blk * SLC_BLK + key_lane % SLC_BLK <= tq),
                 new_picked & (tok < SLC_BLK) & (past_len + tok <= tq)], axis=1)
            kcat = jnp.concatenate([kvbuf[slot, qg_i, 0].astype(BF16), tail_ref[0, kl:kl + hd, :].astype(BF16)], axis=1)
            vcat = jnp.concatenate([kvbuf[slot, qg_i, 1].astype(BF16), tail_ref[1, kl:kl + hd, :].astype(BF16)], axis=1)
            p = _masked_softmax(_dot(q16, kcat)[:NSA_HEADS], mask)
            o_s_g.append(_dot_nt(heads16(p), vcat)[:NSA_HEADS])
            vl = (NSA_KV_HEADS + g) * hd
            bias_p = jnp.where((tw_p <= tq) & (tw_p > tq - WINDOW) & (tw_p >= 0), 0.0, NEG)
            bias_n = jnp.where((tw_n <= tq) & (tw_n > tq - WINDOW), 0.0, NEG)
            s_p = _dot(q16, winp_ref[kl:kl + hd, :].astype(BF16))[:NSA_HEADS] + bias_p
            s_n = _dot(q16, winn_ref[kl:kl + hd, :].astype(BF16))[:NSA_HEADS] + bias_n
            m = jnp.maximum(jnp.max(s_p, axis=-1, keepdims=True), jnp.max(s_n, axis=-1, keepdims=True))
            e_p, e_n = jnp.exp(s_p - m), jnp.exp(s_n - m)
            den = jnp.maximum(jnp.sum(e_p, axis=-1, keepdims=True) + jnp.sum(e_n, axis=-1, keepdims=True), 1e-30)
            o_w_g.append((_dot_nt(heads16(e_p), winp_ref[vl:vl + hd, :].astype(BF16))
                          + _dot_nt(heads16(e_n), winn_ref[vl:vl + hd, :].astype(BF16)))[:NSA_HEADS] / den)
        o_ref[qi] = (gates[:, 0:1] * oc_ref[qi] + gates[:, 1:2] * merge_groups(o_s_g)
                     + gates[:, 2:3] * merge_groups(o_w_g))


def _nsa_sample_attend_fm(page_table, idx_flat, q_heads, gate_logits, o_c, cache_fm, tail_fm, win_past, win_new,
                          past_len, seq_new, n_pick):
    bsz = q_heads.shape[0]
    wb, wn = win_past.shape[2], win_new.shape[2]
    kern = functools.partial(_nsa_sample_attend_fm_kernel, past_len=past_len, seq_new=seq_new, n_pick=n_pick)
    per_b = lambda n, w: pl.BlockSpec((None, n, w), lambda b, pt, ix: (b, 0, 0))
    per_bq = lambda w: pl.BlockSpec((None, seq_new, NSA_HEADS, w), lambda b, pt, ix: (b, 0, 0, 0))
    n_qg = seq_new * NSA_KV_HEADS
    buf = pltpu.VMEM((2, n_qg, 2, NSA_HEAD_DIM, n_pick * PAGE_SIZE), F32)
    grid_spec = pltpu.PrefetchScalarGridSpec(
        num_scalar_prefetch=2,
        grid=(bsz,),
        in_specs=[per_bq(NSA_HEAD_DIM), per_bq(LANES), per_bq(NSA_HEAD_DIM),
                  pl.BlockSpec(memory_space=pl.ANY),
                  pl.BlockSpec((None, 2, C_W, LANES), lambda b, pt, ix: (b, 0, 0, 0)),
                  per_b(KV_W, wb), per_b(KV_W, wn)],
        out_specs=per_bq(NSA_HEAD_DIM),
        scratch_shapes=[buf, pltpu.SemaphoreType.DMA((2,))],
    )
    return pl.pallas_call(
        kern,
        grid_spec=grid_spec,
        out_shape=jax.ShapeDtypeStruct((bsz, seq_new, NSA_HEADS, NSA_HEAD_DIM), F32),
        compiler_params=_cparams(("arbitrary",)),
        name="nsa_attend_sample",
    )(page_table, idx_flat, q_heads, gate_logits, o_c, cache_fm, tail_fm, win_past, win_new)


PROMPT_TM = 512
FFN_TM = 512
GLA_TL = 512
SAMPLE_GLA_ROWS = 16
GLA_PROMPT_NBB = 8
GLA_SAMPLE_NBB = 8


def _pad_rows(x, n):
    return jnp.pad(x, ((0, 0), (0, n - x.shape[1]), (0, 0)))


def kernel(x_prompt, x_sample, state_gla, cache_cmp_kv, cache_slc_kv, cache_win_kv, page_table, c_prompt,
           c_sample, ln_in_g, ln_in_b, w_ada, b_ada, w_in, gla_w_a2, gla_b_a, gla_norm_g, cmp_pe, cmp_w1,
           cmp_b1, cmp_w2, w_o, ln1_g, ln1_b, w_ffn_in, w_ffn_out, ln2_g, ln2_b):
    assert w_in.shape[0] == DEPTH == 1
    l = 0
    bp, lp, d = x_prompt.shape
    bs, ls, _ = x_sample.shape
    n_pool = cache_cmp_kv.shape[1]
    n_pages = page_table.shape[1]
    past_len = n_pages * PAGE_SIZE
    wb = cache_win_kv.shape[2]
    assert ((past_len + ls) // CMP_STRIDE) * CMP_STRIDE <= past_len and past_len % SLC_BLK == 0
    assert ls <= SQ_ROWS and ls <= SLC_BLK and wb == WINDOW and bs % SEL_NBB == 0

    w_perm = _permute_w_in(w_in[l])
    w_o_b, w_fi_b, w_fo_b = w_o[l].astype(BF16), w_ffn_in[l].astype(BF16), w_ffn_out[l].astype(BF16)
    w_a2p = jnp.zeros((AUX_W, GLA_KW), F32).at[:GLA_RANK].set(gla_w_a2[l])
    b_a = gla_b_a[l].reshape(1, GLA_KW)
    norm_g = gla_norm_g[l].reshape(1, GLA_DV)
    w1c, w2c_t, pe_flat = _cmp_weights_fm(cmp_pe[l], cmp_w1[l], cmp_w2[l])
    kvt = (2, NSA_KV_HEADS, NSA_HEAD_DIM)

    def fm_view(a):
        return jnp.transpose(a, (0, 2, 3, 4, 1)).reshape(a.shape[0], KV_W, a.shape[1])

    def tok_view(a_fm):
        n, _, t = a_fm.shape
        return jnp.transpose(a_fm.reshape((n,) + kvt + (t,)), (0, 4, 1, 2, 3))[None]

    mod = _ada(jnp.concatenate([c_prompt, c_sample], axis=0), w_ada[l], b_ada[l])
    mods_p = [m[:bp, None, :] for m in jnp.split(mod, 6, axis=-1)]
    mods_s = [jnp.repeat(m[bp:], ls, axis=0)[None] for m in jnp.split(mod, 6, axis=-1)]

    def out_ffn(x2d, o_g, o_n, mods, tm, rpm):
        sh1, sc1, ga1, sh2, sc2, ga2 = mods
        return _out_ffn(x2d, o_g, o_n, ga1, sc2, sh2, ga2, ln_in_g, ln_in_b, w_o_b, ln1_g[l], ln1_b[l], w_fi_b,
                        w_fo_b, ln2_g[l], ln2_b[l], tm, rpm)

    xp2 = x_prompt.reshape(bp * lp, d)
    rpm = lp // PROMPT_TM
    qk, v, r, qn, kvc, aux, kvc_t, kvs_t, kvw_t = _inproj(
        xp2, mods_p[1], mods_p[0], ln_in_g, ln_in_b, w_perm, PROMPT_TM, rpm, seq_per_batch=lp)
    b3 = lambda a: a.reshape(bp, lp, a.shape[-1])
    o_g, s_p = _gla(b3(qk), b3(v), b3(r), b3(aux), w_a2p, b_a, norm_g, jnp.zeros((bp, GLA_KW, GLA_DV), F32),
                    chunk=GLA_CHUNK, sub=GLA_SUB, tl=GLA_TL, l_valid=lp, nbb=GLA_PROMPT_NBB)
    cmp_t_p = _cmp_prompt_fm(kvc, lp, w1c, pe_flat, cmp_w1[l], cmp_b1[l], w2c_t)
    o_n = _nsa_prompt_fm(b3(qn), b3(aux), cmp_t_p, kvs_t, kvw_t)
    y_p = out_ffn(xp2, o_g.reshape(bp * lp, GLA_WIDTH), o_n.reshape(bp * lp, NSA_WIDTH), mods_p, FFN_TM,
                  lp // FFN_TM)
    w_keep = min(WINDOW, lp)
    outs_p = (y_p.reshape(bp, lp, d), s_p.reshape(1, bp, GLA_HEADS, GLA_DK, GLA_DV),
              tok_view(kvc_t), tok_view(kvs_t), tok_view(kvw_t[:, :, lp - w_keep:]))

    ts = bs * ls
    xs2 = x_sample.reshape(ts, d)
    qk, v, r, qn, kvc, aux, kvs, kvw = _inproj(xs2, mods_s[1], mods_s[0], ln_in_g, ln_in_b, w_perm, ts, 1)
    s3 = lambda a: a.reshape(bs, ls, a.shape[-1])
    g16 = lambda a: _pad_rows(s3(a), SAMPLE_GLA_ROWS)
    o_g, s_s = _gla(g16(qk), g16(v), g16(r), g16(aux), w_a2p, b_a, norm_g, state_gla[l].reshape(bs, GLA_KW, GLA_DV),
                    chunk=SAMPLE_GLA_ROWS, sub=SAMPLE_GLA_ROWS, tl=SAMPLE_GLA_ROWS, l_valid=ls,
                    nbb=GLA_SAMPLE_NBB)
    o_g = o_g[:, :ls].reshape(ts, GLA_WIDTH)
    kvc = jnp.concatenate([kvc[0], kvc[1]], axis=1)
    cmp_t_s = _cmp_sample_fm(page_table, fm_view(cache_cmp_kv[l]), w1c, pe_flat, cmp_w1[l], cmp_b1[l], w2c_t)
    q_pad = _pad_rows(s3(qn), SQ_ROWS)
    o_c, idx = _nsa_sample_select_fm(q_pad, cmp_t_s, past_len, ls)
    n_pick = idx.shape[1]
    idx = idx.reshape(bs // SEL_NBB, n_pick, SEL_NBB, NSA_KV_HEADS, SQ_ROWS)[..., :ls]
    idx_flat = jnp.transpose(idx, (0, 2, 4, 3, 1)).reshape(-1)
    new_fm =lambda a: jnp.pad(jnp.transpose(s3(a), (0, 2, 1)), ((0, 0), (0, 0), (0, LANES - ls)))
    win_past = fm_view(cache_win_kv[l])
    per_head = lambda a, w: a.reshape(bs, -1, NSA_HEADS, w)[:, :ls]
    gate_logits = jnp.pad(per_head(aux[:, GATE_OFF:GATE_OFF + N_GATES], 3), ((0, 0),) * 3 + ((0, LANES - 3),))
    o_n = _nsa_sample_attend_fm(page_table, idx_flat, per_head(qn.astype(F32), NSA_HEAD_DIM), gate_logits,
                                per_head(o_c, NSA_HEAD_DIM),
                                fm_view(cache_slc_kv[l]).reshape(n_pool, 2, C_W, PAGE_SIZE),
                                new_fm(kvs).reshape(bs, 2, C_W, LANES), win_past, new_fm(kvw),
                                past_len, ls, n_pick)
    o_n = o_n.reshape(ts, NSA_WIDTH).astype(BF16)
    y_s = out_ffn(xs2, o_g, o_n, mods_s, ts, 1)
    win_s = jnp.concatenate([win_past[:, :, ls:], jnp.transpose(s3(kvw), (0, 2, 1))], axis=2)
    outs_s = (y_s.reshape(bs, ls, d), s_s.reshape(1, bs, GLA_HEADS, GLA_DK, GLA_DV),
              kvc.reshape((1, bs, ls) + kvt), kvs.reshape((1, bs, ls) + kvt), tok_view(win_s))

    return (outs_p[0], outs_s[0], outs_p[1], outs_s[1], outs_p[2], outs_s[2], outs_p[3], outs_s[3],
            outs_p[4], outs_s[4])
```

```python
import functools
import math

import numpy as np
import jax
import jax.numpy as jnp
from jax import lax
from jax.experimental import pallas as pl
from jax.experimental.pallas import tpu as pltpu

F32 = jnp.float32
BF16 = jnp.bfloat16

D_MODEL = 1024
DEPTH = 1
PAGE_SIZE = 128
GLA_HEADS = 4
GLA_DV = D_MODEL // (2 * GLA_HEADS)
GLA_DK = GLA_DV // 2
GLA_RANK = 16
GLA_TAU = 16.0
GLA_CHUNK = 64
GLA_SUB = 32
GLA_WIDTH = GLA_HEADS * GLA_DV
GLA_KW = GLA_HEADS * GLA_DK
NSA_HEADS = 8
NSA_KV_HEADS = 2
NSA_GROUP = NSA_HEADS // NSA_KV_HEADS
NSA_HEAD_DIM = D_MODEL // (2 * NSA_HEADS)
NSA_WIDTH = NSA_HEADS * NSA_HEAD_DIM
CMP_BLK = 32
CMP_STRIDE = 16
CMP_HIDDEN = 2 * NSA_HEAD_DIM
SLC_BLK = 64
SLC_TOP_N = 16
WINDOW = 512
FORCE_BONUS = 1e4
NEG = -1e30
D_FF = -(-8 * D_MODEL // (3 * 256)) * 256
ALPHA = (2 * DEPTH) ** 0.25
KV_W = 2 * NSA_KV_HEADS * NSA_HEAD_DIM
N_GATES = 3 * NSA_HEADS
IN_SIZES = (GLA_KW, GLA_KW, GLA_WIDTH, GLA_RANK, GLA_WIDTH, NSA_WIDTH, KV_W, KV_W, KV_W, N_GATES)
IN_WIDTH = sum(IN_SIZES)
LN_EPS = 1e-5

LANES = 128
SUBLANES = 8
VMEM_LIMIT_BYTES = 56 * 1024 * 1024

AUX_W = LANES
IN_GROUPS = (2 * GLA_KW, GLA_WIDTH, GLA_WIDTH, NSA_WIDTH, KV_W, KV_W, KV_W, AUX_W)
IN_PERM_W = sum(IN_GROUPS)
GATE_OFF = GLA_RANK


def _cparams(sem):
    return pltpu.CompilerParams(dimension_semantics=sem, vmem_limit_bytes=VMEM_LIMIT_BYTES)


def _split3(a):
    hi = a.astype(BF16)
    r1 = a - hi.astype(F32)
    mid = r1.astype(BF16)
    lo = (r1 - mid.astype(F32)).astype(BF16)
    return hi, mid, lo


def _dot(a, b):
    return jnp.dot(a, b, preferred_element_type=F32)


def _dot_nt(a, b):
    return lax.dot_general(a, b, (((1,), (1,)), ((), ())), preferred_element_type=F32)


def _dot_tn(a, b):
    return lax.dot_general(a, b, (((0,), (0,)), ((), ())), preferred_element_type=F32)


def _layer_norm(x, g, b):
    mu = jnp.mean(x, axis=-1, keepdims=True)
    xc = x - mu
    var = jnp.mean(xc * xc, axis=-1, keepdims=True)
    return xc * lax.rsqrt(var + LN_EPS) * g + b


def _ada_kernel(c_ref, w_ref, b_ref, o_ref):
    c = c_ref[...]
    a = (c * jax.nn.sigmoid(c)).astype(BF16)
    o_ref[...] = _dot(a, w_ref[...].astype(BF16)) + b_ref[...]


def _ada(c, w_ada, b_ada):
    n, d = c.shape
    m = w_ada.shape[1]
    tn = D_MODEL
    return pl.pallas_call(
        _ada_kernel,
        grid=(m // tn,),
        in_specs=[pl.BlockSpec((n, d), lambda j: (0, 0)),
                  pl.BlockSpec((d, tn), lambda j: (0, j)),
                  pl.BlockSpec((1, tn), lambda j: (0, j))],
        out_specs=pl.BlockSpec((n, tn), lambda j: (0, j)),
        out_shape=jax.ShapeDtypeStruct((n, m), F32),
        compiler_params=_cparams(("parallel",)),
        name="ada_mod",
    )(c, w_ada, b_ada.reshape(1, m))


N_KV_GROUPS = 3


def _inproj_kernel(x_ref, sc_ref, sh_ref, g_ref, b_ref, w_ref, *o_refs, feature_major):
    xn = _layer_norm(x_ref[...], g_ref[...], b_ref[...])
    u = (xn * (1.0 + sc_ref[...]) + sh_ref[...]).astype(BF16)
    plain = dict(zip(("qk", "v", "r", "qn", "kvc", "aux"), o_refs[:6]))
    extra = dict(zip(("kvc", "kvs", "kvw") if feature_major else ("kvs", "kvw"), o_refs[6:]))
    lo = 0
    for name, wdt in zip(IN_GROUP_NAMES, IN_GROUPS):
        z = _dot(u, w_ref[:, lo:lo + wdt])
        lo += wdt
        if name == "kvc":
            plain[name][0] = z[:, :KV_W // 2]
            plain[name][1] = z[:, KV_W // 2:]
        elif name in plain:
            plain[name][...] = z.astype(plain[name].dtype)
        if name in extra:
            extra[name][...] = jnp.transpose(z) if feature_major else z


IN_GROUP_NAMES = ("qk", "v", "r", "qn", "kvc", "kvs", "kvw", "aux")


def _inproj(x, sc, sh, ln_g, ln_b, w_perm, tm, rows_per_mod, seq_per_batch=None):
    t, d = x.shape
    r = sc.shape[1]
    feature_major = seq_per_batch is not None
    mod_spec = pl.BlockSpec((None, r, d), lambda i: (i // rows_per_mod, 0, 0))
    tok = lambda w, dt: (pl.BlockSpec((tm, w), lambda i: (i, 0)), jax.ShapeDtypeStruct((t, w), dt))
    outs = [tok(2 * GLA_KW, BF16), tok(GLA_WIDTH, BF16), tok(GLA_WIDTH, BF16), tok(NSA_WIDTH, BF16),
            (pl.BlockSpec((2, tm, KV_W // 2), lambda i: (0, i, 0)), jax.ShapeDtypeStruct((2, t, KV_W // 2), F32)),
            tok(AUX_W, F32)]
    if feature_major:
        tpb = seq_per_batch // tm
        outs += [(pl.BlockSpec((None, KV_W, tm), lambda i: (i // tpb, 0, i % tpb)),
                  jax.ShapeDtypeStruct((t // seq_per_batch, KV_W, seq_per_batch), F32))] * N_KV_GROUPS
    else:
        outs += [tok(KV_W, F32)] * 2
    return pl.pallas_call(
        functools.partial(_inproj_kernel, feature_major=feature_major),
        grid=(t // tm,),
        in_specs=[pl.BlockSpec((tm, d), lambda i: (i, 0)), mod_spec, mod_spec,
                  pl.BlockSpec((1, d), lambda i: (0, 0)), pl.BlockSpec((1, d), lambda i: (0, 0)),
                  pl.BlockSpec((d, IN_PERM_W), lambda i: (0, 0))],
        out_specs=[o[0] for o in outs],
        out_shape=[o[1] for o in outs],
        compiler_params=_cparams(("parallel",)),
        name="ln_mod_inproj",
    )(x, sc, sh, ln_g.reshape(1, d), ln_b.reshape(1, d), w_perm)


def _permute_w_in(w_in):
    q_g, k_g, v_g, a_g, r_g, q_n, kv_c, kv_s, kv_w, g_n = jnp.split(w_in, np.cumsum(IN_SIZES)[:-1], axis=1)
    pad = jnp.zeros((w_in.shape[0], AUX_W - GLA_RANK - N_GATES), w_in.dtype)
    return jnp.concatenate([q_g, k_g, v_g, r_g, q_n, kv_c, kv_s, kv_w, a_g, g_n, pad], axis=1).astype(BF16)


FF_CHUNK = 256


def _out_ffn_kernel(x_ref, og_ref, on_ref, ga1_ref, sc2_ref, sh2_ref, ga2_ref, lng_ref, lnb_ref,
                    wo_ref, l1g_ref, l1b_ref, wfi_ref, wfo_ref, l2g_ref, l2b_ref, y_ref):
    x = _layer_norm(x_ref[...], lng_ref[...], lnb_ref[...])
    mix = _dot(og_ref[...], wo_ref[0:GLA_WIDTH, :]) + _dot(on_ref[...], wo_ref[GLA_WIDTH:, :])
    x1 = _layer_norm(ALPHA * x + ga1_ref[...] * mix, l1g_ref[...], l1b_ref[...])
    u2 = (x1 * (1.0 + sc2_ref[...]) + sh2_ref[...]).astype(BF16)
    ffn = jnp.zeros(x1.shape, F32)
    for c in range(D_FF // FF_CHUNK):
        lo = c * FF_CHUNK
        gate = _dot(u2, wfi_ref[:, lo:lo + FF_CHUNK])
        up = _dot(u2, wfi_ref[:, D_FF + lo:D_FF + lo + FF_CHUNK])
        f = (gate * jax.nn.sigmoid(gate) * up).astype(BF16)
        ffn = ffn + _dot(f, wfo_ref[lo:lo + FF_CHUNK, :])
    y_ref[...] = _layer_norm(ALPHA * x1 + ga2_ref[...] * ffn, l2g_ref[...], l2b_ref[...])


def _out_ffn(x, o_g, o_n, ga1, sc2, sh2, ga2, ln_in_g, ln_in_b, w_o, ln1_g, ln1_b, w_fi, w_fo, ln2_g, ln2_b,
             tm, rows_per_mod):
    t, d = x.shape
    r = ga1.shape[1]
    mod_spec = pl.BlockSpec((None, r, d), lambda i: (i // rows_per_mod, 0, 0))
    vec = lambda: pl.BlockSpec((1, d), lambda i: (0, 0))
    const = lambda shp: pl.BlockSpec(shp, lambda i: (0, 0), pipeline_mode=pl.Buffered(1))
    row = lambda a: a.reshape(1, d)
    return pl.pallas_call(
        _out_ffn_kernel,
        grid=(t // tm,),
        in_specs=[pl.BlockSpec((tm, d), lambda i: (i, 0)),
                  pl.BlockSpec((tm, GLA_WIDTH), lambda i: (i, 0)),
                  pl.BlockSpec((tm, NSA_WIDTH), lambda i: (i, 0)),
                  mod_spec, mod_spec, mod_spec, mod_spec, vec(), vec(),
                  const((d, d)), vec(), vec(), const((d, 2 * D_FF)), const((D_FF, d)), vec(), vec()],
        out_specs=pl.BlockSpec((tm, d), lambda i: (i, 0)),
        out_shape=jax.ShapeDtypeStruct((t, d), F32),
        compiler_params=_cparams(("parallel",)),
        name="outproj_ffn",
    )(x, o_g, o_n, ga1, sc2, sh2, ga2, row(ln_in_g), row(ln_in_b), w_o, row(ln1_g), row(ln1_b), w_fi, w_fo,
      row(ln2_g), row(ln2_b))


GLA_EXP_CLAMP = 80.0


def _gla_kernel(qk_ref, v_ref, r_ref, aux_ref, wa_ref, ba_ref, ng_ref, s0_ref, o_ref, sout_ref, s_scr,
                *, chunk, sub, tl, l_valid, l_pad, nbb):
    t = pl.program_id(1)
    c = chunk
    n_sub = c // sub
    hw = GLA_KW

    @pl.when(t == 0)
    def _():
        s_scr[...] = s0_ref[...]

    ri = lax.broadcasted_iota(jnp.int32, (c, c), 0)
    ci = lax.broadcasted_iota(jnp.int32, (c, c), 1)
    causal = ci <= ri
    tril = causal.astype(BF16)
    rowid = lax.broadcasted_iota(jnp.int32, (c, hw), 0)
    head_of_lane = lax.broadcasted_iota(jnp.int32, (c, hw), 1) // GLA_DK
    wa = wa_ref[...]
    wa_hi = wa.astype(BF16)
    wa_mid = (wa - wa_hi.astype(F32)).astype(BF16)
    ba = ba_ref[...]
    ng = ng_ref[...]

    def body(i, carry):
        for e in range(nbb):
            s_scr[e] = one_chunk(i, e, s_scr[e])
        return carry

    def one_chunk(i, e, s_old):
        r0 = pl.multiple_of(i * c, c)
        aux = aux_ref[e, pl.ds(r0, c), :]
        a_hi = aux.astype(BF16)
        a_mid = (aux - a_hi.astype(F32)).astype(BF16)
        z = _dot(a_hi, wa_hi) + _dot(a_mid, wa_hi) + _dot(a_hi, wa_mid) + ba
        g = jax.nn.log_sigmoid(z) / GLA_TAU
        if l_pad != l_valid:
            g = jnp.where(t * tl + r0 + rowid < l_valid, g, 0.0)
        g_hi, g_mid, g_lo = _split3(g)
        b = _dot(tril, g_hi) + _dot(tril, g_mid) + _dot(tril, g_lo)
        qk = qk_ref[e, pl.ds(r0, c), :]
        q = qk[:, :hw].astype(F32) * (GLA_DK ** -0.5)
        k = qk[:, hw:].astype(F32)
        v = v_ref[e, pl.ds(r0, c), :]
        b_last = b[c - 1:c, :]

        def heads_on_rows(x):
            return jnp.concatenate([jnp.where(head_of_lane == h, x, 0.0) for h in range(GLA_HEADS)], axis=0)

        o_inter =_dot(heads_on_rows(q * jnp.exp(b)).astype(BF16), s_old.astype(BF16))

        q_parts, k_parts = [], []
        for s_i in range(n_sub):
            b_ref = b[s_i * sub - 1:s_i * sub, :] if s_i > 0 else jnp.zeros((1, hw), F32)
            in_rows = (rowid >= s_i * sub) & (rowid < (s_i + 1) * sub)
            qt = jnp.where(in_rows, q * jnp.exp(jnp.minimum(b - b_ref, 0.0)), 0.0)
            kt = jnp.where(rowid < (s_i + 1) * sub, k * jnp.exp(jnp.minimum(b_ref - b, GLA_EXP_CLAMP)), 0.0)
            q_parts.append(heads_on_rows(qt).astype(BF16))
            k_parts.append(kt.astype(BF16))
        q_cat = jnp.concatenate(q_parts, axis=1) if n_sub > 1 else q_parts[0]
        k_cat = jnp.concatenate(k_parts, axis=1) if n_sub > 1 else k_parts[0]
        att = _dot_nt(q_cat, k_cat)

        r_t = r_ref[e, pl.ds(r0, c), :].astype(F32)
        for h in range(GLA_HEADS):
            att_h = jnp.where(causal, att[h * c:(h + 1) * c, :], 0.0).astype(BF16)
            o_h = o_inter[h * c:(h + 1) * c, :] + _dot(att_h, v[:, h * GLA_DV:(h + 1) * GLA_DV])
            o_h = o_h * lax.rsqrt(jnp.mean(o_h * o_h, axis=-1, keepdims=True) + 1e-6) * ng
            r_h = r_t[:, h * GLA_DV:(h + 1) * GLA_DV]
            o_ref[e, pl.ds(r0, c), h * GLA_DV:(h + 1) * GLA_DV] = (
                o_h * (r_h * jax.nn.sigmoid(r_h))).astype(o_ref.dtype)

        kd = jnp.concatenate([k * jnp.exp(b_last - b), jnp.zeros((LANES - c, hw), F32)], axis=0)
        kd_t = jnp.transpose(kd).astype(BF16)
        v_pad = jnp.concatenate([v, jnp.zeros((LANES - c, GLA_WIDTH), v.dtype)], axis=0)
        upd = jnp.concatenate([_dot(kd_t[h * GLA_DK:(h + 1) * GLA_DK, :], v_pad[:, h * GLA_DV:(h + 1) * GLA_DV])
                               for h in range(GLA_HEADS)], axis=0)
        decay = jnp.transpose(jnp.broadcast_to(jnp.exp(b_last), (LANES, hw)))
        return decay * s_old + upd

    lax.fori_loop(0, tl // c, body, 0)

    @pl.when(t == pl.num_programs(1) - 1)
    def _():
        sout_ref[...] = s_scr[...]


def _gla(qk, v, r, aux, w_a2p, b_a, norm_g, s0, *, chunk, sub, tl, l_valid, nbb):
    bsz, l_pad, _ = qk.shape
    nt = l_pad // tl
    kern = functools.partial(_gla_kernel, chunk=chunk, sub=sub, tl=tl, l_valid=l_valid, l_pad=l_pad, nbb=nbb)
    tile = lambda w: pl.BlockSpec((nbb, tl, w), lambda b, t: (b, t, 0))
    full = lambda shp: pl.BlockSpec(shp, lambda b, t: (0, 0))
    st = pl.BlockSpec((nbb, GLA_KW, GLA_DV), lambda b, t: (b, 0, 0))
    return pl.pallas_call(
        kern,
        grid=(bsz // nbb, nt),
        in_specs=[tile(2 * GLA_KW), tile(GLA_WIDTH), tile(GLA_WIDTH), tile(AUX_W),
                  full((AUX_W, GLA_KW)), full((1, GLA_KW)), full((1, GLA_DV)), st],
        out_specs=[tile(GLA_WIDTH), st],
        out_shape=[jax.ShapeDtypeStruct((bsz, l_pad, GLA_WIDTH), BF16),
                   jax.ShapeDtypeStruct((bsz, GLA_KW, GLA_DV), F32)],
        scratch_shapes=[pltpu.VMEM((nbb, GLA_KW, GLA_DV), F32)],
        compiler_params=_cparams(("parallel", "arbitrary")),
        name="gla_scan",
    )(qk, v, r, aux, w_a2p, b_a, norm_g, s0)


CMP_R = CMP_BLK // CMP_STRIDE


QK_SCALE = NSA_HEAD_DIM ** -0.5
LOG2E = math.log2(math.e)


def _masked_softmax(s, mask):
    s = jnp.where(mask, s, NEG)
    e = jnp.where(mask, jnp.exp(s - jnp.max(s, axis=-1, keepdims=True)), 0.0)
    return e / jnp.maximum(jnp.sum(e, axis=-1, keepdims=True), 1e-30)


def _group_queries(q, g):
    hd = NSA_HEAD_DIM
    return jnp.concatenate([q[:, (NSA_GROUP * g + r) * hd:(NSA_GROUP * g + r + 1) * hd]
                            for r in range(NSA_GROUP)], axis=0)


def _topk_rows(score_t, n_pick):
    nb, nq = score_t.shape
    rowid = lax.broadcasted_iota(jnp.int32, (nb, nq), 0)
    taken = jnp.zeros((nb, nq), jnp.int32)
    picks = []
    for _ in range(n_pick):
        free = taken == 0
        cand = jnp.where(free, score_t, -jnp.inf)
        m = jnp.max(cand, axis=0, keepdims=True)
        hit = free & (cand == m)
        idx = jnp.min(jnp.where(hit, rowid, nb), axis=0, keepdims=True)
        taken = jnp.where(rowid == idx, 1, taken)
        picks.append(idx)
    return taken.astype(F32), picks


def _importance_t(psum, ov_t):
    hi, mid, _ = _split3(psum)
    return _dot_nt(ov_t, hi) + _dot_nt(ov_t, mid)


def _overlap_t(nb, nc_pad, nc):
    j = lax.broadcasted_iota(jnp.int32, (nb, nc_pad), 0) * SLC_BLK
    i = lax.broadcasted_iota(jnp.int32, (nb, nc_pad), 1) * CMP_STRIDE
    return ((i < j + SLC_BLK) & (i + CMP_BLK > j) & (i < nc * CMP_STRIDE)).astype(BF16)


def _select_scores_t(imp_t, tq_row):
    nb, nq = imp_t.shape
    j = lax.broadcasted_iota(jnp.int32, (nb, nq), 0)
    cur = tq_row // SLC_BLK
    forced = (j == 0) | (j == cur) | (j == cur - 1)
    return jnp.where(j * SLC_BLK <= tq_row, imp_t + FORCE_BONUS * forced.astype(F32), -jnp.inf)


NSA_TQ = 256
NSA_TK = 512


SEGS_PER_PAGE = PAGE_SIZE // CMP_STRIDE
CMP_PAGES_PER_STEP = 64
CMP_ROW_CHUNK = 256
STAGE_PITCH = 24


SQ_ROWS = SUBLANES


C_W = NSA_KV_HEADS * NSA_HEAD_DIM
HC_W = NSA_KV_HEADS * CMP_HIDDEN
PC_W = CMP_R * HC_W
MXU_DEPTH = 256
CMP_S_PER_DOT = MXU_DEPTH // C_W


def _cmp_weights_fm(cmp_pe, cmp_w1, cmp_w2):
    eye = jnp.eye(NSA_KV_HEADS, dtype=F32)
    w1r = cmp_w1.reshape(2, CMP_R, CMP_STRIDE, NSA_HEAD_DIM, CMP_HIDDEN)
    w1c = jnp.einsum('cmsdh,gG->csgdmGh', w1r, eye).reshape(2, CMP_STRIDE * C_W, PC_W).astype(BF16)
    w2c_t = jnp.einsum('chd,gG->cGdgh', cmp_w2, eye).reshape(2, C_W, HC_W).astype(BF16)
    pe_flat = jnp.transpose(cmp_pe, (1, 0, 2)).reshape(2, CMP_BLK * NSA_HEAD_DIM)
    return w1c, w2c_t, pe_flat


def _cmp_bias_c(pe_ref, w1_ref, b1_ref, c):
    pe = jnp.broadcast_to(pe_ref[c:c + 1, :], (SUBLANES, pe_ref.shape[1]))
    w1c = w1_ref[c]
    pe_hi = pe.astype(BF16)
    pe_mid = (pe - pe_hi.astype(F32)).astype(BF16)
    w_hi = w1c.astype(BF16)
    w_mid = (w1c - w_hi.astype(F32)).astype(BF16)
    pb = _dot(pe_hi, w_hi) + _dot(pe_mid, w_hi) + _dot(pe_hi, w_mid)
    bc = pb[0:1, :] + b1_ref[c:c + 1, :]
    return jnp.concatenate([bc] * NSA_KV_HEADS, axis=1)


def _cmp_first_layer(x_ref, c, row0, n_seg, w1c_ref, pitch=CMP_STRIDE):
    acc = None
    for s in range(0, CMP_STRIDE, CMP_S_PER_DOT):
        xs = jnp.concatenate([x_ref[c, pl.ds(row0 + s + j, n_seg, stride=pitch), :].astype(BF16)
                              for j in range(CMP_S_PER_DOT)], axis=1)
        d = _dot(xs, w1c_ref[c, s * C_W:(s + CMP_S_PER_DOT) * C_W, :])
        acc = d if acc is None else acc + d
    return acc


def _cmp_second_layer_fm(p, bias, w2t):
    n = p.shape[0]
    h = p[:, :HC_W] + pltpu.roll(p[:, HC_W:], n - 1, 0) + bias
    return _dot_nt(w2t, jax.nn.gelu(h).astype(BF16))


def _cmp_prompt_fm_kernel(x_ref, w1c_ref, pe_ref, w1_ref, b1_ref, w2t_ref, o_ref):
    n_seg = o_ref.shape[1]
    for c in range(2):
        p = _cmp_first_layer(x_ref, c, 0, n_seg, w1c_ref)
        o_ref[c * C_W:(c + 1) * C_W, :] = _cmp_second_layer_fm(p, _cmp_bias_c(pe_ref, w1_ref, b1_ref, c), w2t_ref[c])


def _cmp_prompt_fm(x_tok, seq, w1c, pe_flat, cmp_w1, cmp_b1, w2c_t):
    bsz = x_tok.shape[1] // seq
    n_seg = seq // CMP_STRIDE
    const = lambda a: pl.BlockSpec(a.shape, lambda b: (0,) * a.ndim)
    return pl.pallas_call(
        _cmp_prompt_fm_kernel,
        grid=(bsz,),
        in_specs=[pl.BlockSpec((2, seq, C_W), lambda b: (0, b, 0)),
                  const(w1c), const(pe_flat), const(cmp_w1), const(cmp_b1), const(w2c_t)],
        out_specs=pl.BlockSpec((None, KV_W, n_seg), lambda b: (b, 0, 0)),
        out_shape=jax.ShapeDtypeStruct((bsz, KV_W, n_seg), F32),
        compiler_params=_cparams(("parallel",)),
        name="nsa_compress_prompt",
    )(x_tok, w1c, pe_flat, cmp_w1, cmp_b1, w2c_t)


def _scaled_group_queries(q, g):
    return (_group_queries(q, g).astype(F32) * QK_SCALE).astype(BF16)


def _cmp_branch_fm(qg, cmp_ref, g, valid):
    hd = NSA_HEAD_DIM
    kl, vl = g * hd, (NSA_KV_HEADS + g) * hd
    s_c = _dot(qg, cmp_ref[kl:kl + hd, :].astype(BF16))
    p_c = _masked_softmax(s_c, valid)
    return _dot_nt(p_c.astype(BF16), cmp_ref[vl:vl + hd, :].astype(BF16)), p_c


def _sum_heads(p, rows):
    out = p[0:rows]
    for r in range(1, NSA_GROUP):
        out = out + p[r * rows:(r + 1) * rows]
    return out


ATT_RB = 32


def _nsa_prompt_fm_kernel(q_ref, aux_ref, cmp_ref, kvs_ref, kvw_ref, o_ref,
                          s_scr, bias_scr, e_scr, m_scr, corr_scr, acc_scr, *, seq, nc):
    tq_n, tk_n, hd, grp = NSA_TQ, NSA_TK, NSA_HEAD_DIM, NSA_GROUP
    m_rows = grp * tq_n
    q0 = pl.program_id(1) * tq_n
    q_all = q_ref[...]
    gates = jax.nn.sigmoid(aux_ref[...])
    n_cmp = cmp_ref.shape[1]
    nb = seq // SLC_BLK

    tq_col = q0 + lax.broadcasted_iota(jnp.int32, (tq_n, 1), 0)
    tq_rows = jnp.concatenate([tq_col] * grp, axis=0)
    tq_lane = q0 + lax.broadcasted_iota(jnp.int32, (1, NSA_KV_HEADS * tq_n), 1) % tq_n
    t_end = lax.broadcasted_iota(jnp.int32, (1, n_cmp), 1) * CMP_STRIDE + (CMP_BLK - 1)
    ov_t = _overlap_t(nb, n_cmp, nc)
    n_chunks = (q0 + tq_n + tk_n - 1) // tk_n
    w_len = WINDOW + tq_n
    w0 = pl.multiple_of(jnp.maximum(q0 - WINDOW, 0), tq_n)

    qgs = [_scaled_group_queries(q_all, g) for g in range(NSA_KV_HEADS)]
    o_cs, p_cs = [], []
    for g in range(NSA_KV_HEADS):
        o_c, p_c = _cmp_branch_fm(qgs[g], cmp_ref, g, t_end <= tq_rows)
        o_cs.append(o_c)
        p_cs.append(p_c)
    n_pick = min(SLC_TOP_N, nb)

    def ranked_blocks():
        imps = [_importance_t(_sum_heads(p_c, tq_n), ov_t) for p_c in p_cs]
        return _topk_rows(_select_scores_t(jnp.concatenate(imps, axis=1), tq_lane), n_pick)[0]

    def all_visible_blocks():
        j = lax.broadcasted_iota(jnp.int32, (nb, NSA_KV_HEADS * tq_n), 0)
        return (j * SLC_BLK <= tq_lane).astype(F32)

    sel_all = lax.cond(q0 + tq_n <= n_pick * SLC_BLK, all_visible_blocks, ranked_blocks)

    tw = w0 + lax.broadcasted_iota(jnp.int32, (1, w_len), 1)
    bias_w = jnp.where((tw <= tq_col) & (tw > tq_col - WINDOW), 0.0, NEG)

    groups = range(NSA_KV_HEADS)

    def exp_rows(g, bias_g, width, online):
        per_head = tq_n // ATT_RB
        for i in range(m_rows // ATT_RB):
            rows = slice(i * ATT_RB, (i + 1) * ATT_RB)
            brows = slice((i % per_head) * ATT_RB, (i % per_head + 1) * ATT_RB)
            s = s_scr[g, rows, :width] + bias_scr[bias_g, brows, :width]
            m_new = jnp.max(s, axis=-1, keepdims=True)
            if online:
                m_old = m_scr[g, rows, :]
                m_new = jnp.maximum(m_old, m_new)
                corr_scr[g, rows, :] = jnp.exp2(m_old - m_new)
                m_scr[g, rows, :] = m_new
            e_scr[g, rows, :width] = jnp.exp2(s - m_new).astype(BF16)

    def keys_log2(ref, g, cols):
        return (ref[g * hd:(g + 1) * hd, cols] * LOG2E).astype(BF16)

    def gate_lane(g, r, branch):
        return hd + GATE_OFF + 3 * (grp * g + r) + branch

    def values_with_ones(ref, g, cols, branch):
        v = ref[(NSA_KV_HEADS + g) * hd:(NSA_KV_HEADS + g + 1) * hd, cols].astype(BF16)
        row = lax.broadcasted_iota(jnp.int32, v.shape, 0) + hd
        ones = functools.reduce(jnp.logical_or, [row == gate_lane(g, r, branch) for r in range(grp)])
        return jnp.concatenate([v, ones.astype(BF16)], axis=0)

    m_scr[...] = jnp.full(m_scr.shape, NEG, F32)
    acc_scr[...] = jnp.zeros(acc_scr.shape, F32)
    sel_ts = [sel_all[:, g * tq_n:(g + 1) * tq_n].astype(BF16) for g in groups]

    def slc_chunk(ci, carry):
        k0 = pl.multiple_of(ci * tk_n, tk_n)
        cols = pl.ds(k0, tk_n)
        kpos = k0 + lax.broadcasted_iota(jnp.int32, (1, tk_n), 1)
        expand = (lax.broadcasted_iota(jnp.int32, (nb, tk_n), 0) == kpos // SLC_BLK).astype(BF16)
        for g in groups:
            sel_k = _dot_tn(sel_ts[g], expand)
            bias_scr[g, :, :tk_n] = jnp.where((sel_k > 0.5) & (kpos <= tq_col), 0.0, NEG)
            s_scr[g, :, :tk_n] = _dot(qgs[g], keys_log2(kvs_ref, g, cols))
        for g in groups:
            exp_rows(g, g, tk_n, True)
        for g in groups:
            acc_scr[g] = corr_scr[g] * acc_scr[g] + _dot_nt(e_scr[g, :, :tk_n], values_with_ones(kvs_ref, g, cols, 1))
        return carry

    lax.fori_loop(0, n_chunks, slc_chunk, 0)
    wcols = pl.ds(w0, w_len)
    bias_scr[0, :, :w_len] = bias_w
    for g in groups:
        s_scr[g, :, :w_len] = _dot(qgs[g], keys_log2(kvw_ref, g, wcols))
    for g in groups:
        exp_rows(g, 0, w_len, False)
    acc_ws = [_dot_nt(e_scr[g, :, :w_len], values_with_ones(kvw_ref, g, wcols, 2)) for g in groups]
    gates_rot = pltpu.roll(gates, hd, 1)
    for g in groups:
        for r in range(grp):
            h = grp * g + r
            rows = slice(r * tq_n, (r + 1) * tq_n)
            acc_s, acc_w = acc_scr[g, rows, :], acc_ws[g][rows]
            f_s = gates_rot / jnp.maximum(acc_s, 1e-30)
            f_w = gates_rot / jnp.maximum(acc_w, 1e-30)
            ls, lw = gate_lane(g, r, 1), gate_lane(g, r, 2)
            gc = GATE_OFF + 3 * h
            o = (gates[:, gc:gc + 1] * o_cs[g][rows] + f_s[:, ls:ls + 1] * acc_s[:, :hd]
                 + f_w[:, lw:lw + 1] * acc_w[:, :hd])
            o_ref[:, h * hd:(h + 1) * hd] = o.astype(o_ref.dtype)


def _nsa_prompt_fm(q, aux, cmp_t, kvs_t, kvw_t):
    bsz, seq, _ = q.shape
    n_seg = cmp_t.shape[2]
    kern = functools.partial(_nsa_prompt_fm_kernel, seq=seq, nc=n_seg - CMP_R + 1)
    tile = lambda w: pl.BlockSpec((None, NSA_TQ, w), lambda b, t: (b, t, 0))
    whole = lambda n: pl.BlockSpec((None, KV_W, n), lambda b, t: (b, 0, 0))
    m_rows = NSA_GROUP * NSA_TQ
    width = max(NSA_TK, WINDOW + NSA_TQ)
    ng = NSA_KV_HEADS
    return pl.pallas_call(
        kern,
        grid=(bsz, seq // NSA_TQ),
        in_specs=[tile(NSA_WIDTH), tile(AUX_W), whole(n_seg), whole(seq), whole(seq)],
        out_specs=tile(NSA_WIDTH),
        out_shape=jax.ShapeDtypeStruct((bsz, seq, NSA_WIDTH), BF16),
        scratch_shapes=[pltpu.VMEM((ng, m_rows, width), F32),
                        pltpu.VMEM((ng, NSA_TQ, width), F32),
                        pltpu.VMEM((ng, m_rows, width), BF16),
                        pltpu.VMEM((ng, m_rows, 1), F32),
                        pltpu.VMEM((ng, m_rows, 1), F32),
                        pltpu.VMEM((ng, m_rows, 2 * NSA_HEAD_DIM), F32)],
        compiler_params=_cparams(("parallel", "arbitrary")),
        name="nsa_attn_prompt",
    )(q, aux, cmp_t, kvs_t, kvw_t)


def _cmp_sample_fm_kernel(pt_ref, cache_ref, w1f_ref, pe_ref, w1_ref, b1_ref, w2t_ref, o_ref,
                          xbuf, stage_a, stage_b, p_scr, bias_scr, sem, *, steps_per_batch):
    b = pl.program_id(0)
    h = pl.program_id(1)
    step = b * steps_per_batch + h
    n_steps = pl.num_programs(0) * steps_per_batch
    pps = CMP_PAGES_PER_STEP
    segs = pps * SEGS_PER_PAGE

    def page_copy(bb, hh, p, slot):
        return pltpu.make_async_copy(cache_ref.at[pt_ref[bb, hh * pps + p]], xbuf.at[slot, p], sem.at[slot])

    def start_fetch(bb, hh, slot):
        for p in range(pps):
            page_copy(bb, hh, p, slot).start(priority=p % 2)

    @pl.when(step == 0)
    def _():
        start_fetch(b, h, 0)

    @pl.when(step + 1 < n_steps)
    def _():
        wrap = h + 1 == steps_per_batch
        start_fetch(jnp.where(wrap, b + 1, b), jnp.where(wrap, 0, h + 1), (step + 1) % 2)

    slot = step % 2
    for p in range(pps):
        page_copy(b, h, p, slot).wait()

    pages_per_chunk = CMP_ROW_CHUNK // SEGS_PER_PAGE
    n_chunks = segs // CMP_ROW_CHUNK
    stages = (stage_a, stage_b)
    assert n_chunks == len(stages)

    def transpose_page(rc, lp):
        p = rc * pages_per_chunk + lp
        for c in range(2):
            tok = jnp.transpose(xbuf[slot, p, c * C_W:(c + 1) * C_W, :].astype(BF16)).astype(F32)
            for n in range(SEGS_PER_PAGE):
                r0 = (lp * SEGS_PER_PAGE + n) * STAGE_PITCH
                stages[rc][c, r0:r0 + CMP_STRIDE, :] = tok[n * CMP_STRIDE:(n + 1) * CMP_STRIDE]

    def first_layer(rc, between=()):
        between = list(between)
        dots = [(c, s) for c in range(2) for s in range(0, CMP_STRIDE, CMP_S_PER_DOT)]
        per_dot = -(-len(between) // len(dots))
        accs = [None, None]
        for c, s in dots:
            xs = jnp.concatenate([stages[rc][c, pl.ds(s + j, CMP_ROW_CHUNK, stride=STAGE_PITCH), :].astype(BF16)
                                  for j in range(CMP_S_PER_DOT)], axis=1)
            d = _dot(xs, w1f_ref[c, s * C_W:(s + CMP_S_PER_DOT) * C_W, :])
            accs[c] = d if accs[c] is None else accs[c] + d
            for thunk in between[:per_dot]:
                thunk()
            between = between[per_dot:]
        for c in range(2):
            r0 = pl.multiple_of(h * segs + rc * CMP_ROW_CHUNK, CMP_ROW_CHUNK)
            p_scr[c, pl.ds(r0, CMP_ROW_CHUNK), :] = accs[c]

    for lp in range(pages_per_chunk):
        transpose_page(0, lp)
    for rc in range(n_chunks):
        nxt = [functools.partial(transpose_page, rc + 1, lp) for lp in range(pages_per_chunk)] if rc + 1 < n_chunks else []
        first_layer(rc, nxt)

    @pl.when(step == 0)
    def _():
        for c in range(2):
            bias_scr[c] = jnp.broadcast_to(_cmp_bias_c(pe_ref, w1_ref, b1_ref, c), (SUBLANES, HC_W))

    @pl.when(h == steps_per_batch - 1)
    def _():
        for c in range(2):
            o_ref[c * C_W:(c + 1) * C_W, :] = _cmp_second_layer_fm(p_scr[c], bias_scr[c, 0:1, :], w2t_ref[c])


def _cmp_sample_fm(page_table, cache_fm, w1_full, pe_flat, cmp_w1, cmp_b1, w2_bd_t):
    bsz, n_pages = page_table.shape
    steps = n_pages // CMP_PAGES_PER_STEP
    n_seg = n_pages * SEGS_PER_PAGE
    const = lambda a: pl.BlockSpec(a.shape, lambda b, h, pt: (0,) * a.ndim, pipeline_mode=pl.Buffered(1))
    grid_spec = pltpu.PrefetchScalarGridSpec(
        num_scalar_prefetch=1,
        grid=(bsz, steps),
        in_specs=[pl.BlockSpec(memory_space=pl.ANY), const(w1_full), const(pe_flat), const(cmp_w1),
                  const(cmp_b1), const(w2_bd_t)],
        out_specs=pl.BlockSpec((None, KV_W, n_seg), lambda b, h, pt: (b, 0, 0)),
        scratch_shapes=[pltpu.VMEM((2, CMP_PAGES_PER_STEP, KV_W, PAGE_SIZE), F32),
                        pltpu.VMEM((2, CMP_ROW_CHUNK * STAGE_PITCH, C_W), F32),
                        pltpu.VMEM((2, CMP_ROW_CHUNK * STAGE_PITCH, C_W), F32),
                        pltpu.VMEM((2, n_seg, PC_W), F32),
                        pltpu.VMEM((2, SUBLANES, HC_W), F32),
                        pltpu.SemaphoreType.DMA((2,))],
    )
    return pl.pallas_call(
        functools.partial(_cmp_sample_fm_kernel, steps_per_batch=steps),
        grid_spec=grid_spec,
        out_shape=jax.ShapeDtypeStruct((bsz, KV_W, n_seg), F32),
        compiler_params=_cparams(("arbitrary", "arbitrary")),
        name="nsa_compress_sample",
    )(page_table, cache_fm, w1_full, pe_flat, cmp_w1, cmp_b1, w2_bd_t)


SEL_COLS = NSA_KV_HEADS * SQ_ROWS
SEL_NBB = LANES // SEL_COLS


def _nsa_sample_select_fm_kernel(q_ref, cmp_ref, oc_ref, idx_ref, *, past_len, nb, nb_pad, nc):
    hd, grp = NSA_HEAD_DIM, NSA_GROUP
    m_rows = grp * SQ_ROWS
    n_cmp = cmp_ref.shape[2]
    tq_rows = past_len + lax.broadcasted_iota(jnp.int32, (m_rows, 1), 0) % SQ_ROWS
    tq_lane = past_len + lax.broadcasted_iota(jnp.int32, (1, LANES), 1) % SQ_ROWS
    t_end = lax.broadcasted_iota(jnp.int32, (1, n_cmp), 1) * CMP_STRIDE + (CMP_BLK - 1)
    ov_t = _overlap_t(nb_pad, n_cmp, nc)
    psums = []
    for e in range(SEL_NBB):
        q_all = q_ref[e]
        for g in range(NSA_KV_HEADS):
            o_c, p_c = _cmp_branch_fm(_scaled_group_queries(q_all, g), cmp_ref.at[e], g, t_end <= tq_rows)
            psums.append(_sum_heads(p_c, SQ_ROWS))
            for r in range(grp):
                h = grp * g + r
                oc_ref[e, :, h * hd:(h + 1) * hd] = o_c[r * SQ_ROWS:(r + 1) * SQ_ROWS]
    score_t = _select_scores_t(_importance_t(jnp.concatenate(psums, axis=0), ov_t), tq_lane)
    rowid = lax.broadcasted_iota(jnp.int32, score_t.shape, 0)
    _, picks = _topk_rows(jnp.where(rowid < nb, score_t, -jnp.inf), min(SLC_TOP_N, nb))
    idx_ref[...] = jnp.concatenate(picks, axis=0)


def _nsa_sample_select_fm(q_pad, cmp_t, past_len, seq_new):
    bsz = q_pad.shape[0]
    n_seg = cmp_t.shape[2]
    nb = -(-(past_len + seq_new) // SLC_BLK)
    nb_pad = -(-nb // SUBLANES) * SUBLANES
    n_pick = min(SLC_TOP_N, nb)
    kern = functools.partial(_nsa_sample_select_fm_kernel, past_len=past_len, nb=nb, nb_pad=nb_pad,
                             nc=n_seg - CMP_R + 1)
    return pl.pallas_call(
        kern,
        grid=(bsz // SEL_NBB,),
        in_specs=[pl.BlockSpec((SEL_NBB, SQ_ROWS, NSA_WIDTH), lambda b: (b, 0, 0)),
                  pl.BlockSpec((SEL_NBB, KV_W, n_seg), lambda b: (b, 0, 0))],
        out_specs=[pl.BlockSpec((SEL_NBB, SQ_ROWS, NSA_WIDTH), lambda b: (b, 0, 0)),
                   pl.BlockSpec((None, n_pick, LANES), lambda b: (b, 0, 0))],
        out_shape=[jax.ShapeDtypeStruct((bsz, SQ_ROWS, NSA_WIDTH), F32),
                   jax.ShapeDtypeStruct((bsz // SEL_NBB, n_pick, LANES), jnp.int32)],
        compiler_params=_cparams(("parallel",)),
        name="nsa_select_sample",
    )(q_pad, cmp_t)


ATTEND_SLOTS = 3


def _nsa_sample_attend_fm_kernel(pt_ref, idx_ref, q_ref, gate_ref, oc_ref, cache_ref, tail_ref, winp_ref,
                                 winn_ref, o_ref, kvbuf, sem, *, past_len, seq_new, n_pick):
    b = pl.program_id(0)
    nbatch = pl.num_programs(0)
    hd, grp = NSA_HEAD_DIM, NSA_GROUP
    n_items = seq_new * NSA_KV_HEADS * n_pick
    n_past_blocks = past_len // SLC_BLK
    blocks_per_page = PAGE_SIZE // SLC_BLK

    def block_id(bb, item):
        return jnp.minimum(idx_ref[bb * n_items + item], n_past_blocks)

    def copy(bb, item, slot):
        g = (item // n_pick) % NSA_KV_HEADS
        lanes = slice((item % n_pick) * PAGE_SIZE, (item % n_pick + 1) * PAGE_SIZE)
        rows = slice(g * hd, (g + 1) * hd)
        past_blk = jnp.minimum(block_id(bb, item), n_past_blocks - 1)
        page = pt_ref[bb, lax.shift_right_logical(past_blk, blocks_per_page.bit_length() - 1)]
        return pltpu.make_async_copy(cache_ref.at[page, :, rows, :], kvbuf.at[slot, item // n_pick, :, :, lanes],
                                     sem.at[slot])

    def start_fetch(bb, slot):
        for item in range(n_items):
            copy(bb, item, slot).start(priority=item % 2)

    @pl.when(b == 0)
    def _():
        for ahead in range(ATTEND_SLOTS - 1):
            @pl.when(ahead < nbatch)
            def _():
                start_fetch(ahead, ahead)

    nxt = b + ATTEND_SLOTS - 1

    @pl.when(nxt < nbatch)
    def _():
        start_fetch(nxt, nxt % ATTEND_SLOTS)

    slot = b % ATTEND_SLOTS

    for item in range(n_items):
        copy(b, item, slot).wait()

    head_row = lax.broadcasted_iota(jnp.int32, (NSA_HEADS, 1), 0)
    tok = lax.broadcasted_iota(jnp.int32, (1, PAGE_SIZE), 1)
    key_lane = lax.broadcasted_iota(jnp.int32, (1, n_pick * PAGE_SIZE), 1)
    wb, wn = winp_ref.shape[1], winn_ref.shape[1]
    tw_p = past_len - wb + lax.broadcasted_iota(jnp.int32, (1, wb), 1)
    tw_n = past_len + lax.broadcasted_iota(jnp.int32, (1, wn), 1)

    def heads16(x):
        return jnp.concatenate([x.astype(BF16), jnp.zeros_like(x, dtype=BF16)], axis=0)

    def merge_groups(per_group):
        out = per_group[0]
        for g in range(1, NSA_KV_HEADS):
            out = jnp.where(head_row >= g * grp, per_group[g], out)
        return out

    for qi in range(seq_new):
        tq = past_len + qi
        q16 = heads16(q_ref[qi] * QK_SCALE)
        gates = jax.nn.sigmoid(gate_ref[qi])
        o_s_g, o_w_g = [], []
        for g in range(NSA_KV_HEADS):
            qg_i = qi * NSA_KV_HEADS + g
            kl = g * hd
            blk = jnp.zeros_like(key_lane)
            for kk in range(n_pick):
                blk = jnp.where(key_lane // PAGE_SIZE == kk, block_id(b, qg_i * n_pick + kk), blk)
            new_picked = jnp.max(blk, axis=1, keepdims=True) == n_past_blocks
            in_block = (key_lane % PAGE_SIZE) // SLC_BLK == blk % blocks_per_page
            mask = jnp.concatenate(
                [(blk < n_past_blocks) & in_block & (blk * SLC_BLK + key_lane % SLC_BLK <= tq),
                 new_picked & (tok < SLC_BLK) & (past_len + tok <= tq)], axis=1)
            kcat = jnp.concatenate([kvbuf[slot, qg_i, 0].astype(BF16), tail_ref[0, kl:kl + hd, :].astype(BF16)], axis=1)
            vcat = jnp.concatenate([kvbuf[slot, qg_i, 1].astype(BF16), tail_ref[1, kl:kl + hd, :].astype(BF16)], axis=1)
            p = _masked_softmax(_dot(q16, kcat)[:NSA_HEADS], mask)
            o_s_g.append(_dot_nt(heads16(p), vcat)[:NSA_HEADS])
            vl = (NSA_KV_HEADS + g) * hd
            bias_p = jnp.where((tw_p <= tq) & (tw_p > tq - WINDOW) & (tw_p >= 0), 0.0, NEG)
            bias_n = jnp.where((tw_n <= tq) & (tw_n > tq - WINDOW), 0.0, NEG)
            s_p = _dot(q16, winp_ref[kl:kl + hd, :].astype(BF16))[:NSA_HEADS] + bias_p
            s_n = _dot(q16, winn_ref[kl:kl + hd, :].astype(BF16))[:NSA_HEADS] + bias_n
            m = jnp.maximum(jnp.max(s_p, axis=-1, keepdims=True), jnp.max(s_n, axis=-1, keepdims=True))
            e_p, e_n = jnp.exp(s_p - m), jnp.exp(s_n - m)
            den = jnp.maximum(jnp.sum(e_p, axis=-1, keepdims=True) + jnp.sum(e_n, axis=-1, keepdims=True), 1e-30)
            o_w_g.append((_dot_nt(heads16(e_p), winp_ref[vl:vl + hd, :].astype(BF16))
                          + _dot_nt(heads16(e_n), winn_ref[vl:vl + hd, :].astype(BF16)))[:NSA_HEADS] / den)
        o_ref[qi] = (gates[:, 0:1] * oc_ref[qi] + gates[:, 1:2] * merge_groups(o_s_g)
                     + gates[:, 2:3] * merge_groups(o_w_g))


def _nsa_sample_attend_fm(page_table, idx_flat, q_heads, gate_logits, o_c, cache_fm, tail_fm, win_past, win_new,
                          past_len, seq_new, n_pick):
    bsz = q_heads.shape[0]
    wb, wn = win_past.shape[2], win_new.shape[2]
    kern = functools.partial(_nsa_sample_attend_fm_kernel, past_len=past_len, seq_new=seq_new, n_pick=n_pick)
    per_b = lambda n, w: pl.BlockSpec((None, n, w), lambda b, pt, ix: (b, 0, 0))
    per_bq = lambda w: pl.BlockSpec((None, seq_new, NSA_HEADS, w), lambda b, pt, ix: (b, 0, 0, 0))
    n_qg = seq_new * NSA_KV_HEADS
    buf = pltpu.VMEM((ATTEND_SLOTS, n_qg, 2, NSA_HEAD_DIM, n_pick * PAGE_SIZE), F32)
    grid_spec = pltpu.PrefetchScalarGridSpec(
        num_scalar_prefetch=2,
        grid=(bsz,),
        in_specs=[per_bq(NSA_HEAD_DIM), per_bq(LANES), per_bq(NSA_HEAD_DIM),
                  pl.BlockSpec(memory_space=pl.ANY),
                  pl.BlockSpec((None, 2, C_W, LANES), lambda b, pt, ix: (b, 0, 0, 0)),
                  per_b(KV_W, wb), per_b(KV_W, wn)],
        out_specs=per_bq(NSA_HEAD_DIM),
        scratch_shapes=[buf, pltpu.SemaphoreType.DMA((ATTEND_SLOTS,))],
    )
    return pl.pallas_call(
        kern,
        grid_spec=grid_spec,
        out_shape=jax.ShapeDtypeStruct((bsz, seq_new, NSA_HEADS, NSA_HEAD_DIM), F32),
        compiler_params=_cparams(("arbitrary",)),
        name="nsa_attend_sample",
    )(page_table, idx_flat, q_heads, gate_logits, o_c, cache_fm, tail_fm, win_past, win_new)


PROMPT_TM = 512
FFN_TM = 512
GLA_TL = 512
SAMPLE_GLA_ROWS = 16
GLA_PROMPT_NBB = 8
GLA_SAMPLE_NBB = 8


def _pad_rows(x, n):
    return jnp.pad(x, ((0, 0), (0, n - x.shape[1]), (0, 0)))


def kernel(x_prompt, x_sample, state_gla, cache_cmp_kv, cache_slc_kv, cache_win_kv, page_table, c_prompt,
           c_sample, ln_in_g, ln_in_b, w_ada, b_ada, w_in, gla_w_a2, gla_b_a, gla_norm_g, cmp_pe, cmp_w1,
           cmp_b1, cmp_w2, w_o, ln1_g, ln1_b, w_ffn_in, w_ffn_out, ln2_g, ln2_b):
    assert w_in.shape[0] == DEPTH == 1
    l = 0
    bp, lp, d = x_prompt.shape
    bs, ls, _ = x_sample.shape
    n_pool = cache_cmp_kv.shape[1]
    n_pages = page_table.shape[1]
    past_len = n_pages * PAGE_SIZE
    wb = cache_win_kv.shape[2]
    assert ((past_len + ls) // CMP_STRIDE) * CMP_STRIDE <= past_len and past_len % SLC_BLK == 0
    assert ls <= SQ_ROWS and ls <= SLC_BLK and wb == WINDOW and bs % SEL_NBB == 0

    w_perm = _permute_w_in(w_in[l])
    w_o_b, w_fi_b, w_fo_b = w_o[l].astype(BF16), w_ffn_in[l].astype(BF16), w_ffn_out[l].astype(BF16)
    w_a2p = jnp.zeros((AUX_W, GLA_KW), F32).at[:GLA_RANK].set(gla_w_a2[l])
    b_a = gla_b_a[l].reshape(1, GLA_KW)
    norm_g = gla_norm_g[l].reshape(1, GLA_DV)
    w1c, w2c_t, pe_flat = _cmp_weights_fm(cmp_pe[l], cmp_w1[l], cmp_w2[l])
    kvt = (2, NSA_KV_HEADS, NSA_HEAD_DIM)

    def fm_view(a):
        return jnp.transpose(a, (0, 2, 3, 4, 1)).reshape(a.shape[0], KV_W, a.shape[1])

    def tok_view(a_fm):
        n, _, t = a_fm.shape
        return jnp.transpose(a_fm.reshape((n,) + kvt + (t,)), (0, 4, 1, 2, 3))[None]

    mod = _ada(jnp.concatenate([c_prompt, c_sample], axis=0), w_ada[l], b_ada[l])
    mods_p = [m[:bp, None, :] for m in jnp.split(mod, 6, axis=-1)]
    mods_s = [jnp.repeat(m[bp:], ls, axis=0)[None] for m in jnp.split(mod, 6, axis=-1)]

    def out_ffn(x2d, o_g, o_n, mods, tm, rpm):
        sh1, sc1, ga1, sh2, sc2, ga2 = mods
        return _out_ffn(x2d, o_g, o_n, ga1, sc2, sh2, ga2, ln_in_g, ln_in_b, w_o_b, ln1_g[l], ln1_b[l], w_fi_b,
                        w_fo_b, ln2_g[l], ln2_b[l], tm, rpm)

    xp2 = x_prompt.reshape(bp * lp, d)
    rpm = lp // PROMPT_TM
    qk, v, r, qn, kvc, aux, kvc_t, kvs_t, kvw_t = _inproj(
        xp2, mods_p[1], mods_p[0], ln_in_g, ln_in_b, w_perm, PROMPT_TM, rpm, seq_per_batch=lp)
    b3 = lambda a: a.reshape(bp, lp, a.shape[-1])
    o_g, s_p = _gla(b3(qk), b3(v), b3(r), b3(aux), w_a2p, b_a, norm_g, jnp.zeros((bp, GLA_KW, GLA_DV), F32),
                    chunk=GLA_CHUNK, sub=GLA_SUB, tl=GLA_TL, l_valid=lp, nbb=GLA_PROMPT_NBB)
    cmp_t_p = _cmp_prompt_fm(kvc, lp, w1c, pe_flat, cmp_w1[l], cmp_b1[l], w2c_t)
    o_n = _nsa_prompt_fm(b3(qn), b3(aux), cmp_t_p, kvs_t, kvw_t)
    y_p = out_ffn(xp2, o_g.reshape(bp * lp, GLA_WIDTH), o_n.reshape(bp * lp, NSA_WIDTH), mods_p, FFN_TM,
                  lp // FFN_TM)
    w_keep = min(WINDOW, lp)
    outs_p = (y_p.reshape(bp, lp, d), s_p.reshape(1, bp, GLA_HEADS, GLA_DK, GLA_DV),
              tok_view(kvc_t), tok_view(kvs_t), tok_view(kvw_t[:, :, lp - w_keep:]))

    ts = bs * ls
    xs2 = x_sample.reshape(ts, d)
    qk, v, r, qn, kvc, aux, kvs, kvw = _inproj(xs2, mods_s[1], mods_s[0], ln_in_g, ln_in_b, w_perm, ts, 1)
    s3 = lambda a: a.reshape(bs, ls, a.shape[-1])
    g16 = lambda a: _pad_rows(s3(a), SAMPLE_GLA_ROWS)
    o_g, s_s = _gla(g16(qk), g16(v), g16(r), g16(aux), w_a2p, b_a, norm_g, state_gla[l].reshape(bs, GLA_KW, GLA_DV),
                    chunk=SAMPLE_GLA_ROWS, sub=SAMPLE_GLA_ROWS, tl=SAMPLE_GLA_ROWS, l_valid=ls,
                    nbb=GLA_SAMPLE_NBB)
    o_g = o_g[:, :ls].reshape(ts, GLA_WIDTH)
    kvc = jnp.concatenate([kvc[0], kvc[1]], axis=1)
    cmp_t_s = _cmp_sample_fm(page_table, fm_view(cache_cmp_kv[l]), w1c, pe_flat, cmp_w1[l], cmp_b1[l], w2c_t)
    q_pad = _pad_rows(s3(qn), SQ_ROWS)
    o_c, idx = _nsa_sample_select_fm(q_pad, cmp_t_s, past_len, ls)
    n_pick = idx.shape[1]
    idx = idx.reshape(bs // SEL_NBB, n_pick, SEL_NBB, NSA_KV_HEADS, SQ_ROWS)[..., :ls]
    idx_flat = jnp.transpose(idx, (0, 2, 4, 3, 1)).reshape(-1)
    new_fm =lambda a: jnp.pad(jnp.transpose(s3(a), (0, 2, 1)), ((0, 0), (0, 0), (0, LANES - ls)))
    win_past = fm_view(cache_win_kv[l])
    per_head = lambda a, w: a.reshape(bs, -1, NSA_HEADS, w)[:, :ls]
    gate_logits = jnp.pad(per_head(aux[:, GATE_OFF:GATE_OFF + N_GATES], 3), ((0, 0),) * 3 + ((0, LANES - 3),))
    o_n = _nsa_sample_attend_fm(page_table, idx_flat, per_head(qn.astype(F32), NSA_HEAD_DIM), gate_logits,
                                per_head(o_c, NSA_HEAD_DIM),
                                fm_view(cache_slc_kv[l]).reshape(n_pool, 2, C_W, PAGE_SIZE),
                                new_fm(kvs).reshape(bs, 2, C_W, LANES), win_past, new_fm(kvw),
                                past_len, ls, n_pick)
    o_n = o_n.reshape(ts, NSA_WIDTH).astype(BF16)
    y_s = out_ffn(xs2, o_g, o_n, mods_s, ts, 1)
    win_s = jnp.concatenate([win_past[:, :, ls:], jnp.transpose(s3(kvw), (0, 2, 1))], axis=2)
    outs_s = (y_s.reshape(bs, ls, d), s_s.reshape(1, bs, GLA_HEADS, GLA_DK, GLA_DV),
              kvc.reshape((1, bs, ls) + kvt), kvs.reshape((1, bs, ls) + kvt), tok_view(win_s))

    return (outs_p[0], outs_s[0], outs_p[1], outs_s[1], outs_p[2], outs_s[2], outs_p[3], outs_s[3],
            outs_p[4], outs_s[4])
```

```python
import functools
import math

import numpy as np
import jax
import jax.numpy as jnp
from jax import lax
from jax.experimental import pallas as pl
from jax.experimental.pallas import tpu as pltpu

F32 = jnp.float32
BF16 = jnp.bfloat16

D_MODEL = 1024
DEPTH = 1
PAGE_SIZE = 128
GLA_HEADS = 4
GLA_DV = D_MODEL // (2 * GLA_HEADS)
GLA_DK = GLA_DV // 2
GLA_RANK = 16
GLA_TAU = 16.0
GLA_CHUNK = 64
GLA_SUB = 32
GLA_WIDTH = GLA_HEADS * GLA_DV
GLA_KW = GLA_HEADS * GLA_DK
NSA_HEADS = 8
NSA_KV_HEADS = 2
NSA_GROUP = NSA_HEADS // NSA_KV_HEADS
NSA_HEAD_DIM = D_MODEL // (2 * NSA_HEADS)
NSA_WIDTH = NSA_HEADS * NSA_HEAD_DIM
CMP_BLK = 32
CMP_STRIDE = 16
CMP_HIDDEN = 2 * NSA_HEAD_DIM
SLC_BLK = 64
SLC_TOP_N = 16
WINDOW = 512
FORCE_BONUS = 1e4
NEG = -1e30
D_FF = -(-8 * D_MODEL // (3 * 256)) * 256
ALPHA = (2 * DEPTH) ** 0.25
KV_W = 2 * NSA_KV_HEADS * NSA_HEAD_DIM
N_GATES = 3 * NSA_HEADS
IN_SIZES = (GLA_KW, GLA_KW, GLA_WIDTH, GLA_RANK, GLA_WIDTH, NSA_WIDTH, KV_W, KV_W, KV_W, N_GATES)
IN_WIDTH = sum(IN_SIZES)
LN_EPS = 1e-5

LANES = 128
SUBLANES = 8
VMEM_LIMIT_BYTES = 56 * 1024 * 1024

AUX_W = LANES
IN_GROUPS = (2 * GLA_KW, GLA_WIDTH, GLA_WIDTH, NSA_WIDTH, KV_W, KV_W, KV_W, AUX_W)
IN_PERM_W = sum(IN_GROUPS)
GATE_OFF = GLA_RANK


def _cparams(sem):
    return pltpu.CompilerParams(dimension_semantics=sem, vmem_limit_bytes=VMEM_LIMIT_BYTES)


def _split3(a):
    hi = a.astype(BF16)
    r1 = a - hi.astype(F32)
    mid = r1.astype(BF16)
    lo = (r1 - mid.astype(F32)).astype(BF16)
    return hi, mid, lo


def _dot(a, b):
    return jnp.dot(a, b, preferred_element_type=F32)


def _dot_nt(a, b):
    return lax.dot_general(a, b, (((1,), (1,)), ((), ())), preferred_element_type=F32)


def _dot_tn(a, b):
    return lax.dot_general(a, b, (((0,), (0,)), ((), ())), preferred_element_type=F32)


def _layer_norm(x, g, b):
    mu = jnp.mean(x, axis=-1, keepdims=True)
    xc = x - mu
    var = jnp.mean(xc * xc, axis=-1, keepdims=True)
    return xc * lax.rsqrt(var + LN_EPS) * g + b


def _ada_kernel(c_ref, w_ref, b_ref, o_ref):
    c = c_ref[...]
    a = (c * jax.nn.sigmoid(c)).astype(BF16)
    o_ref[...] = _dot(a, w_ref[...].astype(BF16)) + b_ref[...]


def _ada(c, w_ada, b_ada):
    n, d = c.shape
    m = w_ada.shape[1]
    tn = D_MODEL
    return pl.pallas_call(
        _ada_kernel,
        grid=(m // tn,),
        in_specs=[pl.BlockSpec((n, d), lambda j: (0, 0)),
                  pl.BlockSpec((d, tn), lambda j: (0, j)),
                  pl.BlockSpec((1, tn), lambda j: (0, j))],
        out_specs=pl.BlockSpec((n, tn), lambda j: (0, j)),
        out_shape=jax.ShapeDtypeStruct((n, m), F32),
        compiler_params=_cparams(("parallel",)),
        name="ada_mod",
    )(c, w_ada, b_ada.reshape(1, m))


N_KV_GROUPS = 3


def _inproj_kernel(x_ref, sc_ref, sh_ref, g_ref, b_ref, w_ref, *o_refs, feature_major):
    xn = _layer_norm(x_ref[...], g_ref[...], b_ref[...])
    u = (xn * (1.0 + sc_ref[...]) + sh_ref[...]).astype(BF16)
    plain = dict(zip(("qk", "v", "r", "qn", "kvc", "aux"), o_refs[:6]))
    extra = dict(zip(("kvc", "kvs", "kvw") if feature_major else ("kvs", "kvw"), o_refs[6:]))
    lo = 0
    for name, wdt in zip(IN_GROUP_NAMES, IN_GROUPS):
        z = _dot(u, w_ref[:, lo:lo + wdt])
        lo += wdt
        if name == "kvc":
            plain[name][0] = z[:, :KV_W // 2]
            plain[name][1] = z[:, KV_W // 2:]
        elif name in plain:
            plain[name][...] = z.astype(plain[name].dtype)
        if name in extra:
            extra[name][...] = jnp.transpose(z) if feature_major else z


IN_GROUP_NAMES = ("qk", "v", "r", "qn", "kvc", "kvs", "kvw", "aux")


def _inproj(x, sc, sh, ln_g, ln_b, w_perm, tm, rows_per_mod, seq_per_batch=None):
    t, d = x.shape
    r = sc.shape[1]
    feature_major = seq_per_batch is not None
    mod_spec = pl.BlockSpec((None, r, d), lambda i: (i // rows_per_mod, 0, 0))
    tok = lambda w, dt: (pl.BlockSpec((tm, w), lambda i: (i, 0)), jax.ShapeDtypeStruct((t, w), dt))
    outs = [tok(2 * GLA_KW, BF16), tok(GLA_WIDTH, BF16), tok(GLA_WIDTH, BF16), tok(NSA_WIDTH, BF16),
            (pl.BlockSpec((2, tm, KV_W // 2), lambda i: (0, i, 0)), jax.ShapeDtypeStruct((2, t, KV_W // 2), F32)),
            tok(AUX_W, F32)]
    if feature_major:
        tpb = seq_per_batch // tm
        outs += [(pl.BlockSpec((None, KV_W, tm), lambda i: (i // tpb, 0, i % tpb)),
                  jax.ShapeDtypeStruct((t // seq_per_batch, KV_W, seq_per_batch), F32))] * N_KV_GROUPS
    else:
        outs += [tok(KV_W, F32)] * 2
    return pl.pallas_call(
        functools.partial(_inproj_kernel, feature_major=feature_major),
        grid=(t // tm,),
        in_specs=[pl.BlockSpec((tm, d), lambda i: (i, 0)), mod_spec, mod_spec,
                  pl.BlockSpec((1, d), lambda i: (0, 0)), pl.BlockSpec((1, d), lambda i: (0, 0)),
                  pl.BlockSpec((d, IN_PERM_W), lambda i: (0, 0))],
        out_specs=[o[0] for o in outs],
        out_shape=[o[1] for o in outs],
        compiler_params=_cparams(("parallel",)),
        name="ln_mod_inproj",
    )(x, sc, sh, ln_g.reshape(1, d), ln_b.reshape(1, d), w_perm)


def _permute_w_in(w_in):
    q_g, k_g, v_g, a_g, r_g, q_n, kv_c, kv_s, kv_w, g_n = jnp.split(w_in, np.cumsum(IN_SIZES)[:-1], axis=1)
    pad = jnp.zeros((w_in.shape[0], AUX_W - GLA_RANK - N_GATES), w_in.dtype)
    return jnp.concatenate([q_g, k_g, v_g, r_g, q_n, kv_c, kv_s, kv_w, a_g, g_n, pad], axis=1).astype(BF16)


FF_CHUNK = 256


def _out_ffn_kernel(x_ref, og_ref, on_ref, ga1_ref, sc2_ref, sh2_ref, ga2_ref, lng_ref, lnb_ref,
                    wo_ref, l1g_ref, l1b_ref, wfi_ref, wfo_ref, l2g_ref, l2b_ref, y_ref):
    x = _layer_norm(x_ref[...], lng_ref[...], lnb_ref[...])
    mix = _dot(og_ref[...], wo_ref[0:GLA_WIDTH, :]) + _dot(on_ref[...], wo_ref[GLA_WIDTH:, :])
    x1 = _layer_norm(ALPHA * x + ga1_ref[...] * mix, l1g_ref[...], l1b_ref[...])
    u2 = (x1 * (1.0 + sc2_ref[...]) + sh2_ref[...]).astype(BF16)
    ffn = jnp.zeros(x1.shape, F32)
    for c in range(D_FF // FF_CHUNK):
        lo = c * FF_CHUNK
        gate = _dot(u2, wfi_ref[:, lo:lo + FF_CHUNK])
        up = _dot(u2, wfi_ref[:, D_FF + lo:D_FF + lo + FF_CHUNK])
        f = (gate * jax.nn.sigmoid(gate) * up).astype(BF16)
        ffn = ffn + _dot(f, wfo_ref[lo:lo + FF_CHUNK, :])
    y_ref[...] = _layer_norm(ALPHA * x1 + ga2_ref[...] * ffn, l2g_ref[...], l2b_ref[...])


def _out_ffn(x, o_g, o_n, ga1, sc2, sh2, ga2, ln_in_g, ln_in_b, w_o, ln1_g, ln1_b, w_fi, w_fo, ln2_g, ln2_b,
             tm, rows_per_mod):
    t, d = x.shape
    r = ga1.shape[1]
    mod_spec = pl.BlockSpec((None, r, d), lambda i: (i // rows_per_mod, 0, 0))
    vec = lambda: pl.BlockSpec((1, d), lambda i: (0, 0))
    const = lambda shp: pl.BlockSpec(shp, lambda i: (0, 0), pipeline_mode=pl.Buffered(1))
    row = lambda a: a.reshape(1, d)
    return pl.pallas_call(
        _out_ffn_kernel,
        grid=(t // tm,),
        in_specs=[pl.BlockSpec((tm, d), lambda i: (i, 0)),
                  pl.BlockSpec((tm, GLA_WIDTH), lambda i: (i, 0)),
                  pl.BlockSpec((tm, NSA_WIDTH), lambda i: (i, 0)),
                  mod_spec, mod_spec, mod_spec, mod_spec, vec(), vec(),
                  const((d, d)), vec(), vec(), const((d, 2 * D_FF)), const((D_FF, d)), vec(), vec()],
        out_specs=pl.BlockSpec((tm, d), lambda i: (i, 0)),
        out_shape=jax.ShapeDtypeStruct((t, d), F32),
        compiler_params=_cparams(("parallel",)),
        name="outproj_ffn",
    )(x, o_g, o_n, ga1, sc2, sh2, ga2, row(ln_in_g), row(ln_in_b), w_o, row(ln1_g), row(ln1_b), w_fi, w_fo,
      row(ln2_g), row(ln2_b))


GLA_EXP_CLAMP = 80.0


def _gla_kernel(qk_ref, v_ref, r_ref, aux_ref, wa_ref, ba_ref, ng_ref, s0_ref, o_ref, sout_ref, s_scr,
                *, chunk, sub, tl, l_valid, l_pad, nbb):
    t = pl.program_id(1)
    c = chunk
    n_sub = c // sub
    hw = GLA_KW

    @pl.when(t == 0)
    def _():
        s_scr[...] = s0_ref[...]

    ri = lax.broadcasted_iota(jnp.int32, (c, c), 0)
    ci = lax.broadcasted_iota(jnp.int32, (c, c), 1)
    causal = ci <= ri
    tril = causal.astype(BF16)
    rowid = lax.broadcasted_iota(jnp.int32, (c, hw), 0)
    head_of_lane = lax.broadcasted_iota(jnp.int32, (c, hw), 1) // GLA_DK
    wa = wa_ref[...]
    wa_hi = wa.astype(BF16)
    wa_mid = (wa - wa_hi.astype(F32)).astype(BF16)
    ba = ba_ref[...]
    ng = ng_ref[...]

    def body(i, carry):
        for e in range(nbb):
            s_scr[e] = one_chunk(i, e, s_scr[e])
        return carry

    def one_chunk(i, e, s_old):
        r0 = pl.multiple_of(i * c, c)
        aux = aux_ref[e, pl.ds(r0, c), :]
        a_hi = aux.astype(BF16)
        a_mid = (aux - a_hi.astype(F32)).astype(BF16)
        z = _dot(a_hi, wa_hi) + _dot(a_mid, wa_hi) + _dot(a_hi, wa_mid) + ba
        g = jax.nn.log_sigmoid(z) / GLA_TAU
        if l_pad != l_valid:
            g = jnp.where(t * tl + r0 + rowid < l_valid, g, 0.0)
        g_hi, g_mid, g_lo = _split3(g)
        b = _dot(tril, g_hi) + _dot(tril, g_mid) + _dot(tril, g_lo)
        qk = qk_ref[e, pl.ds(r0, c), :]
        q = qk[:, :hw].astype(F32) * (GLA_DK ** -0.5)
        k = qk[:, hw:].astype(F32)
        v = v_ref[e, pl.ds(r0, c), :]
        b_last = b[c - 1:c, :]

        def heads_on_rows(x):
            return jnp.concatenate([jnp.where(head_of_lane == h, x, 0.0) for h in range(GLA_HEADS)], axis=0)

        o_inter =_dot(heads_on_rows(q * jnp.exp(b)).astype(BF16), s_old.astype(BF16))

        q_parts, k_parts = [], []
        for s_i in range(n_sub):
            b_ref = b[s_i * sub - 1:s_i * sub, :] if s_i > 0 else jnp.zeros((1, hw), F32)
            in_rows = (rowid >= s_i * sub) & (rowid < (s_i + 1) * sub)
            qt = jnp.where(in_rows, q * jnp.exp(jnp.minimum(b - b_ref, 0.0)), 0.0)
            kt = jnp.where(rowid < (s_i + 1) * sub, k * jnp.exp(jnp.minimum(b_ref - b, GLA_EXP_CLAMP)), 0.0)
            q_parts.append(heads_on_rows(qt).astype(BF16))
            k_parts.append(kt.astype(BF16))
        q_cat = jnp.concatenate(q_parts, axis=1) if n_sub > 1 else q_parts[0]
        k_cat = jnp.concatenate(k_parts, axis=1) if n_sub > 1 else k_parts[0]
        att = _dot_nt(q_cat, k_cat)

        r_t = r_ref[e, pl.ds(r0, c), :].astype(F32)
        for h in range(GLA_HEADS):
            att_h = jnp.where(causal, att[h * c:(h + 1) * c, :], 0.0).astype(BF16)
            o_h = o_inter[h * c:(h + 1) * c, :] + _dot(att_h, v[:, h * GLA_DV:(h + 1) * GLA_DV])
            o_h = o_h * lax.rsqrt(jnp.mean(o_h * o_h, axis=-1, keepdims=True) + 1e-6) * ng
            r_h = r_t[:, h * GLA_DV:(h + 1) * GLA_DV]
            o_ref[e, pl.ds(r0, c), h * GLA_DV:(h + 1) * GLA_DV] = (
                o_h * (r_h * jax.nn.sigmoid(r_h))).astype(o_ref.dtype)

        kd = jnp.concatenate([k * jnp.exp(b_last - b), jnp.zeros((LANES - c, hw), F32)], axis=0)
        kd_t = jnp.transpose(kd).astype(BF16)
        v_pad = jnp.concatenate([v, jnp.zeros((LANES - c, GLA_WIDTH), v.dtype)], axis=0)
        upd = jnp.concatenate([_dot(kd_t[h * GLA_DK:(h + 1) * GLA_DK, :], v_pad[:, h * GLA_DV:(h + 1) * GLA_DV])
                               for h in range(GLA_HEADS)], axis=0)
        decay = jnp.transpose(jnp.broadcast_to(jnp.exp(b_last), (LANES, hw)))
        return decay * s_old + upd

    lax.fori_loop(0, tl // c, body, 0)

    @pl.when(t == pl.num_programs(1) - 1)
    def _():
        sout_ref[...] = s_scr[...]


def _gla(qk, v, r, aux, w_a2p, b_a, norm_g, s0, *, chunk, sub, tl, l_valid, nbb):
    bsz, l_pad, _ = qk.shape
    nt = l_pad // tl
    kern = functools.partial(_gla_kernel, chunk=chunk, sub=sub, tl=tl, l_valid=l_valid, l_pad=l_pad, nbb=nbb)
    tile = lambda w: pl.BlockSpec((nbb, tl, w), lambda b, t: (b, t, 0))
    full = lambda shp: pl.BlockSpec(shp, lambda b, t: (0, 0))
    st = pl.BlockSpec((nbb, GLA_KW, GLA_DV), lambda b, t: (b, 0, 0))
    return pl.pallas_call(
        kern,
        grid=(bsz // nbb, nt),
        in_specs=[tile(2 * GLA_KW), tile(GLA_WIDTH), tile(GLA_WIDTH), tile(AUX_W),
                  full((AUX_W, GLA_KW)), full((1, GLA_KW)), full((1, GLA_DV)), st],
        out_specs=[tile(GLA_WIDTH), st],
        out_shape=[jax.ShapeDtypeStruct((bsz, l_pad, GLA_WIDTH), BF16),
                   jax.ShapeDtypeStruct((bsz, GLA_KW, GLA_DV), F32)],
        scratch_shapes=[pltpu.VMEM((nbb, GLA_KW, GLA_DV), F32)],
        compiler_params=_cparams(("parallel", "arbitrary")),
        name="gla_scan",
    )(qk, v, r, aux, w_a2p, b_a, norm_g, s0)


CMP_R = CMP_BLK // CMP_STRIDE


QK_SCALE = NSA_HEAD_DIM ** -0.5
LOG2E = math.log2(math.e)


def _masked_softmax(s, mask):
    s = jnp.where(mask, s, NEG)
    e = jnp.where(mask, jnp.exp(s - jnp.max(s, axis=-1, keepdims=True)), 0.0)
    return e / jnp.maximum(jnp.sum(e, axis=-1, keepdims=True), 1e-30)


def _group_queries(q, g):
    hd = NSA_HEAD_DIM
    return jnp.concatenate([q[:, (NSA_GROUP * g + r) * hd:(NSA_GROUP * g + r + 1) * hd]
                            for r in range(NSA_GROUP)], axis=0)


def _topk_rows(score_t, n_pick):
    nb, nq = score_t.shape
    rowid = lax.broadcasted_iota(jnp.int32, (nb, nq), 0)
    taken = jnp.zeros((nb, nq), jnp.int32)
    picks = []
    for _ in range(n_pick):
        free = taken == 0
        cand = jnp.where(free, score_t, -jnp.inf)
        m = jnp.max(cand, axis=0, keepdims=True)
        hit = free & (cand == m)
        idx = jnp.min(jnp.where(hit, rowid, nb), axis=0, keepdims=True)
        taken = jnp.where(rowid == idx, 1, taken)
        picks.append(idx)
    return taken.astype(F32), picks


def _importance_t(psum, ov_t):
    hi, mid, _ = _split3(psum)
    return _dot_nt(ov_t, hi) + _dot_nt(ov_t, mid)


def _overlap_t(nb, nc_pad, nc):
    j = lax.broadcasted_iota(jnp.int32, (nb, nc_pad), 0) * SLC_BLK
    i = lax.broadcasted_iota(jnp.int32, (nb, nc_pad), 1) * CMP_STRIDE
    return ((i < j + SLC_BLK) & (i + CMP_BLK > j) & (i < nc * CMP_STRIDE)).astype(BF16)


def _select_scores_t(imp_t, tq_row):
    nb, nq = imp_t.shape
    j = lax.broadcasted_iota(jnp.int32, (nb, nq), 0)
    cur = tq_row // SLC_BLK
    forced = (j == 0) | (j == cur) | (j == cur - 1)
    return jnp.where(j * SLC_BLK <= tq_row, imp_t + FORCE_BONUS * forced.astype(F32), -jnp.inf)


NSA_TQ = 256
NSA_TK = 512


SEGS_PER_PAGE = PAGE_SIZE // CMP_STRIDE
CMP_PAGES_PER_STEP = 64
CMP_ROW_CHUNK = 256
STAGE_PITCH = 24


SQ_ROWS = SUBLANES


C_W = NSA_KV_HEADS * NSA_HEAD_DIM
HC_W = NSA_KV_HEADS * CMP_HIDDEN
PC_W = CMP_R * HC_W
MXU_DEPTH = 256
CMP_S_PER_DOT = MXU_DEPTH // C_W


def _cmp_weights_fm(cmp_pe, cmp_w1, cmp_w2):
    eye = jnp.eye(NSA_KV_HEADS, dtype=F32)
    w1r = cmp_w1.reshape(2, CMP_R, CMP_STRIDE, NSA_HEAD_DIM, CMP_HIDDEN)
    w1c = jnp.einsum('cmsdh,gG->csgdmGh', w1r, eye).reshape(2, CMP_STRIDE * C_W, PC_W).astype(BF16)
    w2c_t = jnp.einsum('chd,gG->cGdgh', cmp_w2, eye).reshape(2, C_W, HC_W).astype(BF16)
    pe_flat = jnp.transpose(cmp_pe, (1, 0, 2)).reshape(2, CMP_BLK * NSA_HEAD_DIM)
    return w1c, w2c_t, pe_flat


def _cmp_bias_c(pe_ref, w1_ref, b1_ref, c):
    pe = jnp.broadcast_to(pe_ref[c:c + 1, :], (SUBLANES, pe_ref.shape[1]))
    w1c = w1_ref[c]
    pe_hi = pe.astype(BF16)
    pe_mid = (pe - pe_hi.astype(F32)).astype(BF16)
    w_hi = w1c.astype(BF16)
    w_mid = (w1c - w_hi.astype(F32)).astype(BF16)
    pb = _dot(pe_hi, w_hi) + _dot(pe_mid, w_hi) + _dot(pe_hi, w_mid)
    bc = pb[0:1, :] + b1_ref[c:c + 1, :]
    return jnp.concatenate([bc] * NSA_KV_HEADS, axis=1)


def _cmp_first_layer(x_ref, c, row0, n_seg, w1c_ref, pitch=CMP_STRIDE):
    acc = None
    for s in range(0, CMP_STRIDE, CMP_S_PER_DOT):
        xs = jnp.concatenate([x_ref[c, pl.ds(row0 + s + j, n_seg, stride=pitch), :].astype(BF16)
                              for j in range(CMP_S_PER_DOT)], axis=1)
        d = _dot(xs, w1c_ref[c, s * C_W:(s + CMP_S_PER_DOT) * C_W, :])
        acc = d if acc is None else acc + d
    return acc


def _cmp_second_layer_fm(p, bias, w2t):
    n = p.shape[0]
    h = p[:, :HC_W] + pltpu.roll(p[:, HC_W:], n - 1, 0) + bias
    return _dot_nt(w2t, jax.nn.gelu(h).astype(BF16))


def _cmp_prompt_fm_kernel(x_ref, w1c_ref, pe_ref, w1_ref, b1_ref, w2t_ref, o_ref):
    n_seg = o_ref.shape[1]
    for c in range(2):
        p = _cmp_first_layer(x_ref, c, 0, n_seg, w1c_ref)
        o_ref[c * C_W:(c + 1) * C_W, :] = _cmp_second_layer_fm(p, _cmp_bias_c(pe_ref, w1_ref, b1_ref, c), w2t_ref[c])


def _cmp_prompt_fm(x_tok, seq, w1c, pe_flat, cmp_w1, cmp_b1, w2c_t):
    bsz = x_tok.shape[1] // seq
    n_seg = seq // CMP_STRIDE
    const = lambda a: pl.BlockSpec(a.shape, lambda b: (0,) * a.ndim)
    return pl.pallas_call(
        _cmp_prompt_fm_kernel,
        grid=(bsz,),
        in_specs=[pl.BlockSpec((2, seq, C_W), lambda b: (0, b, 0)),
                  const(w1c), const(pe_flat), const(cmp_w1), const(cmp_b1), const(w2c_t)],
        out_specs=pl.BlockSpec((None, KV_W, n_seg), lambda b: (b, 0, 0)),
        out_shape=jax.ShapeDtypeStruct((bsz, KV_W, n_seg), F32),
        compiler_params=_cparams(("parallel",)),
        name="nsa_compress_prompt",
    )(x_tok, w1c, pe_flat, cmp_w1, cmp_b1, w2c_t)


def _scaled_group_queries(q, g):
    return (_group_queries(q, g).astype(F32) * QK_SCALE).astype(BF16)


def _cmp_branch_fm(qg, cmp_ref, g, valid):
    hd = NSA_HEAD_DIM
    kl, vl = g * hd, (NSA_KV_HEADS + g) * hd
    s_c = _dot(qg, cmp_ref[kl:kl + hd, :].astype(BF16))
    p_c = _masked_softmax(s_c, valid)
    return _dot_nt(p_c.astype(BF16), cmp_ref[vl:vl + hd, :].astype(BF16)), p_c


def _sum_heads(p, rows):
    out = p[0:rows]
    for r in range(1, NSA_GROUP):
        out = out + p[r * rows:(r + 1) * rows]
    return out


ATT_RB = 32


def _nsa_prompt_fm_kernel(q_ref, aux_ref, cmp_ref, kvs_ref, kvw_ref, o_ref,
                          s_scr, bias_scr, e_scr, m_scr, corr_scr, acc_scr, *, seq, nc):
    tq_n, tk_n, hd, grp = NSA_TQ, NSA_TK, NSA_HEAD_DIM, NSA_GROUP
    m_rows = grp * tq_n
    q0 = pl.program_id(1) * tq_n
    q_all = q_ref[...]
    gates = jax.nn.sigmoid(aux_ref[...])
    n_cmp = cmp_ref.shape[1]
    nb = seq // SLC_BLK

    tq_col = q0 + lax.broadcasted_iota(jnp.int32, (tq_n, 1), 0)
    tq_rows = jnp.concatenate([tq_col] * grp, axis=0)
    tq_lane = q0 + lax.broadcasted_iota(jnp.int32, (1, NSA_KV_HEADS * tq_n), 1) % tq_n
    t_end = lax.broadcasted_iota(jnp.int32, (1, n_cmp), 1) * CMP_STRIDE + (CMP_BLK - 1)
    ov_t = _overlap_t(nb, n_cmp, nc)
    n_chunks = (q0 + tq_n + tk_n - 1) // tk_n
    w_len = WINDOW + tq_n
    w0 = pl.multiple_of(jnp.maximum(q0 - WINDOW, 0), tq_n)

    qgs = [_scaled_group_queries(q_all, g) for g in range(NSA_KV_HEADS)]
    o_cs, p_cs = [], []
    for g in range(NSA_KV_HEADS):
        o_c, p_c = _cmp_branch_fm(qgs[g], cmp_ref, g, t_end <= tq_rows)
        o_cs.append(o_c)
        p_cs.append(p_c)
    n_pick = min(SLC_TOP_N, nb)

    def ranked_blocks():
        imps = [_importance_t(_sum_heads(p_c, tq_n), ov_t) for p_c in p_cs]
        return _topk_rows(_select_scores_t(jnp.concatenate(imps, axis=1), tq_lane), n_pick)[0]

    def all_visible_blocks():
        j = lax.broadcasted_iota(jnp.int32, (nb, NSA_KV_HEADS * tq_n), 0)
        return (j * SLC_BLK <= tq_lane).astype(F32)

    sel_all = lax.cond(q0 + tq_n <= n_pick * SLC_BLK, all_visible_blocks, ranked_blocks)

    tw = w0 + lax.broadcasted_iota(jnp.int32, (1, w_len), 1)
    bias_w = jnp.where((tw <= tq_col) & (tw > tq_col - WINDOW), 0.0, NEG)

    groups = range(NSA_KV_HEADS)

    def exp_rows(g, bias_g, width, online):
        per_head = tq_n // ATT_RB
        for i in range(m_rows // ATT_RB):
            rows = slice(i * ATT_RB, (i + 1) * ATT_RB)
            brows = slice((i % per_head) * ATT_RB, (i % per_head + 1) * ATT_RB)
            s = s_scr[g, rows, :width] + bias_scr[bias_g, brows, :width]
            m_new = jnp.max(s, axis=-1, keepdims=True)
            if online:
                m_old = m_scr[g, rows, :]
                m_new = jnp.maximum(m_old, m_new)
                corr_scr[g, rows, :] = jnp.exp2(m_old - m_new)
                m_scr[g, rows, :] = m_new
            e_scr[g, rows, :width] = jnp.exp2(s - m_new).astype(BF16)

    def keys_log2(ref, g, cols):
        return (ref[g * hd:(g + 1) * hd, cols] * LOG2E).astype(BF16)

    def gate_lane(g, r, branch):
        return hd + GATE_OFF + 3 * (grp * g + r) + branch

    def values_with_ones(ref, g, cols, branch):
        v = ref[(NSA_KV_HEADS + g) * hd:(NSA_KV_HEADS + g + 1) * hd, cols].astype(BF16)
        row = lax.broadcasted_iota(jnp.int32, v.shape, 0) + hd
        ones = functools.reduce(jnp.logical_or, [row == gate_lane(g, r, branch) for r in range(grp)])
        return jnp.concatenate([v, ones.astype(BF16)], axis=0)

    m_scr[...] = jnp.full(m_scr.shape, NEG, F32)
    acc_scr[...] = jnp.zeros(acc_scr.shape, F32)
    sel_ts = [sel_all[:, g * tq_n:(g + 1) * tq_n].astype(BF16) for g in groups]

    def slc_chunk(ci, carry):
        k0 = pl.multiple_of(ci * tk_n, tk_n)
        cols = pl.ds(k0, tk_n)
        kpos = k0 + lax.broadcasted_iota(jnp.int32, (1, tk_n), 1)
        expand = (lax.broadcasted_iota(jnp.int32, (nb, tk_n), 0) == kpos // SLC_BLK).astype(BF16)
        for g in groups:
            sel_k = _dot_tn(sel_ts[g], expand)
            bias_scr[g, :, :tk_n] = jnp.where((sel_k > 0.5) & (kpos <= tq_col), 0.0, NEG)
            s_scr[g, :, :tk_n] = _dot(qgs[g], keys_log2(kvs_ref, g, cols))
        for g in groups:
            exp_rows(g, g, tk_n, True)
        for g in groups:
            acc_scr[g] = corr_scr[g] * acc_scr[g] + _dot_nt(e_scr[g, :, :tk_n], values_with_ones(kvs_ref, g, cols, 1))
        return carry

    lax.fori_loop(0, n_chunks, slc_chunk, 0)
    wcols = pl.ds(w0, w_len)
    bias_scr[0, :, :w_len] = bias_w
    for g in groups:
        s_scr[g, :, :w_len] = _dot(qgs[g], keys_log2(kvw_ref, g, wcols))
    for g in groups:
        exp_rows(g, 0, w_len, False)
    acc_ws = [_dot_nt(e_scr[g, :, :w_len], values_with_ones(kvw_ref, g, wcols, 2)) for g in groups]
    gates_rot = pltpu.roll(gates, hd, 1)
    for g in groups:
        for r in range(grp):
            h = grp * g + r
            rows = slice(r * tq_n, (r + 1) * tq_n)
            acc_s, acc_w = acc_scr[g, rows, :], acc_ws[g][rows]
            f_s = gates_rot / jnp.maximum(acc_s, 1e-30)
            f_w = gates_rot / jnp.maximum(acc_w, 1e-30)
            ls, lw = gate_lane(g, r, 1), gate_lane(g, r, 2)
            gc = GATE_OFF + 3 * h
            o = (gates[:, gc:gc + 1] * o_cs[g][rows] + f_s[:, ls:ls + 1] * acc_s[:, :hd]
                 + f_w[:, lw:lw + 1] * acc_w[:, :hd])
            o_ref[:, h * hd:(h + 1) * hd] = o.astype(o_ref.dtype)


def _nsa_prompt_fm(q, aux, cmp_t, kvs_t, kvw_t):
    bsz, seq, _ = q.shape
    n_seg = cmp_t.shape[2]
    kern = functools.partial(_nsa_prompt_fm_kernel, seq=seq, nc=n_seg - CMP_R + 1)
    tile = lambda w: pl.BlockSpec((None, NSA_TQ, w), lambda b, t: (b, t, 0))
    whole = lambda n: pl.BlockSpec((None, KV_W, n), lambda b, t: (b, 0, 0))
    m_rows = NSA_GROUP * NSA_TQ
    width = max(NSA_TK, WINDOW + NSA_TQ)
    ng = NSA_KV_HEADS
    return pl.pallas_call(
        kern,
        grid=(bsz, seq // NSA_TQ),
        in_specs=[tile(NSA_WIDTH), tile(AUX_W), whole(n_seg), whole(seq), whole(seq)],
        out_specs=tile(NSA_WIDTH),
        out_shape=jax.ShapeDtypeStruct((bsz, seq, NSA_WIDTH), BF16),
        scratch_shapes=[pltpu.VMEM((ng, m_rows, width), F32),
                        pltpu.VMEM((ng, NSA_TQ, width), F32),
                        pltpu.VMEM((ng, m_rows, width), BF16),
                        pltpu.VMEM((ng, m_rows, 1), F32),
                        pltpu.VMEM((ng, m_rows, 1), F32),
                        pltpu.VMEM((ng, m_rows, 2 * NSA_HEAD_DIM), F32)],
        compiler_params=_cparams(("parallel", "arbitrary")),
        name="nsa_attn_prompt",
    )(q, aux, cmp_t, kvs_t, kvw_t)


def _cmp_sample_fm_kernel(pt_ref, cache_ref, w1f_ref, pe_ref, w1_ref, b1_ref, w2t_ref, o_ref,
                          xbuf, stage_a, stage_b, p_scr, bias_scr, sem, *, steps_per_batch):
    b = pl.program_id(0)
    h = pl.program_id(1)
    step = b * steps_per_batch + h
    n_steps = pl.num_programs(0) * steps_per_batch
    pps = CMP_PAGES_PER_STEP
    segs = pps * SEGS_PER_PAGE

    def page_copy(bb, hh, p, slot):
        return pltpu.make_async_copy(cache_ref.at[pt_ref[bb, hh * pps + p]], xbuf.at[slot, p], sem.at[slot])

    def start_fetch(bb, hh, slot):
        for p in range(pps):
            page_copy(bb, hh, p, slot).start(priority=p % 2)

    @pl.when(step == 0)
    def _():
        start_fetch(b, h, 0)

    @pl.when(step + 1 < n_steps)
    def _():
        wrap = h + 1 == steps_per_batch
        start_fetch(jnp.where(wrap, b + 1, b), jnp.where(wrap, 0, h + 1), (step + 1) % 2)

    slot = step % 2
    for p in range(pps):
        page_copy(b, h, p, slot).wait()

    pages_per_chunk = CMP_ROW_CHUNK // SEGS_PER_PAGE
    n_chunks = segs // CMP_ROW_CHUNK
    stages = (stage_a, stage_b)
    assert n_chunks == len(stages)

    def transpose_page(rc, lp):
        p = rc * pages_per_chunk + lp
        for c in range(2):
            tok = jnp.transpose(xbuf[slot, p, c * C_W:(c + 1) * C_W, :].astype(BF16)).astype(F32)
            for n in range(SEGS_PER_PAGE):
                r0 = (lp * SEGS_PER_PAGE + n) * STAGE_PITCH
                stages[rc][c, r0:r0 + CMP_STRIDE, :] = tok[n * CMP_STRIDE:(n + 1) * CMP_STRIDE]

    def first_layer(rc, between=()):
        between = list(between)
        dots = [(c, s) for c in range(2) for s in range(0, CMP_STRIDE, CMP_S_PER_DOT)]
        per_dot = -(-len(between) // len(dots))
        accs = [None, None]
        for c, s in dots:
            xs = jnp.concatenate([stages[rc][c, pl.ds(s + j, CMP_ROW_CHUNK, stride=STAGE_PITCH), :].astype(BF16)
                                  for j in range(CMP_S_PER_DOT)], axis=1)
            d = _dot(xs, w1f_ref[c, s * C_W:(s + CMP_S_PER_DOT) * C_W, :])
            accs[c] = d if accs[c] is None else accs[c] + d
            for thunk in between[:per_dot]:
                thunk()
            between = between[per_dot:]
        for c in range(2):
            r0 = pl.multiple_of(h * segs + rc * CMP_ROW_CHUNK, CMP_ROW_CHUNK)
            p_scr[c, pl.ds(r0, CMP_ROW_CHUNK), :] = accs[c]

    for lp in range(pages_per_chunk):
        transpose_page(0, lp)
    for rc in range(n_chunks):
        nxt = [functools.partial(transpose_page, rc + 1, lp) for lp in range(pages_per_chunk)] if rc + 1 < n_chunks else []
        first_layer(rc, nxt)

    @pl.when(step == 0)
    def _():
        for c in range(2):
            bias_scr[c] = jnp.broadcast_to(_cmp_bias_c(pe_ref, w1_ref, b1_ref, c), (SUBLANES, HC_W))

    @pl.when(h == steps_per_batch - 1)
    def _():
        for c in range(2):
            o_ref[c * C_W:(c + 1) * C_W, :] = _cmp_second_layer_fm(p_scr[c], bias_scr[c, 0:1, :], w2t_ref[c])


def _cmp_sample_fm(page_table, cache_fm, w1_full, pe_flat, cmp_w1, cmp_b1, w2_bd_t):
    bsz, n_pages = page_table.shape
    steps = n_pages // CMP_PAGES_PER_STEP
    n_seg = n_pages * SEGS_PER_PAGE
    const = lambda a: pl.BlockSpec(a.shape, lambda b, h, pt: (0,) * a.ndim, pipeline_mode=pl.Buffered(1))
    grid_spec = pltpu.PrefetchScalarGridSpec(
        num_scalar_prefetch=1,
        grid=(bsz, steps),
        in_specs=[pl.BlockSpec(memory_space=pl.ANY), const(w1_full), const(pe_flat), const(cmp_w1),
                  const(cmp_b1), const(w2_bd_t)],
        out_specs=pl.BlockSpec((None, KV_W, n_seg), lambda b, h, pt: (b, 0, 0)),
        scratch_shapes=[pltpu.VMEM((2, CMP_PAGES_PER_STEP, KV_W, PAGE_SIZE), F32),
                        pltpu.VMEM((2, CMP_ROW_CHUNK * STAGE_PITCH, C_W), F32),
                        pltpu.VMEM((2, CMP_ROW_CHUNK * STAGE_PITCH, C_W), F32),
                        pltpu.VMEM((2, n_seg, PC_W), F32),
                        pltpu.VMEM((2, SUBLANES, HC_W), F32),
                        pltpu.SemaphoreType.DMA((2,))],
    )
    return pl.pallas_call(
        functools.partial(_cmp_sample_fm_kernel, steps_per_batch=steps),
        grid_spec=grid_spec,
        out_shape=jax.ShapeDtypeStruct((bsz, KV_W, n_seg), F32),
        compiler_params=_cparams(("arbitrary", "arbitrary")),
        name="nsa_compress_sample",
    )(page_table, cache_fm, w1_full, pe_flat, cmp_w1, cmp_b1, w2_bd_t)


SEL_COLS = NSA_KV_HEADS * SQ_ROWS
SEL_NBB = LANES // SEL_COLS


def _nsa_sample_select_fm_kernel(q_ref, cmp_ref, oc_ref, idx_ref, *, past_len, nb, nb_pad, nc):
    hd, grp = NSA_HEAD_DIM, NSA_GROUP
    m_rows = grp * SQ_ROWS
    n_cmp = cmp_ref.shape[2]
    tq_rows = past_len + lax.broadcasted_iota(jnp.int32, (m_rows, 1), 0) % SQ_ROWS
    tq_lane = past_len + lax.broadcasted_iota(jnp.int32, (1, LANES), 1) % SQ_ROWS
    t_end = lax.broadcasted_iota(jnp.int32, (1, n_cmp), 1) * CMP_STRIDE + (CMP_BLK - 1)
    ov_t = _overlap_t(nb_pad, n_cmp, nc)
    psums = []
    for e in range(SEL_NBB):
        q_all = q_ref[e]
        for g in range(NSA_KV_HEADS):
            o_c, p_c = _cmp_branch_fm(_scaled_group_queries(q_all, g), cmp_ref.at[e], g, t_end <= tq_rows)
            psums.append(_sum_heads(p_c, SQ_ROWS))
            for r in range(grp):
                h = grp * g + r
                oc_ref[e, :, h * hd:(h + 1) * hd] = o_c[r * SQ_ROWS:(r + 1) * SQ_ROWS]
    score_t = _select_scores_t(_importance_t(jnp.concatenate(psums, axis=0), ov_t), tq_lane)
    rowid = lax.broadcasted_iota(jnp.int32, score_t.shape, 0)
    _, picks = _topk_rows(jnp.where(rowid < nb, score_t, -jnp.inf), min(SLC_TOP_N, nb))
    idx_ref[...] = jnp.concatenate(picks, axis=0)


def _nsa_sample_select_fm(q_pad, cmp_t, past_len, seq_new):
    bsz = q_pad.shape[0]
    n_seg = cmp_t.shape[2]
    nb = -(-(past_len + seq_new) // SLC_BLK)
    nb_pad = -(-nb // SUBLANES) * SUBLANES
    n_pick = min(SLC_TOP_N, nb)
    kern = functools.partial(_nsa_sample_select_fm_kernel, past_len=past_len, nb=nb, nb_pad=nb_pad,
                             nc=n_seg - CMP_R + 1)
    return pl.pallas_call(
        kern,
        grid=(bsz // SEL_NBB,),
        in_specs=[pl.BlockSpec((SEL_NBB, SQ_ROWS, NSA_WIDTH), lambda b: (b, 0, 0)),
                  pl.BlockSpec((SEL_NBB, KV_W, n_seg), lambda b: (b, 0, 0))],
        out_specs=[pl.BlockSpec((SEL_NBB, SQ_ROWS, NSA_WIDTH), lambda b: (b, 0, 0)),
                   pl.BlockSpec((None, n_pick, LANES), lambda b: (b, 0, 0))],
        out_shape=[jax.ShapeDtypeStruct((bsz, SQ_ROWS, NSA_WIDTH), F32),
                   jax.ShapeDtypeStruct((bsz // SEL_NBB, n_pick, LANES), jnp.int32)],
        compiler_params=_cparams(("parallel",)),
        name="nsa_select_sample",
    )(q_pad, cmp_t)


ATTEND_SLOTS = 2


def _nsa_sample_attend_fm_kernel(page_ref, idx_ref, q_ref, gate_ref, oc_ref, cache_ref, tail_ref, winp_ref,
                                 winn_ref, o_ref, kvbuf, sem, *, past_len, seq_new, n_pick):
    b = pl.program_id(0)
    nbatch = pl.num_programs(0)
    hd, grp = NSA_HEAD_DIM, NSA_GROUP
    n_items = seq_new * NSA_KV_HEADS * n_pick
    n_past_blocks = past_len // SLC_BLK
    blocks_per_page = PAGE_SIZE // SLC_BLK

    def block_id(bb, item):
        return jnp.minimum(idx_ref[bb * n_items + item], n_past_blocks)

    def copy(bb, item, slot):
        g = (item // n_pick) % NSA_KV_HEADS
        lanes = slice((item % n_pick) * PAGE_SIZE, (item % n_pick + 1) * PAGE_SIZE)
        rows = slice(g * hd, (g + 1) * hd)
        page = page_ref[bb * n_items + item]
        return pltpu.make_async_copy(cache_ref.at[page, :, rows, :], kvbuf.at[slot, item // n_pick, :, :, lanes],
                                     sem.at[slot])

    def start_fetch(bb, slot):
        for item in range(n_items):
            copy(bb, item, slot).start(priority=item % 2)

    @pl.when(b == 0)
    def _():
        for ahead in range(ATTEND_SLOTS - 1):
            @pl.when(ahead < nbatch)
            def _():
                start_fetch(ahead, ahead)

    nxt = b + ATTEND_SLOTS - 1

    @pl.when(nxt < nbatch)
    def _():
        start_fetch(nxt, nxt % ATTEND_SLOTS)

    slot = b % ATTEND_SLOTS

    for item in range(n_items):
        copy(b, item, slot).wait()

    head_row = lax.broadcasted_iota(jnp.int32, (NSA_HEADS, 1), 0)
    tok = lax.broadcasted_iota(jnp.int32, (1, PAGE_SIZE), 1)
    key_lane = lax.broadcasted_iota(jnp.int32, (1, n_pick * PAGE_SIZE), 1)
    wb, wn = winp_ref.shape[1], winn_ref.shape[1]
    tw_p = past_len - wb + lax.broadcasted_iota(jnp.int32, (1, wb), 1)
    tw_n = past_len + lax.broadcasted_iota(jnp.int32, (1, wn), 1)

    def heads16(x):
        return jnp.concatenate([x.astype(BF16), jnp.zeros_like(x, dtype=BF16)], axis=0)

    def merge_groups(per_group):
        out = per_group[0]
        for g in range(1, NSA_KV_HEADS):
            out = jnp.where(head_row >= g * grp, per_group[g], out)
        return out

    for qi in range(seq_new):
        tq = past_len + qi
        q16 = heads16(q_ref[qi] * QK_SCALE)
        gates = jax.nn.sigmoid(gate_ref[qi])
        o_s_g, o_w_g = [], []
        for g in range(NSA_KV_HEADS):
            qg_i = qi * NSA_KV_HEADS + g
            kl = g * hd
            blk = jnp.zeros_like(key_lane)
            for kk in range(n_pick):
                blk = jnp.where(key_lane // PAGE_SIZE == kk, block_id(b, qg_i * n_pick + kk), blk)
            new_picked = jnp.max(blk, axis=1, keepdims=True) == n_past_blocks
            in_block = (key_lane % PAGE_SIZE) // SLC_BLK == blk % blocks_per_page
            mask = jnp.concatenate(
                [(blk < n_past_blocks) & in_block & (blk * SLC_BLK + key_lane % SLC_BLK <= tq),
                 new_picked & (tok < SLC_BLK) & (past_len + tok <= tq)], axis=1)
            kcat = jnp.concatenate([kvbuf[slot, qg_i, 0].astype(BF16), tail_ref[0, kl:kl + hd, :].astype(BF16)], axis=1)
            vcat = jnp.concatenate([kvbuf[slot, qg_i, 1].astype(BF16), tail_ref[1, kl:kl + hd, :].astype(BF16)], axis=1)
            p = _masked_softmax(_dot(q16, kcat)[:NSA_HEADS], mask)
            o_s_g.append(_dot_nt(heads16(p), vcat)[:NSA_HEADS])
            vl = (NSA_KV_HEADS + g) * hd
            bias_p = jnp.where((tw_p <= tq) & (tw_p > tq - WINDOW) & (tw_p >= 0), 0.0, NEG)
            bias_n = jnp.where((tw_n <= tq) & (tw_n > tq - WINDOW), 0.0, NEG)
            s_p = _dot(q16, winp_ref[kl:kl + hd, :].astype(BF16))[:NSA_HEADS] + bias_p
            s_n = _dot(q16, winn_ref[kl:kl + hd, :].astype(BF16))[:NSA_HEADS] + bias_n
            m = jnp.maximum(jnp.max(s_p, axis=-1, keepdims=True), jnp.max(s_n, axis=-1, keepdims=True))
            e_p, e_n = jnp.exp(s_p - m), jnp.exp(s_n - m)
            den = jnp.maximum(jnp.sum(e_p, axis=-1, keepdims=True) + jnp.sum(e_n, axis=-1, keepdims=True), 1e-30)
            o_w_g.append((_dot_nt(heads16(e_p), winp_ref[vl:vl + hd, :].astype(BF16))
                          + _dot_nt(heads16(e_n), winn_ref[vl:vl + hd, :].astype(BF16)))[:NSA_HEADS] / den)
        o_ref[qi] = (gates[:, 0:1] * oc_ref[qi] + gates[:, 1:2] * merge_groups(o_s_g)
                     + gates[:, 2:3] * merge_groups(o_w_g))


def _nsa_sample_attend_fm(page_table, idx_flat, q_heads, gate_logits, o_c, cache_fm, tail_fm, win_past, win_new,
                          past_len, seq_new, n_pick):
    bsz = q_heads.shape[0]
    wb, wn = win_past.shape[2], win_new.shape[2]
    past_blk = jnp.minimum(idx_flat.reshape(bsz, -1), past_len // SLC_BLK - 1)
    page_flat = jnp.take_along_axis(page_table, past_blk // (PAGE_SIZE // SLC_BLK), axis=1).reshape(-1)
    kern = functools.partial(_nsa_sample_attend_fm_kernel, past_len=past_len, seq_new=seq_new, n_pick=n_pick)
    per_b = lambda n, w: pl.BlockSpec((None, n, w), lambda b, pt, ix: (b, 0, 0))
    per_bq = lambda w: pl.BlockSpec((None, seq_new, NSA_HEADS, w), lambda b, pt, ix: (b, 0, 0, 0))
    n_qg = seq_new * NSA_KV_HEADS
    buf = pltpu.VMEM((ATTEND_SLOTS, n_qg, 2, NSA_HEAD_DIM, n_pick * PAGE_SIZE), F32)
    grid_spec = pltpu.PrefetchScalarGridSpec(
        num_scalar_prefetch=2,
        grid=(bsz,),
        in_specs=[per_bq(NSA_HEAD_DIM), per_bq(LANES), per_bq(NSA_HEAD_DIM),
                  pl.BlockSpec(memory_space=pl.ANY),
                  pl.BlockSpec((None, 2, C_W, LANES), lambda b, pt, ix: (b, 0, 0, 0)),
                  per_b(KV_W, wb), per_b(KV_W, wn)],
        out_specs=per_bq(NSA_HEAD_DIM),
        scratch_shapes=[buf, pltpu.SemaphoreType.DMA((ATTEND_SLOTS,))],
    )
    return pl.pallas_call(
        kern,
        grid_spec=grid_spec,
        out_shape=jax.ShapeDtypeStruct((bsz, seq_new, NSA_HEADS, NSA_HEAD_DIM), F32),
        compiler_params=_cparams(("arbitrary",)),
        name="nsa_attend_sample",
    )(page_flat, idx_flat, q_heads, gate_logits, o_c, cache_fm, tail_fm, win_past, win_new)


PROMPT_TM = 512
FFN_TM = 512
GLA_TL = 512
SAMPLE_GLA_ROWS = 16
GLA_PROMPT_NBB = 8
GLA_SAMPLE_NBB = 8


def _pad_rows(x, n):
    return jnp.pad(x, ((0, 0), (0, n - x.shape[1]), (0, 0)))


def kernel(x_prompt, x_sample, state_gla, cache_cmp_kv, cache_slc_kv, cache_win_kv, page_table, c_prompt,
           c_sample, ln_in_g, ln_in_b, w_ada, b_ada, w_in, gla_w_a2, gla_b_a, gla_norm_g, cmp_pe, cmp_w1,
           cmp_b1, cmp_w2, w_o, ln1_g, ln1_b, w_ffn_in, w_ffn_out, ln2_g, ln2_b):
    assert w_in.shape[0] == DEPTH == 1
    l = 0
    bp, lp, d = x_prompt.shape
    bs, ls, _ = x_sample.shape
    n_pool = cache_cmp_kv.shape[1]
    n_pages = page_table.shape[1]
    past_len = n_pages * PAGE_SIZE
    wb = cache_win_kv.shape[2]
    assert ((past_len + ls) // CMP_STRIDE) * CMP_STRIDE <= past_len and past_len % SLC_BLK == 0
    assert ls <= SQ_ROWS and ls <= SLC_BLK and wb == WINDOW and bs % SEL_NBB == 0

    w_perm = _permute_w_in(w_in[l])
    w_o_b, w_fi_b, w_fo_b = w_o[l].astype(BF16), w_ffn_in[l].astype(BF16), w_ffn_out[l].astype(BF16)
    w_a2p = jnp.zeros((AUX_W, GLA_KW), F32).at[:GLA_RANK].set(gla_w_a2[l])
    b_a = gla_b_a[l].reshape(1, GLA_KW)
    norm_g = gla_norm_g[l].reshape(1, GLA_DV)
    w1c, w2c_t, pe_flat = _cmp_weights_fm(cmp_pe[l], cmp_w1[l], cmp_w2[l])
    kvt = (2, NSA_KV_HEADS, NSA_HEAD_DIM)

    def fm_view(a):
        return jnp.transpose(a, (0, 2, 3, 4, 1)).reshape(a.shape[0], KV_W, a.shape[1])

    def tok_view(a_fm):
        n, _, t = a_fm.shape
        return jnp.transpose(a_fm.reshape((n,) + kvt + (t,)), (0, 4, 1, 2, 3))[None]

    mod = _ada(jnp.concatenate([c_prompt, c_sample], axis=0), w_ada[l], b_ada[l])
    mods_p = [m[:bp, None, :] for m in jnp.split(mod, 6, axis=-1)]
    mods_s = [jnp.repeat(m[bp:], ls, axis=0)[None] for m in jnp.split(mod, 6, axis=-1)]

    def out_ffn(x2d, o_g, o_n, mods, tm, rpm):
        sh1, sc1, ga1, sh2, sc2, ga2 = mods
        return _out_ffn(x2d, o_g, o_n, ga1, sc2, sh2, ga2, ln_in_g, ln_in_b, w_o_b, ln1_g[l], ln1_b[l], w_fi_b,
                        w_fo_b, ln2_g[l], ln2_b[l], tm, rpm)

    xp2 = x_prompt.reshape(bp * lp, d)
    rpm = lp // PROMPT_TM
    qk, v, r, qn, kvc, aux, kvc_t, kvs_t, kvw_t = _inproj(
        xp2, mods_p[1], mods_p[0], ln_in_g, ln_in_b, w_perm, PROMPT_TM, rpm, seq_per_batch=lp)
    b3 = lambda a: a.reshape(bp, lp, a.shape[-1])
    o_g, s_p = _gla(b3(qk), b3(v), b3(r), b3(aux), w_a2p, b_a, norm_g, jnp.zeros((bp, GLA_KW, GLA_DV), F32),
                    chunk=GLA_CHUNK, sub=GLA_SUB, tl=GLA_TL, l_valid=lp, nbb=GLA_PROMPT_NBB)
    cmp_t_p = _cmp_prompt_fm(kvc, lp, w1c, pe_flat, cmp_w1[l], cmp_b1[l], w2c_t)
    o_n = _nsa_prompt_fm(b3(qn), b3(aux), cmp_t_p, kvs_t, kvw_t)
    y_p = out_ffn(xp2, o_g.reshape(bp * lp, GLA_WIDTH), o_n.reshape(bp * lp, NSA_WIDTH), mods_p, FFN_TM,
                  lp // FFN_TM)
    w_keep = min(WINDOW, lp)
    outs_p = (y_p.reshape(bp, lp, d), s_p.reshape(1, bp, GLA_HEADS, GLA_DK, GLA_DV),
              tok_view(kvc_t), tok_view(kvs_t), tok_view(kvw_t[:, :, lp - w_keep:]))

    ts = bs * ls
    xs2 = x_sample.reshape(ts, d)
    qk, v, r, qn, kvc, aux, kvs, kvw = _inproj(xs2, mods_s[1], mods_s[0], ln_in_g, ln_in_b, w_perm, ts, 1)
    s3 = lambda a: a.reshape(bs, ls, a.shape[-1])
    g16 = lambda a: _pad_rows(s3(a), SAMPLE_GLA_ROWS)
    o_g, s_s = _gla(g16(qk), g16(v), g16(r), g16(aux), w_a2p, b_a, norm_g, state_gla[l].reshape(bs, GLA_KW, GLA_DV),
                    chunk=SAMPLE_GLA_ROWS, sub=SAMPLE_GLA_ROWS, tl=SAMPLE_GLA_ROWS, l_valid=ls,
                    nbb=GLA_SAMPLE_NBB)
    o_g = o_g[:, :ls].reshape(ts, GLA_WIDTH)
    kvc = jnp.concatenate([kvc[0], kvc[1]], axis=1)
    cmp_t_s = _cmp_sample_fm(page_table, fm_view(cache_cmp_kv[l]), w1c, pe_flat, cmp_w1[l], cmp_b1[l], w2c_t)
    q_pad = _pad_rows(s3(qn), SQ_ROWS)
    o_c, idx = _nsa_sample_select_fm(q_pad, cmp_t_s, past_len, ls)
    n_pick = idx.shape[1]
    idx = idx.reshape(bs // SEL_NBB, n_pick, SEL_NBB, NSA_KV_HEADS, SQ_ROWS)[..., :ls]
    idx_flat = jnp.transpose(idx, (0, 2, 4, 3, 1)).reshape(-1)
    new_fm =lambda a: jnp.pad(jnp.transpose(s3(a), (0, 2, 1)), ((0, 0), (0, 0), (0, LANES - ls)))
    win_past = fm_view(cache_win_kv[l])
    per_head = lambda a, w: a.reshape(bs, -1, NSA_HEADS, w)[:, :ls]
    gate_logits = jnp.pad(per_head(aux[:, GATE_OFF:GATE_OFF + N_GATES], 3), ((0, 0),) * 3 + ((0, LANES - 3),))
    o_n = _nsa_sample_attend_fm(page_table, idx_flat, per_head(qn.astype(F32), NSA_HEAD_DIM), gate_logits,
                                per_head(o_c, NSA_HEAD_DIM),
                                fm_view(cache_slc_kv[l]).reshape(n_pool, 2, C_W, PAGE_SIZE),
                                new_fm(kvs).reshape(bs, 2, C_W, LANES), win_past, new_fm(kvw),
                                past_len, ls, n_pick)
    o_n = o_n.reshape(ts, NSA_WIDTH).astype(BF16)
    y_s = out_ffn(xs2, o_g, o_n, mods_s, ts, 1)
    win_s = jnp.concatenate([win_past[:, :, ls:], jnp.transpose(s3(kvw), (0, 2, 1))], axis=2)
    outs_s = (y_s.reshape(bs, ls, d), s_s.reshape(1, bs, GLA_HEADS, GLA_DK, GLA_DV),
              kvc.reshape((1, bs, ls) + kvt), kvs.reshape((1, bs, ls) + kvt), tok_view(win_s))

    return (outs_p[0], outs_s[0], outs_p[1], outs_s[1], outs_p[2], outs_s[2], outs_p[3], outs_s[3],
            outs_p[4], outs_s[4])
```

```python
import functools
import math

import numpy as np
import jax
import jax.numpy as jnp
from jax import lax
from jax.experimental import pallas as pl
from jax.experimental.pallas import tpu as pltpu

F32 = jnp.float32
BF16 = jnp.bfloat16

D_MODEL = 1024
DEPTH = 1
PAGE_SIZE = 128
GLA_HEADS = 4
GLA_DV = D_MODEL // (2 * GLA_HEADS)
GLA_DK = GLA_DV // 2
GLA_RANK = 16
GLA_TAU = 16.0
GLA_CHUNK = 64
GLA_SUB = 32
GLA_WIDTH = GLA_HEADS * GLA_DV
GLA_KW = GLA_HEADS * GLA_DK
NSA_HEADS = 8
NSA_KV_HEADS = 2
NSA_GROUP = NSA_HEADS // NSA_KV_HEADS
NSA_HEAD_DIM = D_MODEL // (2 * NSA_HEADS)
NSA_WIDTH = NSA_HEADS * NSA_HEAD_DIM
CMP_BLK = 32
CMP_STRIDE = 16
CMP_HIDDEN = 2 * NSA_HEAD_DIM
SLC_BLK = 64
SLC_TOP_N = 16
WINDOW = 512
FORCE_BONUS = 1e4
NEG = -1e30
D_FF = -(-8 * D_MODEL // (3 * 256)) * 256
ALPHA = (2 * DEPTH) ** 0.25
KV_W = 2 * NSA_KV_HEADS * NSA_HEAD_DIM
N_GATES = 3 * NSA_HEADS
IN_SIZES = (GLA_KW, GLA_KW, GLA_WIDTH, GLA_RANK, GLA_WIDTH, NSA_WIDTH, KV_W, KV_W, KV_W, N_GATES)
IN_WIDTH = sum(IN_SIZES)
LN_EPS = 1e-5

LANES = 128
SUBLANES = 8
VMEM_LIMIT_BYTES = 56 * 1024 * 1024

AUX_W = LANES
IN_GROUPS = (2 * GLA_KW, GLA_WIDTH, GLA_WIDTH, NSA_WIDTH, KV_W, KV_W, KV_W, AUX_W)
IN_PERM_W = sum(IN_GROUPS)
GATE_OFF = GLA_RANK


def _cparams(sem):
    return pltpu.CompilerParams(dimension_semantics=sem, vmem_limit_bytes=VMEM_LIMIT_BYTES)


def _split3(a):
    hi = a.astype(BF16)
    r1 = a - hi.astype(F32)
    mid = r1.astype(BF16)
    lo = (r1 - mid.astype(F32)).astype(BF16)
    return hi, mid, lo


def _dot(a, b):
    return jnp.dot(a, b, preferred_element_type=F32)


def _dot_nt(a, b):
    return lax.dot_general(a, b, (((1,), (1,)), ((), ())), preferred_element_type=F32)


def _dot_tn(a, b):
    return lax.dot_general(a, b, (((0,), (0,)), ((), ())), preferred_element_type=F32)


def _layer_norm(x, g, b):
    mu = jnp.mean(x, axis=-1, keepdims=True)
    xc = x - mu
    var = jnp.mean(xc * xc, axis=-1, keepdims=True)
    return xc * lax.rsqrt(var + LN_EPS) * g + b


def _ada_kernel(c_ref, w_ref, b_ref, o_ref):
    c = c_ref[...]
    a = (c * jax.nn.sigmoid(c)).astype(BF16)
    o_ref[...] = _dot(a, w_ref[...].astype(BF16)) + b_ref[...]


def _ada(c, w_ada, b_ada):
    n, d = c.shape
    m = w_ada.shape[1]
    tn = D_MODEL
    return pl.pallas_call(
        _ada_kernel,
        grid=(m // tn,),
        in_specs=[pl.BlockSpec((n, d), lambda j: (0, 0)),
                  pl.BlockSpec((d, tn), lambda j: (0, j)),
                  pl.BlockSpec((1, tn), lambda j: (0, j))],
        out_specs=pl.BlockSpec((n, tn), lambda j: (0, j)),
        out_shape=jax.ShapeDtypeStruct((n, m), F32),
        compiler_params=_cparams(("parallel",)),
        name="ada_mod",
    )(c, w_ada, b_ada.reshape(1, m))


N_KV_GROUPS = 3


def _inproj_kernel(x_ref, sc_ref, sh_ref, g_ref, b_ref, w_ref, *o_refs, feature_major):
    xn = _layer_norm(x_ref[...], g_ref[...], b_ref[...])
    u = (xn * (1.0 + sc_ref[...]) + sh_ref[...]).astype(BF16)
    plain = dict(zip(("qk", "v", "r", "qn", "kvc", "aux"), o_refs[:6]))
    extra = dict(zip(("kvc", "kvs", "kvw") if feature_major else ("kvs", "kvw"), o_refs[6:]))
    lo = 0
    for name, wdt in zip(IN_GROUP_NAMES, IN_GROUPS):
        z = _dot(u, w_ref[:, lo:lo + wdt])
        lo += wdt
        if name == "kvc":
            plain[name][0] = z[:, :KV_W // 2]
            plain[name][1] = z[:, KV_W // 2:]
        elif name in plain:
            plain[name][...] = z.astype(plain[name].dtype)
        if name in extra:
            extra[name][...] = jnp.transpose(z) if feature_major else z


IN_GROUP_NAMES = ("qk", "v", "r", "qn", "kvc", "kvs", "kvw", "aux")


def _inproj(x, sc, sh, ln_g, ln_b, w_perm, tm, rows_per_mod, seq_per_batch=None):
    t, d = x.shape
    r = sc.shape[1]
    feature_major = seq_per_batch is not None
    mod_spec = pl.BlockSpec((None, r, d), lambda i: (i // rows_per_mod, 0, 0))
    tok = lambda w, dt: (pl.BlockSpec((tm, w), lambda i: (i, 0)), jax.ShapeDtypeStruct((t, w), dt))
    outs = [tok(2 * GLA_KW, BF16), tok(GLA_WIDTH, BF16), tok(GLA_WIDTH, BF16), tok(NSA_WIDTH, BF16),
            (pl.BlockSpec((2, tm, KV_W // 2), lambda i: (0, i, 0)), jax.ShapeDtypeStruct((2, t, KV_W // 2), F32)),
            tok(AUX_W, F32)]
    if feature_major:
        tpb = seq_per_batch // tm
        outs += [(pl.BlockSpec((None, KV_W, tm), lambda i: (i // tpb, 0, i % tpb)),
                  jax.ShapeDtypeStruct((t // seq_per_batch, KV_W, seq_per_batch), F32))] * N_KV_GROUPS
    else:
        outs += [tok(KV_W, F32)] * 2
    return pl.pallas_call(
        functools.partial(_inproj_kernel, feature_major=feature_major),
        grid=(t // tm,),
        in_specs=[pl.BlockSpec((tm, d), lambda i: (i, 0)), mod_spec, mod_spec,
                  pl.BlockSpec((1, d), lambda i: (0, 0)), pl.BlockSpec((1, d), lambda i: (0, 0)),
                  pl.BlockSpec((d, IN_PERM_W), lambda i: (0, 0))],
        out_specs=[o[0] for o in outs],
        out_shape=[o[1] for o in outs],
        compiler_params=_cparams(("parallel",)),
        name="ln_mod_inproj",
    )(x, sc, sh, ln_g.reshape(1, d), ln_b.reshape(1, d), w_perm)


def _permute_w_in(w_in):
    q_g, k_g, v_g, a_g, r_g, q_n, kv_c, kv_s, kv_w, g_n = jnp.split(w_in, np.cumsum(IN_SIZES)[:-1], axis=1)
    pad = jnp.zeros((w_in.shape[0], AUX_W - GLA_RANK - N_GATES), w_in.dtype)
    return jnp.concatenate([q_g, k_g, v_g, r_g, q_n, kv_c, kv_s, kv_w, a_g, g_n, pad], axis=1).astype(BF16)


FF_CHUNK = 256


def _out_ffn_kernel(x_ref, og_ref, on_ref, ga1_ref, sc2_ref, sh2_ref, ga2_ref, lng_ref, lnb_ref,
                    wo_ref, l1g_ref, l1b_ref, wfi_ref, wfo_ref, l2g_ref, l2b_ref, y_ref):
    x = _layer_norm(x_ref[...], lng_ref[...], lnb_ref[...])
    mix = _dot(og_ref[...], wo_ref[0:GLA_WIDTH, :]) + _dot(on_ref[...], wo_ref[GLA_WIDTH:, :])
    x1 = _layer_norm(ALPHA * x + ga1_ref[...] * mix, l1g_ref[...], l1b_ref[...])
    u2 = (x1 * (1.0 + sc2_ref[...]) + sh2_ref[...]).astype(BF16)
    ffn = jnp.zeros(x1.shape, F32)
    for c in range(D_FF // FF_CHUNK):
        lo = c * FF_CHUNK
        gate = _dot(u2, wfi_ref[:, lo:lo + FF_CHUNK])
        up = _dot(u2, wfi_ref[:, D_FF + lo:D_FF + lo + FF_CHUNK])
        f = (gate * jax.nn.sigmoid(gate) * up).astype(BF16)
        ffn = ffn + _dot(f, wfo_ref[lo:lo + FF_CHUNK, :])
    y_ref[...] = _layer_norm(ALPHA * x1 + ga2_ref[...] * ffn, l2g_ref[...], l2b_ref[...])


def _out_ffn(x, o_g, o_n, ga1, sc2, sh2, ga2, ln_in_g, ln_in_b, w_o, ln1_g, ln1_b, w_fi, w_fo, ln2_g, ln2_b,
             tm, rows_per_mod):
    t, d = x.shape
    r = ga1.shape[1]
    mod_spec = pl.BlockSpec((None, r, d), lambda i: (i // rows_per_mod, 0, 0))
    vec = lambda: pl.BlockSpec((1, d), lambda i: (0, 0))
    const = lambda shp: pl.BlockSpec(shp, lambda i: (0, 0), pipeline_mode=pl.Buffered(1))
    row = lambda a: a.reshape(1, d)
    return pl.pallas_call(
        _out_ffn_kernel,
        grid=(t // tm,),
        in_specs=[pl.BlockSpec((tm, d), lambda i: (i, 0)),
                  pl.BlockSpec((tm, GLA_WIDTH), lambda i: (i, 0)),
                  pl.BlockSpec((tm, NSA_WIDTH), lambda i: (i, 0)),
                  mod_spec, mod_spec, mod_spec, mod_spec, vec(), vec(),
                  const((d, d)), vec(), vec(), const((d, 2 * D_FF)), const((D_FF, d)), vec(), vec()],
        out_specs=pl.BlockSpec((tm, d), lambda i: (i, 0)),
        out_shape=jax.ShapeDtypeStruct((t, d), F32),
        compiler_params=_cparams(("parallel",)),
        name="outproj_ffn",
    )(x, o_g, o_n, ga1, sc2, sh2, ga2, row(ln_in_g), row(ln_in_b), w_o, row(ln1_g), row(ln1_b), w_fi, w_fo,
      row(ln2_g), row(ln2_b))


GLA_EXP_CLAMP = 80.0


def _gla_kernel(qk_ref, v_ref, r_ref, aux_ref, wa_ref, ba_ref, ng_ref, s0_ref, o_ref, sout_ref, s_scr,
                *, chunk, sub, tl, l_valid, l_pad, nbb):
    t = pl.program_id(1)
    c = chunk
    n_sub = c // sub
    hw = GLA_KW

    @pl.when(t == 0)
    def _():
        s_scr[...] = s0_ref[...]

    ri = lax.broadcasted_iota(jnp.int32, (c, c), 0)
    ci = lax.broadcasted_iota(jnp.int32, (c, c), 1)
    causal = ci <= ri
    tril = causal.astype(BF16)
    rowid = lax.broadcasted_iota(jnp.int32, (c, hw), 0)
    head_of_lane = lax.broadcasted_iota(jnp.int32, (c, hw), 1) // GLA_DK
    wa = wa_ref[...]
    wa_hi = wa.astype(BF16)
    wa_mid = (wa - wa_hi.astype(F32)).astype(BF16)
    ba = ba_ref[...]
    ng = ng_ref[...]

    def body(i, carry):
        for e in range(nbb):
            s_scr[e] = one_chunk(i, e, s_scr[e])
        return carry

    def one_chunk(i, e, s_old):
        r0 = pl.multiple_of(i * c, c)
        aux = aux_ref[e, pl.ds(r0, c), :]
        a_hi = aux.astype(BF16)
        a_mid = (aux - a_hi.astype(F32)).astype(BF16)
        z = _dot(a_hi, wa_hi) + _dot(a_mid, wa_hi) + _dot(a_hi, wa_mid) + ba
        g = jax.nn.log_sigmoid(z) / GLA_TAU
        if l_pad != l_valid:
            g = jnp.where(t * tl + r0 + rowid < l_valid, g, 0.0)
        g_hi, g_mid, g_lo = _split3(g)
        b = _dot(tril, g_hi) + _dot(tril, g_mid) + _dot(tril, g_lo)
        qk = qk_ref[e, pl.ds(r0, c), :]
        q = qk[:, :hw].astype(F32) * (GLA_DK ** -0.5)
        k = qk[:, hw:].astype(F32)
        v = v_ref[e, pl.ds(r0, c), :]
        b_last = b[c - 1:c, :]

        def heads_on_rows(x):
            return jnp.concatenate([jnp.where(head_of_lane == h, x, 0.0) for h in range(GLA_HEADS)], axis=0)

        o_inter =_dot(heads_on_rows(q * jnp.exp(b)).astype(BF16), s_old.astype(BF16))

        q_parts, k_parts = [], []
        for s_i in range(n_sub):
            b_ref = b[s_i * sub - 1:s_i * sub, :] if s_i > 0 else jnp.zeros((1, hw), F32)
            in_rows = (rowid >= s_i * sub) & (rowid < (s_i + 1) * sub)
            qt = jnp.where(in_rows, q * jnp.exp(jnp.minimum(b - b_ref, 0.0)), 0.0)
            kt = jnp.where(rowid < (s_i + 1) * sub, k * jnp.exp(jnp.minimum(b_ref - b, GLA_EXP_CLAMP)), 0.0)
            q_parts.append(heads_on_rows(qt).astype(BF16))
            k_parts.append(kt.astype(BF16))
        q_cat = jnp.concatenate(q_parts, axis=1) if n_sub > 1 else q_parts[0]
        k_cat = jnp.concatenate(k_parts, axis=1) if n_sub > 1 else k_parts[0]
        att = _dot_nt(q_cat, k_cat)

        r_t = r_ref[e, pl.ds(r0, c), :].astype(F32)
        for h in range(GLA_HEADS):
            att_h = jnp.where(causal, att[h * c:(h + 1) * c, :], 0.0).astype(BF16)
            o_h = o_inter[h * c:(h + 1) * c, :] + _dot(att_h, v[:, h * GLA_DV:(h + 1) * GLA_DV])
            o_h = o_h * lax.rsqrt(jnp.mean(o_h * o_h, axis=-1, keepdims=True) + 1e-6) * ng
            r_h = r_t[:, h * GLA_DV:(h + 1) * GLA_DV]
            o_ref[e, pl.ds(r0, c), h * GLA_DV:(h + 1) * GLA_DV] = (
                o_h * (r_h * jax.nn.sigmoid(r_h))).astype(o_ref.dtype)

        kd = jnp.concatenate([k * jnp.exp(b_last - b), jnp.zeros((LANES - c, hw), F32)], axis=0)
        kd_t = jnp.transpose(kd).astype(BF16)
        v_pad = jnp.concatenate([v, jnp.zeros((LANES - c, GLA_WIDTH), v.dtype)], axis=0)
        upd = jnp.concatenate([_dot(kd_t[h * GLA_DK:(h + 1) * GLA_DK, :], v_pad[:, h * GLA_DV:(h + 1) * GLA_DV])
                               for h in range(GLA_HEADS)], axis=0)
        decay = jnp.transpose(jnp.broadcast_to(jnp.exp(b_last), (LANES, hw)))
        return decay * s_old + upd

    lax.fori_loop(0, tl // c, body, 0)

    @pl.when(t == pl.num_programs(1) - 1)
    def _():
        sout_ref[...] = s_scr[...]


def _gla(qk, v, r, aux, w_a2p, b_a, norm_g, s0, *, chunk, sub, tl, l_valid, nbb):
    bsz, l_pad, _ = qk.shape
    nt = l_pad // tl
    kern = functools.partial(_gla_kernel, chunk=chunk, sub=sub, tl=tl, l_valid=l_valid, l_pad=l_pad, nbb=nbb)
    tile = lambda w: pl.BlockSpec((nbb, tl, w), lambda b, t: (b, t, 0))
    full = lambda shp: pl.BlockSpec(shp, lambda b, t: (0, 0))
    st = pl.BlockSpec((nbb, GLA_KW, GLA_DV), lambda b, t: (b, 0, 0))
    return pl.pallas_call(
        kern,
        grid=(bsz // nbb, nt),
        in_specs=[tile(2 * GLA_KW), tile(GLA_WIDTH), tile(GLA_WIDTH), tile(AUX_W),
                  full((AUX_W, GLA_KW)), full((1, GLA_KW)), full((1, GLA_DV)), st],
        out_specs=[tile(GLA_WIDTH), st],
        out_shape=[jax.ShapeDtypeStruct((bsz, l_pad, GLA_WIDTH), BF16),
                   jax.ShapeDtypeStruct((bsz, GLA_KW, GLA_DV), F32)],
        scratch_shapes=[pltpu.VMEM((nbb, GLA_KW, GLA_DV), F32)],
        compiler_params=_cparams(("parallel", "arbitrary")),
        name="gla_scan",
    )(qk, v, r, aux, w_a2p, b_a, norm_g, s0)


CMP_R = CMP_BLK // CMP_STRIDE


QK_SCALE = NSA_HEAD_DIM ** -0.5
LOG2E = math.log2(math.e)


def _masked_softmax(s, mask):
    s = jnp.where(mask, s, NEG)
    e = jnp.where(mask, jnp.exp(s - jnp.max(s, axis=-1, keepdims=True)), 0.0)
    return e / jnp.maximum(jnp.sum(e, axis=-1, keepdims=True), 1e-30)


def _group_queries(q, g):
    hd = NSA_HEAD_DIM
    return jnp.concatenate([q[:, (NSA_GROUP * g + r) * hd:(NSA_GROUP * g + r + 1) * hd]
                            for r in range(NSA_GROUP)], axis=0)


def _topk_rows(score_t, n_pick):
    nb, nq = score_t.shape
    rowid = lax.broadcasted_iota(jnp.int32, (nb, nq), 0)
    taken = jnp.zeros((nb, nq), jnp.int32)
    picks = []
    for _ in range(n_pick):
        free = taken == 0
        cand = jnp.where(free, score_t, -jnp.inf)
        m = jnp.max(cand, axis=0, keepdims=True)
        hit = free & (cand == m)
        idx = jnp.min(jnp.where(hit, rowid, nb), axis=0, keepdims=True)
        taken = jnp.where(rowid == idx, 1, taken)
        picks.append(idx)
    return taken.astype(F32), picks


def _importance_t(psum, ov_t):
    hi, mid, _ = _split3(psum)
    return _dot_nt(ov_t, hi) + _dot_nt(ov_t, mid)


def _overlap_t(nb, nc_pad, nc):
    j = lax.broadcasted_iota(jnp.int32, (nb, nc_pad), 0) * SLC_BLK
    i = lax.broadcasted_iota(jnp.int32, (nb, nc_pad), 1) * CMP_STRIDE
    return ((i < j + SLC_BLK) & (i + CMP_BLK > j) & (i < nc * CMP_STRIDE)).astype(BF16)


def _select_scores_t(imp_t, tq_row):
    nb, nq = imp_t.shape
    j = lax.broadcasted_iota(jnp.int32, (nb, nq), 0)
    cur = tq_row // SLC_BLK
    forced = (j == 0) | (j == cur) | (j == cur - 1)
    return jnp.where(j * SLC_BLK <= tq_row, imp_t + FORCE_BONUS * forced.astype(F32), -jnp.inf)


NSA_TQ = 256
NSA_TK = 512


SEGS_PER_PAGE = PAGE_SIZE // CMP_STRIDE
CMP_PAGES_PER_STEP = 64
CMP_ROW_CHUNK = 256
STAGE_PITCH = 24


SQ_ROWS = SUBLANES


C_W = NSA_KV_HEADS * NSA_HEAD_DIM
HC_W = NSA_KV_HEADS * CMP_HIDDEN
PC_W = CMP_R * HC_W
MXU_DEPTH = 256
CMP_S_PER_DOT = MXU_DEPTH // C_W


def _cmp_weights_fm(cmp_pe, cmp_w1, cmp_w2):
    eye = jnp.eye(NSA_KV_HEADS, dtype=F32)
    w1r = cmp_w1.reshape(2, CMP_R, CMP_STRIDE, NSA_HEAD_DIM, CMP_HIDDEN)
    w1c = jnp.einsum('cmsdh,gG->csgdmGh', w1r, eye).reshape(2, CMP_STRIDE * C_W, PC_W).astype(BF16)
    w2c_t = jnp.einsum('chd,gG->cGdgh', cmp_w2, eye).reshape(2, C_W, HC_W).astype(BF16)
    pe_flat = jnp.transpose(cmp_pe, (1, 0, 2)).reshape(2, CMP_BLK * NSA_HEAD_DIM)
    return w1c, w2c_t, pe_flat


def _cmp_bias_c(pe_ref, w1_ref, b1_ref, c):
    pe = jnp.broadcast_to(pe_ref[c:c + 1, :], (SUBLANES, pe_ref.shape[1]))
    w1c = w1_ref[c]
    pe_hi = pe.astype(BF16)
    pe_mid = (pe - pe_hi.astype(F32)).astype(BF16)
    w_hi = w1c.astype(BF16)
    w_mid = (w1c - w_hi.astype(F32)).astype(BF16)
    pb = _dot(pe_hi, w_hi) + _dot(pe_mid, w_hi) + _dot(pe_hi, w_mid)
    bc = pb[0:1, :] + b1_ref[c:c + 1, :]
    return jnp.concatenate([bc] * NSA_KV_HEADS, axis=1)


def _cmp_first_layer(x_ref, c, row0, n_seg, w1c_ref, pitch=CMP_STRIDE):
    acc = None
    for s in range(0, CMP_STRIDE, CMP_S_PER_DOT):
        xs = jnp.concatenate([x_ref[c, pl.ds(row0 + s + j, n_seg, stride=pitch), :].astype(BF16)
                              for j in range(CMP_S_PER_DOT)], axis=1)
        d = _dot(xs, w1c_ref[c, s * C_W:(s + CMP_S_PER_DOT) * C_W, :])
        acc = d if acc is None else acc + d
    return acc


def _cmp_second_layer_fm(p, bias, w2t):
    n = p.shape[0]
    h = p[:, :HC_W] + pltpu.roll(p[:, HC_W:], n - 1, 0) + bias
    return _dot_nt(w2t, jax.nn.gelu(h).astype(BF16))


def _cmp_prompt_fm_kernel(x_ref, w1c_ref, pe_ref, w1_ref, b1_ref, w2t_ref, o_ref):
    n_seg = o_ref.shape[1]
    for c in range(2):
        p = _cmp_first_layer(x_ref, c, 0, n_seg, w1c_ref)
        o_ref[c * C_W:(c + 1) * C_W, :] = _cmp_second_layer_fm(p, _cmp_bias_c(pe_ref, w1_ref, b1_ref, c), w2t_ref[c])


def _cmp_prompt_fm(x_tok, seq, w1c, pe_flat, cmp_w1, cmp_b1, w2c_t):
    bsz = x_tok.shape[1] // seq
    n_seg = seq // CMP_STRIDE
    const = lambda a: pl.BlockSpec(a.shape, lambda b: (0,) * a.ndim)
    return pl.pallas_call(
        _cmp_prompt_fm_kernel,
        grid=(bsz,),
        in_specs=[pl.BlockSpec((2, seq, C_W), lambda b: (0, b, 0)),
                  const(w1c), const(pe_flat), const(cmp_w1), const(cmp_b1), const(w2c_t)],
        out_specs=pl.BlockSpec((None, KV_W, n_seg), lambda b: (b, 0, 0)),
        out_shape=jax.ShapeDtypeStruct((bsz, KV_W, n_seg), F32),
        compiler_params=_cparams(("parallel",)),
        name="nsa_compress_prompt",
    )(x_tok, w1c, pe_flat, cmp_w1, cmp_b1, w2c_t)


def _scaled_group_queries(q, g):
    return (_group_queries(q, g).astype(F32) * QK_SCALE).astype(BF16)


def _cmp_branch_fm(qg, cmp_ref, g, valid):
    hd = NSA_HEAD_DIM
    kl, vl = g * hd, (NSA_KV_HEADS + g) * hd
    s_c = _dot(qg, cmp_ref[kl:kl + hd, :].astype(BF16))
    p_c = _masked_softmax(s_c, valid)
    return _dot_nt(p_c.astype(BF16), cmp_ref[vl:vl + hd, :].astype(BF16)), p_c


def _sum_heads(p, rows):
    out = p[0:rows]
    for r in range(1, NSA_GROUP):
        out = out + p[r * rows:(r + 1) * rows]
    return out


ATT_RB = 32


def _nsa_prompt_fm_kernel(q_ref, aux_ref, cmp_ref, kvs_ref, kvw_ref, o_ref,
                          s_scr, bias_scr, e_scr, m_scr, corr_scr, acc_scr, *, seq, nc):
    tq_n, tk_n, hd, grp = NSA_TQ, NSA_TK, NSA_HEAD_DIM, NSA_GROUP
    m_rows = grp * tq_n
    q0 = pl.program_id(1) * tq_n
    q_all = q_ref[...]
    gates = jax.nn.sigmoid(aux_ref[...])
    n_cmp = cmp_ref.shape[1]
    nb = seq // SLC_BLK

    tq_col = q0 + lax.broadcasted_iota(jnp.int32, (tq_n, 1), 0)
    tq_rows = jnp.concatenate([tq_col] * grp, axis=0)
    tq_lane = q0 + lax.broadcasted_iota(jnp.int32, (1, NSA_KV_HEADS * tq_n), 1) % tq_n
    t_end = lax.broadcasted_iota(jnp.int32, (1, n_cmp), 1) * CMP_STRIDE + (CMP_BLK - 1)
    ov_t = _overlap_t(nb, n_cmp, nc)
    n_chunks = (q0 + tq_n + tk_n - 1) // tk_n
    w_len = WINDOW + tq_n
    w0 = pl.multiple_of(jnp.maximum(q0 - WINDOW, 0), tq_n)

    qgs = [_scaled_group_queries(q_all, g) for g in range(NSA_KV_HEADS)]
    o_cs, p_cs = [], []
    for g in range(NSA_KV_HEADS):
        o_c, p_c = _cmp_branch_fm(qgs[g], cmp_ref, g, t_end <= tq_rows)
        o_cs.append(o_c)
        p_cs.append(p_c)
    n_pick = min(SLC_TOP_N, nb)

    def ranked_blocks():
        imps = [_importance_t(_sum_heads(p_c, tq_n), ov_t) for p_c in p_cs]
        return _topk_rows(_select_scores_t(jnp.concatenate(imps, axis=1), tq_lane), n_pick)[0]

    def all_visible_blocks():
        j = lax.broadcasted_iota(jnp.int32, (nb, NSA_KV_HEADS * tq_n), 0)
        return (j * SLC_BLK <= tq_lane).astype(F32)

    sel_all = lax.cond(q0 + tq_n <= n_pick * SLC_BLK, all_visible_blocks, ranked_blocks)

    tw = w0 + lax.broadcasted_iota(jnp.int32, (1, w_len), 1)
    bias_w = jnp.where((tw <= tq_col) & (tw > tq_col - WINDOW), 0.0, NEG)

    groups = range(NSA_KV_HEADS)

    def exp_rows(g, bias_g, width, online):
        per_head = tq_n // ATT_RB
        for i in range(m_rows // ATT_RB):
            rows = slice(i * ATT_RB, (i + 1) * ATT_RB)
            brows = slice((i % per_head) * ATT_RB, (i % per_head + 1) * ATT_RB)
            s = s_scr[g, rows, :width] + bias_scr[bias_g, brows, :width]
            m_new = jnp.max(s, axis=-1, keepdims=True)
            if online:
                m_old = m_scr[g, rows, :]
                m_new = jnp.maximum(m_old, m_new)
                corr_scr[g, rows, :] = jnp.exp2(m_old - m_new)
                m_scr[g, rows, :] = m_new
            e_scr[g, rows, :width] = jnp.exp2(s - m_new).astype(BF16)

    def keys_log2(ref, g, cols):
        return (ref[g * hd:(g + 1) * hd, cols] * LOG2E).astype(BF16)

    def gate_lane(g, r, branch):
        return hd + GATE_OFF + 3 * (grp * g + r) + branch

    def values_with_ones(ref, g, cols, branch):
        v = ref[(NSA_KV_HEADS + g) * hd:(NSA_KV_HEADS + g + 1) * hd, cols].astype(BF16)
        row = lax.broadcasted_iota(jnp.int32, v.shape, 0) + hd
        ones = functools.reduce(jnp.logical_or, [row == gate_lane(g, r, branch) for r in range(grp)])
        return jnp.concatenate([v, ones.astype(BF16)], axis=0)

    m_scr[...] = jnp.full(m_scr.shape, NEG, F32)
    acc_scr[...] = jnp.zeros(acc_scr.shape, F32)
    sel_ts = [sel_all[:, g * tq_n:(g + 1) * tq_n].astype(BF16) for g in groups]

    def slc_chunk(ci, carry):
        k0 = pl.multiple_of(ci * tk_n, tk_n)
        cols = pl.ds(k0, tk_n)
        kpos = k0 + lax.broadcasted_iota(jnp.int32, (1, tk_n), 1)
        expand = (lax.broadcasted_iota(jnp.int32, (nb, tk_n), 0) == kpos // SLC_BLK).astype(BF16)
        for g in groups:
            sel_k = _dot_tn(sel_ts[g], expand)
            bias_scr[g, :, :tk_n] = jnp.where((sel_k > 0.5) & (kpos <= tq_col), 0.0, NEG)
            s_scr[g, :, :tk_n] = _dot(qgs[g], keys_log2(kvs_ref, g, cols))
        for g in groups:
            exp_rows(g, g, tk_n, True)
        for g in groups:
            acc_scr[g] = corr_scr[g] * acc_scr[g] + _dot_nt(e_scr[g, :, :tk_n], values_with_ones(kvs_ref, g, cols, 1))
        return carry

    lax.fori_loop(0, n_chunks, slc_chunk, 0)
    wcols = pl.ds(w0, w_len)
    bias_scr[0, :, :w_len] = bias_w
    for g in groups:
        s_scr[g, :, :w_len] = _dot(qgs[g], keys_log2(kvw_ref, g, wcols))
    for g in groups:
        exp_rows(g, 0, w_len, False)
    acc_ws = [_dot_nt(e_scr[g, :, :w_len], values_with_ones(kvw_ref, g, wcols, 2)) for g in groups]
    gates_rot = pltpu.roll(gates, hd, 1)
    for g in groups:
        for r in range(grp):
            h = grp * g + r
            rows = slice(r * tq_n, (r + 1) * tq_n)
            acc_s, acc_w = acc_scr[g, rows, :], acc_ws[g][rows]
            f_s = gates_rot / jnp.maximum(acc_s, 1e-30)
            f_w = gates_rot / jnp.maximum(acc_w, 1e-30)
            ls, lw = gate_lane(g, r, 1), gate_lane(g, r, 2)
            gc = GATE_OFF + 3 * h
            o = (gates[:, gc:gc + 1] * o_cs[g][rows] + f_s[:, ls:ls + 1] * acc_s[:, :hd]
                 + f_w[:, lw:lw + 1] * acc_w[:, :hd])
            o_ref[:, h * hd:(h + 1) * hd] = o.astype(o_ref.dtype)


def _nsa_prompt_fm(q, aux, cmp_t, kvs_t, kvw_t):
    bsz, seq, _ = q.shape
    n_seg = cmp_t.shape[2]
    kern = functools.partial(_nsa_prompt_fm_kernel, seq=seq, nc=n_seg - CMP_R + 1)
    tile = lambda w: pl.BlockSpec((None, NSA_TQ, w), lambda b, t: (b, t, 0))
    whole = lambda n: pl.BlockSpec((None, KV_W, n), lambda b, t: (b, 0, 0))
    m_rows = NSA_GROUP * NSA_TQ
    width = max(NSA_TK, WINDOW + NSA_TQ)
    ng = NSA_KV_HEADS
    return pl.pallas_call(
        kern,
        grid=(bsz, seq // NSA_TQ),
        in_specs=[tile(NSA_WIDTH), tile(AUX_W), whole(n_seg), whole(seq), whole(seq)],
        out_specs=tile(NSA_WIDTH),
        out_shape=jax.ShapeDtypeStruct((bsz, seq, NSA_WIDTH), BF16),
        scratch_shapes=[pltpu.VMEM((ng, m_rows, width), F32),
                        pltpu.VMEM((ng, NSA_TQ, width), F32),
                        pltpu.VMEM((ng, m_rows, width), BF16),
                        pltpu.VMEM((ng, m_rows, 1), F32),
                        pltpu.VMEM((ng, m_rows, 1), F32),
                        pltpu.VMEM((ng, m_rows, 2 * NSA_HEAD_DIM), F32)],
        compiler_params=_cparams(("parallel", "arbitrary")),
        name="nsa_attn_prompt",
    )(q, aux, cmp_t, kvs_t, kvw_t)


def _cmp_sample_fm_kernel(pt_ref, cache_ref, w1f_ref, pe_ref, w1_ref, b1_ref, w2t_ref, o_ref,
                          xbuf, stage_a, stage_b, p_scr, bias_scr, sem, *, steps_per_batch):
    b = pl.program_id(0)
    h = pl.program_id(1)
    step = b * steps_per_batch + h
    n_steps = pl.num_programs(0) * steps_per_batch
    pps = CMP_PAGES_PER_STEP
    segs = pps * SEGS_PER_PAGE

    def page_copy(bb, hh, p, slot):
        return pltpu.make_async_copy(cache_ref.at[pt_ref[bb, hh * pps + p]], xbuf.at[slot, p], sem.at[slot])

    def start_fetch(bb, hh, slot):
        for p in range(pps):
            page_copy(bb, hh, p, slot).start(priority=p % 2)

    @pl.when(step == 0)
    def _():
        start_fetch(b, h, 0)

    @pl.when(step + 1 < n_steps)
    def _():
        wrap = h + 1 == steps_per_batch
        start_fetch(jnp.where(wrap, b + 1, b), jnp.where(wrap, 0, h + 1), (step + 1) % 2)

    slot = step % 2
    for p in range(pps):
        page_copy(b, h, p, slot).wait()

    pages_per_chunk = CMP_ROW_CHUNK // SEGS_PER_PAGE
    n_chunks = segs // CMP_ROW_CHUNK
    stages = (stage_a, stage_b)
    assert n_chunks == len(stages)

    def transpose_page(rc, lp):
        p = rc * pages_per_chunk + lp
        for c in range(2):
            tok = jnp.transpose(xbuf[slot, p, c * C_W:(c + 1) * C_W, :].astype(BF16)).astype(F32)
            for n in range(SEGS_PER_PAGE):
                r0 = (lp * SEGS_PER_PAGE + n) * STAGE_PITCH
                stages[rc][c, r0:r0 + CMP_STRIDE, :] = tok[n * CMP_STRIDE:(n + 1) * CMP_STRIDE]

    def first_layer(rc, between=()):
        between = list(between)
        dots = [(c, s) for c in range(2) for s in range(0, CMP_STRIDE, CMP_S_PER_DOT)]
        per_dot = -(-len(between) // len(dots))
        accs = [None, None]
        for c, s in dots:
            xs = jnp.concatenate([stages[rc][c, pl.ds(s + j, CMP_ROW_CHUNK, stride=STAGE_PITCH), :].astype(BF16)
                                  for j in range(CMP_S_PER_DOT)], axis=1)
            d = _dot(xs, w1f_ref[c, s * C_W:(s + CMP_S_PER_DOT) * C_W, :])
            accs[c] = d if accs[c] is None else accs[c] + d
            for thunk in between[:per_dot]:
                thunk()
            between = between[per_dot:]
        for c in range(2):
            r0 = pl.multiple_of(h * segs + rc * CMP_ROW_CHUNK, CMP_ROW_CHUNK)
            p_scr[c, pl.ds(r0, CMP_ROW_CHUNK), :] = accs[c]

    for lp in range(pages_per_chunk):
        transpose_page(0, lp)
    for rc in range(n_chunks):
        nxt = [functools.partial(transpose_page, rc + 1, lp) for lp in range(pages_per_chunk)] if rc + 1 < n_chunks else []
        first_layer(rc, nxt)

    @pl.when(step == 0)
    def _():
        for c in range(2):
            bias_scr[c] = jnp.broadcast_to(_cmp_bias_c(pe_ref, w1_ref, b1_ref, c), (SUBLANES, HC_W))

    @pl.when(h == steps_per_batch - 1)
    def _():
        for c in range(2):
            o_ref[c * C_W:(c + 1) * C_W, :] = _cmp_second_layer_fm(p_scr[c], bias_scr[c, 0:1, :], w2t_ref[c])


def _cmp_sample_fm(page_table, cache_fm, w1_full, pe_flat, cmp_w1, cmp_b1, w2_bd_t):
    bsz, n_pages = page_table.shape
    steps = n_pages // CMP_PAGES_PER_STEP
    n_seg = n_pages * SEGS_PER_PAGE
    const = lambda a: pl.BlockSpec(a.shape, lambda b, h, pt: (0,) * a.ndim, pipeline_mode=pl.Buffered(1))
    grid_spec = pltpu.PrefetchScalarGridSpec(
        num_scalar_prefetch=1,
        grid=(bsz, steps),
        in_specs=[pl.BlockSpec(memory_space=pl.ANY), const(w1_full), const(pe_flat), const(cmp_w1),
                  const(cmp_b1), const(w2_bd_t)],
        out_specs=pl.BlockSpec((None, KV_W, n_seg), lambda b, h, pt: (b, 0, 0)),
        scratch_shapes=[pltpu.VMEM((2, CMP_PAGES_PER_STEP, KV_W, PAGE_SIZE), F32),
                        pltpu.VMEM((2, CMP_ROW_CHUNK * STAGE_PITCH, C_W), F32),
                        pltpu.VMEM((2, CMP_ROW_CHUNK * STAGE_PITCH, C_W), F32),
                        pltpu.VMEM((2, n_seg, PC_W), F32),
                        pltpu.VMEM((2, SUBLANES, HC_W), F32),
                        pltpu.SemaphoreType.DMA((2,))],
    )
    return pl.pallas_call(
        functools.partial(_cmp_sample_fm_kernel, steps_per_batch=steps),
        grid_spec=grid_spec,
        out_shape=jax.ShapeDtypeStruct((bsz, KV_W, n_seg), F32),
        compiler_params=_cparams(("arbitrary", "arbitrary")),
        name="nsa_compress_sample",
    )(page_table, cache_fm, w1_full, pe_flat, cmp_w1, cmp_b1, w2_bd_t)


SEL_COLS = NSA_KV_HEADS * SQ_ROWS
SEL_NBB = LANES // SEL_COLS


def _nsa_sample_select_fm_kernel(q_ref, cmp_ref, oc_ref, idx_ref, *, past_len, nb, nb_pad, nc):
    hd, grp = NSA_HEAD_DIM, NSA_GROUP
    m_rows = grp * SQ_ROWS
    n_cmp = cmp_ref.shape[2]
    tq_rows = past_len + lax.broadcasted_iota(jnp.int32, (m_rows, 1), 0) % SQ_ROWS
    tq_lane = past_len + lax.broadcasted_iota(jnp.int32, (1, LANES), 1) % SQ_ROWS
    t_end = lax.broadcasted_iota(jnp.int32, (1, n_cmp), 1) * CMP_STRIDE + (CMP_BLK - 1)
    ov_t = _overlap_t(nb_pad, n_cmp, nc)
    psums = []
    for e in range(SEL_NBB):
        q_all = q_ref[e]
        for g in range(NSA_KV_HEADS):
            o_c, p_c = _cmp_branch_fm(_scaled_group_queries(q_all, g), cmp_ref.at[e], g, t_end <= tq_rows)
            psums.append(_sum_heads(p_c, SQ_ROWS))
            for r in range(grp):
                h = grp * g + r
                oc_ref[e, :, h * hd:(h + 1) * hd] = o_c[r * SQ_ROWS:(r + 1) * SQ_ROWS]
    score_t = _select_scores_t(_importance_t(jnp.concatenate(psums, axis=0), ov_t), tq_lane)
    rowid = lax.broadcasted_iota(jnp.int32, score_t.shape, 0)
    _, picks = _topk_rows(jnp.where(rowid < nb, score_t, -jnp.inf), min(SLC_TOP_N, nb))
    idx_ref[...] = jnp.concatenate(picks, axis=0)


def _nsa_sample_select_fm(q_pad, cmp_t, past_len, seq_new):
    bsz = q_pad.shape[0]
    n_seg = cmp_t.shape[2]
    nb = -(-(past_len + seq_new) // SLC_BLK)
    nb_pad = -(-nb // SUBLANES) * SUBLANES
    n_pick = min(SLC_TOP_N, nb)
    kern = functools.partial(_nsa_sample_select_fm_kernel, past_len=past_len, nb=nb, nb_pad=nb_pad,
                             nc=n_seg - CMP_R + 1)
    return pl.pallas_call(
        kern,
        grid=(bsz // SEL_NBB,),
        in_specs=[pl.BlockSpec((SEL_NBB, SQ_ROWS, NSA_WIDTH), lambda b: (b, 0, 0)),
                  pl.BlockSpec((SEL_NBB, KV_W, n_seg), lambda b: (b, 0, 0))],
        out_specs=[pl.BlockSpec((SEL_NBB, SQ_ROWS, NSA_WIDTH), lambda b: (b, 0, 0)),
                   pl.BlockSpec((None, n_pick, LANES), lambda b: (b, 0, 0))],
        out_shape=[jax.ShapeDtypeStruct((bsz, SQ_ROWS, NSA_WIDTH), F32),
                   jax.ShapeDtypeStruct((bsz // SEL_NBB, n_pick, LANES), jnp.int32)],
        compiler_params=_cparams(("parallel",)),
        name="nsa_select_sample",
    )(q_pad, cmp_t)


ATTEND_SLOTS = 2


def _nsa_sample_attend_fm_kernel(pt_ref, col_ref, idx_ref, q_ref, gate_ref, oc_ref, cache_ref, tail_ref, winp_ref,
                                 winn_ref, o_ref, kvbuf, sem, *, past_len, seq_new, n_pick):
    b = pl.program_id(0)
    nbatch = pl.num_programs(0)
    hd, grp = NSA_HEAD_DIM, NSA_GROUP
    n_items = seq_new * NSA_KV_HEADS * n_pick
    n_past_blocks = past_len // SLC_BLK
    blocks_per_page = PAGE_SIZE // SLC_BLK

    def block_id(bb, item):
        return jnp.minimum(idx_ref[bb * n_items + item], n_past_blocks)

    def copy(bb, item, slot):
        g = (item // n_pick) % NSA_KV_HEADS
        lanes = slice((item % n_pick) * PAGE_SIZE, (item % n_pick + 1) * PAGE_SIZE)
        rows = slice(g * hd, (g + 1) * hd)
        page = pt_ref[bb, col_ref[bb * n_items + item]]
        return pltpu.make_async_copy(cache_ref.at[page, :, rows, :], kvbuf.at[slot, item // n_pick, :, :, lanes],
                                     sem.at[slot])

    def start_fetch(bb, slot):
        for item in range(n_items):
            copy(bb, item, slot).start(priority=item % 2)

    @pl.when(b == 0)
    def _():
        for ahead in range(ATTEND_SLOTS - 1):
            @pl.when(ahead < nbatch)
            def _():
                start_fetch(ahead, ahead)

    nxt = b + ATTEND_SLOTS - 1

    @pl.when(nxt < nbatch)
    def _():
        start_fetch(nxt, nxt % ATTEND_SLOTS)

    slot = b % ATTEND_SLOTS

    for item in range(n_items):
        copy(b, item, slot).wait()

    head_row = lax.broadcasted_iota(jnp.int32, (NSA_HEADS, 1), 0)
    tok = lax.broadcasted_iota(jnp.int32, (1, PAGE_SIZE), 1)
    key_lane = lax.broadcasted_iota(jnp.int32, (1, n_pick * PAGE_SIZE), 1)
    wb, wn = winp_ref.shape[1], winn_ref.shape[1]
    tw_p = past_len - wb + lax.broadcasted_iota(jnp.int32, (1, wb), 1)
    tw_n = past_len + lax.broadcasted_iota(jnp.int32, (1, wn), 1)

    def heads16(x):
        return jnp.concatenate([x.astype(BF16), jnp.zeros_like(x, dtype=BF16)], axis=0)

    def merge_groups(per_group):
        out = per_group[0]
        for g in range(1, NSA_KV_HEADS):
            out = jnp.where(head_row >= g * grp, per_group[g], out)
        return out

    for qi in range(seq_new):
        tq = past_len + qi
        q16 = heads16(q_ref[qi] * QK_SCALE)
        gates = jax.nn.sigmoid(gate_ref[qi])
        o_s_g, o_w_g = [], []
        for g in range(NSA_KV_HEADS):
            qg_i = qi * NSA_KV_HEADS + g
            kl = g * hd
            blk = jnp.zeros_like(key_lane)
            for kk in range(n_pick):
                blk = jnp.where(key_lane // PAGE_SIZE == kk, block_id(b, qg_i * n_pick + kk), blk)
            new_picked = jnp.max(blk, axis=1, keepdims=True) == n_past_blocks
            in_block = (key_lane % PAGE_SIZE) // SLC_BLK == blk % blocks_per_page
            mask = jnp.concatenate(
                [(blk < n_past_blocks) & in_block & (blk * SLC_BLK + key_lane % SLC_BLK <= tq),
                 new_picked & (tok < SLC_BLK) & (past_len + tok <= tq)], axis=1)
            kcat = jnp.concatenate([kvbuf[slot, qg_i, 0].astype(BF16), tail_ref[0, kl:kl + hd, :].astype(BF16)], axis=1)
            vcat = jnp.concatenate([kvbuf[slot, qg_i, 1].astype(BF16), tail_ref[1, kl:kl + hd, :].astype(BF16)], axis=1)
            p = _masked_softmax(_dot(q16, kcat)[:NSA_HEADS], mask)
            o_s_g.append(_dot_nt(heads16(p), vcat)[:NSA_HEADS])
            vl = (NSA_KV_HEADS + g) * hd
            bias_p = jnp.where((tw_p <= tq) & (tw_p > tq - WINDOW) & (tw_p >= 0), 0.0, NEG)
            bias_n = jnp.where((tw_n <= tq) & (tw_n > tq - WINDOW), 0.0, NEG)
            s_p = _dot(q16, winp_ref[kl:kl + hd, :].astype(BF16))[:NSA_HEADS] + bias_p
            s_n = _dot(q16, winn_ref[kl:kl + hd, :].astype(BF16))[:NSA_HEADS] + bias_n
            m = jnp.maximum(jnp.max(s_p, axis=-1, keepdims=True), jnp.max(s_n, axis=-1, keepdims=True))
            e_p, e_n = jnp.exp(s_p - m), jnp.exp(s_n - m)
            den = jnp.maximum(jnp.sum(e_p, axis=-1, keepdims=True) + jnp.sum(e_n, axis=-1, keepdims=True), 1e-30)
            o_w_g.append((_dot_nt(heads16(e_p), winp_ref[vl:vl + hd, :].astype(BF16))
                          + _dot_nt(heads16(e_n), winn_ref[vl:vl + hd, :].astype(BF16)))[:NSA_HEADS] / den)
        o_ref[qi] = (gates[:, 0:1] * oc_ref[qi] + gates[:, 1:2] * merge_groups(o_s_g)
                     + gates[:, 2:3] * merge_groups(o_w_g))


def _nsa_sample_attend_fm(page_table, idx_flat, q_heads, gate_logits, o_c, cache_fm, tail_fm, win_past, win_new,
                          past_len, seq_new, n_pick):
    bsz = q_heads.shape[0]
    wb, wn = win_past.shape[2], win_new.shape[2]
    col_flat = jnp.minimum(idx_flat, past_len // SLC_BLK - 1) // (PAGE_SIZE // SLC_BLK)
    kern = functools.partial(_nsa_sample_attend_fm_kernel, past_len=past_len, seq_new=seq_new, n_pick=n_pick)
    per_b = lambda n, w: pl.BlockSpec((None, n, w), lambda b, *_: (b, 0, 0))
    per_bq = lambda w: pl.BlockSpec((None, seq_new, NSA_HEADS, w), lambda b, *_: (b, 0, 0, 0))
    n_qg = seq_new * NSA_KV_HEADS
    buf = pltpu.VMEM((ATTEND_SLOTS, n_qg, 2, NSA_HEAD_DIM, n_pick * PAGE_SIZE), F32)
    grid_spec = pltpu.PrefetchScalarGridSpec(
        num_scalar_prefetch=3,
        grid=(bsz,),
        in_specs=[per_bq(NSA_HEAD_DIM), per_bq(LANES), per_bq(NSA_HEAD_DIM),
                  pl.BlockSpec(memory_space=pl.ANY),
                  pl.BlockSpec((None, 2, C_W, LANES), lambda b, *_: (b, 0, 0, 0)),
                  per_b(KV_W, wb), per_b(KV_W, wn)],
        out_specs=per_bq(NSA_HEAD_DIM),
        scratch_shapes=[buf, pltpu.SemaphoreType.DMA((ATTEND_SLOTS,))],
    )
    return pl.pallas_call(
        kern,
        grid_spec=grid_spec,
        out_shape=jax.ShapeDtypeStruct((bsz, seq_new, NSA_HEADS, NSA_HEAD_DIM), F32),
        compiler_params=_cparams(("arbitrary",)),
        name="nsa_attend_sample",
    )(page_table, col_flat, idx_flat, q_heads, gate_logits, o_c, cache_fm, tail_fm, win_past, win_new)


PROMPT_TM = 512
FFN_TM = 512
GLA_TL = 512
SAMPLE_GLA_ROWS = 16
GLA_PROMPT_NBB = 8
GLA_SAMPLE_NBB = 8


def _pad_rows(x, n):
    return jnp.pad(x, ((0, 0), (0, n - x.shape[1]), (0, 0)))


def kernel(x_prompt, x_sample, state_gla, cache_cmp_kv, cache_slc_kv, cache_win_kv, page_table, c_prompt,
           c_sample, ln_in_g, ln_in_b, w_ada, b_ada, w_in, gla_w_a2, gla_b_a, gla_norm_g, cmp_pe, cmp_w1,
           cmp_b1, cmp_w2, w_o, ln1_g, ln1_b, w_ffn_in, w_ffn_out, ln2_g, ln2_b):
    assert w_in.shape[0] == DEPTH == 1
    l = 0
    bp, lp, d = x_prompt.shape
    bs, ls, _ = x_sample.shape
    n_pool = cache_cmp_kv.shape[1]
    n_pages = page_table.shape[1]
    past_len = n_pages * PAGE_SIZE
    wb = cache_win_kv.shape[2]
    assert ((past_len + ls) // CMP_STRIDE) * CMP_STRIDE <= past_len and past_len % SLC_BLK == 0
    assert ls <= SQ_ROWS and ls <= SLC_BLK and wb == WINDOW and bs % SEL_NBB == 0

    w_perm = _permute_w_in(w_in[l])
    w_o_b, w_fi_b, w_fo_b = w_o[l].astype(BF16), w_ffn_in[l].astype(BF16), w_ffn_out[l].astype(BF16)
    w_a2p = jnp.zeros((AUX_W, GLA_KW), F32).at[:GLA_RANK].set(gla_w_a2[l])
    b_a = gla_b_a[l].reshape(1, GLA_KW)
    norm_g = gla_norm_g[l].reshape(1, GLA_DV)
    w1c, w2c_t, pe_flat = _cmp_weights_fm(cmp_pe[l], cmp_w1[l], cmp_w2[l])
    kvt = (2, NSA_KV_HEADS, NSA_HEAD_DIM)

    def fm_view(a):
        return jnp.transpose(a, (0, 2, 3, 4, 1)).reshape(a.shape[0], KV_W, a.shape[1])

    def tok_view(a_fm):
        n, _, t = a_fm.shape
        return jnp.transpose(a_fm.reshape((n,) + kvt + (t,)), (0, 4, 1, 2, 3))[None]

    mod = _ada(jnp.concatenate([c_prompt, c_sample], axis=0), w_ada[l], b_ada[l])
    mods_p = [m[:bp, None, :] for m in jnp.split(mod, 6, axis=-1)]
    mods_s = [jnp.repeat(m[bp:], ls, axis=0)[None] for m in jnp.split(mod, 6, axis=-1)]

    def out_ffn(x2d, o_g, o_n, mods, tm, rpm):
        sh1, sc1, ga1, sh2, sc2, ga2 = mods
        return _out_ffn(x2d, o_g, o_n, ga1, sc2, sh2, ga2, ln_in_g, ln_in_b, w_o_b, ln1_g[l], ln1_b[l], w_fi_b,
                        w_fo_b, ln2_g[l], ln2_b[l], tm, rpm)

    xp2 = x_prompt.reshape(bp * lp, d)
    rpm = lp // PROMPT_TM
    qk, v, r, qn, kvc, aux, kvc_t, kvs_t, kvw_t = _inproj(
        xp2, mods_p[1], mods_p[0], ln_in_g, ln_in_b, w_perm, PROMPT_TM, rpm, seq_per_batch=lp)
    b3 = lambda a: a.reshape(bp, lp, a.shape[-1])
    o_g, s_p = _gla(b3(qk), b3(v), b3(r), b3(aux), w_a2p, b_a, norm_g, jnp.zeros((bp, GLA_KW, GLA_DV), F32),
                    chunk=GLA_CHUNK, sub=GLA_SUB, tl=GLA_TL, l_valid=lp, nbb=GLA_PROMPT_NBB)
    cmp_t_p = _cmp_prompt_fm(kvc, lp, w1c, pe_flat, cmp_w1[l], cmp_b1[l], w2c_t)
    o_n = _nsa_prompt_fm(b3(qn), b3(aux), cmp_t_p, kvs_t, kvw_t)
    y_p = out_ffn(xp2, o_g.reshape(bp * lp, GLA_WIDTH), o_n.reshape(bp * lp, NSA_WIDTH), mods_p, FFN_TM,
                  lp // FFN_TM)
    w_keep = min(WINDOW, lp)
    outs_p = (y_p.reshape(bp, lp, d), s_p.reshape(1, bp, GLA_HEADS, GLA_DK, GLA_DV),
              tok_view(kvc_t), tok_view(kvs_t), tok_view(kvw_t[:, :, lp - w_keep:]))

    ts = bs * ls
    xs2 = x_sample.reshape(ts, d)
    qk, v, r, qn, kvc, aux, kvs, kvw = _inproj(xs2, mods_s[1], mods_s[0], ln_in_g, ln_in_b, w_perm, ts, 1)
    s3 = lambda a: a.reshape(bs, ls, a.shape[-1])
    g16 = lambda a: _pad_rows(s3(a), SAMPLE_GLA_ROWS)
    o_g, s_s = _gla(g16(qk), g16(v), g16(r), g16(aux), w_a2p, b_a, norm_g, state_gla[l].reshape(bs, GLA_KW, GLA_DV),
                    chunk=SAMPLE_GLA_ROWS, sub=SAMPLE_GLA_ROWS, tl=SAMPLE_GLA_ROWS, l_valid=ls,
                    nbb=GLA_SAMPLE_NBB)
    o_g = o_g[:, :ls].reshape(ts, GLA_WIDTH)
    kvc = jnp.concatenate([kvc[0], kvc[1]], axis=1)
    cmp_t_s = _cmp_sample_fm(page_table, fm_view(cache_cmp_kv[l]), w1c, pe_flat, cmp_w1[l], cmp_b1[l], w2c_t)
    q_pad = _pad_rows(s3(qn), SQ_ROWS)
    o_c, idx = _nsa_sample_select_fm(q_pad, cmp_t_s, past_len, ls)
    n_pick = idx.shape[1]
    idx = idx.reshape(bs // SEL_NBB, n_pick, SEL_NBB, NSA_KV_HEADS, SQ_ROWS)[..., :ls]
    idx_flat = jnp.transpose(idx, (0, 2, 4, 3, 1)).reshape(-1)
    new_fm =lambda a: jnp.pad(jnp.transpose(s3(a), (0, 2, 1)), ((0, 0), (0, 0), (0, LANES - ls)))
    win_past = fm_view(cache_win_kv[l])
    per_head = lambda a, w: a.reshape(bs, -1, NSA_HEADS, w)[:, :ls]
    gate_logits = jnp.pad(per_head(aux[:, GATE_OFF:GATE_OFF + N_GATES], 3), ((0, 0),) * 3 + ((0, LANES - 3),))
    o_n = _nsa_sample_attend_fm(page_table, idx_flat, per_head(qn.astype(F32), NSA_HEAD_DIM), gate_logits,
                                per_head(o_c, NSA_HEAD_DIM),
                                fm_view(cache_slc_kv[l]).reshape(n_pool, 2, C_W, PAGE_SIZE),
                                new_fm(kvs).reshape(bs, 2, C_W, LANES), win_past, new_fm(kvw),
                                past_len, ls, n_pick)
    o_n = o_n.reshape(ts, NSA_WIDTH).astype(BF16)
    y_s = out_ffn(xs2, o_g, o_n, mods_s, ts, 1)
    win_s = jnp.concatenate([win_past[:, :, ls:], jnp.transpose(s3(kvw), (0, 2, 1))], axis=2)
    outs_s = (y_s.reshape(bs, ls, d), s_s.reshape(1, bs, GLA_HEADS, GLA_DK, GLA_DV),
              kvc.reshape((1, bs, ls) + kvt), kvs.reshape((1, bs, ls) + kvt), tok_view(win_s))

    return (outs_p[0], outs_s[0], outs_p[1], outs_s[1], outs_p[2], outs_s[2], outs_p[3], outs_s[3],
            outs_p[4], outs_s[4])
```
